```python
import math
import jax
import jax.numpy as jnp
from jax import lax
import numpy as np

D_MODEL = 1024
BATCH = 32
SEQ = 256
DEPTH = 4
DEC_BATCH = 2
DEC_SEQ = 2048
PAST_LEN = 512

GRID_W = 64
N_MIXERS = 4
NORM_EPS = 1e-6
ROPE_BASE = 10000.0
Q_BLOCK = 128
S5_GROUP = 16
S5_GROUPS = D_MODEL // S5_GROUP
S5_STATE = 64
S5_DT_MIN = 1e-3
S5_DT_MAX = 1e-1
RWKV_HEAD = 64
RWKV_HEADS = D_MODEL // RWKV_HEAD
RWKV_DECAY_LORA = 64
RWKV_ICLR_LORA = 64
RWKV_GATE_LORA = 128
RWKV_DECAY_SCALE = math.exp(-0.5)
RWKV_GN_EPS = 64e-5
GQA_HEADS = 16
GQA_KV_HEADS = 4
GQA_GROUP = GQA_HEADS // GQA_KV_HEADS
GQA_HEAD_DIM = 64
WINDOW = 128
BAND_BLOCK = 128
MLA_HEADS = 16
MLA_Q_LORA = 384
MLA_KV_LORA = 256
MLA_NOPE = 64
MLA_ROPE = 32
MLA_QK = MLA_NOPE + MLA_ROPE
MLA_V = 64
MOE_GROUPS = 4
MOE_PER_GROUP = 4
MOE_EXPERTS = MOE_GROUPS * MOE_PER_GROUP
MOE_TOP_K = 2
MOE_HIDDEN = 512

F32 = jnp.float32

kernel_name = 'hybrid_dit_s5_rwkv7_swa_mla_hmoe_step'


def rms_norm(x, g):
    xf = x.astype(F32)
    y = xf * lax.rsqrt(jnp.mean(xf * xf, axis=-1, keepdims=True) + NORM_EPS)
    return (y * g.astype(F32)).astype(x.dtype)


def ada_modulation(cond, w, b):
    m = jax.nn.silu(cond) @ w + b
    return jnp.split(m[:, None, :], 6, axis=-1)


def modulate(x, g, shift, scale):
    return rms_norm(x, g) * (1 + scale) + shift


def rope_1d(x, pos):
    half = x.shape[-1] // 2
    inv = ROPE_BASE ** (-jnp.arange(half, dtype=F32) / half)
    ang = pos.astype(F32)[:, None] * inv[None, :]
    cos = jnp.cos(ang)[:, None, :]
    sin = jnp.sin(ang)[:, None, :]
    x1 = x[..., :half].astype(F32)
    x2 = x[..., half:].astype(F32)
    return jnp.concatenate([x1 * cos - x2 * sin, x2 * cos + x1 * sin], axis=-1).astype(x.dtype)


def rope_2d(x, row, col):
    h = x.shape[-1] // 2
    return jnp.concatenate([rope_1d(x[..., :h], row), rope_1d(x[..., h:], col)], axis=-1)


def complex_mul(ar, ai, br, bi):
    return ar * br - ai * bi, ar * bi + ai * br


def s5_discretize(lam_re, lam_im, log_step):
    lam_re = lam_re.astype(F32)
    lam_im = lam_im.astype(F32)
    dt = jnp.exp(log_step.astype(F32))
    mag = jnp.exp(lam_re * dt)
    ab_re = mag * jnp.cos(lam_im * dt)
    ab_im = mag * jnp.sin(lam_im * dt)
    nr = ab_re - 1.0
    den = lam_re * lam_re + lam_im * lam_im
    f_re = (nr * lam_re + ab_im * lam_im) / den
    f_im = (ab_im * lam_re - nr * lam_im) / den
    return ab_re, ab_im, f_re, f_im


def _s5_combine(left, right):
    lar, lai, lbr, lbi = left
    rar, rai, rbr, rbi = right
    ar, ai = complex_mul(rar, rai, lar, lai)
    br, bi = complex_mul(rar, rai, lbr, lbi)
    return ar, ai, br + rbr, bi + rbi


def s5_scan(ab_re, ab_im, bu_re, bu_im, s0_re, s0_im, reverse):
    first = -1 if reverse else 0
    bu_re = bu_re.at[:, first].add(ab_re * s0_re - ab_im * s0_im)
    bu_im = bu_im.at[:, first].add(ab_re * s0_im + ab_im * s0_re)
    a_re = jnp.broadcast_to(ab_re, bu_re.shape)
    a_im = jnp.broadcast_to(ab_im, bu_im.shape)
    _, _, sr, si = lax.associative_scan(_s5_combine, (a_re, a_im, bu_re, bu_im), reverse=reverse, axis=1)
    return sr, si


def s5_mixer(u, s0_re, s0_im, lam_re, lam_im, log_step, b_re, b_im, c_re, c_im, d_skip, glu_w1, glu_w2):
    bsz, length, d = u.shape
    ug = u.astype(F32).reshape(bsz, length, S5_GROUPS, S5_GROUP)
    y = u.astype(F32) * d_skip.astype(F32)
    fin_re, fin_im = [], []
    for dr in range(2):
        ab_re, ab_im, f_re, f_im = s5_discretize(lam_re[dr], lam_im[dr], log_step[dr])
        br = jnp.einsum('blgc,gpc->blgp', ug, b_re[dr].astype(F32))
        bi = jnp.einsum('blgc,gpc->blgp', ug, b_im[dr].astype(F32))
        bu_re, bu_im = complex_mul(f_re, f_im, br, bi)
        sr, si = s5_scan(ab_re, ab_im, bu_re, bu_im, s0_re[:, dr].astype(F32), s0_im[:, dr].astype(F32), dr == 1)
        yg = (jnp.einsum('blgp,gcp->blgc', sr, c_re[dr].astype(F32))
              - jnp.einsum('blgp,gcp->blgc', si, c_im[dr].astype(F32)))
        y = y + yg.reshape(bsz, length, d)
        end = 0 if dr == 1 else length - 1
        fin_re.append(sr[:, end])
        fin_im.append(si[:, end])
    z = jax.nn.gelu(y).astype(u.dtype)
    out = (z @ glu_w1) * jax.nn.sigmoid(z @ glu_w2)
    return out, jnp.stack(fin_re, axis=1), jnp.stack(fin_im, axis=1)


def centred_shift(x):
    prev = jnp.pad(x[:, :-1], ((0, 0), (1, 0), (0, 0)))
    nxt = jnp.pad(x[:, 1:], ((0, 0), (0, 1), (0, 0)))
    return 0.5 * (prev + nxt)


def rwkv_scan(s0, r, w, k, v, kk, a, reverse):
    def step(S, inp):
        r_t, w_t, k_t, v_t, kk_t, a_t = inp
        sa = jnp.einsum('bhvk,bhk->bhv', S, kk_t)
        S = (S * w_t[:, :, None, :] - sa[..., None] * (kk_t * a_t)[:, :, None, :]
             + v_t[..., None] * k_t[:, :, None, :])
        return S, jnp.einsum('bhvk,bhk->bhv', S, r_t)
    xs = tuple(jnp.moveaxis(t, 1, 0) for t in (r, w, k, v, kk, a))
    s_fin, ys = lax.scan(step, s0, xs, reverse=reverse)
    return s_fin, jnp.moveaxis(ys, 0, 1)


def rwkv_mixer(u, s0, mu, w_r, w_k, w_v, w_o, w0, w_l1, w_l2, a0, a_l1, a_l2, g_l1, g_l2,
               k_k, k_a, r_k, ln_g, ln_b):
    bsz, length, d = u.shape
    H, N = RWKV_HEADS, RWKV_HEAD

    def heads(t):
        return t.astype(F32).reshape(bsz, length, H, N)

    dx = centred_shift(u) - u
    xr, xw, xk, xv, xa, xg = [u + dx * mu[i] for i in range(6)]
    r = heads(xr @ w_r)
    v = heads(xv @ w_v)
    k = xk @ w_k
    kk = heads(k * k_k)
    kk = kk * lax.rsqrt(jnp.sum(kk * kk, axis=-1, keepdims=True) + 1e-12)
    kh = heads(k)
    k_a_h = k_a.astype(F32).reshape(H, N)
    g = jax.nn.sigmoid(xg @ g_l1) @ g_l2
    ys, bonuses, finals = [], [], []
    for dr in range(2):
        w = heads(jnp.exp(-RWKV_DECAY_SCALE * jax.nn.sigmoid((w0[dr] + jnp.tanh(xw @ w_l1[dr]) @ w_l2[dr]).astype(F32))))
        a = heads(jax.nn.sigmoid((a0[dr] + (xa @ a_l1[dr]) @ a_l2[dr]).astype(F32)))
        kd = kh * (1 + (a - 1) * k_a_h)
        s_fin, yd = rwkv_scan(s0[:, dr].astype(F32), r, w, kd, v, kk, a, dr == 1)
        ys.append(yd)
        bonuses.append(jnp.sum(r * kd * r_k.astype(F32), axis=-1, keepdims=True) * v)
        finals.append(s_fin)
    y = ys[0] + ys[1]
    mean = jnp.mean(y, axis=-1, keepdims=True)
    var = jnp.mean(jnp.square(y - mean), axis=-1, keepdims=True)
    yn = ((y - mean) * lax.rsqrt(var + RWKV_GN_EPS)).reshape(bsz, length, d)
    yn = yn * ln_g.astype(F32) + ln_b.astype(F32) + (bonuses[0] + bonuses[1]).reshape(bsz, length, d)
    out = (yn * g.astype(F32)).astype(u.dtype) @ w_o
    return out, jnp.stack(finals, axis=1)


def dense_attention(q, k, v, sink):
    bsz, lq, kh, gq, dh = q.shape
    nq = lq // Q_BLOCK
    scale = dh ** -0.5
    qb = jnp.moveaxis(q.reshape(bsz, nq, Q_BLOCK, kh, gq, dh), 1, 0)

    def attend(qblk):
        s = jnp.einsum('bqhgd,bkhd->bhgqk', qblk, k, preferred_element_type=F32) * scale
        if sink is not None:
            snk = jnp.broadcast_to(sink.astype(F32)[None, :, :, None, None], s.shape[:-1] + (1,))
            s = jnp.concatenate([s, snk], axis=-1)
        p = jax.nn.softmax(s, axis=-1)
        if sink is not None:
            p = p[..., :-1]
        return jnp.einsum('bhgqk,bkhd->bqhgd', p.astype(v.dtype), v)

    out = lax.map(attend, qb)
    return jnp.moveaxis(out, 0, 1).reshape(bsz, lq, kh * gq * v.shape[-1])


def banded_attention_with_context(q, k, v, ck, cv, sink):
    bsz, length, kh, gq, dh = q.shape
    nb = length // BAND_BLOCK
    scale = dh ** -0.5
    pad = ((0, 0), (BAND_BLOCK, BAND_BLOCK), (0, 0), (0, 0))
    kp = jnp.pad(k, pad).reshape(bsz, nb + 2, BAND_BLOCK, kh, dh)
    vp = jnp.pad(v, pad).reshape(bsz, nb + 2, BAND_BLOCK, kh, v.shape[-1])
    kb = jnp.concatenate([kp[:, :-2], kp[:, 1:-1], kp[:, 2:]], axis=2)
    vb = jnp.concatenate([vp[:, :-2], vp[:, 1:-1], vp[:, 2:]], axis=2)
    qb = q.reshape(bsz, nb, BAND_BLOCK, kh, gq, dh)
    s_band = jnp.einsum('bnqhgd,bnkhd->bnhgqk', qb, kb, preferred_element_type=F32) * scale
    qpos = jnp.arange(nb)[:, None] * BAND_BLOCK + jnp.arange(BAND_BLOCK)[None, :]
    kpos = (jnp.arange(nb)[:, None] - 1) * BAND_BLOCK + jnp.arange(3 * BAND_BLOCK)[None, :]
    kpos_b = kpos[:, None, :]
    valid = (jnp.abs(qpos[:, :, None] - kpos_b) <= WINDOW) & (kpos_b >= 0) & (kpos_b < length)
    s_band = jnp.where(valid[None, :, None, None], s_band, -jnp.inf)
    s_ctx = jnp.einsum('bnqhgd,bkhd->bnhgqk', qb, ck, preferred_element_type=F32) * scale
    snk = jnp.broadcast_to(sink.astype(F32)[None, None, :, :, None, None], s_ctx.shape[:-1] + (1,))
    p = jax.nn.softmax(jnp.concatenate([s_ctx, s_band, snk], axis=-1), axis=-1).astype(v.dtype)
    lc = ck.shape[1]
    out = (jnp.einsum('bnhgqk,bkhd->bnqhgd', p[..., :lc], cv)
           + jnp.einsum('bnhgqk,bnkhd->bnqhgd', p[..., lc:lc + 3 * BAND_BLOCK], vb))
    return out.reshape(bsz, length, kh * gq * v.shape[-1])


def gqa_project(u, w_qkv, q_g, k_g):
    bsz, length, _ = u.shape
    qkv = u @ w_qkv
    nq = GQA_HEADS * GQA_HEAD_DIM
    nk = GQA_KV_HEADS * GQA_HEAD_DIM
    q = rms_norm(qkv[..., :nq].reshape(bsz, length, GQA_HEADS, GQA_HEAD_DIM), q_g)
    k = rms_norm(qkv[..., nq:nq + nk].reshape(bsz, length, GQA_KV_HEADS, GQA_HEAD_DIM), k_g)
    v = qkv[..., nq + nk:].reshape(bsz, length, GQA_KV_HEADS, GQA_HEAD_DIM)
    return q, k, v


def mla_query(u, w_dq, q_lora_g, w_uq, q_g):
    bsz, length, _ = u.shape
    q = (rms_norm(u @ w_dq, q_lora_g) @ w_uq).reshape(bsz, length, MLA_HEADS, MLA_QK)
    return rms_norm(q, q_g)


def mla_compress(u, w_dkv, kv_g):
    ckr = u @ w_dkv
    return rms_norm(ckr[..., :MLA_KV_LORA], kv_g), ckr[..., MLA_KV_LORA:]


def mla_expand(ckv, krope, w_ukv, k_g):
    bsz, length, _ = ckv.shape
    kv = (ckv @ w_ukv).reshape(bsz, length, MLA_HEADS, MLA_NOPE + MLA_V)
    kr = jnp.broadcast_to(krope[:, :, None, :].astype(kv.dtype), (bsz, length, MLA_HEADS, MLA_ROPE))
    k = rms_norm(jnp.concatenate([kv[..., :MLA_NOPE], kr], axis=-1), k_g)
    return k, kv[..., MLA_NOPE:]


def rope_tail(x, row, col):
    return jnp.concatenate([x[..., :MLA_NOPE], rope_2d(x[..., MLA_NOPE:], row, col)], axis=-1)


def hier_moe(x, w_grp, b_grp, w_exp, b_exp, w_gate, w_up, w_down):
    bsz, length, d = x.shape
    t = x.reshape(-1, d)
    n_tok = t.shape[0]
    g_logits = (t @ w_grp + b_grp).astype(F32)
    g_prob = jax.nn.softmax(g_logits, axis=-1)
    g_top = jnp.argmax(g_logits, axis=-1)
    p_g = jnp.take_along_axis(g_prob, g_top[:, None], axis=-1)
    e_logits = (t @ w_exp + b_exp).astype(F32).reshape(n_tok, MOE_GROUPS, MOE_PER_GROUP)
    e_sel = jnp.take_along_axis(e_logits, g_top[:, None, None], axis=1)[:, 0]
    top_v, top_i = lax.top_k(e_sel, MOE_TOP_K)
    top_w = jax.nn.softmax(top_v, axis=-1) * p_g
    expert_id = g_top[:, None] * MOE_PER_GROUP + top_i
    gates = jnp.sum(jax.nn.one_hot(expert_id, MOE_EXPERTS, dtype=F32) * top_w[..., None], axis=1)
    h = jax.nn.silu(jnp.einsum('td,edf->tef', t, w_gate)) * jnp.einsum('td,edf->tef', t, w_up)
    y = jnp.einsum('tef,efd->td', h * gates[..., None].astype(h.dtype), w_down)
    return y.reshape(bsz, length, d)


def setup_inputs(seed: int = 0) -> dict:
    key = jax.random.key(seed)
    ks = iter(jax.random.split(key, 96))

    def nrm(shape, scale=1.0):
        return jax.random.normal(next(ks), shape, F32) * scale

    def gain(shape):
        return 1.0 + nrm(shape, 0.05)

    D = D_MODEL
    G, P, CG = S5_GROUPS, S5_STATE, S5_GROUP
    H, N = RWKV_HEADS, RWKV_HEAD
    gqa_cols = (GQA_HEADS + 2 * GQA_KV_HEADS) * GQA_HEAD_DIM
    return {
        'x_prompt': nrm((BATCH, SEQ, D)),
        'x_sample': nrm((DEC_BATCH, DEC_SEQ, D)),
        'state_s5_re': nrm((DEC_BATCH, 2, G, P), 0.1),
        'state_s5_im': nrm((DEC_BATCH, 2, G, P), 0.1),
        'state_rwkv': nrm((DEC_BATCH, 2, H, N, N), 0.5),
        'cache_gqa_k': nrm((DEC_BATCH, PAST_LEN, GQA_KV_HEADS, GQA_HEAD_DIM)),
        'cache_gqa_v': nrm((DEC_BATCH, PAST_LEN, GQA_KV_HEADS, GQA_HEAD_DIM)),
        'cache_mla_ckv': nrm((DEC_BATCH, PAST_LEN, MLA_KV_LORA)),
        'cache_mla_krope': nrm((DEC_BATCH, PAST_LEN, MLA_ROPE)),
        'c': nrm((DEC_BATCH, D)),
        'c_ctx': nrm((D,)),
        'ada_w': nrm((DEPTH, D, 6 * D), 0.01),
        'ada_b': nrm((DEPTH, 6 * D), 0.01),
        'norm1_g': gain((DEPTH, D)),
        'norm2_g': gain((DEPTH, D)),
        's5_lambda_re': -0.5 * jnp.exp(nrm((2, G, P), 0.02)),
        's5_lambda_im': jnp.pi * jnp.arange(P, dtype=F32) + nrm((2, G, P), 0.01),
        's5_log_step': jax.random.uniform(next(ks), (2, G, P), F32, math.log(S5_DT_MIN), math.log(S5_DT_MAX)),
        's5_b_re': nrm((2, G, P, CG), (2 * CG) ** -0.5),
        's5_b_im': nrm((2, G, P, CG), (2 * CG) ** -0.5),
        's5_c_re': nrm((2, G, CG, P), P ** -0.5),
        's5_c_im': nrm((2, G, CG, P), P ** -0.5),
        's5_d': nrm((D,)),
        's5_glu_w1': nrm((D, D), D ** -0.5),
        's5_glu_w2': nrm((D, D), D ** -0.5),
        'rwkv_mu': jax.random.uniform(next(ks), (6, D), F32),
        'rwkv_w_r': nrm((D, D), D ** -0.5),
        'rwkv_w_k': nrm((D, D), D ** -0.5),
        'rwkv_w_v': nrm((D, D), D ** -0.5),
        'rwkv_w_o': nrm((D, D), D ** -0.5),
        'rwkv_w0': nrm((2, D)),
        'rwkv_w_l1': nrm((2, D, RWKV_DECAY_LORA), D ** -0.5),
        'rwkv_w_l2': nrm((2, RWKV_DECAY_LORA, D), 0.5 * RWKV_DECAY_LORA ** -0.5),
        'rwkv_a0': nrm((2, D), 0.5),
        'rwkv_a_l1': nrm((2, D, RWKV_ICLR_LORA), D ** -0.5),
        'rwkv_a_l2': nrm((2, RWKV_ICLR_LORA, D), 0.5 * RWKV_ICLR_LORA ** -0.5),
        'rwkv_g_l1': nrm((D, RWKV_GATE_LORA), D ** -0.5),
        'rwkv_g_l2': nrm((RWKV_GATE_LORA, D), RWKV_GATE_LORA ** -0.5),
        'rwkv_k_k': 0.85 + nrm((D,), 0.05),
        'rwkv_k_a': 1.0 + nrm((D,), 0.05),
        'rwkv_r_k': nrm((H, N), 0.1),
        'rwkv_ln_g': gain((D,)),
        'rwkv_ln_b': nrm((D,), 0.01),
        'gqa_w_qkv': nrm((D, gqa_cols), D ** -0.5),
        'gqa_q_norm': gain((GQA_HEAD_DIM,)),
        'gqa_k_norm': gain((GQA_HEAD_DIM,)),
        'gqa_sink': nrm((GQA_HEADS,), 0.5),
        'gqa_w_o': nrm((GQA_HEADS * GQA_HEAD_DIM, D), (GQA_HEADS * GQA_HEAD_DIM) ** -0.5),
        'mla_w_dq': nrm((D, MLA_Q_LORA), D ** -0.5),
        'mla_q_lora_norm': gain((MLA_Q_LORA,)),
        'mla_w_uq': nrm((MLA_Q_LORA, MLA_HEADS * MLA_QK), MLA_Q_LORA ** -0.5),
        'mla_w_dkv': nrm((D, MLA_KV_LORA + MLA_ROPE), D ** -0.5),
        'mla_kv_norm': gain((MLA_KV_LORA,)),
        'mla_w_ukv': nrm((MLA_KV_LORA, MLA_HEADS * (MLA_NOPE + MLA_V)), MLA_KV_LORA ** -0.5),
        'mla_q_norm': gain((MLA_QK,)),
        'mla_k_norm': gain((MLA_QK,)),
        'mla_w_o': nrm((MLA_HEADS * MLA_V, D), (MLA_HEADS * MLA_V) ** -0.5),
        'moe_w_group': nrm((DEPTH, D, MOE_GROUPS), D ** -0.5),
        'moe_b_group': nrm((DEPTH, MOE_GROUPS), 0.01),
        'moe_w_expert': nrm((DEPTH, D, MOE_EXPERTS), D ** -0.5),
        'moe_b_expert': nrm((DEPTH, MOE_EXPERTS), 0.01),
        'moe_w_gate': nrm((DEPTH, MOE_EXPERTS, D, MOE_HIDDEN), D ** -0.5),
        'moe_w_up': nrm((DEPTH, MOE_EXPERTS, D, MOE_HIDDEN), D ** -0.5),
        'moe_w_down': nrm((DEPTH, MOE_EXPERTS, MOE_HIDDEN, D), MOE_HIDDEN ** -0.5),
    }


def reference(x_prompt, x_sample, state_s5_re, state_s5_im, state_rwkv, cache_gqa_k, cache_gqa_v,
              cache_mla_ckv, cache_mla_krope, c, c_ctx, ada_w, ada_b, norm1_g, norm2_g,
              s5_lambda_re, s5_lambda_im, s5_log_step, s5_b_re, s5_b_im, s5_c_re, s5_c_im, s5_d,
              s5_glu_w1, s5_glu_w2,
              rwkv_mu, rwkv_w_r, rwkv_w_k, rwkv_w_v, rwkv_w_o, rwkv_w0, rwkv_w_l1, rwkv_w_l2,
              rwkv_a0, rwkv_a_l1, rwkv_a_l2, rwkv_g_l1, rwkv_g_l2, rwkv_k_k, rwkv_k_a, rwkv_r_k,
              rwkv_ln_g, rwkv_ln_b,
              gqa_w_qkv, gqa_q_norm, gqa_k_norm, gqa_sink, gqa_w_o,
              mla_w_dq, mla_q_lora_norm, mla_w_uq, mla_w_dkv, mla_kv_norm, mla_w_ukv, mla_q_norm,
              mla_k_norm, mla_w_o,
              moe_w_group, moe_b_group, moe_w_expert, moe_b_expert, moe_w_gate, moe_w_up, moe_w_down):
    bp = x_prompt.shape[0]
    lp = x_prompt.shape[1]
    ls = x_sample.shape[1]
    rows = ls // GRID_W
    row = jnp.repeat(jnp.arange(rows), GRID_W)
    col = jnp.tile(jnp.arange(GRID_W), rows)
    s5w = (s5_lambda_re, s5_lambda_im, s5_log_step, s5_b_re, s5_b_im, s5_c_re, s5_c_im, s5_d,
           s5_glu_w1, s5_glu_w2)
    rww = (rwkv_mu, rwkv_w_r, rwkv_w_k, rwkv_w_v, rwkv_w_o, rwkv_w0, rwkv_w_l1, rwkv_w_l2, rwkv_a0,
           rwkv_a_l1, rwkv_a_l2, rwkv_g_l1, rwkv_g_l2, rwkv_k_k, rwkv_k_a, rwkv_r_k, rwkv_ln_g, rwkv_ln_b)
    sink = gqa_sink.reshape(GQA_KV_HEADS, GQA_GROUP)
    hp, hs = x_prompt, x_sample
    for layer in range(DEPTH):
        kind = layer % N_MIXERS
        sh1p, sc1p, g1p, sh2p, sc2p, g2p = ada_modulation(c_ctx[None, :], ada_w[layer], ada_b[layer])
        sh1s, sc1s, g1s, sh2s, sc2s, g2s = ada_modulation(c, ada_w[layer], ada_b[layer])
        up = modulate(hp, norm1_g[layer], sh1p, sc1p)
        us = modulate(hs, norm1_g[layer], sh1s, sc1s)
        if kind == 0:
            zeros = jnp.zeros((bp, 2, S5_GROUPS, S5_STATE), F32)
            mp, new_s5_re, new_s5_im = s5_mixer(up, zeros, zeros, *s5w)
            ms, _, _ = s5_mixer(us, state_s5_re, state_s5_im, *s5w)
        elif kind == 1:
            zeros = jnp.zeros((bp, 2, RWKV_HEADS, RWKV_HEAD, RWKV_HEAD), F32)
            mp, new_rwkv = rwkv_mixer(up, zeros, *rww)
            ms, _ = rwkv_mixer(us, state_rwkv, *rww)
        elif kind == 2:
            qp, kp, vp = gqa_project(up, gqa_w_qkv, gqa_q_norm, gqa_k_norm)
            mp = dense_attention(qp.reshape(bp, lp, GQA_KV_HEADS, GQA_GROUP, GQA_HEAD_DIM), kp, vp, sink) @ gqa_w_o
            new_gqa_k, new_gqa_v = kp, vp
            qs, ks_, vs_ = gqa_project(us, gqa_w_qkv, gqa_q_norm, gqa_k_norm)
            qs = rope_2d(qs, row, col)
            ks_ = rope_2d(ks_, row, col)
            qs = qs.reshape(qs.shape[0], ls, GQA_KV_HEADS, GQA_GROUP, GQA_HEAD_DIM)
            ms = banded_attention_with_context(qs, ks_, vs_, cache_gqa_k, cache_gqa_v, sink) @ gqa_w_o
        else:
            qp = mla_query(up, mla_w_dq, mla_q_lora_norm, mla_w_uq, mla_q_norm)
            ckv_p, kr_p = mla_compress(up, mla_w_dkv, mla_kv_norm)
            kp, vp = mla_expand(ckv_p, kr_p, mla_w_ukv, mla_k_norm)
            mp = dense_attention(qp[:, :, :, None, :], kp, vp, None) @ mla_w_o
            new_mla_ckv, new_mla_krope = ckv_p, kr_p
            qs = rope_tail(mla_query(us, mla_w_dq, mla_q_lora_norm, mla_w_uq, mla_q_norm), row, col)
            ckv_s, kr_s = mla_compress(us, mla_w_dkv, mla_kv_norm)
            ks_, vs_ = mla_expand(ckv_s, kr_s, mla_w_ukv, mla_k_norm)
            ks_ = rope_tail(ks_, row, col)
            kc, vc = mla_expand(cache_mla_ckv, cache_mla_krope, mla_w_ukv, mla_k_norm)
            k_all = jnp.concatenate([kc.astype(ks_.dtype), ks_], axis=1)
            v_all = jnp.concatenate([vc.astype(vs_.dtype), vs_], axis=1)
            ms = dense_attention(qs[:, :, :, None, :], k_all, v_all, None) @ mla_w_o
        hp = hp + g1p * mp
        hs = hs + g1s * ms
        vmp = modulate(hp, norm2_g[layer], sh2p, sc2p)
        vms = modulate(hs, norm2_g[layer], sh2s, sc2s)
        moe_w = (moe_w_group[layer], moe_b_group[layer], moe_w_expert[layer], moe_b_expert[layer],
                 moe_w_gate[layer], moe_w_up[layer], moe_w_down[layer])
        hp = hp + g2p * hier_moe(vmp, *moe_w)
        hs = hs + g2s * hier_moe(vms, *moe_w)
    return (hp, hs, new_s5_re, new_s5_im, new_rwkv, new_gqa_k, new_gqa_v, new_mla_ckv, new_mla_krope)
```

```python
import functools
import math

import jax
import jax.numpy as jnp
from jax import lax
from jax.experimental import pallas as pl
from jax.experimental.pallas import tpu as pltpu

F32 = jnp.float32
BF16 = jnp.bfloat16

D_MODEL = 1024
NORM_EPS = 1e-6
ROPE_BASE = 10000.0
GRID_W = 64
S5_GROUP = 16
S5_GROUPS = D_MODEL // S5_GROUP
S5_STATE = 64
S5_N = S5_GROUPS * S5_STATE
RWKV_HEAD = 64
RWKV_HEADS = D_MODEL // RWKV_HEAD
RWKV_DECAY_SCALE = math.exp(-0.5)
RWKV_GN_EPS = 64e-5
GQA_HEADS = 16
GQA_KV_HEADS = 4
GQA_HEAD_DIM = 64
WINDOW = 128
MLA_HEADS = 16
MLA_Q_LORA = 384
MLA_KV_LORA = 256
MLA_NOPE = 64
MLA_ROPE = 32
MLA_QK = MLA_NOPE + MLA_ROPE
MLA_V = 64
MLA_PAD = 128
MOE_GROUPS = 4
MOE_PER_GROUP = 4
MOE_EXPERTS = 16
MOE_HIDDEN = 512

LANES = 128
SUBLANES = 8
MXU_DIM = 256
VMEM_LIMIT = 56 * 1024 * 1024
MOD_ROWS = 8


def _cparams(*sem):
    return pltpu.CompilerParams(dimension_semantics=sem, vmem_limit_bytes=VMEM_LIMIT)


def _dot(a, b):
    return jnp.dot(a, b, preferred_element_type=F32)


def _dot_nt(a, b):
    return lax.dot_general(a, b, (((1,), (1,)), ((), ())), preferred_element_type=F32)


def _split3(x):
    hi = x.astype(BF16)
    r1 = x - hi.astype(F32)
    mid = r1.astype(BF16)
    lo = (r1 - mid.astype(F32)).astype(BF16)
    return hi, mid, lo


def _dot_x3(x, w_hi, w_lo):
    hi, mid, _ = _split3(x)
    return _dot(hi, w_hi) + (_dot(mid, w_hi) + _dot(hi, w_lo))


def _group_sum(x, ones_bd):
    n = x.shape[-1]
    outs = []
    for c in range(n // MXU_DIM):
        hi, mid, lo = _split3(x[:, c * MXU_DIM:(c + 1) * MXU_DIM])
        outs.append(_dot(hi, ones_bd) + (_dot(mid, ones_bd) + _dot(lo, ones_bd)))
    return outs[0] if len(outs) == 1 else jnp.concatenate(outs, axis=-1)


def _modulate(x, g, shift, scale):
    ms = jnp.mean(x * x, axis=-1, keepdims=True)
    return x * lax.rsqrt(ms + NORM_EPS) * g * (1.0 + scale) + shift


def _sigmoid(x):
    return 1.0 / (1.0 + jnp.exp(-x))


def _silu(x):
    return x * _sigmoid(x)


def _ones_blockdiag(group):
    i = jnp.arange(MXU_DIM) // group
    return (i[:, None] == i[None, :]).astype(BF16)


def _hi_lo(w):
    hi = w.astype(BF16)
    return hi, (w - hi.astype(F32)).astype(BF16)


class _Seg:
    def __init__(self, nseq, length, mod0, shared_mod):
        self.nseq, self.length, self.mod0, self.shared_mod = nseq, length, mod0, shared_mod
        self.n = nseq * length

    def mod_row(self, tm):
        if self.shared_mod:
            return lambda i: self.mod0
        per = self.length // tm
        return lambda i: self.mod0 + i // per

    def mod_rows(self, mod, k):
        if self.shared_mod:
            return jnp.broadcast_to(mod[self.mod0, k][None], (self.nseq, mod.shape[-1]))
        return mod[self.mod0:self.mod0 + self.nseq, k]


def _ada_kernel(cond_ref, w_ref, b_ref, o_ref):
    s = _silu(cond_ref[...])
    w = w_ref[0]
    w_hi = w.astype(BF16)
    w_lo = (w - w_hi.astype(F32)).astype(BF16)
    o_ref[0] = _dot_x3(s, w_hi, w_lo) + b_ref[0]


def _ada_all(cond, ada_w, ada_b):
    depth, d, n6 = ada_w.shape
    tn = 1536
    return pl.pallas_call(
        _ada_kernel,
        grid=(depth, n6 // tn),
        in_specs=[
            pl.BlockSpec((MOD_ROWS, d), lambda l, j: (0, 0)),
            pl.BlockSpec((1, d, tn), lambda l, j: (l, 0, j)),
            pl.BlockSpec((1, 1, tn), lambda l, j: (l, 0, j)),
        ],
        out_specs=pl.BlockSpec((1, MOD_ROWS, tn), lambda l, j: (l, 0, j)),
        out_shape=jax.ShapeDtypeStruct((depth, MOD_ROWS, n6), F32),
        compiler_params=_cparams("parallel", "parallel"),
        name="ada_mod",
    )(cond, ada_w, ada_b.reshape(depth, 1, n6))


ROUTE_W = LANES


def _route(logits):
    lane = lax.broadcasted_iota(jnp.int32, logits.shape, 1)
    neg = jnp.float32(-jnp.inf)
    big = jnp.int32(ROUTE_W)
    gmask = lane < MOE_GROUPS
    gl = jnp.where(gmask, logits, neg)
    gmax = jnp.max(gl, axis=-1, keepdims=True)
    g_top = jnp.min(jnp.where(gl == gmax, lane, big), axis=-1, keepdims=True)
    p_g = 1.0 / jnp.sum(jnp.where(gmask, jnp.exp(gl - gmax), 0.0), axis=-1, keepdims=True)
    lo = MOE_GROUPS + MOE_PER_GROUP * g_top
    emask = (lane >= lo) & (lane < lo + MOE_PER_GROUP)
    el = jnp.where(emask, logits, neg)
    m1 = jnp.max(el, axis=-1, keepdims=True)
    i1 = jnp.min(jnp.where(el == m1, lane, big), axis=-1, keepdims=True)
    el2 = jnp.where(lane == i1, neg, el)
    m2 = jnp.max(el2, axis=-1, keepdims=True)
    i2 = jnp.min(jnp.where(el2 == m2, lane, big), axis=-1, keepdims=True)
    e2 = jnp.exp(m2 - m1)
    w1 = p_g / (1.0 + e2)
    w2 = p_g * e2 / (1.0 + e2)
    return jnp.where(lane == i1, w1, 0.0) + jnp.where(lane == i2, w2, 0.0)


def _moe_kernel(h_ref, mod_ref, g_ref, wr_hi_ref, wr_lo_ref, br_ref, wg_ref, wu_ref, wd_ref, o_ref,
                v_scr, gate_scr, acc_scr):
    e = pl.program_id(1)

    @pl.when(e == 0)
    def _():
        mod = mod_ref[0]
        v = _modulate(h_ref[...], g_ref[...], mod[3:4], mod[4:5])
        logits = _dot_x3(v, wr_hi_ref[...], wr_lo_ref[...]) + br_ref[...]
        gate_scr[...] = _route(logits)
        v_scr[...] = v.astype(BF16)
        acc_scr[...] = jnp.zeros_like(acc_scr)

    v = v_scr[...]
    lane = lax.broadcasted_iota(jnp.int32, gate_scr.shape, 1)
    gate = jnp.sum(jnp.where(lane == e + MOE_GROUPS, gate_scr[...], 0.0), axis=-1, keepdims=True)
    hg = _dot(v, wg_ref[0])
    hu = _dot(v, wu_ref[0])
    hh = (_silu(hg) * hu * gate).astype(BF16)
    acc_scr[...] += _dot(hh, wd_ref[0])

    @pl.when(e == pl.num_programs(1) - 1)
    def _():
        o_ref[...] = h_ref[...] + mod_ref[0][5:6] * acc_scr[...]


def _moe_layer(h, seg, mod, norm_g, wr_hi, wr_lo, br, wg, wu, wd, tm=1024):
    n, d = h.shape
    nexp, _, f = wg.shape
    tm = min(tm, n if seg.shared_mod else seg.length)
    rid = seg.mod_row(tm)
    return pl.pallas_call(
        _moe_kernel,
        grid=(n // tm, nexp),
        in_specs=[
            pl.BlockSpec((tm, d), lambda i, e: (i, 0)),
            pl.BlockSpec((1, 6, d), lambda i, e: (rid(i), 0, 0)),
            pl.BlockSpec((1, d), lambda i, e: (0, 0)),
            pl.BlockSpec((d, ROUTE_W), lambda i, e: (0, 0)),
            pl.BlockSpec((d, ROUTE_W), lambda i, e: (0, 0)),
            pl.BlockSpec((1, ROUTE_W), lambda i, e: (0, 0)),
            pl.BlockSpec((1, d, f), lambda i, e: (e, 0, 0)),
            pl.BlockSpec((1, d, f), lambda i, e: (e, 0, 0)),
            pl.BlockSpec((1, f, d), lambda i, e: (e, 0, 0)),
        ],
        out_specs=pl.BlockSpec((tm, d), lambda i, e: (i, 0)),
        out_shape=jax.ShapeDtypeStruct((n, d), F32),
        scratch_shapes=[
            pltpu.VMEM((tm, d), BF16),
            pltpu.VMEM((tm, ROUTE_W), F32),
            pltpu.VMEM((tm, d), F32),
        ],
        compiler_params=_cparams("parallel", "arbitrary"),
        name="moe",
    )(h, mod, norm_g, wr_hi, wr_lo, br, wg, wu, wd)


S5_TL = 32
S5_CG = MXU_DIM // S5_GROUP
S5_NCH = D_MODEL // MXU_DIM
S5_CN = S5_CG * S5_STATE
S5_SCAN_W = 512


def _s5_scan_kernel(h_ref, sh_ref, sc_ref, g_ref, s0r_ref, s0i_ref, ar_ref, ai_ref, wb_ref, wcr_ref, wci_ref,
                    y_ref, fr_ref, fi_ref, bur, bui, str_, sti):
    d = pl.program_id(1)
    c = pl.program_id(2)
    tl, nb, dm = h_ref.shape

    @pl.when(c == 0)
    def _():
        str_[...] = s0r_ref[0]
        sti[...] = s0i_ref[0]

    u = _modulate(h_ref[...], g_ref[...], sh_ref[...], sc_ref[...])
    u2 = u.reshape(tl * nb, dm).astype(BF16)
    for cc in range(S5_NCH):
        r = _dot(u2[:, cc * MXU_DIM:(cc + 1) * MXU_DIM], wb_ref[0, cc])
        bur[:, cc * S5_CN:(cc + 1) * S5_CN] = r[:, :S5_CN]
        bui[:, cc * S5_CN:(cc + 1) * S5_CN] = r[:, S5_CN:]

    for lc in range(S5_N // S5_SCAN_W):
        cols = slice(lc * S5_SCAN_W, (lc + 1) * S5_SCAN_W)
        a_r = jnp.broadcast_to(ar_ref[0, :, cols], (nb, S5_SCAN_W))
        a_i = jnp.broadcast_to(ai_ref[0, :, cols], (nb, S5_SCAN_W))

        def body(i, carry, cols=cols, a_r=a_r, a_i=a_i):
            s_r, s_i = carry
            tt = jnp.where(d == 0, i, tl - 1 - i)
            rows = pl.ds(pl.multiple_of(tt * nb, nb), nb)
            n_r = a_r * s_r - a_i * s_i + bur[rows, cols]
            n_i = a_r * s_i + a_i * s_r + bui[rows, cols]
            bur[rows, cols] = n_r
            bui[rows, cols] = n_i
            return n_r, n_i

        s_r, s_i = lax.fori_loop(0, tl, body, (str_[:, cols], sti[:, cols]))
        str_[:, cols] = s_r
        sti[:, cols] = s_i

    for cc in range(S5_NCH):
        cols = slice(cc * S5_CN, (cc + 1) * S5_CN)
        y = _dot(bur[:, cols].astype(BF16), wcr_ref[0, cc]) + _dot(bui[:, cols].astype(BF16), wci_ref[0, cc])
        y_ref[0, :, :, cc * MXU_DIM:(cc + 1) * MXU_DIM] = y.reshape(tl, nb, MXU_DIM)

    @pl.when(c == pl.num_programs(2) - 1)
    def _():
        fr_ref[0] = str_[...]
        fi_ref[0] = sti[...]


def _s5_scan(h_t, shift, scale, norm_g, s0r, s0i, a_r, a_i, wb, wcr, wci):
    length, nbt, dm = h_t.shape
    nb = SUBLANES
    tl = S5_TL
    nch = length // tl

    def tch(d, c):
        return jnp.where(d == 0, c, nch - 1 - c)

    st_spec = pl.BlockSpec((1, nb, S5_N), lambda b, d, c: (d, b, 0))
    a_spec = pl.BlockSpec((1, 1, S5_N), lambda b, d, c: (d, 0, 0))
    return pl.pallas_call(
        _s5_scan_kernel,
        grid=(nbt // nb, 2, nch),
        in_specs=[
            pl.BlockSpec((tl, nb, dm), lambda b, d, c: (tch(d, c), b, 0)),
            pl.BlockSpec((nb, dm), lambda b, d, c: (b, 0)),
            pl.BlockSpec((nb, dm), lambda b, d, c: (b, 0)),
            pl.BlockSpec((1, dm), lambda b, d, c: (0, 0)),
            st_spec, st_spec, a_spec, a_spec,
            pl.BlockSpec((1, S5_NCH, MXU_DIM, 2 * S5_CN), lambda b, d, c: (d, 0, 0, 0)),
            pl.BlockSpec((1, S5_NCH, S5_CN, MXU_DIM), lambda b, d, c: (d, 0, 0, 0)),
            pl.BlockSpec((1, S5_NCH, S5_CN, MXU_DIM), lambda b, d, c: (d, 0, 0, 0)),
        ],
        out_specs=[
            pl.BlockSpec((1, tl, nb, dm), lambda b, d, c: (d, tch(d, c), b, 0)),
            st_spec, st_spec,
        ],
        out_shape=[
            jax.ShapeDtypeStruct((2, length, nbt, dm), F32),
            jax.ShapeDtypeStruct((2, nbt, S5_N), F32),
            jax.ShapeDtypeStruct((2, nbt, S5_N), F32),
        ],
        scratch_shapes=[
            pltpu.VMEM((tl * nb, S5_N), F32), pltpu.VMEM((tl * nb, S5_N), F32),
            pltpu.VMEM((nb, S5_N), F32), pltpu.VMEM((nb, S5_N), F32),
        ],
        compiler_params=_cparams("parallel", "arbitrary", "arbitrary"),
        name="s5_scan",
    )(h_t, shift, scale, norm_g, s0r, s0i, a_r, a_i, wb, wcr, wci)


def _gelu_tanh(x):
    return 0.5 * x * (1.0 + jnp.tanh(0.7978845608028654 * (x + 0.044715 * (x * x * x))))


def _s5_glu_kernel(h_ref, mod_ref, g_ref, dsk_ref, yf_ref, yb_ref, w1_ref, w2_ref, o_ref):
    mod = mod_ref[0]
    h = h_ref[...]
    u = _modulate(h, g_ref[...], mod[0:1], mod[1:2])
    y = u * dsk_ref[...] + (yf_ref[0] + yb_ref[0])
    z = _gelu_tanh(y).astype(BF16)
    out = _dot(z, w1_ref[...]) * _sigmoid(_dot(z, w2_ref[...]))
    o_ref[...] = h + mod[2:3] * out


def _s5_glu(h, seg, mod, norm_g, d_skip, y_t, w1, w2, tm=256):
    n, dm = h.shape
    _, length, nbt, _ = y_t.shape
    y2 = y_t.reshape(2, length, nbt * dm)
    tm = min(tm, length)
    per = length // tm
    rid = seg.mod_row(tm)
    return pl.pallas_call(
        _s5_glu_kernel,
        grid=(n // tm,),
        in_specs=[
            pl.BlockSpec((tm, dm), lambda i: (i, 0)),
            pl.BlockSpec((1, 6, dm), lambda i: (rid(i), 0, 0)),
            pl.BlockSpec((1, dm), lambda i: (0, 0)),
            pl.BlockSpec((1, dm), lambda i: (0, 0)),
            pl.BlockSpec((1, tm, dm), lambda i: (0, i % per, i // per)),
            pl.BlockSpec((1, tm, dm), lambda i: (1, i % per, i // per)),
            pl.BlockSpec((dm, dm), lambda i: (0, 0)),
            pl.BlockSpec((dm, dm), lambda i: (0, 0)),
        ],
        out_specs=pl.BlockSpec((tm, dm), lambda i: (i, 0)),
        out_shape=jax.ShapeDtypeStruct((n, dm), F32),
        compiler_params=_cparams("parallel"),
        name="s5_glu",
    )(h, mod, norm_g, d_skip, y2, y2, w1, w2)


def _s5_weights(lam_re, lam_im, log_step, b_re, b_im, c_re, c_im):
    dt = jnp.exp(log_step)
    mag = jnp.exp(lam_re * dt)
    ab_re = mag * jnp.cos(lam_im * dt)
    ab_im = mag * jnp.sin(lam_im * dt)
    nr = ab_re - 1.0
    den = lam_re * lam_re + lam_im * lam_im
    f_re = (nr * lam_re + ab_im * lam_im) / den
    f_im = (ab_im * lam_re - nr * lam_im) / den
    fb_re = f_re[..., None] * b_re - f_im[..., None] * b_im
    fb_im = f_re[..., None] * b_im + f_im[..., None] * b_re
    eye = jnp.eye(S5_CG, dtype=F32)

    def bd_in(w):
        w = w.reshape(2, S5_NCH, S5_CG, S5_STATE, S5_GROUP)
        m = jnp.einsum('dnjpc,jk->dnjckp', w, eye)
        return m.reshape(2, S5_NCH, MXU_DIM, S5_CN)

    def bd_out(w):
        w = w.reshape(2, S5_NCH, S5_CG, S5_GROUP, S5_STATE)
        m = jnp.einsum('dnjcp,jk->dnjpkc', w, eye)
        return m.reshape(2, S5_NCH, S5_CN, MXU_DIM)

    wb = jnp.concatenate([bd_in(fb_re), bd_in(fb_im)], axis=-1).astype(BF16)
    return (ab_re.reshape(2, 1, S5_N), ab_im.reshape(2, 1, S5_N), wb,
            bd_out(c_re).astype(BF16), bd_out(-c_im).astype(BF16))


def _to_time_major(x, seg, nbt):
    xt = jnp.transpose(x.reshape(seg.nseq, seg.length, -1), (1, 0, 2))
    if nbt > seg.nseq:
        xt = jnp.pad(xt, ((0, 0), (0, nbt - seg.nseq), (0, 0)))
    return xt


def _pad_rows(x, nbt):
    return x if x.shape[0] == nbt else jnp.pad(x, ((0, nbt - x.shape[0]),) + ((0, 0),) * (x.ndim - 1))


def _s5_layer(h, seg, mod, norm_g, s0_re, s0_im, sw, d_skip, w1, w2):
    a_r, a_i, wb, wcr, wci = sw
    nbt = -(-seg.nseq // SUBLANES) * SUBLANES
    h_t = _to_time_major(h, seg, nbt)
    shift = _pad_rows(seg.mod_rows(mod, 0), nbt)
    scale = _pad_rows(seg.mod_rows(mod, 1), nbt)
    pad = ((0, 0), (0, nbt - seg.nseq), (0, 0))
    s0r = jnp.pad(jnp.transpose(s0_re, (1, 0, 2)), pad)
    s0i = jnp.pad(jnp.transpose(s0_im, (1, 0, 2)), pad)
    y_t, f_r, f_i = _s5_scan(h_t, shift, scale, norm_g, s0r, s0i, a_r, a_i, wb, wcr, wci)
    h_new = _s5_glu(h, seg, mod, norm_g, d_skip, y_t, w1, w2)
    return h_new, jnp.transpose(f_r[:, :seg.nseq], (1, 0, 2)), jnp.transpose(f_i[:, :seg.nseq], (1, 0, 2))


def _rwkv_proj_kernel(h_ref, hp_ref, hn_ref, mod_ref, g_ref, mu_ref, wr_ref, wk_ref, wv_ref,
                      wl1_ref, wl2_ref, al1_ref, al2_ref, gl1_ref, gl2_ref, vec_ref, w0_ref, a0_ref, ones_ref,
                      r_ref, v_ref, kk_ref, w_ref, ka_ref, kd_ref, g_out_ref, bonus_ref, *, per):
    i = pl.program_id(0)
    mod = mod_ref[0]
    gn = g_ref[...]
    tm = h_ref.shape[0]
    u = _modulate(h_ref[...], gn, mod[0:1], mod[1:2])
    first = (i % per) == 0
    last = (i % per) == per - 1
    up = _modulate(hp_ref[SUBLANES - 1:SUBLANES, :], gn, mod[0:1], mod[1:2]) * jnp.where(first, 0.0, 1.0)
    un = _modulate(hn_ref[0:1, :], gn, mod[0:1], mod[1:2]) * jnp.where(last, 0.0, 1.0)
    row = lax.broadcasted_iota(jnp.int32, u.shape, 0)
    prev = jnp.where(row == 0, up, pltpu.roll(u, 1, 0))
    nxt = jnp.where(row == tm - 1, un, pltpu.roll(u, tm - 1, 0))
    dx = 0.5 * (prev + nxt) - u
    mu = mu_ref[...]

    def mix(j):
        return (u + dx * mu[j:j + 1]).astype(BF16)

    vec = vec_ref[...]
    r = _dot(mix(0), wr_ref[...])
    k = _dot(mix(2), wk_ref[...])
    v = _dot(mix(3), wv_ref[...])
    ones = ones_ref[...]
    kk = k * vec[0:1]
    kk = kk * lax.rsqrt(_group_sum(kk * kk, ones) + 1e-12)
    g_out_ref[...] = _dot(_sigmoid(_dot(mix(5), gl1_ref[...])).astype(BF16), gl2_ref[...])
    xw = mix(1)
    xa = mix(4)
    kd_sum = None
    for d in range(2):
        wl = _dot(jnp.tanh(_dot(xw, wl1_ref[d])).astype(BF16), wl2_ref[d])
        w_ref[d] = jnp.exp(-RWKV_DECAY_SCALE * _sigmoid(w0_ref[d] + wl))
        al = _dot(_dot(xa, al1_ref[d]).astype(BF16), al2_ref[d])
        a = _sigmoid(a0_ref[d] + al)
        kd = k * (1.0 + (a - 1.0) * vec[1:2])
        kd_ref[d] = kd
        ka_ref[d] = kk * a
        kd_sum = kd if kd_sum is None else kd_sum + kd
    r_ref[...] = r
    v_ref[...] = v
    kk_ref[...] = kk
    bonus_ref[...] = _group_sum(r * kd_sum * vec[2:3], ones) * v


def _rwkv_proj(h, seg, mod, norm_g, mu, wr, wk, wv, wl1, wl2, al1, al2, gl1, gl2, vec, w0, a0, tm=256):
    n, dm = h.shape
    tm = min(tm, seg.length)
    per = seg.length // tm
    rid = seg.mod_row(tm)
    hb = tm // SUBLANES
    nblk = n // SUBLANES
    full = lambda a: pl.BlockSpec(a.shape, lambda i, nd=a.ndim: (0,) * nd)
    tile = pl.BlockSpec((tm, dm), lambda i: (i, 0))
    tile2 = pl.BlockSpec((2, tm, dm), lambda i: (0, i, 0))
    ones = _ones_blockdiag(RWKV_HEAD)
    one = jax.ShapeDtypeStruct((n, dm), F32)
    two = jax.ShapeDtypeStruct((2, n, dm), F32)
    return pl.pallas_call(
        functools.partial(_rwkv_proj_kernel, per=per),
        grid=(n // tm,),
        in_specs=[
            tile,
            pl.BlockSpec((SUBLANES, dm), lambda i: (jnp.maximum(i * hb - 1, 0), 0)),
            pl.BlockSpec((SUBLANES, dm), lambda i: (jnp.minimum((i + 1) * hb, nblk - 1), 0)),
            pl.BlockSpec((1, 6, dm), lambda i: (rid(i), 0, 0)),
            full(norm_g), full(mu), full(wr), full(wk), full(wv), full(wl1), full(wl2), full(al1), full(al2),
            full(gl1), full(gl2), full(vec), full(w0), full(a0), full(ones),
        ],
        out_specs=[tile, tile, tile, tile2, tile2, tile2, tile, tile],
        out_shape=[one, one, one, two, two, two, one, one],
        compiler_params=_cparams("parallel"),
        name="rwkv_proj",
    )(h, h, h, mod, norm_g, mu, wr, wk, wv, wl1, wl2, al1, al2, gl1, gl2, vec, w0, a0, ones)


RWKV_TC = 32


def _rwkv_scan_kernel(kk_ref, ka_ref, w_ref, kd_ref, r_ref, v_ref, s0_ref, y_ref, sf_ref, st):
    c = pl.program_id(1)
    tc, nv, _ = v_ref.shape

    @pl.when(c == 0)
    def _():
        st[...] = s0_ref[...]

    def step(t, carry):
        def vgroup(v8, carry2):
            base = pl.multiple_of(v8 * SUBLANES, SUBLANES)
            vblk = v_ref[t, pl.ds(base, SUBLANES), :]
            rows = []
            for j in range(SUBLANES):
                s = st[base + j]
                sa = jnp.sum(s * kk_ref[t], axis=0, keepdims=True)
                s = s * w_ref[t] + (vblk[j:j + 1] * kd_ref[t] - sa * ka_ref[t])
                st[base + j] = s
                rows.append(jnp.sum(s * r_ref[t], axis=0, keepdims=True))
            y_ref[t, pl.ds(base, SUBLANES), :] = jnp.concatenate(rows, axis=0)
            return carry2
        return lax.fori_loop(0, nv // SUBLANES, vgroup, carry)

    lax.fori_loop(0, tc, step, 0)

    @pl.when(c == pl.num_programs(1) - 1)
    def _():
        sf_ref[...] = st[...]


def _rwkv_scan(kk, ka, w, kd, r, v, s0):
    n, kdim, c = kk.shape
    nv = v.shape[1]
    tc = RWKV_TC
    kspec = pl.BlockSpec((tc, kdim, LANES), lambda l, t: (t, 0, l))
    vspec = pl.BlockSpec((tc, nv, LANES), lambda l, t: (t, 0, l))
    sspec = pl.BlockSpec((nv, kdim, LANES), lambda l, t: (0, 0, l))
    return pl.pallas_call(
        _rwkv_scan_kernel,
        grid=(c // LANES, n // tc),
        in_specs=[kspec, kspec, kspec, kspec, kspec, vspec, sspec],
        out_specs=[vspec, sspec],
        out_shape=[jax.ShapeDtypeStruct((n, nv, c), F32), jax.ShapeDtypeStruct((nv, kdim, c), F32)],
        scratch_shapes=[pltpu.VMEM((nv, kdim, LANES), F32)],
        compiler_params=_cparams("parallel", "arbitrary"),
        name="rwkv_scan",
    )(kk, ka, w, kd, r, v, s0)


def _rwkv_out_kernel(h_ref, mod_ref, yf_ref, yb_ref, bonus_ref, g_ref, lng_ref, lnb_ref, wo_ref, ones_ref, o_ref):
    ones = ones_ref[...]
    y = yf_ref[...] + yb_ref[...]
    inv = 1.0 / RWKV_HEAD
    mean = _group_sum(y, ones) * inv
    yc = y - mean
    var = _group_sum(yc * yc, ones) * inv
    yn = yc * lax.rsqrt(var + RWKV_GN_EPS) * lng_ref[...] + lnb_ref[...] + bonus_ref[...]
    out = _dot((yn * g_ref[...]).astype(BF16), wo_ref[...])
    o_ref[...] = h_ref[...] + mod_ref[0][2:3] * out


def _rwkv_out(h, seg, mod, yf, yb, bonus, g, ln_g, ln_b, wo, tm=256):
    n, dm = h.shape
    tm = min(tm, seg.length)
    rid = seg.mod_row(tm)
    tile = pl.BlockSpec((tm, dm), lambda i: (i, 0))
    row = pl.BlockSpec((1, dm), lambda i: (0, 0))
    ones = _ones_blockdiag(RWKV_HEAD)
    return pl.pallas_call(
        _rwkv_out_kernel,
        grid=(n // tm,),
        in_specs=[tile, pl.BlockSpec((1, 6, dm), lambda i: (rid(i), 0, 0)), tile, tile, tile, tile, row, row,
                  pl.BlockSpec((dm, dm), lambda i: (0, 0)), pl.BlockSpec(ones.shape, lambda i: (0, 0))],
        out_specs=tile,
        out_shape=jax.ShapeDtypeStruct((n, dm), F32),
        compiler_params=_cparams("parallel"),
        name="rwkv_out",
    )(h, mod, yf, yb, bonus, g, ln_g, ln_b, wo, ones)


def _rwkv_layer(h, seg, mod, norm_g, s0, rw):
    nseq, length = seg.nseq, seg.length
    hh, nk = RWKV_HEADS, RWKV_HEAD
    r, v, kk, w, ka, kd, g, bonus = _rwkv_proj(h, seg, mod, norm_g, *rw['proj'])
    chains = 2 * nseq * hh
    vs = max(1, LANES // chains)
    nv = nk // vs

    def slab(x):
        return jnp.transpose(x.reshape(nseq, length, hh, nk), (1, 3, 0, 2)).reshape(length, nk, nseq * hh)

    def both(xf, xb):
        s = jnp.concatenate([slab(xf), jnp.flip(slab(xb), axis=0)], axis=-1)
        return jnp.concatenate([s] * vs, axis=-1)

    def vsplit(s):
        return jnp.transpose(s.reshape(length, vs, nv, chains), (0, 2, 1, 3)).reshape(length, nv, vs * chains)

    kk_s, r_s = both(kk, kk), both(r, r)
    w_s, ka_s, kd_s = both(w[0], w[1]), both(ka[0], ka[1]), both(kd[0], kd[1])
    v_s = vsplit(jnp.concatenate([slab(v), jnp.flip(slab(v), axis=0)], axis=-1))
    s0_s = jnp.transpose(s0.reshape(nseq, 2, hh, vs, nv, nk), (4, 5, 3, 1, 0, 2)).reshape(nv, nk, vs * chains)
    y_s, sf = _rwkv_scan(kk_s, ka_s, w_s, kd_s, r_s, v_s, s0_s)
    y5 = jnp.transpose(y_s.reshape(length, nv, vs, 2, nseq, hh), (3, 4, 0, 5, 2, 1)).reshape(2, nseq, length, hh * nk)
    yf = y5[0].reshape(nseq * length, hh * nk)
    yb = jnp.flip(y5[1], axis=1).reshape(nseq * length, hh * nk)
    h_new = _rwkv_out(h, seg, mod, yf, yb, bonus, g, *rw['out'])
    s_fin = jnp.transpose(sf.reshape(nv, nk, vs, 2, nseq, hh), (4, 3, 5, 2, 0, 1)).reshape(nseq, 2, hh, nk, nk)
    return h_new, s_fin


def _rope_tables(length, half, first_lane, period):
    lane = jnp.arange(LANES) % period - first_lane
    active = (lane >= 0) & (lane < 4 * half)
    quarter = jnp.clip(lane, 0, 4 * half - 1) // half
    freq = ROPE_BASE ** (-(jnp.clip(lane, 0, 4 * half - 1) % half).astype(F32) / half)
    t = jnp.arange(length)
    pos = jnp.where(quarter[None, :] < 2, (t // GRID_W)[:, None], (t % GRID_W)[:, None]).astype(F32)
    ang = pos * freq[None, :]
    cos = jnp.where(active[None], jnp.cos(ang), 1.0)
    sin = jnp.where(active[None], jnp.sin(ang), 0.0)
    first = (quarter % 2 == 0)[None]
    return cos, jnp.where(first, -sin, 0.0), jnp.where(first, 0.0, sin)


def _rope(x, cos, sa, sb, half):
    w = x.shape[-1]
    rep = w // LANES
    tile = (lambda a: a) if rep == 1 else (lambda a: jnp.concatenate([a] * rep, axis=-1))
    return x * tile(cos) + pltpu.roll(x, w - half, 1) * tile(sa) + pltpu.roll(x, half, 1) * tile(sb)


def _attn_kernel(*refs, groups, dq, dv, hpb, scale, band, has_ctx, has_sink):
    it = iter(refs)
    q_ref, k_ref, v_ref = next(it), next(it), next(it)
    kc_ref, vc_ref = (next(it), next(it)) if has_ctx else (None, None)
    sink_ref = next(it) if has_sink else None
    o_ref = next(it)
    tq = q_ref.shape[1]
    lk = k_ref.shape[1]
    if band:
        qpos = pl.program_id(2) * tq + lax.broadcasted_iota(jnp.int32, (tq, lk), 0)
        kpos = lax.broadcasted_iota(jnp.int32, (tq, lk), 1)
        valid = jnp.abs(qpos - kpos) <= WINDOW
    for j in range(hpb):
        kh = k_ref[0, :, j * dq:(j + 1) * dq].astype(BF16)
        vh = v_ref[0, :, j * dv:(j + 1) * dv].astype(BF16)
        if has_ctx:
            kch = kc_ref[0, :, j * dq:(j + 1) * dq].astype(BF16)
            vch = vc_ref[0, :, j * dv:(j + 1) * dv].astype(BF16)
        for g in range(groups):
            hq = j * groups + g
            qh = q_ref[0, :, hq * dq:(hq + 1) * dq].astype(BF16)
            s = _dot_nt(qh, kh) * scale
            if band:
                s = jnp.where(valid, s, -jnp.inf)
            m = jnp.max(s, axis=-1, keepdims=True)
            if has_ctx:
                sc = _dot_nt(qh, kch) * scale
                m = jnp.maximum(m, jnp.max(sc, axis=-1, keepdims=True))
            if has_sink:
                snk = sink_ref[0, :, hq:hq + 1]
                m = jnp.maximum(m, snk)
            p = jnp.exp(s - m)
            den = jnp.sum(p, axis=-1, keepdims=True)
            acc = _dot(p.astype(BF16), vh)
            if has_ctx:
                pc = jnp.exp(sc - m)
                den = den + jnp.sum(pc, axis=-1, keepdims=True)
                acc = acc + _dot(pc.astype(BF16), vch)
            if has_sink:
                den = den + jnp.exp(snk - m)
            o_ref[0, :, hq * dv:(hq + 1) * dv] = acc / den


def _attention(q, k, v, ctx, sink, *, heads, groups, dq, dv, scale, band, tq=256):
    b, lq, _ = q.shape
    lk = k.shape[1]
    hpb = 2
    nhb = heads // hpb
    ins = [q, k, v]
    specs = [
        pl.BlockSpec((1, tq, hpb * groups * dq), lambda bi, hb, i: (bi, i, hb)),
        pl.BlockSpec((1, lk, hpb * dq), lambda bi, hb, i: (bi, 0, hb)),
        pl.BlockSpec((1, lk, hpb * dv), lambda bi, hb, i: (bi, 0, hb)),
    ]
    if ctx is not None:
        lc = ctx[0].shape[1]
        ins += list(ctx)
        specs += [pl.BlockSpec((1, lc, hpb * dq), lambda bi, hb, i: (bi, 0, hb)),
                  pl.BlockSpec((1, lc, hpb * dv), lambda bi, hb, i: (bi, 0, hb))]
    if sink is not None:
        sk = jnp.pad(sink.reshape(nhb, 1, hpb * groups), ((0, 0), (0, 0), (0, LANES - hpb * groups)))
        ins.append(sk)
        specs.append(pl.BlockSpec((1, 1, LANES), lambda bi, hb, i: (hb, 0, 0)))
    return pl.pallas_call(
        functools.partial(_attn_kernel, groups=groups, dq=dq, dv=dv, hpb=hpb, scale=scale, band=band,
                          has_ctx=ctx is not None, has_sink=sink is not None),
        grid=(b, nhb, lq // tq),
        in_specs=specs,
        out_specs=pl.BlockSpec((1, tq, hpb * groups * dv), lambda bi, hb, i: (bi, i, hb)),
        out_shape=jax.ShapeDtypeStruct((b, lq, heads * groups * dv), F32),
        compiler_params=_cparams("parallel", "parallel", "arbitrary"),
        name="attention",
    )(*ins)


def _proj_res_kernel(h_ref, mod_ref, a_ref, w_ref, o_ref):
    o_ref[...] = h_ref[...] + mod_ref[0][2:3] * _dot(a_ref[...].astype(BF16), w_ref[...])


def _proj_residual(h, seg, mod, a, w, tm=256):
    n, dm = h.shape
    tm = min(tm, seg.length)
    rid = seg.mod_row(tm)
    return pl.pallas_call(
        _proj_res_kernel,
        grid=(n // tm,),
        in_specs=[pl.BlockSpec((tm, dm), lambda i: (i, 0)), pl.BlockSpec((1, 6, dm), lambda i: (rid(i), 0, 0)),
                  pl.BlockSpec((tm, a.shape[1]), lambda i: (i, 0)), pl.BlockSpec(w.shape, lambda i: (0, 0))],
        out_specs=pl.BlockSpec((tm, dm), lambda i: (i, 0)),
        out_shape=jax.ShapeDtypeStruct((n, dm), F32),
        compiler_params=_cparams("parallel"),
        name="proj_residual",
    )(h, mod, a, w)


GQA_NQ = GQA_HEADS * GQA_HEAD_DIM
GQA_NK = GQA_KV_HEADS * GQA_HEAD_DIM
GQA_ROPE_HALF = GQA_HEAD_DIM // 4


def _gqa_qkv_kernel(*refs, use_rope):
    if use_rope:
        h_ref, mod_ref, g_ref, w_ref, gq_ref, gk_ref, ones_ref, cos_ref, sa_ref, sb_ref, q_ref, k_ref, v_ref = refs
    else:
        h_ref, mod_ref, g_ref, w_ref, gq_ref, gk_ref, ones_ref, q_ref, k_ref, v_ref = refs
    mod = mod_ref[0]
    u = _modulate(h_ref[...], g_ref[...], mod[0:1], mod[1:2]).astype(BF16)
    qkv = _dot(u, w_ref[...])
    ones = ones_ref[...]
    inv = 1.0 / GQA_HEAD_DIM
    q = qkv[:, :GQA_NQ]
    k = qkv[:, GQA_NQ:GQA_NQ + GQA_NK]
    q = q * lax.rsqrt(_group_sum(q * q, ones) * inv + NORM_EPS) * gq_ref[...]
    k = k * lax.rsqrt(_group_sum(k * k, ones) * inv + NORM_EPS) * gk_ref[...]
    if use_rope:
        cos, sa, sb = cos_ref[...], sa_ref[...], sb_ref[...]
        q = _rope(q, cos, sa, sb, GQA_ROPE_HALF)
        k = _rope(k, cos, sa, sb, GQA_ROPE_HALF)
    q_ref[...] = q
    k_ref[...] = k
    v_ref[...] = qkv[:, GQA_NQ + GQA_NK:]


def _gqa_qkv(h, seg, mod, norm_g, w_qkv, gq, gk, rope, tm=256):
    n, dm = h.shape
    tm = min(tm, seg.length)
    per = seg.length // tm
    rid = seg.mod_row(tm)
    ones = _ones_blockdiag(GQA_HEAD_DIM)
    full = lambda a: pl.BlockSpec(a.shape, lambda i, nd=a.ndim: (0,) * nd)
    ins = [h, mod, norm_g, w_qkv, gq, gk, ones]
    specs = [pl.BlockSpec((tm, dm), lambda i: (i, 0)), pl.BlockSpec((1, 6, dm), lambda i: (rid(i), 0, 0)),
             full(norm_g), full(w_qkv), full(gq), full(gk), full(ones)]
    if rope is not None:
        ins += list(rope)
        specs += [pl.BlockSpec((tm, LANES), lambda i: (i % per, 0))] * 3
    return pl.pallas_call(
        functools.partial(_gqa_qkv_kernel, use_rope=rope is not None),
        grid=(n // tm,),
        in_specs=specs,
        out_specs=[pl.BlockSpec((tm, GQA_NQ), lambda i: (i, 0)), pl.BlockSpec((tm, GQA_NK), lambda i: (i, 0)),
                   pl.BlockSpec((tm, GQA_NK), lambda i: (i, 0))],
        out_shape=[jax.ShapeDtypeStruct((n, GQA_NQ), F32), jax.ShapeDtypeStruct((n, GQA_NK), F32),
                   jax.ShapeDtypeStruct((n, GQA_NK), F32)],
        compiler_params=_cparams("parallel"),
        name="gqa_qkv",
    )(*ins)


def _gqa_layer(h, seg, mod, norm_g, gw, ctx):
    w_qkv, gq, gk, sink, w_o = gw
    latent = ctx is not None
    rope = _rope_tables(seg.length, GQA_ROPE_HALF, 0, 4 * GQA_ROPE_HALF) if latent else None
    q, k, v = _gqa_qkv(h, seg, mod, norm_g, w_qkv, gq, gk, rope)
    b3 = lambda a: a.reshape(seg.nseq, seg.length, a.shape[-1])
    att = _attention(b3(q), b3(k), b3(v), ctx, sink, heads=GQA_KV_HEADS, groups=GQA_HEADS // GQA_KV_HEADS,
                     dq=GQA_HEAD_DIM, dv=GQA_HEAD_DIM, scale=GQA_HEAD_DIM ** -0.5, band=latent)
    h_new = _proj_residual(h, seg, mod, att.reshape(seg.n, -1), w_o)
    return h_new, k, v


MLA_QW = MLA_HEADS * MLA_PAD
MLA_VW = MLA_HEADS * MLA_V
MLA_CKR = MLA_KV_LORA + LANES
MLA_ROPE_HALF = MLA_ROPE // 4


def _mla_expand(ckv, kr, wuk_ref, wuv_ref, kg_ref, ones_ref):
    cb = ckv.astype(BF16)
    kn = _dot(cb, wuk_ref[...])
    kr_t = jnp.concatenate([pltpu.roll(kr, MLA_NOPE, 1)] * MLA_HEADS, axis=-1)
    k = kn + kr_t
    k = k * lax.rsqrt(_group_sum(k * k, ones_ref[...]) * (1.0 / MLA_QK) + NORM_EPS) * kg_ref[...]
    return k, _dot(cb, wuv_ref[...])


def _mla_proj_kernel(*refs, use_rope):
    (h_ref, mod_ref, g_ref, wdq_ref, qlg_ref, wuq_ref, qg_ref, wdkv_ref, kvg_ref, wuk_ref, wuv_ref, kg_ref,
     ones_ref) = refs[:13]
    if use_rope:
        cos_ref, sa_ref, sb_ref = refs[13:16]
    q_ref, k_ref, v_ref, ckv_ref, kr_ref = refs[-5:]
    mod = mod_ref[0]
    u = _modulate(h_ref[...], g_ref[...], mod[0:1], mod[1:2]).astype(BF16)
    qd = _dot(u, wdq_ref[...])
    qd = qd * lax.rsqrt(jnp.mean(qd * qd, axis=-1, keepdims=True) + NORM_EPS) * qlg_ref[...]
    q = _dot(qd.astype(BF16), wuq_ref[...])
    q = q * lax.rsqrt(_group_sum(q * q, ones_ref[...]) * (1.0 / MLA_QK) + NORM_EPS) * qg_ref[...]
    ckr = _dot(u, wdkv_ref[...])
    ckv = ckr[:, :MLA_KV_LORA]
    ckv = ckv * lax.rsqrt(jnp.mean(ckv * ckv, axis=-1, keepdims=True) + NORM_EPS) * kvg_ref[...]
    kr = ckr[:, MLA_KV_LORA:]
    k, v = _mla_expand(ckv, kr, wuk_ref, wuv_ref, kg_ref, ones_ref)
    if use_rope:
        cos, sa, sb = cos_ref[...], sa_ref[...], sb_ref[...]
        q = _rope(q, cos, sa, sb, MLA_ROPE_HALF)
        k = _rope(k, cos, sa, sb, MLA_ROPE_HALF)
    q_ref[...] = q
    k_ref[...] = k
    v_ref[...] = v
    ckv_ref[...] = ckv
    kr_ref[...] = kr


def _mla_proj(h, seg, mod, norm_g, pw, rope, tm=256):
    n, dm = h.shape
    tm = min(tm, seg.length)
    per = seg.length // tm
    rid = seg.mod_row(tm)
    ones = _ones_blockdiag(MLA_PAD)
    full = lambda a: pl.BlockSpec(a.shape, lambda i, nd=a.ndim: (0,) * nd)
    ins = [h, mod, norm_g, *pw, ones]
    specs = [pl.BlockSpec((tm, dm), lambda i: (i, 0)), pl.BlockSpec((1, 6, dm), lambda i: (rid(i), 0, 0)),
             full(norm_g)] + [full(a) for a in pw] + [full(ones)]
    if rope is not None:
        ins += list(rope)
        specs += [pl.BlockSpec((tm, LANES), lambda i: (i % per, 0))] * 3
    widths = (MLA_QW, MLA_QW, MLA_VW, MLA_KV_LORA, LANES)
    return pl.pallas_call(
        functools.partial(_mla_proj_kernel, use_rope=rope is not None),
        grid=(n // tm,),
        in_specs=specs,
        out_specs=[pl.BlockSpec((tm, w), lambda i: (i, 0)) for w in widths],
        out_shape=[jax.ShapeDtypeStruct((n, w), F32) for w in widths],
        compiler_params=_cparams("parallel"),
        name="mla_proj",
    )(*ins)


def _mla_ctx_kernel(ckv_ref, kr_ref, wuk_ref, wuv_ref, kg_ref, ones_ref, k_ref, v_ref):
    k, v = _mla_expand(ckv_ref[...], kr_ref[...], wuk_ref, wuv_ref, kg_ref, ones_ref)
    k_ref[...] = k
    v_ref[...] = v


def _mla_ctx_expand(ckv, kr, wuk, wuv, kg, tm=256):
    n = ckv.shape[0]
    tm = min(tm, n)
    ones = _ones_blockdiag(MLA_PAD)
    full = lambda a: pl.BlockSpec(a.shape, lambda i, nd=a.ndim: (0,) * nd)
    return pl.pallas_call(
        _mla_ctx_kernel,
        grid=(n // tm,),
        in_specs=[pl.BlockSpec((tm, MLA_KV_LORA), lambda i: (i, 0)), pl.BlockSpec((tm, LANES), lambda i: (i, 0)),
                  full(wuk), full(wuv), full(kg), full(ones)],
        out_specs=[pl.BlockSpec((tm, MLA_QW), lambda i: (i, 0)), pl.BlockSpec((tm, MLA_VW), lambda i: (i, 0))],
        out_shape=[jax.ShapeDtypeStruct((n, MLA_QW), F32), jax.ShapeDtypeStruct((n, MLA_VW), F32)],
        compiler_params=_cparams("parallel"),
        name="mla_ctx_expand",
    )(ckv, kr, wuk, wuv, kg, ones)


def _mla_weights(w_dq, q_lora_g, w_uq, w_dkv, kv_g, w_ukv, q_g, k_g, w_o):
    padh = lambda a: jnp.pad(a, [(0, 0)] * (a.ndim - 1) + [(0, MLA_PAD - a.shape[-1])])
    wuq = padh(w_uq.reshape(MLA_Q_LORA, MLA_HEADS, MLA_QK)).reshape(MLA_Q_LORA, MLA_QW).astype(BF16)
    wdkv = jnp.pad(w_dkv, ((0, 0), (0, MLA_CKR - w_dkv.shape[1]))).astype(BF16)
    ukv = w_ukv.reshape(MLA_KV_LORA, MLA_HEADS, MLA_NOPE + MLA_V)
    wuk = padh(ukv[:, :, :MLA_NOPE]).reshape(MLA_KV_LORA, MLA_QW).astype(BF16)
    wuv = ukv[:, :, MLA_NOPE:].reshape(MLA_KV_LORA, MLA_VW).astype(BF16)
    tile_h = lambda g: jnp.tile(padh(g), MLA_HEADS)[None]
    proj = (w_dq.astype(BF16), q_lora_g[None], wuq, tile_h(q_g), wdkv, kv_g[None], wuk, wuv, tile_h(k_g))
    return proj, w_o.astype(BF16)


def _mla_layer(h, seg, mod, norm_g, mw, cache):
    proj, w_o = mw
    latent = cache is not None
    rope = _rope_tables(seg.length, MLA_ROPE_HALF, MLA_NOPE, MLA_PAD) if latent else None
    q, k, v, ckv, kr = _mla_proj(h, seg, mod, norm_g, proj, rope)
    ctx = None
    if latent:
        c_ckv, c_kr = cache
        lc = c_ckv.shape[1]
        kr_p = jnp.pad(c_kr.reshape(-1, MLA_ROPE), ((0, 0), (0, LANES - MLA_ROPE)))
        kc, vc = _mla_ctx_expand(c_ckv.reshape(-1, MLA_KV_LORA), kr_p, proj[6], proj[7], proj[8])
        ctx = (kc.reshape(seg.nseq, lc, MLA_QW), vc.reshape(seg.nseq, lc, MLA_VW))
    b3 = lambda a: a.reshape(seg.nseq, seg.length, a.shape[-1])
    att = _attention(b3(q), b3(k), b3(v), ctx, None, heads=MLA_HEADS, groups=1, dq=MLA_PAD, dv=MLA_V,
                     scale=MLA_QK ** -0.5, band=False)
    h_new = _proj_residual(h, seg, mod, att.reshape(seg.n, -1), w_o)
    return h_new, ckv, kr[:, :MLA_ROPE]


def kernel(x_prompt, x_sample, state_s5_re, state_s5_im, state_rwkv, cache_gqa_k, cache_gqa_v, cache_mla_ckv, cache_mla_krope, c, c_ctx, ada_w, ada_b, norm1_g, norm2_g, s5_lambda_re, s5_lambda_im, s5_log_step, s5_b_re, s5_b_im, s5_c_re, s5_c_im, s5_d, s5_glu_w1, s5_glu_w2, rwkv_mu, rwkv_w_r, rwkv_w_k, rwkv_w_v, rwkv_w_o, rwkv_w0, rwkv_w_l1, rwkv_w_l2, rwkv_a0, rwkv_a_l1, rwkv_a_l2, rwkv_g_l1, rwkv_g_l2, rwkv_k_k, rwkv_k_a, rwkv_r_k, rwkv_ln_g, rwkv_ln_b, gqa_w_qkv, gqa_q_norm, gqa_k_norm, gqa_sink, gqa_w_o, mla_w_dq, mla_q_lora_norm, mla_w_uq, mla_w_dkv, mla_kv_norm, mla_w_ukv, mla_q_norm, mla_k_norm, mla_w_o, moe_w_group, moe_b_group, moe_w_expert, moe_b_expert, moe_w_gate, moe_w_up, moe_w_down):
    bp, lp, dm = x_prompt.shape
    bs, ls, _ = x_sample.shape
    depth = ada_w.shape[0]
    assert dm == D_MODEL and 1 + bs <= MOD_ROWS
    seg_p = _Seg(bp, lp, 0, True)
    seg_s = _Seg(bs, ls, 1, False)
    bf = lambda a: a.astype(BF16)

    cond = jnp.zeros((MOD_ROWS, dm), F32).at[0].set(c_ctx).at[1:1 + bs].set(c)
    mods = _ada_all(cond, ada_w, ada_b).reshape(depth, MOD_ROWS, 6, dm)

    hp = x_prompt.reshape(bp * lp, dm)
    hs = x_sample.reshape(bs * ls, dm)
    outs = {}
    for layer in range(depth):
        kind = layer % 4
        mod = mods[layer]
        g1 = norm1_g[layer][None]
        if kind == 0:
            sw = _s5_weights(s5_lambda_re, s5_lambda_im, s5_log_step, s5_b_re, s5_b_im, s5_c_re, s5_c_im)
            w1, w2, dsk = bf(s5_glu_w1), bf(s5_glu_w2), s5_d[None]
            zeros = jnp.zeros((bp, 2, S5_N), F32)
            hp, f_re, f_im = _s5_layer(hp, seg_p, mod, g1, zeros, zeros, sw, dsk, w1, w2)
            hs, _, _ = _s5_layer(hs, seg_s, mod, g1, state_s5_re.reshape(bs, 2, S5_N),
                                 state_s5_im.reshape(bs, 2, S5_N), sw, dsk, w1, w2)
            outs['s5_re'] = f_re.reshape(bp, 2, S5_GROUPS, S5_STATE)
            outs['s5_im'] = f_im.reshape(bp, 2, S5_GROUPS, S5_STATE)
        elif kind == 1:
            vec = jnp.zeros((SUBLANES, dm), F32).at[0].set(rwkv_k_k).at[1].set(rwkv_k_a).at[2].set(rwkv_r_k.reshape(-1))
            rw = dict(
                proj=(rwkv_mu, bf(rwkv_w_r), bf(rwkv_w_k), bf(rwkv_w_v), bf(rwkv_w_l1), bf(rwkv_w_l2),
                      bf(rwkv_a_l1), bf(rwkv_a_l2), bf(rwkv_g_l1), bf(rwkv_g_l2), vec,
                      rwkv_w0[:, None, :], rwkv_a0[:, None, :]),
                out=(rwkv_ln_g[None], rwkv_ln_b[None], bf(rwkv_w_o)))
            zeros = jnp.zeros((bp, 2, RWKV_HEADS, RWKV_HEAD, RWKV_HEAD), F32)
            hp, outs['rwkv'] = _rwkv_layer(hp, seg_p, mod, g1, zeros, rw)
            hs, _ = _rwkv_layer(hs, seg_s, mod, g1, state_rwkv, rw)
        elif kind == 2:
            gw = (bf(gqa_w_qkv), jnp.tile(gqa_q_norm, GQA_HEADS)[None], jnp.tile(gqa_k_norm, GQA_KV_HEADS)[None],
                  gqa_sink, bf(gqa_w_o))
            hp, kp, vp = _gqa_layer(hp, seg_p, mod, g1, gw, None)
            outs['gqa_k'] = kp.reshape(bp, lp, GQA_KV_HEADS, GQA_HEAD_DIM)
            outs['gqa_v'] = vp.reshape(bp, lp, GQA_KV_HEADS, GQA_HEAD_DIM)
            lc = cache_gqa_k.shape[1]
            ctx = (cache_gqa_k.reshape(bs, lc, GQA_NK), cache_gqa_v.reshape(bs, lc, GQA_NK))
            hs, _, _ = _gqa_layer(hs, seg_s, mod, g1, gw, ctx)
        else:
            mw = _mla_weights(mla_w_dq, mla_q_lora_norm, mla_w_uq, mla_w_dkv, mla_kv_norm, mla_w_ukv, mla_q_norm,
                              mla_k_norm, mla_w_o)
            hp, ckv_p, kr_p = _mla_layer(hp, seg_p, mod, g1, mw, None)
            outs['mla_ckv'] = ckv_p.reshape(bp, lp, MLA_KV_LORA)
            outs['mla_kr'] = kr_p.reshape(bp, lp, MLA_ROPE)
            hs, _, _ = _mla_layer(hs, seg_s, mod, g1, mw, (cache_mla_ckv, cache_mla_krope))
        wr = jnp.zeros((dm, ROUTE_W), F32).at[:, :MOE_GROUPS].set(moe_w_group[layer])
        wr = wr.at[:, MOE_GROUPS:MOE_GROUPS + MOE_EXPERTS].set(moe_w_expert[layer])
        br = jnp.zeros((1, ROUTE_W), F32).at[0, :MOE_GROUPS].set(moe_b_group[layer])
        br = br.at[0, MOE_GROUPS:MOE_GROUPS + MOE_EXPERTS].set(moe_b_expert[layer])
        wr_hi, wr_lo = _hi_lo(wr)
        mo = (mod, norm2_g[layer][None], wr_hi, wr_lo, br, bf(moe_w_gate[layer]), bf(moe_w_up[layer]),
              bf(moe_w_down[layer]))
        hp = _moe_layer(hp, seg_p, *mo)
        hs = _moe_layer(hs, seg_s, *mo)
    return (hp.reshape(bp, lp, dm), hs.reshape(bs, ls, dm), outs['s5_re'], outs['s5_im'], outs['rwkv'],
            outs['gqa_k'], outs['gqa_v'], outs['mla_ckv'], outs['mla_kr'])
```

```python
import functools
import math

import jax
import jax.numpy as jnp
from jax import lax
from jax.experimental import pallas as pl
from jax.experimental.pallas import tpu as pltpu

F32 = jnp.float32
BF16 = jnp.bfloat16

D_MODEL = 1024
NORM_EPS = 1e-6
ROPE_BASE = 10000.0
GRID_W = 64
S5_GROUP = 16
S5_GROUPS = D_MODEL // S5_GROUP
S5_STATE = 64
S5_N = S5_GROUPS * S5_STATE
RWKV_HEAD = 64
RWKV_HEADS = D_MODEL // RWKV_HEAD
RWKV_DECAY_SCALE = math.exp(-0.5)
RWKV_GN_EPS = 64e-5
GQA_HEADS = 16
GQA_KV_HEADS = 4
GQA_HEAD_DIM = 64
WINDOW = 128
MLA_HEADS = 16
MLA_Q_LORA = 384
MLA_KV_LORA = 256
MLA_NOPE = 64
MLA_ROPE = 32
MLA_QK = MLA_NOPE + MLA_ROPE
MLA_V = 64
MLA_PAD = 128
MOE_GROUPS = 4
MOE_PER_GROUP = 4
MOE_EXPERTS = 16
MOE_HIDDEN = 512

LANES = 128
SUBLANES = 8
MXU_DIM = 256
VMEM_LIMIT = 56 * 1024 * 1024
MOD_ROWS = 8


def _cparams(*sem):
    return pltpu.CompilerParams(dimension_semantics=sem, vmem_limit_bytes=VMEM_LIMIT)


def _dot(a, b):
    return jnp.dot(a, b, preferred_element_type=F32)


def _dot_nt(a, b):
    return lax.dot_general(a, b, (((1,), (1,)), ((), ())), preferred_element_type=F32)


def _split3(x):
    hi = x.astype(BF16)
    r1 = x - hi.astype(F32)
    mid = r1.astype(BF16)
    lo = (r1 - mid.astype(F32)).astype(BF16)
    return hi, mid, lo


def _dot_x3(x, w_hi, w_lo):
    hi, mid, _ = _split3(x)
    return _dot(hi, w_hi) + (_dot(mid, w_hi) + _dot(hi, w_lo))


def _group_sum(x, ones_bd):
    n = x.shape[-1]
    outs = []
    for c in range(n // MXU_DIM):
        hi, mid, lo = _split3(x[:, c * MXU_DIM:(c + 1) * MXU_DIM])
        outs.append(_dot(hi, ones_bd) + (_dot(mid, ones_bd) + _dot(lo, ones_bd)))
    return outs[0] if len(outs) == 1 else jnp.concatenate(outs, axis=-1)


def _modulate(x, g, shift, scale):
    ms = jnp.mean(x * x, axis=-1, keepdims=True)
    return x * lax.rsqrt(ms + NORM_EPS) * g * (1.0 + scale) + shift


def _sigmoid(x):
    return 1.0 / (1.0 + jnp.exp(-x))


def _silu(x):
    return x * _sigmoid(x)


def _ones_blockdiag(group):
    i = jnp.arange(MXU_DIM) // group
    return (i[:, None] == i[None, :]).astype(BF16)


def _hi_lo(w):
    hi = w.astype(BF16)
    return hi, (w - hi.astype(F32)).astype(BF16)


class _Seg:
    def __init__(self, nseq, length, mod0, shared_mod):
        self.nseq, self.length, self.mod0, self.shared_mod = nseq, length, mod0, shared_mod
        self.n = nseq * length

    def mod_row(self, tm):
        if self.shared_mod:
            return lambda i: self.mod0
        per = self.length // tm
        return lambda i: self.mod0 + i // per

    def mod_rows(self, mod, k):
        if self.shared_mod:
            return jnp.broadcast_to(mod[self.mod0, k][None], (self.nseq, mod.shape[-1]))
        return mod[self.mod0:self.mod0 + self.nseq, k]


def _ada_kernel(cond_ref, w_ref, b_ref, o_ref):
    s = _silu(cond_ref[...])
    w = w_ref[0]
    w_hi = w.astype(BF16)
    w_lo = (w - w_hi.astype(F32)).astype(BF16)
    o_ref[0] = _dot_x3(s, w_hi, w_lo) + b_ref[0]


def _ada_all(cond, ada_w, ada_b):
    depth, d, n6 = ada_w.shape
    tn = 1536
    return pl.pallas_call(
        _ada_kernel,
        grid=(depth, n6 // tn),
        in_specs=[
            pl.BlockSpec((MOD_ROWS, d), lambda l, j: (0, 0)),
            pl.BlockSpec((1, d, tn), lambda l, j: (l, 0, j)),
            pl.BlockSpec((1, 1, tn), lambda l, j: (l, 0, j)),
        ],
        out_specs=pl.BlockSpec((1, MOD_ROWS, tn), lambda l, j: (l, 0, j)),
        out_shape=jax.ShapeDtypeStruct((depth, MOD_ROWS, n6), F32),
        compiler_params=_cparams("parallel", "parallel"),
        name="ada_mod",
    )(cond, ada_w, ada_b.reshape(depth, 1, n6))


ROUTE_W = LANES


def _route(logits):
    lane = lax.broadcasted_iota(jnp.int32, logits.shape, 1)
    neg = jnp.float32(-jnp.inf)
    big = jnp.int32(ROUTE_W)
    gmask = lane < MOE_GROUPS
    gl = jnp.where(gmask, logits, neg)
    gmax = jnp.max(gl, axis=-1, keepdims=True)
    g_top = jnp.min(jnp.where(gl == gmax, lane, big), axis=-1, keepdims=True)
    p_g = 1.0 / jnp.sum(jnp.where(gmask, jnp.exp(gl - gmax), 0.0), axis=-1, keepdims=True)
    lo = MOE_GROUPS + MOE_PER_GROUP * g_top
    emask = (lane >= lo) & (lane < lo + MOE_PER_GROUP)
    el = jnp.where(emask, logits, neg)
    m1 = jnp.max(el, axis=-1, keepdims=True)
    i1 = jnp.min(jnp.where(el == m1, lane, big), axis=-1, keepdims=True)
    el2 = jnp.where(lane == i1, neg, el)
    m2 = jnp.max(el2, axis=-1, keepdims=True)
    i2 = jnp.min(jnp.where(el2 == m2, lane, big), axis=-1, keepdims=True)
    e2 = jnp.exp(m2 - m1)
    w1 = p_g / (1.0 + e2)
    w2 = p_g * e2 / (1.0 + e2)
    gates = jnp.where(lane == i1, w1, 0.0) + jnp.where(lane == i2, w2, 0.0)
    return gates, g_top


def _lane_col(x, idx):
    lane = lax.broadcasted_iota(jnp.int32, x.shape, 1)
    return jnp.sum(jnp.where(lane == idx, x, 0.0), axis=-1, keepdims=True)


def _moe_kernel(h_ref, mod_ref, g_ref, wr_hi_ref, wr_lo_ref, br_ref, tri_ref, wg_ref, wu_ref, wd_ref, o_ref,
                v_scr, gate_scr, pos_scr, post_scr, x_scr, gx_scr, y_scr, acc_scr, cnt_scr, *, rows):
    e = pl.program_id(1)
    grp = e // MOE_PER_GROUP
    tm = v_scr.shape[0]
    nchunk = x_scr.shape[0] // rows

    @pl.when(e == 0)
    def _():
        mod = mod_ref[0]
        v = _modulate(h_ref[...], g_ref[...], mod[3:4], mod[4:5])
        logits = _dot_x3(v, wr_hi_ref[...], wr_lo_ref[...]) + br_ref[...]
        gates, g_top = _route(logits)
        gate_scr[...] = gates
        v_scr[...] = v.astype(BF16)
        acc_scr[...] = jnp.zeros_like(acc_scr)
        lane = lax.broadcasted_iota(jnp.int32, gates.shape, 1)
        member = lane == g_top
        incl = _dot(tri_ref[...], jnp.where(member, 1.0, 0.0).astype(BF16))
        pos = jnp.where(member, incl - 1.0, -1.0)
        pos_scr[...] = pos
        pos_t = pos.T
        for g in range(MOE_GROUPS):
            post_scr[g] = jnp.broadcast_to(pos_t[g:g + 1], post_scr.shape[1:])
            cnt_scr[g] = jnp.sum(jnp.where(member & (lane == g), 1.0, 0.0)).astype(jnp.int32)

    cnt = cnt_scr[grp]

    @pl.when(e % MOE_PER_GROUP == 0)
    def _():
        g_hi, g_mid, g_lo = _split3(gate_scr[...])
        for c in range(nchunk):
            @pl.when(cnt > c * rows)
            def _(c=c):
                slot = c * rows + lax.broadcasted_iota(jnp.int32, (rows, tm), 0)
                onehot = jnp.where(post_scr[grp, 0:1, :].astype(jnp.int32) == slot, 1.0, 0.0).astype(BF16)
                sl = slice(c * rows, (c + 1) * rows)
                x_scr[sl] = _dot(onehot, v_scr[...]).astype(BF16)
                gx_scr[sl] = _dot(onehot, g_hi) + (_dot(onehot, g_mid) + _dot(onehot, g_lo))
                y_scr[sl] = jnp.zeros((rows, y_scr.shape[1]), F32)

    for c in range(nchunk):
        @pl.when(cnt > c * rows)
        def _(c=c):
            sl = slice(c * rows, (c + 1) * rows)
            x = x_scr[sl]
            gate = _lane_col(gx_scr[sl], e + MOE_GROUPS)
            hh = (_silu(_dot(x, wg_ref[0])) * _dot(x, wu_ref[0]) * gate).astype(BF16)
            y_scr[sl] += _dot(hh, wd_ref[0])

    @pl.when(e % MOE_PER_GROUP == MOE_PER_GROUP - 1)
    def _():
        pos = _lane_col(pos_scr[...], grp).astype(jnp.int32)
        for c in range(nchunk):
            @pl.when(cnt > c * rows)
            def _(c=c):
                slot = c * rows + lax.broadcasted_iota(jnp.int32, (tm, rows), 1)
                onehot_t = jnp.where(pos == slot, 1.0, 0.0).astype(BF16)
                y_hi, y_lo = _hi_lo(y_scr[c * rows:(c + 1) * rows])
                acc_scr[...] += _dot(onehot_t, y_hi) + _dot(onehot_t, y_lo)

    @pl.when(e == pl.num_programs(1) - 1)
    def _():
        o_ref[...] = h_ref[...] + mod_ref[0][5:6] * acc_scr[...]


MOE_TM = 1024
MOE_ROWS = 320


def _moe_layer(h, seg, mod, norm_g, wr_hi, wr_lo, br, wg, wu, wd, tm=MOE_TM, rows=MOE_ROWS):
    n, d = h.shape
    nexp, _, f = wg.shape
    tm = min(tm, n if seg.shared_mod else seg.length)
    rows = min(rows, tm)
    nchunk = -(-tm // rows)
    rid = seg.mod_row(tm)
    tri = jnp.tri(tm, dtype=BF16)
    return pl.pallas_call(
        functools.partial(_moe_kernel, rows=rows),
        grid=(n // tm, nexp),
        in_specs=[
            pl.BlockSpec((tm, d), lambda i, e: (i, 0)),
            pl.BlockSpec((1, 6, d), lambda i, e: (rid(i), 0, 0)),
            pl.BlockSpec((1, d), lambda i, e: (0, 0)),
            pl.BlockSpec((d, ROUTE_W), lambda i, e: (0, 0)),
            pl.BlockSpec((d, ROUTE_W), lambda i, e: (0, 0)),
            pl.BlockSpec((1, ROUTE_W), lambda i, e: (0, 0)),
            pl.BlockSpec((tm, tm), lambda i, e: (0, 0)),
            pl.BlockSpec((1, d, f), lambda i, e: (e, 0, 0)),
            pl.BlockSpec((1, d, f), lambda i, e: (e, 0, 0)),
            pl.BlockSpec((1, f, d), lambda i, e: (e, 0, 0)),
        ],
        out_specs=pl.BlockSpec((tm, d), lambda i, e: (i, 0)),
        out_shape=jax.ShapeDtypeStruct((n, d), F32),
        scratch_shapes=[
            pltpu.VMEM((tm, d), BF16),
            pltpu.VMEM((tm, ROUTE_W), F32),
            pltpu.VMEM((tm, ROUTE_W), F32),
            pltpu.VMEM((MOE_GROUPS, SUBLANES, tm), F32),
            pltpu.VMEM((nchunk * rows, d), BF16),
            pltpu.VMEM((nchunk * rows, ROUTE_W), F32),
            pltpu.VMEM((nchunk * rows, d), F32),
            pltpu.VMEM((tm, d), F32),
            pltpu.SMEM((MOE_GROUPS,), jnp.int32),
        ],
        compiler_params=_cparams("parallel", "arbitrary"),
        name="moe",
    )(h, mod, norm_g, wr_hi, wr_lo, br, tri, wg, wu, wd)


S5_TL = 32
S5_CG = MXU_DIM // S5_GROUP
S5_NCH = D_MODEL // MXU_DIM
S5_CN = S5_CG * S5_STATE
S5_SCAN_W = 512


def _s5_scan_kernel(h_ref, sh_ref, sc_ref, g_ref, s0r_ref, s0i_ref, ar_ref, ai_ref, wb_ref, wcr_ref, wci_ref,
                    y_ref, fr_ref, fi_ref, bur, bui, str_, sti):
    d = pl.program_id(1)
    c = pl.program_id(2)
    tl, nb, dm = h_ref.shape

    @pl.when(c == 0)
    def _():
        str_[...] = s0r_ref[0]
        sti[...] = s0i_ref[0]

    u = _modulate(h_ref[...], g_ref[...], sh_ref[...], sc_ref[...])
    u2 = u.reshape(tl * nb, dm).astype(BF16)
    for cc in range(S5_NCH):
        r = _dot(u2[:, cc * MXU_DIM:(cc + 1) * MXU_DIM], wb_ref[0, cc])
        bur[:, cc * S5_CN:(cc + 1) * S5_CN] = r[:, :S5_CN]
        bui[:, cc * S5_CN:(cc + 1) * S5_CN] = r[:, S5_CN:]

    for lc in range(S5_N // S5_SCAN_W):
        cols = slice(lc * S5_SCAN_W, (lc + 1) * S5_SCAN_W)
        a_r = jnp.broadcast_to(ar_ref[0, :, cols], (nb, S5_SCAN_W))
        a_i = jnp.broadcast_to(ai_ref[0, :, cols], (nb, S5_SCAN_W))

        def body(i, carry, cols=cols, a_r=a_r, a_i=a_i):
            s_r, s_i = carry
            tt = jnp.where(d == 0, i, tl - 1 - i)
            rows = pl.ds(pl.multiple_of(tt * nb, nb), nb)
            n_r = a_r * s_r - a_i * s_i + bur[rows, cols]
            n_i = a_r * s_i + a_i * s_r + bui[rows, cols]
            bur[rows, cols] = n_r
            bui[rows, cols] = n_i
            return n_r, n_i

        s_r, s_i = lax.fori_loop(0, tl, body, (str_[:, cols], sti[:, cols]))
        str_[:, cols] = s_r
        sti[:, cols] = s_i

    for cc in range(S5_NCH):
        cols = slice(cc * S5_CN, (cc + 1) * S5_CN)
        y = _dot(bur[:, cols].astype(BF16), wcr_ref[0, cc]) + _dot(bui[:, cols].astype(BF16), wci_ref[0, cc])
        y_ref[0, :, :, cc * MXU_DIM:(cc + 1) * MXU_DIM] = y.reshape(tl, nb, MXU_DIM)

    @pl.when(c == pl.num_programs(2) - 1)
    def _():
        fr_ref[0] = str_[...]
        fi_ref[0] = sti[...]


def _s5_scan(h_t, shift, scale, norm_g, s0r, s0i, a_r, a_i, wb, wcr, wci):
    length, nbt, dm = h_t.shape
    nb = SUBLANES
    tl = S5_TL
    nch = length // tl

    def tch(d, c):
        return jnp.where(d == 0, c, nch - 1 - c)

    st_spec = pl.BlockSpec((1, nb, S5_N), lambda b, d, c: (d, b, 0))
    a_spec = pl.BlockSpec((1, 1, S5_N), lambda b, d, c: (d, 0, 0))
    return pl.pallas_call(
        _s5_scan_kernel,
        grid=(nbt // nb, 2, nch),
        in_specs=[
            pl.BlockSpec((tl, nb, dm), lambda b, d, c: (tch(d, c), b, 0)),
            pl.BlockSpec((nb, dm), lambda b, d, c: (b, 0)),
            pl.BlockSpec((nb, dm), lambda b, d, c: (b, 0)),
            pl.BlockSpec((1, dm), lambda b, d, c: (0, 0)),
            st_spec, st_spec, a_spec, a_spec,
            pl.BlockSpec((1, S5_NCH, MXU_DIM, 2 * S5_CN), lambda b, d, c: (d, 0, 0, 0)),
            pl.BlockSpec((1, S5_NCH, S5_CN, MXU_DIM), lambda b, d, c: (d, 0, 0, 0)),
            pl.BlockSpec((1, S5_NCH, S5_CN, MXU_DIM), lambda b, d, c: (d, 0, 0, 0)),
        ],
        out_specs=[
            pl.BlockSpec((1, tl, nb, dm), lambda b, d, c: (d, tch(d, c), b, 0)),
            st_spec, st_spec,
        ],
        out_shape=[
            jax.ShapeDtypeStruct((2, length, nbt, dm), F32),
            jax.ShapeDtypeStruct((2, nbt, S5_N), F32),
            jax.ShapeDtypeStruct((2, nbt, S5_N), F32),
        ],
        scratch_shapes=[
            pltpu.VMEM((tl * nb, S5_N), F32), pltpu.VMEM((tl * nb, S5_N), F32),
            pltpu.VMEM((nb, S5_N), F32), pltpu.VMEM((nb, S5_N), F32),
        ],
        compiler_params=_cparams("parallel", "arbitrary", "arbitrary"),
        name="s5_scan",
    )(h_t, shift, scale, norm_g, s0r, s0i, a_r, a_i, wb, wcr, wci)


def _gelu_tanh(x):
    return 0.5 * x * (1.0 + jnp.tanh(0.7978845608028654 * (x + 0.044715 * (x * x * x))))


def _s5_glu_kernel(h_ref, mod_ref, g_ref, dsk_ref, yf_ref, yb_ref, w1_ref, w2_ref, o_ref):
    mod = mod_ref[0]
    h = h_ref[...]
    u = _modulate(h, g_ref[...], mod[0:1], mod[1:2])
    y = u * dsk_ref[...] + (yf_ref[0] + yb_ref[0])
    z = _gelu_tanh(y).astype(BF16)
    out = _dot(z, w1_ref[...]) * _sigmoid(_dot(z, w2_ref[...]))
    o_ref[...] = h + mod[2:3] * out


def _s5_glu(h, seg, mod, norm_g, d_skip, y_t, w1, w2, tm=256):
    n, dm = h.shape
    _, length, nbt, _ = y_t.shape
    y2 = y_t.reshape(2, length, nbt * dm)
    tm = min(tm, length)
    per = length // tm
    rid = seg.mod_row(tm)
    return pl.pallas_call(
        _s5_glu_kernel,
        grid=(n // tm,),
        in_specs=[
            pl.BlockSpec((tm, dm), lambda i: (i, 0)),
            pl.BlockSpec((1, 6, dm), lambda i: (rid(i), 0, 0)),
            pl.BlockSpec((1, dm), lambda i: (0, 0)),
            pl.BlockSpec((1, dm), lambda i: (0, 0)),
            pl.BlockSpec((1, tm, dm), lambda i: (0, i % per, i // per)),
            pl.BlockSpec((1, tm, dm), lambda i: (1, i % per, i // per)),
            pl.BlockSpec((dm, dm), lambda i: (0, 0)),
            pl.BlockSpec((dm, dm), lambda i: (0, 0)),
        ],
        out_specs=pl.BlockSpec((tm, dm), lambda i: (i, 0)),
        out_shape=jax.ShapeDtypeStruct((n, dm), F32),
        compiler_params=_cparams("parallel"),
        name="s5_glu",
    )(h, mod, norm_g, d_skip, y2, y2, w1, w2)


def _s5_weights(lam_re, lam_im, log_step, b_re, b_im, c_re, c_im):
    dt = jnp.exp(log_step)
    mag = jnp.exp(lam_re * dt)
    ab_re = mag * jnp.cos(lam_im * dt)
    ab_im = mag * jnp.sin(lam_im * dt)
    nr = ab_re - 1.0
    den = lam_re * lam_re + lam_im * lam_im
    f_re = (nr * lam_re + ab_im * lam_im) / den
    f_im = (ab_im * lam_re - nr * lam_im) / den
    fb_re = f_re[..., None] * b_re - f_im[..., None] * b_im
    fb_im = f_re[..., None] * b_im + f_im[..., None] * b_re
    eye = jnp.eye(S5_CG, dtype=F32)

    def bd_in(w):
        w = w.reshape(2, S5_NCH, S5_CG, S5_STATE, S5_GROUP)
        m = jnp.einsum('dnjpc,jk->dnjckp', w, eye)
        return m.reshape(2, S5_NCH, MXU_DIM, S5_CN)

    def bd_out(w):
        w = w.reshape(2, S5_NCH, S5_CG, S5_GROUP, S5_STATE)
        m = jnp.einsum('dnjcp,jk->dnjpkc', w, eye)
        return m.reshape(2, S5_NCH, S5_CN, MXU_DIM)

    wb = jnp.concatenate([bd_in(fb_re), bd_in(fb_im)], axis=-1).astype(BF16)
    return (ab_re.reshape(2, 1, S5_N), ab_im.reshape(2, 1, S5_N), wb,
            bd_out(c_re).astype(BF16), bd_out(-c_im).astype(BF16))


def _to_time_major(x, seg, nbt):
    xt = jnp.transpose(x.reshape(seg.nseq, seg.length, -1), (1, 0, 2))
    if nbt > seg.nseq:
        xt = jnp.pad(xt, ((0, 0), (0, nbt - seg.nseq), (0, 0)))
    return xt


def _pad_rows(x, nbt):
    return x if x.shape[0] == nbt else jnp.pad(x, ((0, nbt - x.shape[0]),) + ((0, 0),) * (x.ndim - 1))


def _s5_layer(h, seg, mod, norm_g, s0_re, s0_im, sw, d_skip, w1, w2):
    a_r, a_i, wb, wcr, wci = sw
    nbt = -(-seg.nseq // SUBLANES) * SUBLANES
    h_t = _to_time_major(h, seg, nbt)
    shift = _pad_rows(seg.mod_rows(mod, 0), nbt)
    scale = _pad_rows(seg.mod_rows(mod, 1), nbt)
    pad = ((0, 0), (0, nbt - seg.nseq), (0, 0))
    s0r = jnp.pad(jnp.transpose(s0_re, (1, 0, 2)), pad)
    s0i = jnp.pad(jnp.transpose(s0_im, (1, 0, 2)), pad)
    y_t, f_r, f_i = _s5_scan(h_t, shift, scale, norm_g, s0r, s0i, a_r, a_i, wb, wcr, wci)
    h_new = _s5_glu(h, seg, mod, norm_g, d_skip, y_t, w1, w2)
    return h_new, jnp.transpose(f_r[:, :seg.nseq], (1, 0, 2)), jnp.transpose(f_i[:, :seg.nseq], (1, 0, 2))


def _rwkv_proj_kernel(h_ref, hp_ref, hn_ref, mod_ref, g_ref, mu_ref, wr_ref, wk_ref, wv_ref,
                      wl1_ref, wl2_ref, al1_ref, al2_ref, gl1_ref, gl2_ref, vec_ref, w0_ref, a0_ref, ones_ref,
                      r_ref, v_ref, kk_ref, w_ref, ka_ref, kd_ref, g_out_ref, bonus_ref, *, per):
    i = pl.program_id(0)
    mod = mod_ref[0]
    gn = g_ref[...]
    tm = h_ref.shape[0]
    u = _modulate(h_ref[...], gn, mod[0:1], mod[1:2])
    first = (i % per) == 0
    last = (i % per) == per - 1
    up = _modulate(hp_ref[SUBLANES - 1:SUBLANES, :], gn, mod[0:1], mod[1:2]) * jnp.where(first, 0.0, 1.0)
    un = _modulate(hn_ref[0:1, :], gn, mod[0:1], mod[1:2]) * jnp.where(last, 0.0, 1.0)
    row = lax.broadcasted_iota(jnp.int32, u.shape, 0)
    prev = jnp.where(row == 0, up, pltpu.roll(u, 1, 0))
    nxt = jnp.where(row == tm - 1, un, pltpu.roll(u, tm - 1, 0))
    dx = 0.5 * (prev + nxt) - u
    mu = mu_ref[...]

    def mix(j):
        return (u + dx * mu[j:j + 1]).astype(BF16)

    vec = vec_ref[...]
    r = _dot(mix(0), wr_ref[...])
    k = _dot(mix(2), wk_ref[...])
    v = _dot(mix(3), wv_ref[...])
    ones = ones_ref[...]
    kk = k * vec[0:1]
    kk = kk * lax.rsqrt(_group_sum(kk * kk, ones) + 1e-12)
    g_out_ref[...] = _dot(_sigmoid(_dot(mix(5), gl1_ref[...])).astype(BF16), gl2_ref[...])
    xw = mix(1)
    xa = mix(4)
    kd_sum = None
    for d in range(2):
        wl = _dot(jnp.tanh(_dot(xw, wl1_ref[d])).astype(BF16), wl2_ref[d])
        w_ref[d] = jnp.exp(-RWKV_DECAY_SCALE * _sigmoid(w0_ref[d] + wl))
        al = _dot(_dot(xa, al1_ref[d]).astype(BF16), al2_ref[d])
        a = _sigmoid(a0_ref[d] + al)
        kd = k * (1.0 + (a - 1.0) * vec[1:2])
        kd_ref[d] = kd
        ka_ref[d] = kk * a
        kd_sum = kd if kd_sum is None else kd_sum + kd
    r_ref[...] = r
    v_ref[...] = v
    kk_ref[...] = kk
    bonus_ref[...] = _group_sum(r * kd_sum * vec[2:3], ones) * v


def _rwkv_proj(h, seg, mod, norm_g, mu, wr, wk, wv, wl1, wl2, al1, al2, gl1, gl2, vec, w0, a0, tm=256):
    n, dm = h.shape
    tm = min(tm, seg.length)
    per = seg.length // tm
    rid = seg.mod_row(tm)
    hb = tm // SUBLANES
    nblk = n // SUBLANES
    full = lambda a: pl.BlockSpec(a.shape, lambda i, nd=a.ndim: (0,) * nd)
    tile = pl.BlockSpec((tm, dm), lambda i: (i, 0))
    tile2 = pl.BlockSpec((2, tm, dm), lambda i: (0, i, 0))
    ones = _ones_blockdiag(RWKV_HEAD)
    one = jax.ShapeDtypeStruct((n, dm), F32)
    two = jax.ShapeDtypeStruct((2, n, dm), F32)
    return pl.pallas_call(
        functools.partial(_rwkv_proj_kernel, per=per),
        grid=(n // tm,),
        in_specs=[
            tile,
            pl.BlockSpec((SUBLANES, dm), lambda i: (jnp.maximum(i * hb - 1, 0), 0)),
            pl.BlockSpec((SUBLANES, dm), lambda i: (jnp.minimum((i + 1) * hb, nblk - 1), 0)),
            pl.BlockSpec((1, 6, dm), lambda i: (rid(i), 0, 0)),
            full(norm_g), full(mu), full(wr), full(wk), full(wv), full(wl1), full(wl2), full(al1), full(al2),
            full(gl1), full(gl2), full(vec), full(w0), full(a0), full(ones),
        ],
        out_specs=[tile, tile, tile, tile2, tile2, tile2, tile, tile],
        out_shape=[one, one, one, two, two, two, one, one],
        compiler_params=_cparams("parallel"),
        name="rwkv_proj",
    )(h, h, h, mod, norm_g, mu, wr, wk, wv, wl1, wl2, al1, al2, gl1, gl2, vec, w0, a0, ones)


RWKV_TC = 32


def _rwkv_chain_steps(tc, nv, get, put_y, st):
    def step(i, carry):
        kk, ka, w, kd, r, vblk = get(i)

        def vgroup(v8, carry2):
            base = pl.multiple_of(v8 * SUBLANES, SUBLANES)
            vb = vblk(base)
            rows = []
            for j in range(SUBLANES):
                s = st[base + j]
                sa = jnp.sum(s * kk(), axis=0, keepdims=True)
                s = s * w() + (vb[j:j + 1] * kd() - sa * ka())
                st[base + j] = s
                rows.append(jnp.sum(s * r(), axis=0, keepdims=True))
            put_y(i, base, jnp.concatenate(rows, axis=0))
            return carry2
        return lax.fori_loop(0, nv // SUBLANES, vgroup, carry)

    lax.fori_loop(0, tc, step, 0)


def _rwkv_scan_grid_kernel(kk_ref, r_ref, v_ref, ka_ref, w_ref, kd_ref, s0_ref, y_ref, sf_ref, st):
    d = pl.program_id(0)
    c = pl.program_id(2)
    tc, nv, _ = v_ref.shape

    @pl.when(c == 0)
    def _():
        st[...] = s0_ref[0]

    def get(i):
        t = jnp.where(d == 0, i, tc - 1 - i)
        return (lambda: kk_ref[t], lambda: ka_ref[0, t], lambda: w_ref[0, t], lambda: kd_ref[0, t],
                lambda: r_ref[t], lambda base: v_ref[t, pl.ds(base, SUBLANES), :])

    def put_y(i, base, rows):
        t = jnp.where(d == 0, i, tc - 1 - i)
        y_ref[0, t, pl.ds(base, SUBLANES), :] = rows

    _rwkv_chain_steps(tc, nv, get, put_y, st)

    @pl.when(c == pl.num_programs(2) - 1)
    def _():
        sf_ref[0] = st[...]


def _rwkv_scan_grid(kk, r, v, ka, w, kd, s0):
    n, kdim, c = kk.shape
    tc = RWKV_TC
    nch = n // tc
    tch = lambda d, t: jnp.where(d == 0, t, nch - 1 - t)
    shared = pl.BlockSpec((tc, kdim, LANES), lambda d, l, t: (tch(d, t), 0, l))
    perdir = pl.BlockSpec((1, tc, kdim, LANES), lambda d, l, t: (d, tch(d, t), 0, l))
    sspec = pl.BlockSpec((1, kdim, kdim, LANES), lambda d, l, t: (d, 0, 0, l))
    return pl.pallas_call(
        _rwkv_scan_grid_kernel,
        grid=(2, c // LANES, nch),
        in_specs=[shared, shared, shared, perdir, perdir, perdir, sspec],
        out_specs=[perdir, sspec],
        out_shape=[jax.ShapeDtypeStruct((2, n, kdim, c), F32), jax.ShapeDtypeStruct((2, kdim, kdim, c), F32)],
        scratch_shapes=[pltpu.VMEM((kdim, kdim, LANES), F32)],
        compiler_params=_cparams("parallel", "parallel", "arbitrary"),
        name="rwkv_scan_grid",
    )(kk, r, v, ka, w, kd, s0)


def _rwkv_scan_lanes_kernel(*refs, per_dir):
    fwd = refs[0:6]
    rev = refs[6:12]
    s0_ref, yf_ref, yr_ref, sf_ref, st, mk, mv = refs[12:]
    c = pl.program_id(0)
    tc, nv, _ = fwd[5].shape
    lane = lax.broadcasted_iota(jnp.int32, (1, LANES), 1)
    bwd = (lane // per_dir) % 2 == 1

    @pl.when(c == 0)
    def _():
        st[...] = s0_ref[...]

    def get(i):
        for q in range(5):
            mk[q] = jnp.where(bwd, rev[q][tc - 1 - i], fwd[q][i])
        mv[...] = jnp.where(bwd, rev[5][tc - 1 - i], fwd[5][i])
        return (lambda: mk[0], lambda: mk[1], lambda: mk[2], lambda: mk[3], lambda: mk[4],
                lambda base: mv[pl.ds(base, SUBLANES), :])

    def put_y(i, base, rows):
        yf_ref[i, pl.ds(base, SUBLANES), :] = rows
        yr_ref[tc - 1 - i, pl.ds(base, SUBLANES), :] = rows

    _rwkv_chain_steps(tc, nv, get, put_y, st)

    @pl.when(c == pl.num_programs(0) - 1)
    def _():
        sf_ref[...] = st[...]


def _rwkv_scan_lanes(qs, v, s0, per_dir):
    n, kdim, _ = qs[0].shape
    nv = v.shape[1]
    tc = RWKV_TC
    nch = n // tc
    kf = pl.BlockSpec((tc, kdim, LANES), lambda t: (t, 0, 0))
    kr = pl.BlockSpec((tc, kdim, LANES), lambda t: (nch - 1 - t, 0, 0))
    vf = pl.BlockSpec((tc, nv, LANES), lambda t: (t, 0, 0))
    vr = pl.BlockSpec((tc, nv, LANES), lambda t: (nch - 1 - t, 0, 0))
    sspec = pl.BlockSpec((nv, kdim, LANES), lambda t: (0, 0, 0))
    yshape = jax.ShapeDtypeStruct((n, nv, LANES), F32)
    return pl.pallas_call(
        functools.partial(_rwkv_scan_lanes_kernel, per_dir=per_dir),
        grid=(nch,),
        in_specs=[kf] * 5 + [vf] + [kr] * 5 + [vr] + [sspec],
        out_specs=[vf, vr, sspec],
        out_shape=[yshape, yshape, jax.ShapeDtypeStruct((nv, kdim, LANES), F32)],
        scratch_shapes=[pltpu.VMEM((nv, kdim, LANES), F32), pltpu.VMEM((5, kdim, LANES), F32),
                        pltpu.VMEM((nv, LANES), F32)],
        compiler_params=_cparams("arbitrary"),
        name="rwkv_scan_lanes",
    )(*qs, v, *qs, v, s0)


def _rwkv_out_kernel(h_ref, mod_ref, yf_ref, yb_ref, bonus_ref, g_ref, lng_ref, lnb_ref, wo_ref, ones_ref, o_ref):
    ones = ones_ref[...]
    y = yf_ref[...] + yb_ref[...]
    inv = 1.0 / RWKV_HEAD
    mean = _group_sum(y, ones) * inv
    yc = y - mean
    var = _group_sum(yc * yc, ones) * inv
    yn = yc * lax.rsqrt(var + RWKV_GN_EPS) * lng_ref[...] + lnb_ref[...] + bonus_ref[...]
    out = _dot((yn * g_ref[...]).astype(BF16), wo_ref[...])
    o_ref[...] = h_ref[...] + mod_ref[0][2:3] * out


def _rwkv_out(h, seg, mod, yf, yb, bonus, g, ln_g, ln_b, wo, tm=256):
    n, dm = h.shape
    tm = min(tm, seg.length)
    rid = seg.mod_row(tm)
    tile = pl.BlockSpec((tm, dm), lambda i: (i, 0))
    row = pl.BlockSpec((1, dm), lambda i: (0, 0))
    ones = _ones_blockdiag(RWKV_HEAD)
    return pl.pallas_call(
        _rwkv_out_kernel,
        grid=(n // tm,),
        in_specs=[tile, pl.BlockSpec((1, 6, dm), lambda i: (rid(i), 0, 0)), tile, tile, tile, tile, row, row,
                  pl.BlockSpec((dm, dm), lambda i: (0, 0)), pl.BlockSpec(ones.shape, lambda i: (0, 0))],
        out_specs=tile,
        out_shape=jax.ShapeDtypeStruct((n, dm), F32),
        compiler_params=_cparams("parallel"),
        name="rwkv_out",
    )(h, mod, yf, yb, bonus, g, ln_g, ln_b, wo, ones)


def _rwkv_layer(h, seg, mod, norm_g, s0, rw):
    nseq, length = seg.nseq, seg.length
    hh, nk = RWKV_HEADS, RWKV_HEAD
    r, v, kk, w, ka, kd, g, bonus = _rwkv_proj(h, seg, mod, norm_g, *rw['proj'])
    per_dir = nseq * hh
    n = nseq * length

    def slab(x):
        lead = x.shape[:-2]
        nl = len(lead)
        x = x.reshape(*lead, nseq, length, hh, nk)
        perm = tuple(range(nl)) + (nl + 1, nl + 3, nl + 0, nl + 2)
        return jnp.transpose(x, perm).reshape(*lead, length, nk, per_dir)

    if per_dir % LANES == 0:
        s0_s = jnp.transpose(s0, (1, 3, 4, 0, 2)).reshape(2, nk, nk, per_dir)
        y_s, sf = _rwkv_scan_grid(slab(kk), slab(r), slab(v), slab(ka), slab(w), slab(kd), s0_s)
        y2 = jnp.transpose(y_s.reshape(2, length, nk, nseq, hh), (0, 3, 1, 4, 2)).reshape(2, n, hh * nk)
        yf, yb = y2[0], y2[1]
        s_fin = jnp.transpose(sf.reshape(2, nk, nk, nseq, hh), (3, 0, 4, 1, 2))
    else:
        vs = LANES // (2 * per_dir)
        assert vs * 2 * per_dir == LANES
        nv = nk // vs
        shared = lambda x: jnp.concatenate([slab(x)] * (2 * vs), axis=-1)
        dirs = lambda x: jnp.concatenate([slab(x[0]), slab(x[1])] * vs, axis=-1)
        v_s = jnp.transpose(slab(v).reshape(length, vs, nv, 1, per_dir), (0, 2, 1, 3, 4))
        v_s = jnp.broadcast_to(v_s, (length, nv, vs, 2, per_dir)).reshape(length, nv, LANES)
        s0_s = jnp.transpose(s0.reshape(nseq, 2, hh, vs, nv, nk), (4, 5, 3, 1, 0, 2)).reshape(nv, nk, LANES)
        y_f, y_r, sf = _rwkv_scan_lanes((shared(kk), dirs(ka), dirs(w), dirs(kd), shared(r)), v_s, s0_s, per_dir)

        def nat(y, d):
            y = y.reshape(length, nv, vs, 2, nseq, hh)[:, :, :, d]
            return jnp.transpose(y, (3, 0, 4, 2, 1)).reshape(n, hh * nk)

        yf, yb = nat(y_f, 0), nat(y_r, 1)
        s_fin = jnp.transpose(sf.reshape(nv, nk, vs, 2, nseq, hh), (4, 3, 5, 2, 0, 1)).reshape(nseq, 2, hh, nk, nk)
    h_new = _rwkv_out(h, seg, mod, yf, yb, bonus, g, *rw['out'])
    return h_new, s_fin


def _rope_tables(length, half, first_lane, period):
    lane = jnp.arange(LANES) % period - first_lane
    active = (lane >= 0) & (lane < 4 * half)
    quarter = jnp.clip(lane, 0, 4 * half - 1) // half
    freq = ROPE_BASE ** (-(jnp.clip(lane, 0, 4 * half - 1) % half).astype(F32) / half)
    t = jnp.arange(length)
    pos = jnp.where(quarter[None, :] < 2, (t // GRID_W)[:, None], (t % GRID_W)[:, None]).astype(F32)
    ang = pos * freq[None, :]
    cos = jnp.where(active[None], jnp.cos(ang), 1.0)
    sin = jnp.where(active[None], jnp.sin(ang), 0.0)
    first = (quarter % 2 == 0)[None]
    return cos, jnp.where(first, -sin, 0.0), jnp.where(first, 0.0, sin)


def _rope(x, cos, sa, sb, half):
    w = x.shape[-1]
    rep = w // LANES
    tile = (lambda a: a) if rep == 1 else (lambda a: jnp.concatenate([a] * rep, axis=-1))
    return x * tile(cos) + pltpu.roll(x, w - half, 1) * tile(sa) + pltpu.roll(x, half, 1) * tile(sb)


def _attn_kernel(*refs, groups, dq, dv, hpb, scale, band, has_ctx, has_sink):
    it = iter(refs)
    q_ref, k_ref, v_ref = next(it), next(it), next(it)
    kc_ref, vc_ref = (next(it), next(it)) if has_ctx else (None, None)
    sink_ref = next(it) if has_sink else None
    o_ref = next(it)
    tq = q_ref.shape[1]
    lk = k_ref.shape[1]
    rows = groups * tq
    if band:
        lw = min(lk, tq + 2 * WINDOW)
        q0 = pl.program_id(2) * tq
        start = pl.multiple_of(jnp.clip(q0 - WINDOW, 0, lk - lw), LANES)
        ksl = pl.ds(start, lw)
        qpos = q0 + (lax.broadcasted_iota(jnp.int32, (rows, lw), 0) & (tq - 1))
        kpos = start + lax.broadcasted_iota(jnp.int32, (rows, lw), 1)
        valid = jnp.abs(qpos - kpos) <= WINDOW
    else:
        ksl = slice(None)
    for j in range(hpb):
        kh = k_ref[0, ksl, j * dq:(j + 1) * dq].astype(BF16)
        vh = v_ref[0, ksl, j * dv:(j + 1) * dv].astype(BF16)
        heads = range(j * groups, (j + 1) * groups)
        qs = [q_ref[0, :, hq * dq:(hq + 1) * dq].astype(BF16) for hq in heads]
        qh = qs[0] if groups == 1 else jnp.concatenate(qs, axis=0)
        s = _dot_nt(qh, kh) * scale
        if band:
            s = jnp.where(valid, s, -jnp.inf)
        m = jnp.max(s, axis=-1, keepdims=True)
        if has_ctx:
            kch = kc_ref[0, :, j * dq:(j + 1) * dq].astype(BF16)
            vch = vc_ref[0, :, j * dv:(j + 1) * dv].astype(BF16)
            sc = _dot_nt(qh, kch) * scale
            m = jnp.maximum(m, jnp.max(sc, axis=-1, keepdims=True))
        if has_sink:
            sk = [jnp.broadcast_to(sink_ref[0, :, hq:hq + 1], (tq, 1)) for hq in heads]
            snk = sk[0] if groups == 1 else jnp.concatenate(sk, axis=0)
            m = jnp.maximum(m, snk)
        p = jnp.exp(s - m)
        den = jnp.sum(p, axis=-1, keepdims=True)
        acc = _dot(p.astype(BF16), vh)
        if has_ctx:
            pc = jnp.exp(sc - m)
            den = den + jnp.sum(pc, axis=-1, keepdims=True)
            acc = acc + _dot(pc.astype(BF16), vch)
        if has_sink:
            den = den + jnp.exp(snk - m)
        out = acc / den
        for g, hq in enumerate(heads):
            o_ref[0, :, hq * dv:(hq + 1) * dv] = out[g * tq:(g + 1) * tq]


def _attention(q, k, v, ctx, sink, *, heads, groups, dq, dv, scale, band, tq=256):
    b, lq, _ = q.shape
    lk = k.shape[1]
    hpb = 2
    nhb = heads // hpb
    ins = [q, k, v]
    specs = [
        pl.BlockSpec((1, tq, hpb * groups * dq), lambda bi, hb, i: (bi, i, hb)),
        pl.BlockSpec((1, lk, hpb * dq), lambda bi, hb, i: (bi, 0, hb)),
        pl.BlockSpec((1, lk, hpb * dv), lambda bi, hb, i: (bi, 0, hb)),
    ]
    if ctx is not None:
        lc = ctx[0].shape[1]
        ins += list(ctx)
        specs += [pl.BlockSpec((1, lc, hpb * dq), lambda bi, hb, i: (bi, 0, hb)),
                  pl.BlockSpec((1, lc, hpb * dv), lambda bi, hb, i: (bi, 0, hb))]
    if sink is not None:
        sk = jnp.pad(sink.reshape(nhb, 1, hpb * groups), ((0, 0), (0, 0), (0, LANES - hpb * groups)))
        ins.append(sk)
        specs.append(pl.BlockSpec((1, 1, LANES), lambda bi, hb, i: (hb, 0, 0)))
    return pl.pallas_call(
        functools.partial(_attn_kernel, groups=groups, dq=dq, dv=dv, hpb=hpb, scale=scale, band=band,
                          has_ctx=ctx is not None, has_sink=sink is not None),
        grid=(b, nhb, lq // tq),
        in_specs=specs,
        out_specs=pl.BlockSpec((1, tq, hpb * groups * dv), lambda bi, hb, i: (bi, i, hb)),
        out_shape=jax.ShapeDtypeStruct((b, lq, heads * groups * dv), F32),
        compiler_params=_cparams("parallel", "parallel", "arbitrary"),
        name="attention",
    )(*ins)


def _proj_res_kernel(h_ref, mod_ref, a_ref, w_ref, o_ref):
    o_ref[...] = h_ref[...] + mod_ref[0][2:3] * _dot(a_ref[...].astype(BF16), w_ref[...])


def _proj_residual(h, seg, mod, a, w, tm=256):
    n, dm = h.shape
    tm = min(tm, seg.length)
    rid = seg.mod_row(tm)
    return pl.pallas_call(
        _proj_res_kernel,
        grid=(n // tm,),
        in_specs=[pl.BlockSpec((tm, dm), lambda i: (i, 0)), pl.BlockSpec((1, 6, dm), lambda i: (rid(i), 0, 0)),
                  pl.BlockSpec((tm, a.shape[1]), lambda i: (i, 0)), pl.BlockSpec(w.shape, lambda i: (0, 0))],
        out_specs=pl.BlockSpec((tm, dm), lambda i: (i, 0)),
        out_shape=jax.ShapeDtypeStruct((n, dm), F32),
        compiler_params=_cparams("parallel"),
        name="proj_residual",
    )(h, mod, a, w)


GQA_NQ = GQA_HEADS * GQA_HEAD_DIM
GQA_NK = GQA_KV_HEADS * GQA_HEAD_DIM
GQA_ROPE_HALF = GQA_HEAD_DIM // 4


def _gqa_qkv_kernel(*refs, use_rope):
    if use_rope:
        h_ref, mod_ref, g_ref, w_ref, gq_ref, gk_ref, ones_ref, cos_ref, sa_ref, sb_ref, q_ref, k_ref, v_ref = refs
    else:
        h_ref, mod_ref, g_ref, w_ref, gq_ref, gk_ref, ones_ref, q_ref, k_ref, v_ref = refs
    mod = mod_ref[0]
    u = _modulate(h_ref[...], g_ref[...], mod[0:1], mod[1:2]).astype(BF16)
    qkv = _dot(u, w_ref[...])
    ones = ones_ref[...]
    inv = 1.0 / GQA_HEAD_DIM
    q = qkv[:, :GQA_NQ]
    k = qkv[:, GQA_NQ:GQA_NQ + GQA_NK]
    q = q * lax.rsqrt(_group_sum(q * q, ones) * inv + NORM_EPS) * gq_ref[...]
    k = k * lax.rsqrt(_group_sum(k * k, ones) * inv + NORM_EPS) * gk_ref[...]
    if use_rope:
        cos, sa, sb = cos_ref[...], sa_ref[...], sb_ref[...]
        q = _rope(q, cos, sa, sb, GQA_ROPE_HALF)
        k = _rope(k, cos, sa, sb, GQA_ROPE_HALF)
    q_ref[...] = q
    k_ref[...] = k
    v_ref[...] = qkv[:, GQA_NQ + GQA_NK:]


def _gqa_qkv(h, seg, mod, norm_g, w_qkv, gq, gk, rope, tm=256):
    n, dm = h.shape
    tm = min(tm, seg.length)
    per = seg.length // tm
    rid = seg.mod_row(tm)
    ones = _ones_blockdiag(GQA_HEAD_DIM)
    full = lambda a: pl.BlockSpec(a.shape, lambda i, nd=a.ndim: (0,) * nd)
    ins = [h, mod, norm_g, w_qkv, gq, gk, ones]
    specs = [pl.BlockSpec((tm, dm), lambda i: (i, 0)), pl.BlockSpec((1, 6, dm), lambda i: (rid(i), 0, 0)),
             full(norm_g), full(w_qkv), full(gq), full(gk), full(ones)]
    if rope is not None:
        ins += list(rope)
        specs += [pl.BlockSpec((tm, LANES), lambda i: (i % per, 0))] * 3
    return pl.pallas_call(
        functools.partial(_gqa_qkv_kernel, use_rope=rope is not None),
        grid=(n // tm,),
        in_specs=specs,
        out_specs=[pl.BlockSpec((tm, GQA_NQ), lambda i: (i, 0)), pl.BlockSpec((tm, GQA_NK), lambda i: (i, 0)),
                   pl.BlockSpec((tm, GQA_NK), lambda i: (i, 0))],
        out_shape=[jax.ShapeDtypeStruct((n, GQA_NQ), F32), jax.ShapeDtypeStruct((n, GQA_NK), F32),
                   jax.ShapeDtypeStruct((n, GQA_NK), F32)],
        compiler_params=_cparams("parallel"),
        name="gqa_qkv",
    )(*ins)


def _gqa_layer(h, seg, mod, norm_g, gw, ctx):
    w_qkv, gq, gk, sink, w_o = gw
    latent = ctx is not None
    rope = _rope_tables(seg.length, GQA_ROPE_HALF, 0, 4 * GQA_ROPE_HALF) if latent else None
    q, k, v = _gqa_qkv(h, seg, mod, norm_g, w_qkv, gq, gk, rope)
    b3 = lambda a: a.reshape(seg.nseq, seg.length, a.shape[-1])
    att = _attention(b3(q), b3(k), b3(v), ctx, sink, heads=GQA_KV_HEADS, groups=GQA_HEADS // GQA_KV_HEADS,
                     dq=GQA_HEAD_DIM, dv=GQA_HEAD_DIM, scale=GQA_HEAD_DIM ** -0.5, band=latent)
    h_new = _proj_residual(h, seg, mod, att.reshape(seg.n, -1), w_o)
    return h_new, k, v


MLA_QW = MLA_HEADS * MLA_PAD
MLA_VW = MLA_HEADS * MLA_V
MLA_CKR = MLA_KV_LORA + LANES
MLA_ROPE_HALF = MLA_ROPE // 4


def _mla_expand(ckv, kr, wuk_ref, wuv_ref, kg_ref, ones_ref):
    cb = ckv.astype(BF16)
    kn = _dot(cb, wuk_ref[...])
    kr_t = jnp.concatenate([pltpu.roll(kr, MLA_NOPE, 1)] * MLA_HEADS, axis=-1)
    k = kn + kr_t
    k = k * lax.rsqrt(_group_sum(k * k, ones_ref[...]) * (1.0 / MLA_QK) + NORM_EPS) * kg_ref[...]
    return k, _dot(cb, wuv_ref[...])


def _mla_proj_kernel(*refs, use_rope):
    (h_ref, mod_ref, g_ref, wdq_ref, qlg_ref, wuq_ref, qg_ref, wdkv_ref, kvg_ref, wuk_ref, wuv_ref, kg_ref,
     ones_ref) = refs[:13]
    if use_rope:
        cos_ref, sa_ref, sb_ref = refs[13:16]
    q_ref, k_ref, v_ref, ckv_ref, kr_ref = refs[-5:]
    mod = mod_ref[0]
    u = _modulate(h_ref[...], g_ref[...], mod[0:1], mod[1:2]).astype(BF16)
    qd = _dot(u, wdq_ref[...])
    qd = qd * lax.rsqrt(jnp.mean(qd * qd, axis=-1, keepdims=True) + NORM_EPS) * qlg_ref[...]
    q = _dot(qd.astype(BF16), wuq_ref[...])
    q = q * lax.rsqrt(_group_sum(q * q, ones_ref[...]) * (1.0 / MLA_QK) + NORM_EPS) * qg_ref[...]
    ckr = _dot(u, wdkv_ref[...])
    ckv = ckr[:, :MLA_KV_LORA]
    ckv = ckv * lax.rsqrt(jnp.mean(ckv * ckv, axis=-1, keepdims=True) + NORM_EPS) * kvg_ref[...]
    kr = ckr[:, MLA_KV_LORA:]
    k, v = _mla_expand(ckv, kr, wuk_ref, wuv_ref, kg_ref, ones_ref)
    if use_rope:
        cos, sa, sb = cos_ref[...], sa_ref[...], sb_ref[...]
        q = _rope(q, cos, sa, sb, MLA_ROPE_HALF)
        k = _rope(k, cos, sa, sb, MLA_ROPE_HALF)
    q_ref[...] = q.astype(BF16)
    k_ref[...] = k.astype(BF16)
    v_ref[...] = v.astype(BF16)
    ckv_ref[...] = ckv
    kr_ref[...] = kr


def _mla_proj(h, seg, mod, norm_g, pw, rope, tm=256):
    n, dm = h.shape
    tm = min(tm, seg.length)
    per = seg.length // tm
    rid = seg.mod_row(tm)
    ones = _ones_blockdiag(MLA_PAD)
    full = lambda a: pl.BlockSpec(a.shape, lambda i, nd=a.ndim: (0,) * nd)
    ins = [h, mod, norm_g, *pw, ones]
    specs = [pl.BlockSpec((tm, dm), lambda i: (i, 0)), pl.BlockSpec((1, 6, dm), lambda i: (rid(i), 0, 0)),
             full(norm_g)] + [full(a) for a in pw] + [full(ones)]
    if rope is not None:
        ins += list(rope)
        specs += [pl.BlockSpec((tm, LANES), lambda i: (i % per, 0))] * 3
    widths = (MLA_QW, MLA_QW, MLA_VW, MLA_KV_LORA, LANES)
    return pl.pallas_call(
        functools.partial(_mla_proj_kernel, use_rope=rope is not None),
        grid=(n // tm,),
        in_specs=specs,
        out_specs=[pl.BlockSpec((tm, w), lambda i: (i, 0)) for w in widths],
        out_shape=[jax.ShapeDtypeStruct((n, w), BF16 if j < 3 else F32) for j, w in enumerate(widths)],
        compiler_params=_cparams("parallel"),
        name="mla_proj",
    )(*ins)


def _mla_ctx_kernel(ckv_ref, kr_ref, wuk_ref, wuv_ref, kg_ref, ones_ref, k_ref, v_ref):
    k, v = _mla_expand(ckv_ref[...], kr_ref[...], wuk_ref, wuv_ref, kg_ref, ones_ref)
    k_ref[...] = k.astype(BF16)
    v_ref[...] = v.astype(BF16)


def _mla_ctx_expand(ckv, kr, wuk, wuv, kg, tm=256):
    n = ckv.shape[0]
    tm = min(tm, n)
    ones = _ones_blockdiag(MLA_PAD)
    full = lambda a: pl.BlockSpec(a.shape, lambda i, nd=a.ndim: (0,) * nd)
    return pl.pallas_call(
        _mla_ctx_kernel,
        grid=(n // tm,),
        in_specs=[pl.BlockSpec((tm, MLA_KV_LORA), lambda i: (i, 0)), pl.BlockSpec((tm, LANES), lambda i: (i, 0)),
                  full(wuk), full(wuv), full(kg), full(ones)],
        out_specs=[pl.BlockSpec((tm, MLA_QW), lambda i: (i, 0)), pl.BlockSpec((tm, MLA_VW), lambda i: (i, 0))],
        out_shape=[jax.ShapeDtypeStruct((n, MLA_QW), BF16), jax.ShapeDtypeStruct((n, MLA_VW), BF16)],
        compiler_params=_cparams("parallel"),
        name="mla_ctx_expand",
    )(ckv, kr, wuk, wuv, kg, ones)


def _mla_weights(w_dq, q_lora_g, w_uq, w_dkv, kv_g, w_ukv, q_g, k_g, w_o):
    padh = lambda a: jnp.pad(a, [(0, 0)] * (a.ndim - 1) + [(0, MLA_PAD - a.shape[-1])])
    wuq = padh(w_uq.reshape(MLA_Q_LORA, MLA_HEADS, MLA_QK)).reshape(MLA_Q_LORA, MLA_QW).astype(BF16)
    wdkv = jnp.pad(w_dkv, ((0, 0), (0, MLA_CKR - w_dkv.shape[1]))).astype(BF16)
    ukv = w_ukv.reshape(MLA_KV_LORA, MLA_HEADS, MLA_NOPE + MLA_V)
    wuk = padh(ukv[:, :, :MLA_NOPE]).reshape(MLA_KV_LORA, MLA_QW).astype(BF16)
    wuv = ukv[:, :, MLA_NOPE:].reshape(MLA_KV_LORA, MLA_VW).astype(BF16)
    tile_h = lambda g: jnp.tile(padh(g), MLA_HEADS)[None]
    proj = (w_dq.astype(BF16), q_lora_g[None], wuq, tile_h(q_g), wdkv, kv_g[None], wuk, wuv, tile_h(k_g))
    return proj, w_o.astype(BF16)


def _mla_layer(h, seg, mod, norm_g, mw, cache):
    proj, w_o = mw
    latent = cache is not None
    rope = _rope_tables(seg.length, MLA_ROPE_HALF, MLA_NOPE, MLA_PAD) if latent else None
    q, k, v, ckv, kr = _mla_proj(h, seg, mod, norm_g, proj, rope)
    ctx = None
    if latent:
        c_ckv, c_kr = cache
        lc = c_ckv.shape[1]
        kr_p = jnp.pad(c_kr.reshape(-1, MLA_ROPE), ((0, 0), (0, LANES - MLA_ROPE)))
        kc, vc = _mla_ctx_expand(c_ckv.reshape(-1, MLA_KV_LORA), kr_p, proj[6], proj[7], proj[8])
        ctx = (kc.reshape(seg.nseq, lc, MLA_QW), vc.reshape(seg.nseq, lc, MLA_VW))
    b3 = lambda a: a.reshape(seg.nseq, seg.length, a.shape[-1])
    att = _attention(b3(q), b3(k), b3(v), ctx, None, heads=MLA_HEADS, groups=1, dq=MLA_PAD, dv=MLA_V,
                     scale=MLA_QK ** -0.5, band=False)
    h_new = _proj_residual(h, seg, mod, att.reshape(seg.n, -1), w_o)
    return h_new, ckv, kr[:, :MLA_ROPE]


def kernel(x_prompt, x_sample, state_s5_re, state_s5_im, state_rwkv, cache_gqa_k, cache_gqa_v, cache_mla_ckv, cache_mla_krope, c, c_ctx, ada_w, ada_b, norm1_g, norm2_g, s5_lambda_re, s5_lambda_im, s5_log_step, s5_b_re, s5_b_im, s5_c_re, s5_c_im, s5_d, s5_glu_w1, s5_glu_w2, rwkv_mu, rwkv_w_r, rwkv_w_k, rwkv_w_v, rwkv_w_o, rwkv_w0, rwkv_w_l1, rwkv_w_l2, rwkv_a0, rwkv_a_l1, rwkv_a_l2, rwkv_g_l1, rwkv_g_l2, rwkv_k_k, rwkv_k_a, rwkv_r_k, rwkv_ln_g, rwkv_ln_b, gqa_w_qkv, gqa_q_norm, gqa_k_norm, gqa_sink, gqa_w_o, mla_w_dq, mla_q_lora_norm, mla_w_uq, mla_w_dkv, mla_kv_norm, mla_w_ukv, mla_q_norm, mla_k_norm, mla_w_o, moe_w_group, moe_b_group, moe_w_expert, moe_b_expert, moe_w_gate, moe_w_up, moe_w_down):
    bp, lp, dm = x_prompt.shape
    bs, ls, _ = x_sample.shape
    depth = ada_w.shape[0]
    assert dm == D_MODEL and 1 + bs <= MOD_ROWS
    seg_p = _Seg(bp, lp, 0, True)
    seg_s = _Seg(bs, ls, 1, False)
    bf = lambda a: a.astype(BF16)

    cond = jnp.zeros((MOD_ROWS, dm), F32).at[0].set(c_ctx).at[1:1 + bs].set(c)
    mods = _ada_all(cond, ada_w, ada_b).reshape(depth, MOD_ROWS, 6, dm)

    hp = x_prompt.reshape(bp * lp, dm)
    hs = x_sample.reshape(bs * ls, dm)
    outs = {}
    for layer in range(depth):
        kind = layer % 4
        mod = mods[layer]
        g1 = norm1_g[layer][None]
        if kind == 0:
            sw = _s5_weights(s5_lambda_re, s5_lambda_im, s5_log_step, s5_b_re, s5_b_im, s5_c_re, s5_c_im)
            w1, w2, dsk = bf(s5_glu_w1), bf(s5_glu_w2), s5_d[None]
            zeros = jnp.zeros((bp, 2, S5_N), F32)
            hp, f_re, f_im = _s5_layer(hp, seg_p, mod, g1, zeros, zeros, sw, dsk, w1, w2)
            hs, _, _ = _s5_layer(hs, seg_s, mod, g1, state_s5_re.reshape(bs, 2, S5_N),
                                 state_s5_im.reshape(bs, 2, S5_N), sw, dsk, w1, w2)
            outs['s5_re'] = f_re.reshape(bp, 2, S5_GROUPS, S5_STATE)
            outs['s5_im'] = f_im.reshape(bp, 2, S5_GROUPS, S5_STATE)
        elif kind == 1:
            vec = jnp.zeros((SUBLANES, dm), F32).at[0].set(rwkv_k_k).at[1].set(rwkv_k_a).at[2].set(rwkv_r_k.reshape(-1))
            rw = dict(
                proj=(rwkv_mu, bf(rwkv_w_r), bf(rwkv_w_k), bf(rwkv_w_v), bf(rwkv_w_l1), bf(rwkv_w_l2),
                      bf(rwkv_a_l1), bf(rwkv_a_l2), bf(rwkv_g_l1), bf(rwkv_g_l2), vec,
                      rwkv_w0[:, None, :], rwkv_a0[:, None, :]),
                out=(rwkv_ln_g[None], rwkv_ln_b[None], bf(rwkv_w_o)))
            zeros = jnp.zeros((bp, 2, RWKV_HEADS, RWKV_HEAD, RWKV_HEAD), F32)
            hp, outs['rwkv'] = _rwkv_layer(hp, seg_p, mod, g1, zeros, rw)
            hs, _ = _rwkv_layer(hs, seg_s, mod, g1, state_rwkv, rw)
        elif kind == 2:
            gw = (bf(gqa_w_qkv), jnp.tile(gqa_q_norm, GQA_HEADS)[None], jnp.tile(gqa_k_norm, GQA_KV_HEADS)[None],
                  gqa_sink, bf(gqa_w_o))
            hp, kp, vp = _gqa_layer(hp, seg_p, mod, g1, gw, None)
            outs['gqa_k'] = kp.reshape(bp, lp, GQA_KV_HEADS, GQA_HEAD_DIM)
            outs['gqa_v'] = vp.reshape(bp, lp, GQA_KV_HEADS, GQA_HEAD_DIM)
            lc = cache_gqa_k.shape[1]
            ctx = (cache_gqa_k.reshape(bs, lc, GQA_NK), cache_gqa_v.reshape(bs, lc, GQA_NK))
            hs, _, _ = _gqa_layer(hs, seg_s, mod, g1, gw, ctx)
        else:
            mw = _mla_weights(mla_w_dq, mla_q_lora_norm, mla_w_uq, mla_w_dkv, mla_kv_norm, mla_w_ukv, mla_q_norm,
                              mla_k_norm, mla_w_o)
            hp, ckv_p, kr_p = _mla_layer(hp, seg_p, mod, g1, mw, None)
            outs['mla_ckv'] = ckv_p.reshape(bp, lp, MLA_KV_LORA)
            outs['mla_kr'] = kr_p.reshape(bp, lp, MLA_ROPE)
            hs, _, _ = _mla_layer(hs, seg_s, mod, g1, mw, (cache_mla_ckv, cache_mla_krope))
        wr = jnp.zeros((dm, ROUTE_W), F32).at[:, :MOE_GROUPS].set(moe_w_group[layer])
        wr = wr.at[:, MOE_GROUPS:MOE_GROUPS + MOE_EXPERTS].set(moe_w_expert[layer])
        br = jnp.zeros((1, ROUTE_W), F32).at[0, :MOE_GROUPS].set(moe_b_group[layer])
        br = br.at[0, MOE_GROUPS:MOE_GROUPS + MOE_EXPERTS].set(moe_b_expert[layer])
        wr_hi, wr_lo = _hi_lo(wr)
        mo = (mod, norm2_g[layer][None], wr_hi, wr_lo, br, bf(moe_w_gate[layer]), bf(moe_w_up[layer]),
              bf(moe_w_down[layer]))
        hp = _moe_layer(hp, seg_p, *mo)
        hs = _moe_layer(hs, seg_s, *mo)
    return (hp.reshape(bp, lp, dm), hs.reshape(bs, ls, dm), outs['s5_re'], outs['s5_im'], outs['rwkv'],
            outs['gqa_k'], outs['gqa_v'], outs['mla_ckv'], outs['mla_kr'])
```

```python
import functools
import math

import jax
import jax.numpy as jnp
from jax import lax
from jax.experimental import pallas as pl
from jax.experimental.pallas import tpu as pltpu

F32 = jnp.float32
BF16 = jnp.bfloat16

D_MODEL = 1024
NORM_EPS = 1e-6
ROPE_BASE = 10000.0
GRID_W = 64
S5_GROUP = 16
S5_GROUPS = D_MODEL // S5_GROUP
S5_STATE = 64
S5_N = S5_GROUPS * S5_STATE
RWKV_HEAD = 64
RWKV_HEADS = D_MODEL // RWKV_HEAD
RWKV_DECAY_SCALE = math.exp(-0.5)
RWKV_GN_EPS = 64e-5
GQA_HEADS = 16
GQA_KV_HEADS = 4
GQA_HEAD_DIM = 64
WINDOW = 128
MLA_HEADS = 16
MLA_Q_LORA = 384
MLA_KV_LORA = 256
MLA_NOPE = 64
MLA_ROPE = 32
MLA_QK = MLA_NOPE + MLA_ROPE
MLA_V = 64
MLA_PAD = 128
MOE_GROUPS = 4
MOE_PER_GROUP = 4
MOE_EXPERTS = 16
MOE_HIDDEN = 512

LANES = 128
SUBLANES = 8
MXU_DIM = 256
VMEM_LIMIT = 56 * 1024 * 1024
MOD_ROWS = 8


def _cparams(*sem):
    return pltpu.CompilerParams(dimension_semantics=sem, vmem_limit_bytes=VMEM_LIMIT)


def _dot(a, b):
    return jnp.dot(a, b, preferred_element_type=F32)


def _dot_nt(a, b):
    return lax.dot_general(a, b, (((1,), (1,)), ((), ())), preferred_element_type=F32)


def _split3(x):
    hi = x.astype(BF16)
    r1 = x - hi.astype(F32)
    mid = r1.astype(BF16)
    lo = (r1 - mid.astype(F32)).astype(BF16)
    return hi, mid, lo


def _dot_x3(x, w_hi, w_lo):
    hi, mid, _ = _split3(x)
    return _dot(hi, w_hi) + (_dot(mid, w_hi) + _dot(hi, w_lo))


def _group_sum(x, ones_bd):
    n = x.shape[-1]
    outs = []
    for c in range(n // MXU_DIM):
        hi, mid, lo = _split3(x[:, c * MXU_DIM:(c + 1) * MXU_DIM])
        outs.append(_dot(hi, ones_bd) + (_dot(mid, ones_bd) + _dot(lo, ones_bd)))
    return outs[0] if len(outs) == 1 else jnp.concatenate(outs, axis=-1)


def _modulate(x, g, shift, scale):
    ms = jnp.mean(x * x, axis=-1, keepdims=True)
    return x * lax.rsqrt(ms + NORM_EPS) * g * (1.0 + scale) + shift


def _sigmoid(x):
    return 1.0 / (1.0 + jnp.exp(-x))


def _silu(x):
    return x * _sigmoid(x)


def _ones_blockdiag(group):
    i = jnp.arange(MXU_DIM) // group
    return (i[:, None] == i[None, :]).astype(BF16)


def _hi_lo(w):
    hi = w.astype(BF16)
    return hi, (w - hi.astype(F32)).astype(BF16)


class _Seg:
    def __init__(self, nseq, length, mod0, shared_mod):
        self.nseq, self.length, self.mod0, self.shared_mod = nseq, length, mod0, shared_mod
        self.n = nseq * length

    def mod_row(self, tm):
        if self.shared_mod:
            return lambda i: self.mod0
        per = self.length // tm
        return lambda i: self.mod0 + i // per

    def mod_rows(self, mod, k):
        if self.shared_mod:
            return jnp.broadcast_to(mod[self.mod0, k][None], (self.nseq, mod.shape[-1]))
        return mod[self.mod0:self.mod0 + self.nseq, k]


def _ada_kernel(cond_ref, w_ref, b_ref, o_ref):
    s = _silu(cond_ref[...])
    w = w_ref[0]
    w_hi = w.astype(BF16)
    w_lo = (w - w_hi.astype(F32)).astype(BF16)
    o_ref[0] = _dot_x3(s, w_hi, w_lo) + b_ref[0]


def _ada_all(cond, ada_w, ada_b):
    depth, d, n6 = ada_w.shape
    tn = 1536
    return pl.pallas_call(
        _ada_kernel,
        grid=(depth, n6 // tn),
        in_specs=[
            pl.BlockSpec((MOD_ROWS, d), lambda l, j: (0, 0)),
            pl.BlockSpec((1, d, tn), lambda l, j: (l, 0, j)),
            pl.BlockSpec((1, 1, tn), lambda l, j: (l, 0, j)),
        ],
        out_specs=pl.BlockSpec((1, MOD_ROWS, tn), lambda l, j: (l, 0, j)),
        out_shape=jax.ShapeDtypeStruct((depth, MOD_ROWS, n6), F32),
        compiler_params=_cparams("parallel", "parallel"),
        name="ada_mod",
    )(cond, ada_w, ada_b.reshape(depth, 1, n6))


ROUTE_W = LANES


def _route(logits):
    lane = lax.broadcasted_iota(jnp.int32, logits.shape, 1)
    neg = jnp.float32(-jnp.inf)
    big = jnp.int32(ROUTE_W)
    gmask = lane < MOE_GROUPS
    gl = jnp.where(gmask, logits, neg)
    gmax = jnp.max(gl, axis=-1, keepdims=True)
    g_top = jnp.min(jnp.where(gl == gmax, lane, big), axis=-1, keepdims=True)
    p_g = 1.0 / jnp.sum(jnp.where(gmask, jnp.exp(gl - gmax), 0.0), axis=-1, keepdims=True)
    lo = MOE_GROUPS + MOE_PER_GROUP * g_top
    emask = (lane >= lo) & (lane < lo + MOE_PER_GROUP)
    el = jnp.where(emask, logits, neg)
    m1 = jnp.max(el, axis=-1, keepdims=True)
    i1 = jnp.min(jnp.where(el == m1, lane, big), axis=-1, keepdims=True)
    el2 = jnp.where(lane == i1, neg, el)
    m2 = jnp.max(el2, axis=-1, keepdims=True)
    i2 = jnp.min(jnp.where(el2 == m2, lane, big), axis=-1, keepdims=True)
    e2 = jnp.exp(m2 - m1)
    w1 = p_g / (1.0 + e2)
    w2 = p_g * e2 / (1.0 + e2)
    gates = jnp.where(lane == i1, w1, 0.0) + jnp.where(lane == i2, w2, 0.0)
    return gates, g_top


def _lane_col(x, idx):
    lane = lax.broadcasted_iota(jnp.int32, x.shape, 1)
    return jnp.sum(jnp.where(lane == idx, x, 0.0), axis=-1, keepdims=True)


def _moe_kernel(h_ref, mod_ref, g_ref, wr_hi_ref, wr_lo_ref, br_ref, tri_ref, wg_ref, wu_ref, wd_ref, o_ref,
                v_scr, gate_scr, pos_scr, post_scr, x_scr, gx_scr, y_scr, acc_scr, cnt_scr, *, rows):
    e = pl.program_id(1)
    grp = e // MOE_PER_GROUP
    tm = v_scr.shape[0]
    nchunk = x_scr.shape[0] // rows

    @pl.when(e == 0)
    def _():
        mod = mod_ref[0]
        v = _modulate(h_ref[...], g_ref[...], mod[3:4], mod[4:5])
        logits = _dot_x3(v, wr_hi_ref[...], wr_lo_ref[...]) + br_ref[...]
        gates, g_top = _route(logits)
        gate_scr[...] = gates
        v_scr[...] = v.astype(BF16)
        acc_scr[...] = jnp.zeros_like(acc_scr)
        lane = lax.broadcasted_iota(jnp.int32, gates.shape, 1)
        member = lane == g_top
        incl = _dot(tri_ref[...], jnp.where(member, 1.0, 0.0).astype(BF16))
        pos = jnp.where(member, incl - 1.0, -1.0)
        pos_scr[...] = pos
        pos_t = pos.T
        for g in range(MOE_GROUPS):
            post_scr[g] = jnp.broadcast_to(pos_t[g:g + 1], post_scr.shape[1:])
            cnt_scr[g] = jnp.sum(jnp.where(member & (lane == g), 1.0, 0.0)).astype(jnp.int32)

    cnt = cnt_scr[grp]

    @pl.when(e % MOE_PER_GROUP == 0)
    def _():
        g_hi, g_mid, g_lo = _split3(gate_scr[...])
        for c in range(nchunk):
            @pl.when(cnt > c * rows)
            def _(c=c):
                slot = c * rows + lax.broadcasted_iota(jnp.int32, (rows, tm), 0)
                onehot = jnp.where(post_scr[grp, 0:1, :].astype(jnp.int32) == slot, 1.0, 0.0).astype(BF16)
                sl = slice(c * rows, (c + 1) * rows)
                x_scr[sl] = _dot(onehot, v_scr[...]).astype(BF16)
                gx_scr[sl] = _dot(onehot, g_hi) + (_dot(onehot, g_mid) + _dot(onehot, g_lo))
                y_scr[sl] = jnp.zeros((rows, y_scr.shape[1]), F32)

    for c in range(nchunk):
        @pl.when(cnt > c * rows)
        def _(c=c):
            sl = slice(c * rows, (c + 1) * rows)
            x = x_scr[sl]
            gate = _lane_col(gx_scr[sl], e + MOE_GROUPS)
            hh = (_silu(_dot(x, wg_ref[0])) * _dot(x, wu_ref[0]) * gate).astype(BF16)
            y_scr[sl] += _dot(hh, wd_ref[0])

    @pl.when(e % MOE_PER_GROUP == MOE_PER_GROUP - 1)
    def _():
        pos = _lane_col(pos_scr[...], grp).astype(jnp.int32)
        for c in range(nchunk):
            @pl.when(cnt > c * rows)
            def _(c=c):
                slot = c * rows + lax.broadcasted_iota(jnp.int32, (tm, rows), 1)
                onehot_t = jnp.where(pos == slot, 1.0, 0.0).astype(BF16)
                y_hi, y_lo = _hi_lo(y_scr[c * rows:(c + 1) * rows])
                acc_scr[...] += _dot(onehot_t, y_hi) + _dot(onehot_t, y_lo)

    @pl.when(e == pl.num_programs(1) - 1)
    def _():
        o_ref[...] = h_ref[...] + mod_ref[0][5:6] * acc_scr[...]


MOE_TM = 1024
MOE_ROWS = 320


def _moe_layer(h, seg, mod, norm_g, wr_hi, wr_lo, br, wg, wu, wd, tm=MOE_TM, rows=MOE_ROWS):
    n, d = h.shape
    nexp, _, f = wg.shape
    tm = min(tm, n if seg.shared_mod else seg.length)
    rows = min(rows, tm)
    nchunk = -(-tm // rows)
    rid = seg.mod_row(tm)
    tri = jnp.tri(tm, dtype=BF16)
    return pl.pallas_call(
        functools.partial(_moe_kernel, rows=rows),
        grid=(n // tm, nexp),
        in_specs=[
            pl.BlockSpec((tm, d), lambda i, e: (i, 0)),
            pl.BlockSpec((1, 6, d), lambda i, e: (rid(i), 0, 0)),
            pl.BlockSpec((1, d), lambda i, e: (0, 0)),
            pl.BlockSpec((d, ROUTE_W), lambda i, e: (0, 0)),
            pl.BlockSpec((d, ROUTE_W), lambda i, e: (0, 0)),
            pl.BlockSpec((1, ROUTE_W), lambda i, e: (0, 0)),
            pl.BlockSpec((tm, tm), lambda i, e: (0, 0)),
            pl.BlockSpec((1, d, f), lambda i, e: (e, 0, 0)),
            pl.BlockSpec((1, d, f), lambda i, e: (e, 0, 0)),
            pl.BlockSpec((1, f, d), lambda i, e: (e, 0, 0)),
        ],
        out_specs=pl.BlockSpec((tm, d), lambda i, e: (i, 0)),
        out_shape=jax.ShapeDtypeStruct((n, d), F32),
        scratch_shapes=[
            pltpu.VMEM((tm, d), BF16),
            pltpu.VMEM((tm, ROUTE_W), F32),
            pltpu.VMEM((tm, ROUTE_W), F32),
            pltpu.VMEM((MOE_GROUPS, SUBLANES, tm), F32),
            pltpu.VMEM((nchunk * rows, d), BF16),
            pltpu.VMEM((nchunk * rows, ROUTE_W), F32),
            pltpu.VMEM((nchunk * rows, d), F32),
            pltpu.VMEM((tm, d), F32),
            pltpu.SMEM((MOE_GROUPS,), jnp.int32),
        ],
        compiler_params=_cparams("parallel", "arbitrary"),
        name="moe",
    )(h, mod, norm_g, wr_hi, wr_lo, br, tri, wg, wu, wd)


S5_TL = 32
S5_CG = MXU_DIM // S5_GROUP
S5_NCH = D_MODEL // MXU_DIM
S5_CN = S5_CG * S5_STATE
S5_SCAN_W = 512


def _s5_scan_kernel(h_ref, sh_ref, sc_ref, g_ref, s0r_ref, s0i_ref, ar_ref, ai_ref, wb_ref, wcr_ref, wci_ref,
                    y_ref, fr_ref, fi_ref, bur, bui, str_, sti):
    d = pl.program_id(1)
    c = pl.program_id(2)
    tl, nb, dm = h_ref.shape

    @pl.when(c == 0)
    def _():
        str_[...] = s0r_ref[0]
        sti[...] = s0i_ref[0]

    u = _modulate(h_ref[...], g_ref[...], sh_ref[...], sc_ref[...])
    u2 = u.reshape(tl * nb, dm).astype(BF16)
    for cc in range(S5_NCH):
        r = _dot(u2[:, cc * MXU_DIM:(cc + 1) * MXU_DIM], wb_ref[0, cc])
        bur[:, cc * S5_CN:(cc + 1) * S5_CN] = r[:, :S5_CN]
        bui[:, cc * S5_CN:(cc + 1) * S5_CN] = r[:, S5_CN:]

    for lc in range(S5_N // S5_SCAN_W):
        cols = slice(lc * S5_SCAN_W, (lc + 1) * S5_SCAN_W)
        a_r = jnp.broadcast_to(ar_ref[0, :, cols], (nb, S5_SCAN_W))
        a_i = jnp.broadcast_to(ai_ref[0, :, cols], (nb, S5_SCAN_W))

        def body(i, carry, cols=cols, a_r=a_r, a_i=a_i):
            s_r, s_i = carry
            tt = jnp.where(d == 0, i, tl - 1 - i)
            rows = pl.ds(pl.multiple_of(tt * nb, nb), nb)
            n_r = a_r * s_r - a_i * s_i + bur[rows, cols]
            n_i = a_r * s_i + a_i * s_r + bui[rows, cols]
            bur[rows, cols] = n_r
            bui[rows, cols] = n_i
            return n_r, n_i

        s_r, s_i = lax.fori_loop(0, tl, body, (str_[:, cols], sti[:, cols]))
        str_[:, cols] = s_r
        sti[:, cols] = s_i

    for cc in range(S5_NCH):
        cols = slice(cc * S5_CN, (cc + 1) * S5_CN)
        y = _dot(bur[:, cols].astype(BF16), wcr_ref[0, cc]) + _dot(bui[:, cols].astype(BF16), wci_ref[0, cc])
        y_ref[0, :, :, cc * MXU_DIM:(cc + 1) * MXU_DIM] = y.reshape(tl, nb, MXU_DIM)

    @pl.when(c == pl.num_programs(2) - 1)
    def _():
        fr_ref[0] = str_[...]
        fi_ref[0] = sti[...]


def _s5_scan(h_t, shift, scale, norm_g, s0r, s0i, a_r, a_i, wb, wcr, wci):
    length, nbt, dm = h_t.shape
    nb = SUBLANES
    tl = S5_TL
    nch = length // tl

    def tch(d, c):
        return jnp.where(d == 0, c, nch - 1 - c)

    st_spec = pl.BlockSpec((1, nb, S5_N), lambda b, d, c: (d, b, 0))
    a_spec = pl.BlockSpec((1, 1, S5_N), lambda b, d, c: (d, 0, 0))
    return pl.pallas_call(
        _s5_scan_kernel,
        grid=(nbt // nb, 2, nch),
        in_specs=[
            pl.BlockSpec((tl, nb, dm), lambda b, d, c: (tch(d, c), b, 0)),
            pl.BlockSpec((nb, dm), lambda b, d, c: (b, 0)),
            pl.BlockSpec((nb, dm), lambda b, d, c: (b, 0)),
            pl.BlockSpec((1, dm), lambda b, d, c: (0, 0)),
            st_spec, st_spec, a_spec, a_spec,
            pl.BlockSpec((1, S5_NCH, MXU_DIM, 2 * S5_CN), lambda b, d, c: (d, 0, 0, 0)),
            pl.BlockSpec((1, S5_NCH, S5_CN, MXU_DIM), lambda b, d, c: (d, 0, 0, 0)),
            pl.BlockSpec((1, S5_NCH, S5_CN, MXU_DIM), lambda b, d, c: (d, 0, 0, 0)),
        ],
        out_specs=[
            pl.BlockSpec((1, tl, nb, dm), lambda b, d, c: (d, tch(d, c), b, 0)),
            st_spec, st_spec,
        ],
        out_shape=[
            jax.ShapeDtypeStruct((2, length, nbt, dm), F32),
            jax.ShapeDtypeStruct((2, nbt, S5_N), F32),
            jax.ShapeDtypeStruct((2, nbt, S5_N), F32),
        ],
        scratch_shapes=[
            pltpu.VMEM((tl * nb, S5_N), F32), pltpu.VMEM((tl * nb, S5_N), F32),
            pltpu.VMEM((nb, S5_N), F32), pltpu.VMEM((nb, S5_N), F32),
        ],
        compiler_params=_cparams("parallel", "arbitrary", "arbitrary"),
        name="s5_scan",
    )(h_t, shift, scale, norm_g, s0r, s0i, a_r, a_i, wb, wcr, wci)


def _gelu_tanh(x):
    return 0.5 * x * (1.0 + jnp.tanh(0.7978845608028654 * (x + 0.044715 * (x * x * x))))


def _s5_glu_kernel(h_ref, mod_ref, g_ref, dsk_ref, yf_ref, yb_ref, w1_ref, w2_ref, o_ref):
    mod = mod_ref[0]
    h = h_ref[...]
    u = _modulate(h, g_ref[...], mod[0:1], mod[1:2])
    y = u * dsk_ref[...] + (yf_ref[0] + yb_ref[0])
    z = _gelu_tanh(y).astype(BF16)
    out = _dot(z, w1_ref[...]) * _sigmoid(_dot(z, w2_ref[...]))
    o_ref[...] = h + mod[2:3] * out


def _s5_glu(h, seg, mod, norm_g, d_skip, y_t, w1, w2, tm=256):
    n, dm = h.shape
    _, length, nbt, _ = y_t.shape
    y2 = y_t.reshape(2, length, nbt * dm)
    tm = min(tm, length)
    per = length // tm
    rid = seg.mod_row(tm)
    return pl.pallas_call(
        _s5_glu_kernel,
        grid=(n // tm,),
        in_specs=[
            pl.BlockSpec((tm, dm), lambda i: (i, 0)),
            pl.BlockSpec((1, 6, dm), lambda i: (rid(i), 0, 0)),
            pl.BlockSpec((1, dm), lambda i: (0, 0)),
            pl.BlockSpec((1, dm), lambda i: (0, 0)),
            pl.BlockSpec((1, tm, dm), lambda i: (0, i % per, i // per)),
            pl.BlockSpec((1, tm, dm), lambda i: (1, i % per, i // per)),
            pl.BlockSpec((dm, dm), lambda i: (0, 0)),
            pl.BlockSpec((dm, dm), lambda i: (0, 0)),
        ],
        out_specs=pl.BlockSpec((tm, dm), lambda i: (i, 0)),
        out_shape=jax.ShapeDtypeStruct((n, dm), F32),
        compiler_params=_cparams("parallel"),
        name="s5_glu",
    )(h, mod, norm_g, d_skip, y2, y2, w1, w2)


def _s5_weights(lam_re, lam_im, log_step, b_re, b_im, c_re, c_im):
    dt = jnp.exp(log_step)
    mag = jnp.exp(lam_re * dt)
    ab_re = mag * jnp.cos(lam_im * dt)
    ab_im = mag * jnp.sin(lam_im * dt)
    nr = ab_re - 1.0
    den = lam_re * lam_re + lam_im * lam_im
    f_re = (nr * lam_re + ab_im * lam_im) / den
    f_im = (ab_im * lam_re - nr * lam_im) / den
    fb_re = f_re[..., None] * b_re - f_im[..., None] * b_im
    fb_im = f_re[..., None] * b_im + f_im[..., None] * b_re
    eye = jnp.eye(S5_CG, dtype=F32)

    def bd_in(w):
        w = w.reshape(2, S5_NCH, S5_CG, S5_STATE, S5_GROUP)
        m = jnp.einsum('dnjpc,jk->dnjckp', w, eye)
        return m.reshape(2, S5_NCH, MXU_DIM, S5_CN)

    def bd_out(w):
        w = w.reshape(2, S5_NCH, S5_CG, S5_GROUP, S5_STATE)
        m = jnp.einsum('dnjcp,jk->dnjpkc', w, eye)
        return m.reshape(2, S5_NCH, S5_CN, MXU_DIM)

    wb = jnp.concatenate([bd_in(fb_re), bd_in(fb_im)], axis=-1).astype(BF16)
    return (ab_re.reshape(2, 1, S5_N), ab_im.reshape(2, 1, S5_N), wb,
            bd_out(c_re).astype(BF16), bd_out(-c_im).astype(BF16))


def _to_time_major(x, seg, nbt):
    xt = jnp.transpose(x.reshape(seg.nseq, seg.length, -1), (1, 0, 2))
    if nbt > seg.nseq:
        xt = jnp.pad(xt, ((0, 0), (0, nbt - seg.nseq), (0, 0)))
    return xt


def _pad_rows(x, nbt):
    return x if x.shape[0] == nbt else jnp.pad(x, ((0, nbt - x.shape[0]),) + ((0, 0),) * (x.ndim - 1))


def _s5_layer(h, seg, mod, norm_g, s0_re, s0_im, sw, d_skip, w1, w2):
    a_r, a_i, wb, wcr, wci = sw
    nbt = -(-seg.nseq // SUBLANES) * SUBLANES
    h_t = _to_time_major(h, seg, nbt)
    shift = _pad_rows(seg.mod_rows(mod, 0), nbt)
    scale = _pad_rows(seg.mod_rows(mod, 1), nbt)
    pad = ((0, 0), (0, nbt - seg.nseq), (0, 0))
    s0r = jnp.pad(jnp.transpose(s0_re, (1, 0, 2)), pad)
    s0i = jnp.pad(jnp.transpose(s0_im, (1, 0, 2)), pad)
    y_t, f_r, f_i = _s5_scan(h_t, shift, scale, norm_g, s0r, s0i, a_r, a_i, wb, wcr, wci)
    h_new = _s5_glu(h, seg, mod, norm_g, d_skip, y_t, w1, w2)
    return h_new, jnp.transpose(f_r[:, :seg.nseq], (1, 0, 2)), jnp.transpose(f_i[:, :seg.nseq], (1, 0, 2))


def _rwkv_proj_kernel(h_ref, hp_ref, hn_ref, mod_ref, g_ref, mu_ref, wr_ref, wk_ref, wv_ref,
                      wl1_ref, wl2_ref, al1_ref, al2_ref, gl1_ref, gl2_ref, vec_ref, w0_ref, a0_ref, ones_ref,
                      r_ref, v_ref, kk_ref, w_ref, ka_ref, kd_ref, g_out_ref, bonus_ref, *, per):
    i = pl.program_id(0)
    mod = mod_ref[0]
    gn = g_ref[...]
    tm = h_ref.shape[0]
    u = _modulate(h_ref[...], gn, mod[0:1], mod[1:2])
    first = (i % per) == 0
    last = (i % per) == per - 1
    up = _modulate(hp_ref[SUBLANES - 1:SUBLANES, :], gn, mod[0:1], mod[1:2]) * jnp.where(first, 0.0, 1.0)
    un = _modulate(hn_ref[0:1, :], gn, mod[0:1], mod[1:2]) * jnp.where(last, 0.0, 1.0)
    row = lax.broadcasted_iota(jnp.int32, u.shape, 0)
    prev = jnp.where(row == 0, up, pltpu.roll(u, 1, 0))
    nxt = jnp.where(row == tm - 1, un, pltpu.roll(u, tm - 1, 0))
    dx = 0.5 * (prev + nxt) - u
    mu = mu_ref[...]

    def mix(j):
        return (u + dx * mu[j:j + 1]).astype(BF16)

    vec = vec_ref[...]
    r = _dot(mix(0), wr_ref[...])
    k = _dot(mix(2), wk_ref[...])
    v = _dot(mix(3), wv_ref[...])
    ones = ones_ref[...]
    kk = k * vec[0:1]
    kk = kk * lax.rsqrt(_group_sum(kk * kk, ones) + 1e-12)
    g_out_ref[...] = _dot(_sigmoid(_dot(mix(5), gl1_ref[...])).astype(BF16), gl2_ref[...])
    xw = mix(1)
    xa = mix(4)
    kd_sum = None
    for d in range(2):
        wl = _dot(jnp.tanh(_dot(xw, wl1_ref[d])).astype(BF16), wl2_ref[d])
        w_ref[d] = jnp.exp(-RWKV_DECAY_SCALE * _sigmoid(w0_ref[d] + wl))
        al = _dot(_dot(xa, al1_ref[d]).astype(BF16), al2_ref[d])
        a = _sigmoid(a0_ref[d] + al)
        kd = k * (1.0 + (a - 1.0) * vec[1:2])
        kd_ref[d] = kd
        ka_ref[d] = kk * a
        kd_sum = kd if kd_sum is None else kd_sum + kd
    r_ref[...] = r
    v_ref[...] = v
    kk_ref[...] = kk
    bonus_ref[...] = _group_sum(r * kd_sum * vec[2:3], ones) * v


def _rwkv_proj(h, seg, mod, norm_g, mu, wr, wk, wv, wl1, wl2, al1, al2, gl1, gl2, vec, w0, a0, tm=256):
    n, dm = h.shape
    tm = min(tm, seg.length)
    per = seg.length // tm
    rid = seg.mod_row(tm)
    hb = tm // SUBLANES
    nblk = n // SUBLANES
    full = lambda a: pl.BlockSpec(a.shape, lambda i, nd=a.ndim: (0,) * nd)
    tile = pl.BlockSpec((tm, dm), lambda i: (i, 0))
    tile2 = pl.BlockSpec((2, tm, dm), lambda i: (0, i, 0))
    ones = _ones_blockdiag(RWKV_HEAD)
    one = jax.ShapeDtypeStruct((n, dm), F32)
    two = jax.ShapeDtypeStruct((2, n, dm), F32)
    return pl.pallas_call(
        functools.partial(_rwkv_proj_kernel, per=per),
        grid=(n // tm,),
        in_specs=[
            tile,
            pl.BlockSpec((SUBLANES, dm), lambda i: (jnp.maximum(i * hb - 1, 0), 0)),
            pl.BlockSpec((SUBLANES, dm), lambda i: (jnp.minimum((i + 1) * hb, nblk - 1), 0)),
            pl.BlockSpec((1, 6, dm), lambda i: (rid(i), 0, 0)),
            full(norm_g), full(mu), full(wr), full(wk), full(wv), full(wl1), full(wl2), full(al1), full(al2),
            full(gl1), full(gl2), full(vec), full(w0), full(a0), full(ones),
        ],
        out_specs=[tile, tile, tile, tile2, tile2, tile2, tile, tile],
        out_shape=[one, one, one, two, two, two, one, one],
        compiler_params=_cparams("parallel"),
        name="rwkv_proj",
    )(h, h, h, mod, norm_g, mu, wr, wk, wv, wl1, wl2, al1, al2, gl1, gl2, vec, w0, a0, ones)


RWKV_TC = 32


def _rwkv_chain_steps(tc, nv, load_pieces, store_pieces, st, mk, mv, ys):
    ngroups = nv // SUBLANES
    ys[...] = jnp.zeros_like(ys)
    for piece in load_pieces(0, 0):
        piece()

    def step(i, slot):
        other = 1 - slot
        side = load_pieces(jnp.minimum(i + 1, tc - 1), other) + store_pieces(jnp.maximum(i - 1, 0), other)
        for v8 in range(ngroups):
            base = v8 * SUBLANES
            vb = mv[slot, base:base + SUBLANES, :]
            rows = []
            for j in range(SUBLANES):
                s = st[base + j]
                sa = jnp.sum(s * mk[slot, 0], axis=0, keepdims=True)
                s = s * mk[slot, 2] + (vb[j:j + 1] * mk[slot, 3] - sa * mk[slot, 1])
                st[base + j] = s
                rows.append(jnp.sum(s * mk[slot, 4], axis=0, keepdims=True))
            ys[slot, base:base + SUBLANES, :] = jnp.concatenate(rows, axis=0)
            for piece in side[v8::ngroups]:
                piece()

    def two_steps(j, carry):
        step(2 * j, 0)
        step(2 * j + 1, 1)
        return carry

    lax.fori_loop(0, tc // 2, two_steps, 0)
    for piece in store_pieces(tc - 1, 1):
        piece()


def _heads_to_rows(s_scr, x, lanes, rows_per_head):
    for h in range(RWKV_HEADS):
        s_scr[h * rows_per_head:(h + 1) * rows_per_head, lanes] = x[:, h * RWKV_HEAD:(h + 1) * RWKV_HEAD]


def _rwkv_scan_grid_kernel(kk_ref, r_ref, v_ref, ka_ref, w_ref, kd_ref, s0_ref, y_ref, sf_ref,
                           st, mk, mv, ys, s_scr):
    d = pl.program_id(0)
    c = pl.program_id(2)
    tc, nb, _ = kk_ref.shape
    nk = RWKV_HEAD
    lo, hi = slice(0, nk), slice(nk, 2 * nk)

    @pl.when(c == 0)
    def _():
        st[...] = s0_ref[0]

    def load_pieces(i, slot):
        t = jnp.where(d == 0, i, tc - 1 - i)
        pairs = ((kk_ref, None, ka_ref, 0), (w_ref, 0, kd_ref, 0), (r_ref, None, v_ref, None))

        def piece(p):
            ra, da, rb, db = pairs[p]
            xa = ra[t] if da is None else ra[da, t]
            xb = rb[t] if db is None else rb[db, t]
            _heads_to_rows(s_scr.at[p], xa, lo, nb)
            _heads_to_rows(s_scr.at[p], xb, hi, nb)
            tt = s_scr[p].T
            mk[slot, 2 * p] = tt[lo]
            if p < 2:
                mk[slot, 2 * p + 1] = tt[hi]
            else:
                mv[slot] = tt[hi]
        return [functools.partial(piece, p) for p in range(3)]

    def store_pieces(i, slot):
        t = jnp.where(d == 0, i, tc - 1 - i)

        def piece():
            yt = ys[slot].T
            for h in range(RWKV_HEADS):
                y_ref[0, t, :, h * nk:(h + 1) * nk] = yt[h * nb:(h + 1) * nb, :]
        return [piece]

    _rwkv_chain_steps(tc, nk, load_pieces, store_pieces, st, mk, mv, ys)

    @pl.when(c == pl.num_programs(2) - 1)
    def _():
        sf_ref[0] = st[...]


def _rwkv_scan_grid(kk, r, v, ka, w, kd, s0):
    n, nseq, dm = kk.shape
    nk = RWKV_HEAD
    nb = SUBLANES
    tc = RWKV_TC
    nch = n // tc
    tch = lambda d, t: jnp.where(d == 0, t, nch - 1 - t)
    shared = pl.BlockSpec((tc, nb, dm), lambda d, l, t: (tch(d, t), l, 0))
    perdir = pl.BlockSpec((1, tc, nb, dm), lambda d, l, t: (d, tch(d, t), l, 0))
    sspec = pl.BlockSpec((1, nk, nk, LANES), lambda d, l, t: (d, 0, 0, l))
    return pl.pallas_call(
        _rwkv_scan_grid_kernel,
        grid=(2, nseq // nb, nch),
        in_specs=[shared, shared, shared, perdir, perdir, perdir, sspec],
        out_specs=[perdir, sspec],
        out_shape=[jax.ShapeDtypeStruct((2, n, nseq, dm), F32), jax.ShapeDtypeStruct(s0.shape, F32)],
        scratch_shapes=[pltpu.VMEM((nk, nk, LANES), F32), pltpu.VMEM((2, 5, nk, LANES), F32),
                        pltpu.VMEM((2, nk, LANES), F32), pltpu.VMEM((2, nk, LANES), F32),
                        pltpu.VMEM((3, LANES, LANES), F32)],
        compiler_params=_cparams("parallel", "parallel", "arbitrary"),
        name="rwkv_scan_grid",
    )(kk, r, v, ka, w, kd, s0)


def _rwkv_scan_lanes_kernel(kkf, rf, vf, kaf, wf, kdf, kkr, rr, vr, kar, wr, kdr, s0_ref, yf_ref, yr_ref, sf_ref,
                            st, mk, mv, ys, s_scr, y_scr):
    c = pl.program_id(0)
    tc, nb, _ = kkf.shape
    nk = RWKV_HEAD
    chains = s_scr.shape[1]
    vs = LANES // chains
    nv = nk // vs
    lo, hi = slice(0, nk), slice(nk, 2 * nk)

    @pl.when(c == 0)
    def _():
        st[...] = s0_ref[...]

    def load_pieces(i, slot):
        j = tc - 1 - i
        both = lambda f, r: jnp.concatenate([f, r], axis=0)
        rep = lambda a: a if vs == 1 else jnp.concatenate([a] * vs, axis=1)

        def piece(p):
            if p == 0:
                xa, xb = both(kkf[i], kkr[j]), both(kaf[0, i], kar[0, j])
            elif p == 1:
                xa, xb = both(wf[0, i], wr[0, j]), both(kdf[0, i], kdr[0, j])
            else:
                xa, xb = both(rf[i], rr[j]), both(vf[i], vr[j])
            _heads_to_rows(s_scr.at[p], xa, lo, 2 * nb)
            _heads_to_rows(s_scr.at[p], xb, hi, 2 * nb)
            tt = s_scr[p].T
            mk[slot, 2 * p] = rep(tt[lo])
            if p < 2:
                mk[slot, 2 * p + 1] = rep(tt[hi])
            else:
                tv = tt[hi]
                mv[slot] = tv if vs == 1 else jnp.concatenate([tv[s * nv:(s + 1) * nv] for s in range(vs)], axis=1)
        return [functools.partial(piece, p) for p in range(3)]

    def store_pieces(i, slot):
        def piece():
            yt = ys[slot].T
            for s in range(vs):
                for h in range(RWKV_HEADS):
                    r0 = s * chains + h * 2 * nb
                    y_scr[0:2 * nb, h * nk + s * nv:h * nk + (s + 1) * nv] = yt[r0:r0 + 2 * nb, :]
            yf_ref[i] = y_scr[0:nb, :]
            yr_ref[tc - 1 - i] = y_scr[nb:2 * nb, :]
        return [piece]

    _rwkv_chain_steps(tc, nv, load_pieces, store_pieces, st, mk, mv, ys)

    @pl.when(c == pl.num_programs(0) - 1)
    def _():
        sf_ref[...] = st[...]


def _rwkv_scan_lanes(kk, r, v, ka, w, kd, s0):
    n, nb, dm = kk.shape
    nk = RWKV_HEAD
    nv = s0.shape[0]
    chains = RWKV_HEADS * 2 * nb
    tc = RWKV_TC
    nch = n // tc
    sf_ = pl.BlockSpec((tc, nb, dm), lambda t: (t, 0, 0))
    sr_ = pl.BlockSpec((tc, nb, dm), lambda t: (nch - 1 - t, 0, 0))
    df_ = pl.BlockSpec((1, tc, nb, dm), lambda t: (0, t, 0, 0))
    dr_ = pl.BlockSpec((1, tc, nb, dm), lambda t: (1, nch - 1 - t, 0, 0))
    sspec = pl.BlockSpec((nv, nk, LANES), lambda t: (0, 0, 0))
    yshape = jax.ShapeDtypeStruct((n, nb, dm), F32)
    return pl.pallas_call(
        _rwkv_scan_lanes_kernel,
        grid=(nch,),
        in_specs=[sf_, sf_, sf_, df_, df_, df_, sr_, sr_, sr_, dr_, dr_, dr_, sspec],
        out_specs=[sf_, sr_, sspec],
        out_shape=[yshape, yshape, jax.ShapeDtypeStruct((nv, nk, LANES), F32)],
        scratch_shapes=[pltpu.VMEM((nv, nk, LANES), F32), pltpu.VMEM((2, 5, nk, LANES), F32),
                        pltpu.VMEM((2, nv, LANES), F32), pltpu.VMEM((2, nv, LANES), F32),
                        pltpu.VMEM((3, chains, LANES), F32), pltpu.VMEM((SUBLANES, dm), F32)],
        compiler_params=_cparams("arbitrary"),
        name="rwkv_scan_lanes",
    )(kk, r, v, ka, w, kd, kk, r, v, ka, w, kd, s0)


def _rwkv_out_kernel(h_ref, mod_ref, yf_ref, yb_ref, bonus_ref, g_ref, lng_ref, lnb_ref, wo_ref, ones_ref, o_ref):
    ones = ones_ref[...]
    y = yf_ref[...] + yb_ref[...]
    inv = 1.0 / RWKV_HEAD
    mean = _group_sum(y, ones) * inv
    yc = y - mean
    var = _group_sum(yc * yc, ones) * inv
    yn = yc * lax.rsqrt(var + RWKV_GN_EPS) * lng_ref[...] + lnb_ref[...] + bonus_ref[...]
    out = _dot((yn * g_ref[...]).astype(BF16), wo_ref[...])
    o_ref[...] = h_ref[...] + mod_ref[0][2:3] * out


def _rwkv_out(h, seg, mod, yf, yb, bonus, g, ln_g, ln_b, wo, tm=256):
    n, dm = h.shape
    tm = min(tm, seg.length)
    rid = seg.mod_row(tm)
    tile = pl.BlockSpec((tm, dm), lambda i: (i, 0))
    row = pl.BlockSpec((1, dm), lambda i: (0, 0))
    ones = _ones_blockdiag(RWKV_HEAD)
    return pl.pallas_call(
        _rwkv_out_kernel,
        grid=(n // tm,),
        in_specs=[tile, pl.BlockSpec((1, 6, dm), lambda i: (rid(i), 0, 0)), tile, tile, tile, tile, row, row,
                  pl.BlockSpec((dm, dm), lambda i: (0, 0)), pl.BlockSpec(ones.shape, lambda i: (0, 0))],
        out_specs=tile,
        out_shape=jax.ShapeDtypeStruct((n, dm), F32),
        compiler_params=_cparams("parallel"),
        name="rwkv_out",
    )(h, mod, yf, yb, bonus, g, ln_g, ln_b, wo, ones)


def _rwkv_layer(h, seg, mod, norm_g, s0, rw):
    nseq, length = seg.nseq, seg.length
    hh, nk = RWKV_HEADS, RWKV_HEAD
    r, v, kk, w, ka, kd, g, bonus = _rwkv_proj(h, seg, mod, norm_g, *rw['proj'])
    per_dir = nseq * hh
    n = nseq * length
    dm = hh * nk

    def tmaj(x):
        lead = x.shape[:-2]
        nl = len(lead)
        return jnp.transpose(x.reshape(*lead, nseq, length, dm), tuple(range(nl)) + (nl + 1, nl, nl + 2))

    def nat(y):
        lead = y.shape[:-3]
        nl = len(lead)
        return jnp.transpose(y, tuple(range(nl)) + (nl + 1, nl, nl + 2)).reshape(*lead, n, dm)

    if per_dir % LANES == 0:
        ng = nseq // SUBLANES
        s0_s = jnp.transpose(s0.reshape(ng, SUBLANES, 2, hh, nk, nk), (2, 4, 5, 0, 3, 1)).reshape(2, nk, nk, per_dir)
        y_t, sf = _rwkv_scan_grid(tmaj(kk), tmaj(r), tmaj(v), tmaj(ka), tmaj(w), tmaj(kd), s0_s)
        y2 = nat(y_t)
        yf, yb = y2[0], y2[1]
        s_fin = jnp.transpose(sf.reshape(2, nk, nk, ng, hh, SUBLANES), (3, 5, 0, 4, 1, 2)).reshape(nseq, 2, hh, nk, nk)
    else:
        vs = LANES // (2 * per_dir)
        assert vs * 2 * per_dir == LANES
        nv = nk // vs
        s0_s = jnp.transpose(s0.reshape(nseq, 2, hh, vs, nv, nk), (4, 5, 3, 2, 1, 0)).reshape(nv, nk, LANES)
        y_f, y_r, sf = _rwkv_scan_lanes(tmaj(kk), tmaj(r), tmaj(v), tmaj(ka), tmaj(w), tmaj(kd), s0_s)
        yf, yb = nat(y_f), nat(y_r)
        s_fin = jnp.transpose(sf.reshape(nv, nk, vs, hh, 2, nseq), (5, 4, 3, 2, 0, 1)).reshape(nseq, 2, hh, nk, nk)
    h_new = _rwkv_out(h, seg, mod, yf, yb, bonus, g, *rw['out'])
    return h_new, s_fin


def _rope_tables(length, half, first_lane, period):
    lane = jnp.arange(LANES) % period - first_lane
    active = (lane >= 0) & (lane < 4 * half)
    quarter = jnp.clip(lane, 0, 4 * half - 1) // half
    freq = ROPE_BASE ** (-(jnp.clip(lane, 0, 4 * half - 1) % half).astype(F32) / half)
    t = jnp.arange(length)
    pos = jnp.where(quarter[None, :] < 2, (t // GRID_W)[:, None], (t % GRID_W)[:, None]).astype(F32)
    ang = pos * freq[None, :]
    cos = jnp.where(active[None], jnp.cos(ang), 1.0)
    sin = jnp.where(active[None], jnp.sin(ang), 0.0)
    first = (quarter % 2 == 0)[None]
    return cos, jnp.where(first, -sin, 0.0), jnp.where(first, 0.0, sin)


def _rope(x, cos, sa, sb, half):
    w = x.shape[-1]
    rep = w // LANES
    tile = (lambda a: a) if rep == 1 else (lambda a: jnp.concatenate([a] * rep, axis=-1))
    return x * tile(cos) + pltpu.roll(x, w - half, 1) * tile(sa) + pltpu.roll(x, half, 1) * tile(sb)


def _attn_kernel(*refs, groups, dq, dv, hpb, scale, band, has_ctx, has_sink):
    it = iter(refs)
    q_ref, k_ref, v_ref = next(it), next(it), next(it)
    kc_ref, vc_ref = (next(it), next(it)) if has_ctx else (None, None)
    sink_ref = next(it) if has_sink else None
    o_ref = next(it)
    tq = q_ref.shape[1]
    lk = k_ref.shape[1]
    rows = groups * tq
    if band:
        lw = min(lk, tq + 2 * WINDOW)
        q0 = pl.program_id(2) * tq
        start = pl.multiple_of(jnp.clip(q0 - WINDOW, 0, lk - lw), LANES)
        ksl = pl.ds(start, lw)
        qpos = q0 + (lax.broadcasted_iota(jnp.int32, (rows, lw), 0) & (tq - 1))
        kpos = start + lax.broadcasted_iota(jnp.int32, (rows, lw), 1)
        valid = jnp.abs(qpos - kpos) <= WINDOW
    else:
        ksl = slice(None)
    for j in range(hpb):
        kh = k_ref[0, ksl, j * dq:(j + 1) * dq].astype(BF16)
        vh = v_ref[0, ksl, j * dv:(j + 1) * dv].astype(BF16)
        heads = range(j * groups, (j + 1) * groups)
        qs = [q_ref[0, :, hq * dq:(hq + 1) * dq].astype(BF16) for hq in heads]
        qh = qs[0] if groups == 1 else jnp.concatenate(qs, axis=0)
        s = _dot_nt(qh, kh) * scale
        if band:
            s = jnp.where(valid, s, -jnp.inf)
        m = jnp.max(s, axis=-1, keepdims=True)
        if has_ctx:
            kch = kc_ref[0, :, j * dq:(j + 1) * dq].astype(BF16)
            vch = vc_ref[0, :, j * dv:(j + 1) * dv].astype(BF16)
            sc = _dot_nt(qh, kch) * scale
            m = jnp.maximum(m, jnp.max(sc, axis=-1, keepdims=True))
        if has_sink:
            sk = [jnp.broadcast_to(sink_ref[0, :, hq:hq + 1], (tq, 1)) for hq in heads]
            snk = sk[0] if groups == 1 else jnp.concatenate(sk, axis=0)
            m = jnp.maximum(m, snk)
        p = jnp.exp(s - m)
        den = jnp.sum(p, axis=-1, keepdims=True)
        acc = _dot(p.astype(BF16), vh)
        if has_ctx:
            pc = jnp.exp(sc - m)
            den = den + jnp.sum(pc, axis=-1, keepdims=True)
            acc = acc + _dot(pc.astype(BF16), vch)
        if has_sink:
            den = den + jnp.exp(snk - m)
        out = acc / den
        for g, hq in enumerate(heads):
            o_ref[0, :, hq * dv:(hq + 1) * dv] = out[g * tq:(g + 1) * tq]


def _attention(q, k, v, ctx, sink, *, heads, groups, dq, dv, scale, band, tq=256):
    b, lq, _ = q.shape
    lk = k.shape[1]
    hpb = 2
    nhb = heads // hpb
    ins = [q, k, v]
    specs = [
        pl.BlockSpec((1, tq, hpb * groups * dq), lambda bi, hb, i: (bi, i, hb)),
        pl.BlockSpec((1, lk, hpb * dq), lambda bi, hb, i: (bi, 0, hb)),
        pl.BlockSpec((1, lk, hpb * dv), lambda bi, hb, i: (bi, 0, hb)),
    ]
    if ctx is not None:
        lc = ctx[0].shape[1]
        ins += list(ctx)
        specs += [pl.BlockSpec((1, lc, hpb * dq), lambda bi, hb, i: (bi, 0, hb)),
                  pl.BlockSpec((1, lc, hpb * dv), lambda bi, hb, i: (bi, 0, hb))]
    if sink is not None:
        sk = jnp.pad(sink.reshape(nhb, 1, hpb * groups), ((0, 0), (0, 0), (0, LANES - hpb * groups)))
        ins.append(sk)
        specs.append(pl.BlockSpec((1, 1, LANES), lambda bi, hb, i: (hb, 0, 0)))
    return pl.pallas_call(
        functools.partial(_attn_kernel, groups=groups, dq=dq, dv=dv, hpb=hpb, scale=scale, band=band,
                          has_ctx=ctx is not None, has_sink=sink is not None),
        grid=(b, nhb, lq // tq),
        in_specs=specs,
        out_specs=pl.BlockSpec((1, tq, hpb * groups * dv), lambda bi, hb, i: (bi, i, hb)),
        out_shape=jax.ShapeDtypeStruct((b, lq, heads * groups * dv), F32),
        compiler_params=_cparams("parallel", "parallel", "arbitrary"),
        name="attention",
    )(*ins)


def _proj_res_kernel(h_ref, mod_ref, a_ref, w_ref, o_ref):
    o_ref[...] = h_ref[...] + mod_ref[0][2:3] * _dot(a_ref[...].astype(BF16), w_ref[...])


def _proj_residual(h, seg, mod, a, w, tm=256):
    n, dm = h.shape
    tm = min(tm, seg.length)
    rid = seg.mod_row(tm)
    return pl.pallas_call(
        _proj_res_kernel,
        grid=(n // tm,),
        in_specs=[pl.BlockSpec((tm, dm), lambda i: (i, 0)), pl.BlockSpec((1, 6, dm), lambda i: (rid(i), 0, 0)),
                  pl.BlockSpec((tm, a.shape[1]), lambda i: (i, 0)), pl.BlockSpec(w.shape, lambda i: (0, 0))],
        out_specs=pl.BlockSpec((tm, dm), lambda i: (i, 0)),
        out_shape=jax.ShapeDtypeStruct((n, dm), F32),
        compiler_params=_cparams("parallel"),
        name="proj_residual",
    )(h, mod, a, w)


GQA_NQ = GQA_HEADS * GQA_HEAD_DIM
GQA_NK = GQA_KV_HEADS * GQA_HEAD_DIM
GQA_ROPE_HALF = GQA_HEAD_DIM // 4


def _gqa_qkv_kernel(*refs, use_rope):
    if use_rope:
        h_ref, mod_ref, g_ref, w_ref, gq_ref, gk_ref, ones_ref, cos_ref, sa_ref, sb_ref, q_ref, k_ref, v_ref = refs
    else:
        h_ref, mod_ref, g_ref, w_ref, gq_ref, gk_ref, ones_ref, q_ref, k_ref, v_ref = refs
    mod = mod_ref[0]
    u = _modulate(h_ref[...], g_ref[...], mod[0:1], mod[1:2]).astype(BF16)
    qkv = _dot(u, w_ref[...])
    ones = ones_ref[...]
    inv = 1.0 / GQA_HEAD_DIM
    q = qkv[:, :GQA_NQ]
    k = qkv[:, GQA_NQ:GQA_NQ + GQA_NK]
    q = q * lax.rsqrt(_group_sum(q * q, ones) * inv + NORM_EPS) * gq_ref[...]
    k = k * lax.rsqrt(_group_sum(k * k, ones) * inv + NORM_EPS) * gk_ref[...]
    if use_rope:
        cos, sa, sb = cos_ref[...], sa_ref[...], sb_ref[...]
        q = _rope(q, cos, sa, sb, GQA_ROPE_HALF)
        k = _rope(k, cos, sa, sb, GQA_ROPE_HALF)
    q_ref[...] = q
    k_ref[...] = k
    v_ref[...] = qkv[:, GQA_NQ + GQA_NK:]


def _gqa_qkv(h, seg, mod, norm_g, w_qkv, gq, gk, rope, tm=256):
    n, dm = h.shape
    tm = min(tm, seg.length)
    per = seg.length // tm
    rid = seg.mod_row(tm)
    ones = _ones_blockdiag(GQA_HEAD_DIM)
    full = lambda a: pl.BlockSpec(a.shape, lambda i, nd=a.ndim: (0,) * nd)
    ins = [h, mod, norm_g, w_qkv, gq, gk, ones]
    specs = [pl.BlockSpec((tm, dm), lambda i: (i, 0)), pl.BlockSpec((1, 6, dm), lambda i: (rid(i), 0, 0)),
             full(norm_g), full(w_qkv), full(gq), full(gk), full(ones)]
    if rope is not None:
        ins += list(rope)
        specs += [pl.BlockSpec((tm, LANES), lambda i: (i % per, 0))] * 3
    return pl.pallas_call(
        functools.partial(_gqa_qkv_kernel, use_rope=rope is not None),
        grid=(n // tm,),
        in_specs=specs,
        out_specs=[pl.BlockSpec((tm, GQA_NQ), lambda i: (i, 0)), pl.BlockSpec((tm, GQA_NK), lambda i: (i, 0)),
                   pl.BlockSpec((tm, GQA_NK), lambda i: (i, 0))],
        out_shape=[jax.ShapeDtypeStruct((n, GQA_NQ), F32), jax.ShapeDtypeStruct((n, GQA_NK), F32),
                   jax.ShapeDtypeStruct((n, GQA_NK), F32)],
        compiler_params=_cparams("parallel"),
        name="gqa_qkv",
    )(*ins)


def _gqa_layer(h, seg, mod, norm_g, gw, ctx):
    w_qkv, gq, gk, sink, w_o = gw
    latent = ctx is not None
    rope = _rope_tables(seg.length, GQA_ROPE_HALF, 0, 4 * GQA_ROPE_HALF) if latent else None
    q, k, v = _gqa_qkv(h, seg, mod, norm_g, w_qkv, gq, gk, rope)
    b3 = lambda a: a.reshape(seg.nseq, seg.length, a.shape[-1])
    att = _attention(b3(q), b3(k), b3(v), ctx, sink, heads=GQA_KV_HEADS, groups=GQA_HEADS // GQA_KV_HEADS,
                     dq=GQA_HEAD_DIM, dv=GQA_HEAD_DIM, scale=GQA_HEAD_DIM ** -0.5, band=latent)
    h_new = _proj_residual(h, seg, mod, att.reshape(seg.n, -1), w_o)
    return h_new, k, v


MLA_QW = MLA_HEADS * MLA_PAD
MLA_VW = MLA_HEADS * MLA_V
MLA_CKR = MLA_KV_LORA + LANES
MLA_ROPE_HALF = MLA_ROPE // 4


def _mla_expand(ckv, kr, wuk_ref, wuv_ref, kg_ref, ones_ref):
    cb = ckv.astype(BF16)
    kn = _dot(cb, wuk_ref[...])
    kr_t = jnp.concatenate([pltpu.roll(kr, MLA_NOPE, 1)] * MLA_HEADS, axis=-1)
    k = kn + kr_t
    k = k * lax.rsqrt(_group_sum(k * k, ones_ref[...]) * (1.0 / MLA_QK) + NORM_EPS) * kg_ref[...]
    return k, _dot(cb, wuv_ref[...])


def _mla_proj_kernel(*refs, use_rope):
    (h_ref, mod_ref, g_ref, wdq_ref, qlg_ref, wuq_ref, qg_ref, wdkv_ref, kvg_ref, wuk_ref, wuv_ref, kg_ref,
     ones_ref) = refs[:13]
    if use_rope:
        cos_ref, sa_ref, sb_ref = refs[13:16]
    q_ref, k_ref, v_ref, ckv_ref, kr_ref = refs[-5:]
    mod = mod_ref[0]
    u = _modulate(h_ref[...], g_ref[...], mod[0:1], mod[1:2]).astype(BF16)
    qd = _dot(u, wdq_ref[...])
    qd = qd * lax.rsqrt(jnp.mean(qd * qd, axis=-1, keepdims=True) + NORM_EPS) * qlg_ref[...]
    q = _dot(qd.astype(BF16), wuq_ref[...])
    q = q * lax.rsqrt(_group_sum(q * q, ones_ref[...]) * (1.0 / MLA_QK) + NORM_EPS) * qg_ref[...]
    ckr = _dot(u, wdkv_ref[...])
    ckv = ckr[:, :MLA_KV_LORA]
    ckv = ckv * lax.rsqrt(jnp.mean(ckv * ckv, axis=-1, keepdims=True) + NORM_EPS) * kvg_ref[...]
    kr = ckr[:, MLA_KV_LORA:]
    k, v = _mla_expand(ckv, kr, wuk_ref, wuv_ref, kg_ref, ones_ref)
    if use_rope:
        cos, sa, sb = cos_ref[...], sa_ref[...], sb_ref[...]
        q = _rope(q, cos, sa, sb, MLA_ROPE_HALF)
        k = _rope(k, cos, sa, sb, MLA_ROPE_HALF)
    q_ref[...] = q.astype(BF16)
    k_ref[...] = k.astype(BF16)
    v_ref[...] = v.astype(BF16)
    ckv_ref[...] = ckv
    kr_ref[...] = kr


def _mla_proj(h, seg, mod, norm_g, pw, rope, tm=256):
    n, dm = h.shape
    tm = min(tm, seg.length)
    per = seg.length // tm
    rid = seg.mod_row(tm)
    ones = _ones_blockdiag(MLA_PAD)
    full = lambda a: pl.BlockSpec(a.shape, lambda i, nd=a.ndim: (0,) * nd)
    ins = [h, mod, norm_g, *pw, ones]
    specs = [pl.BlockSpec((tm, dm), lambda i: (i, 0)), pl.BlockSpec((1, 6, dm), lambda i: (rid(i), 0, 0)),
             full(norm_g)] + [full(a) for a in pw] + [full(ones)]
    if rope is not None:
        ins += list(rope)
        specs += [pl.BlockSpec((tm, LANES), lambda i: (i % per, 0))] * 3
    widths = (MLA_QW, MLA_QW, MLA_VW, MLA_KV_LORA, LANES)
    return pl.pallas_call(
        functools.partial(_mla_proj_kernel, use_rope=rope is not None),
        grid=(n // tm,),
        in_specs=specs,
        out_specs=[pl.BlockSpec((tm, w), lambda i: (i, 0)) for w in widths],
        out_shape=[jax.ShapeDtypeStruct((n, w), BF16 if j < 3 else F32) for j, w in enumerate(widths)],
        compiler_params=_cparams("parallel"),
        name="mla_proj",
    )(*ins)


def _mla_ctx_kernel(ckv_ref, kr_ref, wuk_ref, wuv_ref, kg_ref, ones_ref, k_ref, v_ref):
    k, v = _mla_expand(ckv_ref[...], kr_ref[...], wuk_ref, wuv_ref, kg_ref, ones_ref)
    k_ref[...] = k.astype(BF16)
    v_ref[...] = v.astype(BF16)


def _mla_ctx_expand(ckv, kr, wuk, wuv, kg, tm=256):
    n = ckv.shape[0]
    tm = min(tm, n)
    ones = _ones_blockdiag(MLA_PAD)
    full = lambda a: pl.BlockSpec(a.shape, lambda i, nd=a.ndim: (0,) * nd)
    return pl.pallas_call(
        _mla_ctx_kernel,
        grid=(n // tm,),
        in_specs=[pl.BlockSpec((tm, MLA_KV_LORA), lambda i: (i, 0)), pl.BlockSpec((tm, LANES), lambda i: (i, 0)),
                  full(wuk), full(wuv), full(kg), full(ones)],
        out_specs=[pl.BlockSpec((tm, MLA_QW), lambda i: (i, 0)), pl.BlockSpec((tm, MLA_VW), lambda i: (i, 0))],
        out_shape=[jax.ShapeDtypeStruct((n, MLA_QW), BF16), jax.ShapeDtypeStruct((n, MLA_VW), BF16)],
        compiler_params=_cparams("parallel"),
        name="mla_ctx_expand",
    )(ckv, kr, wuk, wuv, kg, ones)


def _mla_weights(w_dq, q_lora_g, w_uq, w_dkv, kv_g, w_ukv, q_g, k_g, w_o):
    padh = lambda a: jnp.pad(a, [(0, 0)] * (a.ndim - 1) + [(0, MLA_PAD - a.shape[-1])])
    wuq = padh(w_uq.reshape(MLA_Q_LORA, MLA_HEADS, MLA_QK)).reshape(MLA_Q_LORA, MLA_QW).astype(BF16)
    wdkv = jnp.pad(w_dkv, ((0, 0), (0, MLA_CKR - w_dkv.shape[1]))).astype(BF16)
    ukv = w_ukv.reshape(MLA_KV_LORA, MLA_HEADS, MLA_NOPE + MLA_V)
    wuk = padh(ukv[:, :, :MLA_NOPE]).reshape(MLA_KV_LORA, MLA_QW).astype(BF16)
    wuv = ukv[:, :, MLA_NOPE:].reshape(MLA_KV_LORA, MLA_VW).astype(BF16)
    tile_h = lambda g: jnp.tile(padh(g), MLA_HEADS)[None]
    proj = (w_dq.astype(BF16), q_lora_g[None], wuq, tile_h(q_g), wdkv, kv_g[None], wuk, wuv, tile_h(k_g))
    return proj, w_o.astype(BF16)


def _mla_layer(h, seg, mod, norm_g, mw, cache):
    proj, w_o = mw
    latent = cache is not None
    rope = _rope_tables(seg.length, MLA_ROPE_HALF, MLA_NOPE, MLA_PAD) if latent else None
    q, k, v, ckv, kr = _mla_proj(h, seg, mod, norm_g, proj, rope)
    ctx = None
    if latent:
        c_ckv, c_kr = cache
        lc = c_ckv.shape[1]
        kr_p = jnp.pad(c_kr.reshape(-1, MLA_ROPE), ((0, 0), (0, LANES - MLA_ROPE)))
        kc, vc = _mla_ctx_expand(c_ckv.reshape(-1, MLA_KV_LORA), kr_p, proj[6], proj[7], proj[8])
        ctx = (kc.reshape(seg.nseq, lc, MLA_QW), vc.reshape(seg.nseq, lc, MLA_VW))
    b3 = lambda a: a.reshape(seg.nseq, seg.length, a.shape[-1])
    att = _attention(b3(q), b3(k), b3(v), ctx, None, heads=MLA_HEADS, groups=1, dq=MLA_PAD, dv=MLA_V,
                     scale=MLA_QK ** -0.5, band=False)
    h_new = _proj_residual(h, seg, mod, att.reshape(seg.n, -1), w_o)
    return h_new, ckv, kr[:, :MLA_ROPE]


def kernel(x_prompt, x_sample, state_s5_re, state_s5_im, state_rwkv, cache_gqa_k, cache_gqa_v, cache_mla_ckv, cache_mla_krope, c, c_ctx, ada_w, ada_b, norm1_g, norm2_g, s5_lambda_re, s5_lambda_im, s5_log_step, s5_b_re, s5_b_im, s5_c_re, s5_c_im, s5_d, s5_glu_w1, s5_glu_w2, rwkv_mu, rwkv_w_r, rwkv_w_k, rwkv_w_v, rwkv_w_o, rwkv_w0, rwkv_w_l1, rwkv_w_l2, rwkv_a0, rwkv_a_l1, rwkv_a_l2, rwkv_g_l1, rwkv_g_l2, rwkv_k_k, rwkv_k_a, rwkv_r_k, rwkv_ln_g, rwkv_ln_b, gqa_w_qkv, gqa_q_norm, gqa_k_norm, gqa_sink, gqa_w_o, mla_w_dq, mla_q_lora_norm, mla_w_uq, mla_w_dkv, mla_kv_norm, mla_w_ukv, mla_q_norm, mla_k_norm, mla_w_o, moe_w_group, moe_b_group, moe_w_expert, moe_b_expert, moe_w_gate, moe_w_up, moe_w_down):
    bp, lp, dm = x_prompt.shape
    bs, ls, _ = x_sample.shape
    depth = ada_w.shape[0]
    assert dm == D_MODEL and 1 + bs <= MOD_ROWS
    seg_p = _Seg(bp, lp, 0, True)
    seg_s = _Seg(bs, ls, 1, False)
    bf = lambda a: a.astype(BF16)

    cond = jnp.zeros((MOD_ROWS, dm), F32).at[0].set(c_ctx).at[1:1 + bs].set(c)
    mods = _ada_all(cond, ada_w, ada_b).reshape(depth, MOD_ROWS, 6, dm)

    hp = x_prompt.reshape(bp * lp, dm)
    hs = x_sample.reshape(bs * ls, dm)
    outs = {}
    for layer in range(depth):
        kind = layer % 4
        mod = mods[layer]
        g1 = norm1_g[layer][None]
        if kind == 0:
            sw = _s5_weights(s5_lambda_re, s5_lambda_im, s5_log_step, s5_b_re, s5_b_im, s5_c_re, s5_c_im)
            w1, w2, dsk = bf(s5_glu_w1), bf(s5_glu_w2), s5_d[None]
            zeros = jnp.zeros((bp, 2, S5_N), F32)
            hp, f_re, f_im = _s5_layer(hp, seg_p, mod, g1, zeros, zeros, sw, dsk, w1, w2)
            hs, _, _ = _s5_layer(hs, seg_s, mod, g1, state_s5_re.reshape(bs, 2, S5_N),
                                 state_s5_im.reshape(bs, 2, S5_N), sw, dsk, w1, w2)
            outs['s5_re'] = f_re.reshape(bp, 2, S5_GROUPS, S5_STATE)
            outs['s5_im'] = f_im.reshape(bp, 2, S5_GROUPS, S5_STATE)
        elif kind == 1:
            vec = jnp.zeros((SUBLANES, dm), F32).at[0].set(rwkv_k_k).at[1].set(rwkv_k_a).at[2].set(rwkv_r_k.reshape(-1))
            rw = dict(
                proj=(rwkv_mu, bf(rwkv_w_r), bf(rwkv_w_k), bf(rwkv_w_v), bf(rwkv_w_l1), bf(rwkv_w_l2),
                      bf(rwkv_a_l1), bf(rwkv_a_l2), bf(rwkv_g_l1), bf(rwkv_g_l2), vec,
                      rwkv_w0[:, None, :], rwkv_a0[:, None, :]),
                out=(rwkv_ln_g[None], rwkv_ln_b[None], bf(rwkv_w_o)))
            zeros = jnp.zeros((bp, 2, RWKV_HEADS, RWKV_HEAD, RWKV_HEAD), F32)
            hp, outs['rwkv'] = _rwkv_layer(hp, seg_p, mod, g1, zeros, rw)
            hs, _ = _rwkv_layer(hs, seg_s, mod, g1, state_rwkv, rw)
        elif kind == 2:
            gw = (bf(gqa_w_qkv), jnp.tile(gqa_q_norm, GQA_HEADS)[None], jnp.tile(gqa_k_norm, GQA_KV_HEADS)[None],
                  gqa_sink, bf(gqa_w_o))
            hp, kp, vp = _gqa_layer(hp, seg_p, mod, g1, gw, None)
            outs['gqa_k'] = kp.reshape(bp, lp, GQA_KV_HEADS, GQA_HEAD_DIM)
            outs['gqa_v'] = vp.reshape(bp, lp, GQA_KV_HEADS, GQA_HEAD_DIM)
            lc = cache_gqa_k.shape[1]
            ctx = (cache_gqa_k.reshape(bs, lc, GQA_NK), cache_gqa_v.reshape(bs, lc, GQA_NK))
            hs, _, _ = _gqa_layer(hs, seg_s, mod, g1, gw, ctx)
        else:
            mw = _mla_weights(mla_w_dq, mla_q_lora_norm, mla_w_uq, mla_w_dkv, mla_kv_norm, mla_w_ukv, mla_q_norm,
                              mla_k_norm, mla_w_o)
            hp, ckv_p, kr_p = _mla_layer(hp, seg_p, mod, g1, mw, None)
            outs['mla_ckv'] = ckv_p.reshape(bp, lp, MLA_KV_LORA)
            outs['mla_kr'] = kr_p.reshape(bp, lp, MLA_ROPE)
            hs, _, _ = _mla_layer(hs, seg_s, mod, g1, mw, (cache_mla_ckv, cache_mla_krope))
        wr = jnp.zeros((dm, ROUTE_W), F32).at[:, :MOE_GROUPS].set(moe_w_group[layer])
        wr = wr.at[:, MOE_GROUPS:MOE_GROUPS + MOE_EXPERTS].set(moe_w_expert[layer])
        br = jnp.zeros((1, ROUTE_W), F32).at[0, :MOE_GROUPS].set(moe_b_group[layer])
        br = br.at[0, MOE_GROUPS:MOE_GROUPS + MOE_EXPERTS].set(moe_b_expert[layer])
        wr_hi, wr_lo = _hi_lo(wr)
        mo = (mod, norm2_g[layer][None], wr_hi, wr_lo, br, bf(moe_w_gate[layer]), bf(moe_w_up[layer]),
              bf(moe_w_down[layer]))
        hp = _moe_layer(hp, seg_p, *mo)
        hs = _moe_layer(hs, seg_s, *mo)
    return (hp.reshape(bp, lp, dm), hs.reshape(bs, ls, dm), outs['s5_re'], outs['s5_im'], outs['rwkv'],
            outs['gqa_k'], outs['gqa_v'], outs['mla_ckv'], outs['mla_kr'])
```

```python
import functools
import math

import jax
import jax.numpy as jnp
from jax import lax
from jax.experimental import pallas as pl
from jax.experimental.pallas import tpu as pltpu

F32 = jnp.float32
BF16 = jnp.bfloat16

D_MODEL = 1024
NORM_EPS = 1e-6
ROPE_BASE = 10000.0
GRID_W = 64
S5_GROUP = 16
S5_GROUPS = D_MODEL // S5_GROUP
S5_STATE = 64
S5_N = S5_GROUPS * S5_STATE
RWKV_HEAD = 64
RWKV_HEADS = D_MODEL // RWKV_HEAD
RWKV_DECAY_SCALE = math.exp(-0.5)
RWKV_GN_EPS = 64e-5
GQA_HEADS = 16
GQA_KV_HEADS = 4
GQA_HEAD_DIM = 64
WINDOW = 128
MLA_HEADS = 16
MLA_Q_LORA = 384
MLA_KV_LORA = 256
MLA_NOPE = 64
MLA_ROPE = 32
MLA_QK = MLA_NOPE + MLA_ROPE
MLA_V = 64
MLA_PAD = 128
MOE_GROUPS = 4
MOE_PER_GROUP = 4
MOE_EXPERTS = 16
MOE_HIDDEN = 512

LANES = 128
SUBLANES = 8
MXU_DIM = 256
VMEM_LIMIT = 56 * 1024 * 1024
MOD_ROWS = 8


def _cparams(*sem):
    return pltpu.CompilerParams(dimension_semantics=sem, vmem_limit_bytes=VMEM_LIMIT)


def _dot(a, b):
    return jnp.dot(a, b, preferred_element_type=F32)


def _dot_nt(a, b):
    return lax.dot_general(a, b, (((1,), (1,)), ((), ())), preferred_element_type=F32)


def _split3(x):
    hi = x.astype(BF16)
    r1 = x - hi.astype(F32)
    mid = r1.astype(BF16)
    lo = (r1 - mid.astype(F32)).astype(BF16)
    return hi, mid, lo


def _dot_x3(x, w_hi, w_lo):
    hi, mid, _ = _split3(x)
    return _dot(hi, w_hi) + (_dot(mid, w_hi) + _dot(hi, w_lo))


def _group_sum(x, ones_bd):
    n = x.shape[-1]
    outs = []
    for c in range(n // MXU_DIM):
        hi, mid, lo = _split3(x[:, c * MXU_DIM:(c + 1) * MXU_DIM])
        outs.append(_dot(hi, ones_bd) + (_dot(mid, ones_bd) + _dot(lo, ones_bd)))
    return outs[0] if len(outs) == 1 else jnp.concatenate(outs, axis=-1)


def _modulate(x, g, shift, scale):
    ms = jnp.mean(x * x, axis=-1, keepdims=True)
    return x * lax.rsqrt(ms + NORM_EPS) * g * (1.0 + scale) + shift


def _sigmoid(x):
    return 1.0 / (1.0 + jnp.exp(-x))


def _silu(x):
    return x * _sigmoid(x)


def _ones_blockdiag(group):
    i = jnp.arange(MXU_DIM) // group
    return (i[:, None] == i[None, :]).astype(BF16)


def _hi_lo(w):
    hi = w.astype(BF16)
    return hi, (w - hi.astype(F32)).astype(BF16)


class _Seg:
    def __init__(self, nseq, length, mod0, shared_mod):
        self.nseq, self.length, self.mod0, self.shared_mod = nseq, length, mod0, shared_mod
        self.n = nseq * length

    def mod_row(self, tm):
        if self.shared_mod:
            return lambda i: self.mod0
        per = self.length // tm
        return lambda i: self.mod0 + i // per

    def mod_rows(self, mod, k):
        if self.shared_mod:
            return jnp.broadcast_to(mod[self.mod0, k][None], (self.nseq, mod.shape[-1]))
        return mod[self.mod0:self.mod0 + self.nseq, k]


def _ada_kernel(cond_ref, w_ref, b_ref, o_ref):
    s = _silu(cond_ref[...])
    w = w_ref[0]
    w_hi = w.astype(BF16)
    w_lo = (w - w_hi.astype(F32)).astype(BF16)
    o_ref[0] = _dot_x3(s, w_hi, w_lo) + b_ref[0]


def _ada_all(cond, ada_w, ada_b):
    depth, d, n6 = ada_w.shape
    tn = 1536
    return pl.pallas_call(
        _ada_kernel,
        grid=(depth, n6 // tn),
        in_specs=[
            pl.BlockSpec((MOD_ROWS, d), lambda l, j: (0, 0)),
            pl.BlockSpec((1, d, tn), lambda l, j: (l, 0, j)),
            pl.BlockSpec((1, 1, tn), lambda l, j: (l, 0, j)),
        ],
        out_specs=pl.BlockSpec((1, MOD_ROWS, tn), lambda l, j: (l, 0, j)),
        out_shape=jax.ShapeDtypeStruct((depth, MOD_ROWS, n6), F32),
        compiler_params=_cparams("parallel", "parallel"),
        name="ada_mod",
    )(cond, ada_w, ada_b.reshape(depth, 1, n6))


ROUTE_W = LANES


def _route(logits):
    lane = lax.broadcasted_iota(jnp.int32, logits.shape, 1)
    neg = jnp.float32(-jnp.inf)
    big = jnp.int32(ROUTE_W)
    gmask = lane < MOE_GROUPS
    gl = jnp.where(gmask, logits, neg)
    gmax = jnp.max(gl, axis=-1, keepdims=True)
    g_top = jnp.min(jnp.where(gl == gmax, lane, big), axis=-1, keepdims=True)
    p_g = 1.0 / jnp.sum(jnp.where(gmask, jnp.exp(gl - gmax), 0.0), axis=-1, keepdims=True)
    lo = MOE_GROUPS + MOE_PER_GROUP * g_top
    emask = (lane >= lo) & (lane < lo + MOE_PER_GROUP)
    el = jnp.where(emask, logits, neg)
    m1 = jnp.max(el, axis=-1, keepdims=True)
    i1 = jnp.min(jnp.where(el == m1, lane, big), axis=-1, keepdims=True)
    el2 = jnp.where(lane == i1, neg, el)
    m2 = jnp.max(el2, axis=-1, keepdims=True)
    i2 = jnp.min(jnp.where(el2 == m2, lane, big), axis=-1, keepdims=True)
    e2 = jnp.exp(m2 - m1)
    w1 = p_g / (1.0 + e2)
    w2 = p_g * e2 / (1.0 + e2)
    gates = jnp.where(lane == i1, w1, 0.0) + jnp.where(lane == i2, w2, 0.0)
    return gates, g_top


def _lane_col(x, idx):
    lane = lax.broadcasted_iota(jnp.int32, x.shape, 1)
    return jnp.sum(jnp.where(lane == idx, x, 0.0), axis=-1, keepdims=True)


def _moe_kernel(h_ref, mod_ref, g_ref, wr_hi_ref, wr_lo_ref, br_ref, tri_ref, wg_ref, wu_ref, wd_ref, o_ref,
                v_scr, gate_scr, pos_scr, post_scr, x_scr, gx_scr, y_scr, acc_scr, cnt_scr, *, rows):
    e = pl.program_id(1)
    grp = e // MOE_PER_GROUP
    tm = v_scr.shape[0]
    nchunk = x_scr.shape[0] // rows

    @pl.when(e == 0)
    def _():
        mod = mod_ref[0]
        v = _modulate(h_ref[...], g_ref[...], mod[3:4], mod[4:5])
        logits = _dot_x3(v, wr_hi_ref[...], wr_lo_ref[...]) + br_ref[...]
        gates, g_top = _route(logits)
        gate_scr[...] = gates
        v_scr[...] = v.astype(BF16)
        acc_scr[...] = jnp.zeros_like(acc_scr)
        lane = lax.broadcasted_iota(jnp.int32, gates.shape, 1)
        member = lane == g_top
        incl = _dot(tri_ref[...], jnp.where(member, 1.0, 0.0).astype(BF16))
        pos = jnp.where(member, incl - 1.0, -1.0)
        pos_scr[...] = pos
        pos_t = pos.T
        for g in range(MOE_GROUPS):
            post_scr[g] = jnp.broadcast_to(pos_t[g:g + 1], post_scr.shape[1:])
            cnt_scr[g] = jnp.sum(jnp.where(member & (lane == g), 1.0, 0.0)).astype(jnp.int32)

    cnt = cnt_scr[grp]

    @pl.when(e % MOE_PER_GROUP == 0)
    def _():
        g_hi, g_mid, g_lo = _split3(gate_scr[...])
        for c in range(nchunk):
            @pl.when(cnt > c * rows)
            def _(c=c):
                slot = c * rows + lax.broadcasted_iota(jnp.int32, (rows, tm), 0)
                onehot = jnp.where(post_scr[grp, 0:1, :].astype(jnp.int32) == slot, 1.0, 0.0).astype(BF16)
                sl = slice(c * rows, (c + 1) * rows)
                x_scr[sl] = _dot(onehot, v_scr[...]).astype(BF16)
                gx_scr[sl] = _dot(onehot, g_hi) + (_dot(onehot, g_mid) + _dot(onehot, g_lo))
                y_scr[sl] = jnp.zeros((rows, y_scr.shape[1]), F32)

    for c in range(nchunk):
        @pl.when(cnt > c * rows)
        def _(c=c):
            sl = slice(c * rows, (c + 1) * rows)
            x = x_scr[sl]
            gate = _lane_col(gx_scr[sl], e + MOE_GROUPS)
            hh = (_silu(_dot(x, wg_ref[0])) * _dot(x, wu_ref[0]) * gate).astype(BF16)
            y_scr[sl] += _dot(hh, wd_ref[0])

    @pl.when(e % MOE_PER_GROUP == MOE_PER_GROUP - 1)
    def _():
        pos = _lane_col(pos_scr[...], grp).astype(jnp.int32)
        for c in range(nchunk):
            @pl.when(cnt > c * rows)
            def _(c=c):
                slot = c * rows + lax.broadcasted_iota(jnp.int32, (tm, rows), 1)
                onehot_t = jnp.where(pos == slot, 1.0, 0.0).astype(BF16)
                y_hi, y_lo = _hi_lo(y_scr[c * rows:(c + 1) * rows])
                acc_scr[...] += _dot(onehot_t, y_hi) + _dot(onehot_t, y_lo)

    @pl.when(e == pl.num_programs(1) - 1)
    def _():
        o_ref[...] = h_ref[...] + mod_ref[0][5:6] * acc_scr[...]


MOE_TM = 1024
MOE_ROWS = 320


def _moe_layer(h, seg, mod, norm_g, wr_hi, wr_lo, br, wg, wu, wd, tm=MOE_TM, rows=MOE_ROWS):
    n, d = h.shape
    nexp, _, f = wg.shape
    tm = min(tm, n if seg.shared_mod else seg.length)
    rows = min(rows, tm)
    nchunk = -(-tm // rows)
    rid = seg.mod_row(tm)
    tri = jnp.tri(tm, dtype=BF16)
    return pl.pallas_call(
        functools.partial(_moe_kernel, rows=rows),
        grid=(n // tm, nexp),
        in_specs=[
            pl.BlockSpec((tm, d), lambda i, e: (i, 0)),
            pl.BlockSpec((1, 6, d), lambda i, e: (rid(i), 0, 0)),
            pl.BlockSpec((1, d), lambda i, e: (0, 0)),
            pl.BlockSpec((d, ROUTE_W), lambda i, e: (0, 0)),
            pl.BlockSpec((d, ROUTE_W), lambda i, e: (0, 0)),
            pl.BlockSpec((1, ROUTE_W), lambda i, e: (0, 0)),
            pl.BlockSpec((tm, tm), lambda i, e: (0, 0)),
            pl.BlockSpec((1, d, f), lambda i, e: (e, 0, 0)),
            pl.BlockSpec((1, d, f), lambda i, e: (e, 0, 0)),
            pl.BlockSpec((1, f, d), lambda i, e: (e, 0, 0)),
        ],
        out_specs=pl.BlockSpec((tm, d), lambda i, e: (i, 0)),
        out_shape=jax.ShapeDtypeStruct((n, d), F32),
        scratch_shapes=[
            pltpu.VMEM((tm, d), BF16),
            pltpu.VMEM((tm, ROUTE_W), F32),
            pltpu.VMEM((tm, ROUTE_W), F32),
            pltpu.VMEM((MOE_GROUPS, SUBLANES, tm), F32),
            pltpu.VMEM((nchunk * rows, d), BF16),
            pltpu.VMEM((nchunk * rows, ROUTE_W), F32),
            pltpu.VMEM((nchunk * rows, d), F32),
            pltpu.VMEM((tm, d), F32),
            pltpu.SMEM((MOE_GROUPS,), jnp.int32),
        ],
        compiler_params=_cparams("parallel", "arbitrary"),
        name="moe",
    )(h, mod, norm_g, wr_hi, wr_lo, br, tri, wg, wu, wd)


S5_TL = 32
S5_CG = MXU_DIM // S5_GROUP
S5_NCH = D_MODEL // MXU_DIM
S5_CN = S5_CG * S5_STATE
S5_SCAN_W = 512


def _s5_scan_kernel(h_ref, sh_ref, sc_ref, g_ref, s0r_ref, s0i_ref, ar_ref, ai_ref, wb_ref, wcr_ref, wci_ref,
                    y_ref, fr_ref, fi_ref, bur, bui, str_, sti):
    d = pl.program_id(1)
    c = pl.program_id(2)
    tl, nb, dm = h_ref.shape

    @pl.when(c == 0)
    def _():
        str_[...] = s0r_ref[0]
        sti[...] = s0i_ref[0]

    u = _modulate(h_ref[...], g_ref[...], sh_ref[...], sc_ref[...])
    u2 = u.reshape(tl * nb, dm).astype(BF16)
    for cc in range(S5_NCH):
        r = _dot(u2[:, cc * MXU_DIM:(cc + 1) * MXU_DIM], wb_ref[0, cc])
        bur[:, cc * S5_CN:(cc + 1) * S5_CN] = r[:, :S5_CN]
        bui[:, cc * S5_CN:(cc + 1) * S5_CN] = r[:, S5_CN:]

    for lc in range(S5_N // S5_SCAN_W):
        cols = slice(lc * S5_SCAN_W, (lc + 1) * S5_SCAN_W)
        a_r = jnp.broadcast_to(ar_ref[0, :, cols], (nb, S5_SCAN_W))
        a_i = jnp.broadcast_to(ai_ref[0, :, cols], (nb, S5_SCAN_W))

        def body(i, carry, cols=cols, a_r=a_r, a_i=a_i):
            s_r, s_i = carry
            tt = jnp.where(d == 0, i, tl - 1 - i)
            rows = pl.ds(pl.multiple_of(tt * nb, nb), nb)
            n_r = a_r * s_r - a_i * s_i + bur[rows, cols]
            n_i = a_r * s_i + a_i * s_r + bui[rows, cols]
            bur[rows, cols] = n_r
            bui[rows, cols] = n_i
            return n_r, n_i

        s_r, s_i = lax.fori_loop(0, tl, body, (str_[:, cols], sti[:, cols]))
        str_[:, cols] = s_r
        sti[:, cols] = s_i

    for cc in range(S5_NCH):
        cols = slice(cc * S5_CN, (cc + 1) * S5_CN)
        y = _dot(bur[:, cols].astype(BF16), wcr_ref[0, cc]) + _dot(bui[:, cols].astype(BF16), wci_ref[0, cc])
        y_ref[0, :, :, cc * MXU_DIM:(cc + 1) * MXU_DIM] = y.reshape(tl, nb, MXU_DIM)

    @pl.when(c == pl.num_programs(2) - 1)
    def _():
        fr_ref[0] = str_[...]
        fi_ref[0] = sti[...]


def _s5_scan(h_t, shift, scale, norm_g, s0r, s0i, a_r, a_i, wb, wcr, wci):
    length, nbt, dm = h_t.shape
    nb = SUBLANES
    tl = S5_TL
    nch = length // tl

    def tch(d, c):
        return jnp.where(d == 0, c, nch - 1 - c)

    st_spec = pl.BlockSpec((1, nb, S5_N), lambda b, d, c: (d, b, 0))
    a_spec = pl.BlockSpec((1, 1, S5_N), lambda b, d, c: (d, 0, 0))
    return pl.pallas_call(
        _s5_scan_kernel,
        grid=(nbt // nb, 2, nch),
        in_specs=[
            pl.BlockSpec((tl, nb, dm), lambda b, d, c: (tch(d, c), b, 0)),
            pl.BlockSpec((nb, dm), lambda b, d, c: (b, 0)),
            pl.BlockSpec((nb, dm), lambda b, d, c: (b, 0)),
            pl.BlockSpec((1, dm), lambda b, d, c: (0, 0)),
            st_spec, st_spec, a_spec, a_spec,
            pl.BlockSpec((1, S5_NCH, MXU_DIM, 2 * S5_CN), lambda b, d, c: (d, 0, 0, 0)),
            pl.BlockSpec((1, S5_NCH, S5_CN, MXU_DIM), lambda b, d, c: (d, 0, 0, 0)),
            pl.BlockSpec((1, S5_NCH, S5_CN, MXU_DIM), lambda b, d, c: (d, 0, 0, 0)),
        ],
        out_specs=[
            pl.BlockSpec((1, tl, nb, dm), lambda b, d, c: (d, tch(d, c), b, 0)),
            st_spec, st_spec,
        ],
        out_shape=[
            jax.ShapeDtypeStruct((2, length, nbt, dm), F32),
            jax.ShapeDtypeStruct((2, nbt, S5_N), F32),
            jax.ShapeDtypeStruct((2, nbt, S5_N), F32),
        ],
        scratch_shapes=[
            pltpu.VMEM((tl * nb, S5_N), F32), pltpu.VMEM((tl * nb, S5_N), F32),
            pltpu.VMEM((nb, S5_N), F32), pltpu.VMEM((nb, S5_N), F32),
        ],
        compiler_params=_cparams("parallel", "arbitrary", "arbitrary"),
        name="s5_scan",
    )(h_t, shift, scale, norm_g, s0r, s0i, a_r, a_i, wb, wcr, wci)


def _s5_corr_kernel(fr_ref, fi_ref, pr_ref, pi_ref, ar_ref, ai_ref, wcr_ref, wci_ref, y_in_ref, y_ref,
                    zr, zi, str_, sti, *, nseq):
    d = pl.program_id(0)
    c = pl.program_id(1)
    tl, nb, _ = y_ref.shape[1:]

    @pl.when(c == 0)
    def _():
        f_r, f_i = fr_ref[0], fi_ref[0]
        p_r, p_i = pr_ref[0], pi_ref[0]
        row = lax.broadcasted_iota(jnp.int32, f_r.shape, 0)
        e_r = jnp.zeros_like(f_r)
        e_i = jnp.zeros_like(f_i)
        for _ in range(nb // nseq - 1):
            t_r = f_r + (p_r * e_r - p_i * e_i)
            t_i = f_i + (p_r * e_i + p_i * e_r)
            fwd_r = jnp.where(row < nseq, 0.0, pltpu.roll(t_r, nseq, 0))
            fwd_i = jnp.where(row < nseq, 0.0, pltpu.roll(t_i, nseq, 0))
            bwd_r = jnp.where(row >= nb - nseq, 0.0, pltpu.roll(t_r, nb - nseq, 0))
            bwd_i = jnp.where(row >= nb - nseq, 0.0, pltpu.roll(t_i, nb - nseq, 0))
            e_r = jnp.where(d == 0, fwd_r, bwd_r)
            e_i = jnp.where(d == 0, fwd_i, bwd_i)
        str_[...] = e_r
        sti[...] = e_i

    for lc in range(S5_N // S5_SCAN_W):
        cols = slice(lc * S5_SCAN_W, (lc + 1) * S5_SCAN_W)
        a_r = jnp.broadcast_to(ar_ref[0, :, cols], (nb, S5_SCAN_W))
        a_i = jnp.broadcast_to(ai_ref[0, :, cols], (nb, S5_SCAN_W))

        def body(i, carry, cols=cols, a_r=a_r, a_i=a_i):
            s_r, s_i = carry
            tt = jnp.where(d == 0, i, tl - 1 - i)
            rows = pl.ds(pl.multiple_of(tt * nb, nb), nb)
            n_r = a_r * s_r - a_i * s_i
            n_i = a_r * s_i + a_i * s_r
            zr[rows, cols] = n_r
            zi[rows, cols] = n_i
            return n_r, n_i

        s_r, s_i = lax.fori_loop(0, tl, body, (str_[:, cols], sti[:, cols]))
        str_[:, cols] = s_r
        sti[:, cols] = s_i

    for cc in range(S5_NCH):
        cols = slice(cc * S5_CN, (cc + 1) * S5_CN)
        y = _dot(zr[:, cols].astype(BF16), wcr_ref[0, cc]) + _dot(zi[:, cols].astype(BF16), wci_ref[0, cc])
        lanes = slice(cc * MXU_DIM, (cc + 1) * MXU_DIM)
        y_ref[0, :, :, lanes] = y_in_ref[0, :, :, lanes] + y.reshape(tl, nb, MXU_DIM)


def _s5_corr(y, f_r, f_i, p_r, p_i, a_r, a_i, wcr, wci, nseq):
    _, length, nb, dm = y.shape
    tl = S5_TL
    nch = length // tl
    tch = lambda d, c: jnp.where(d == 0, c, nch - 1 - c)
    st_spec = pl.BlockSpec((1, nb, S5_N), lambda d, c: (d, 0, 0))
    a_spec = pl.BlockSpec((1, 1, S5_N), lambda d, c: (d, 0, 0))
    w_spec = pl.BlockSpec((1, S5_NCH, S5_CN, MXU_DIM), lambda d, c: (d, 0, 0, 0))
    y_spec = pl.BlockSpec((1, tl, nb, dm), lambda d, c: (d, tch(d, c), 0, 0))
    return pl.pallas_call(
        functools.partial(_s5_corr_kernel, nseq=nseq),
        grid=(2, nch),
        in_specs=[st_spec, st_spec, a_spec, a_spec, a_spec, a_spec, w_spec, w_spec, y_spec],
        out_specs=y_spec,
        out_shape=jax.ShapeDtypeStruct(y.shape, F32),
        scratch_shapes=[pltpu.VMEM((tl * nb, S5_N), F32), pltpu.VMEM((tl * nb, S5_N), F32),
                        pltpu.VMEM((nb, S5_N), F32), pltpu.VMEM((nb, S5_N), F32)],
        input_output_aliases={8: 0},
        compiler_params=_cparams("parallel", "arbitrary"),
        name="s5_corr",
    )(f_r, f_i, p_r, p_i, a_r, a_i, wcr, wci, y)


def _gelu_tanh(x):
    return 0.5 * x * (1.0 + jnp.tanh(0.7978845608028654 * (x + 0.044715 * (x * x * x))))


def _s5_glu_kernel(h_ref, mod_ref, g_ref, dsk_ref, yf_ref, yb_ref, w1_ref, w2_ref, o_ref):
    mod = mod_ref[0]
    h = h_ref[...]
    u = _modulate(h, g_ref[...], mod[0:1], mod[1:2])
    y = u * dsk_ref[...] + (yf_ref[0] + yb_ref[0])
    z = _gelu_tanh(y).astype(BF16)
    out = _dot(z, w1_ref[...]) * _sigmoid(_dot(z, w2_ref[...]))
    o_ref[...] = h + mod[2:3] * out


def _s5_glu(h, seg, mod, norm_g, d_skip, y_t, w1, w2, tm=256):
    n, dm = h.shape
    _, seg_len, nbt, _ = y_t.shape
    y2 = y_t.reshape(2, seg_len, nbt * dm)
    tm = min(tm, seg_len)
    per_seq = seg.length // tm
    per_seg = seg_len // tm
    rid = seg.mod_row(tm)
    yidx = lambda d: (lambda i: (d, (i % per_seq) % per_seg, ((i % per_seq) // per_seg) * seg.nseq + i // per_seq))
    return pl.pallas_call(
        _s5_glu_kernel,
        grid=(n // tm,),
        in_specs=[
            pl.BlockSpec((tm, dm), lambda i: (i, 0)),
            pl.BlockSpec((1, 6, dm), lambda i: (rid(i), 0, 0)),
            pl.BlockSpec((1, dm), lambda i: (0, 0)),
            pl.BlockSpec((1, dm), lambda i: (0, 0)),
            pl.BlockSpec((1, tm, dm), yidx(0)),
            pl.BlockSpec((1, tm, dm), yidx(1)),
            pl.BlockSpec((dm, dm), lambda i: (0, 0)),
            pl.BlockSpec((dm, dm), lambda i: (0, 0)),
        ],
        out_specs=pl.BlockSpec((tm, dm), lambda i: (i, 0)),
        out_shape=jax.ShapeDtypeStruct((n, dm), F32),
        compiler_params=_cparams("parallel"),
        name="s5_glu",
    )(h, mod, norm_g, d_skip, y2, y2, w1, w2)


def _s5_weights(lam_re, lam_im, log_step, b_re, b_im, c_re, c_im):
    dt = jnp.exp(log_step)
    mag = jnp.exp(lam_re * dt)
    ab_re = mag * jnp.cos(lam_im * dt)
    ab_im = mag * jnp.sin(lam_im * dt)
    nr = ab_re - 1.0
    den = lam_re * lam_re + lam_im * lam_im
    f_re = (nr * lam_re + ab_im * lam_im) / den
    f_im = (ab_im * lam_re - nr * lam_im) / den
    fb_re = f_re[..., None] * b_re - f_im[..., None] * b_im
    fb_im = f_re[..., None] * b_im + f_im[..., None] * b_re
    eye = jnp.eye(S5_CG, dtype=F32)

    def bd_in(w):
        w = w.reshape(2, S5_NCH, S5_CG, S5_STATE, S5_GROUP)
        m = jnp.einsum('dnjpc,jk->dnjckp', w, eye)
        return m.reshape(2, S5_NCH, MXU_DIM, S5_CN)

    def bd_out(w):
        w = w.reshape(2, S5_NCH, S5_CG, S5_GROUP, S5_STATE)
        m = jnp.einsum('dnjcp,jk->dnjpkc', w, eye)
        return m.reshape(2, S5_NCH, S5_CN, MXU_DIM)

    wb = jnp.concatenate([bd_in(fb_re), bd_in(fb_im)], axis=-1).astype(BF16)
    return (ab_re.reshape(2, 1, S5_N), ab_im.reshape(2, 1, S5_N), wb,
            bd_out(c_re).astype(BF16), bd_out(-c_im).astype(BF16),
            (lam_re * dt).reshape(2, 1, S5_N), (lam_im * dt).reshape(2, 1, S5_N))


def _to_time_major(x, seg, nbt):
    xt = jnp.transpose(x.reshape(seg.nseq, seg.length, -1), (1, 0, 2))
    if nbt > seg.nseq:
        xt = jnp.pad(xt, ((0, 0), (0, nbt - seg.nseq), (0, 0)))
    return xt


def _pad_rows(x, nbt):
    return x if x.shape[0] == nbt else jnp.pad(x, ((0, nbt - x.shape[0]),) + ((0, 0),) * (x.ndim - 1))


def _s5_layer(h, seg, mod, norm_g, s0_re, s0_im, sw, d_skip, w1, w2, need_final):
    a_r, a_i, wb, wcr, wci, la_r, la_i = sw
    nseg = SUBLANES // seg.nseq if SUBLANES % seg.nseq == 0 else 1
    if not need_final and nseg > 1 and seg.length % (nseg * S5_TL) == 0:
        nseq, seg_len, dm = seg.nseq, seg.length // nseg, h.shape[1]
        h_t = jnp.transpose(h.reshape(nseq, nseg, seg_len, dm), (2, 1, 0, 3)).reshape(seg_len, SUBLANES, dm)
        shift = jnp.tile(seg.mod_rows(mod, 0), (nseg, 1))
        scale = jnp.tile(seg.mod_rows(mod, 1), (nseg, 1))

        def first_rows(s0):
            z = jnp.zeros((nseg - 1, nseq, S5_N), F32)
            fwd = jnp.concatenate([s0[None, :, 0], z], axis=0).reshape(SUBLANES, S5_N)
            bwd = jnp.concatenate([z, s0[None, :, 1]], axis=0).reshape(SUBLANES, S5_N)
            return jnp.stack([fwd, bwd])

        y_t, f_r, f_i = _s5_scan(h_t, shift, scale, norm_g, first_rows(s0_re), first_rows(s0_im), a_r, a_i,
                                 wb, wcr, wci)
        mag = jnp.exp(la_r * seg_len)
        y_t = _s5_corr(y_t, f_r, f_i, mag * jnp.cos(la_i * seg_len), mag * jnp.sin(la_i * seg_len), a_r, a_i,
                       wcr, wci, nseq)
        h_new = _s5_glu(h, seg, mod, norm_g, d_skip, y_t, w1, w2)
        return h_new, None, None
    nbt = -(-seg.nseq // SUBLANES) * SUBLANES
    h_t = _to_time_major(h, seg, nbt)
    shift = _pad_rows(seg.mod_rows(mod, 0), nbt)
    scale = _pad_rows(seg.mod_rows(mod, 1), nbt)
    pad = ((0, 0), (0, nbt - seg.nseq), (0, 0))
    s0r = jnp.pad(jnp.transpose(s0_re, (1, 0, 2)), pad)
    s0i = jnp.pad(jnp.transpose(s0_im, (1, 0, 2)), pad)
    y_t, f_r, f_i = _s5_scan(h_t, shift, scale, norm_g, s0r, s0i, a_r, a_i, wb, wcr, wci)
    h_new = _s5_glu(h, seg, mod, norm_g, d_skip, y_t, w1, w2)
    return h_new, jnp.transpose(f_r[:, :seg.nseq], (1, 0, 2)), jnp.transpose(f_i[:, :seg.nseq], (1, 0, 2))


def _rwkv_proj_kernel(h_ref, hp_ref, hn_ref, mod_ref, g_ref, mu_ref, wr_ref, wk_ref, wv_ref,
                      wl1_ref, wl2_ref, al1_ref, al2_ref, gl1_ref, gl2_ref, vec_ref, w0_ref, a0_ref, ones_ref,
                      r_ref, v_ref, kk_ref, w_ref, ka_ref, kd_ref, g_out_ref, bonus_ref, *, per):
    i = pl.program_id(0)
    mod = mod_ref[0]
    gn = g_ref[...]
    tm = h_ref.shape[0]
    u = _modulate(h_ref[...], gn, mod[0:1], mod[1:2])
    first = (i % per) == 0
    last = (i % per) == per - 1
    up = _modulate(hp_ref[SUBLANES - 1:SUBLANES, :], gn, mod[0:1], mod[1:2]) * jnp.where(first, 0.0, 1.0)
    un = _modulate(hn_ref[0:1, :], gn, mod[0:1], mod[1:2]) * jnp.where(last, 0.0, 1.0)
    row = lax.broadcasted_iota(jnp.int32, u.shape, 0)
    prev = jnp.where(row == 0, up, pltpu.roll(u, 1, 0))
    nxt = jnp.where(row == tm - 1, un, pltpu.roll(u, tm - 1, 0))
    dx = 0.5 * (prev + nxt) - u
    mu = mu_ref[...]

    def mix(j):
        return (u + dx * mu[j:j + 1]).astype(BF16)

    vec = vec_ref[...]
    r = _dot(mix(0), wr_ref[...])
    k = _dot(mix(2), wk_ref[...])
    v = _dot(mix(3), wv_ref[...])
    ones = ones_ref[...]
    kk = k * vec[0:1]
    kk = kk * lax.rsqrt(_group_sum(kk * kk, ones) + 1e-12)
    g_out_ref[...] = _dot(_sigmoid(_dot(mix(5), gl1_ref[...])).astype(BF16), gl2_ref[...])
    xw = mix(1)
    xa = mix(4)
    kd_sum = None
    for d in range(2):
        wl = _dot(jnp.tanh(_dot(xw, wl1_ref[d])).astype(BF16), wl2_ref[d])
        w_ref[d] = jnp.exp(-RWKV_DECAY_SCALE * _sigmoid(w0_ref[d] + wl))
        al = _dot(_dot(xa, al1_ref[d]).astype(BF16), al2_ref[d])
        a = _sigmoid(a0_ref[d] + al)
        kd = k * (1.0 + (a - 1.0) * vec[1:2])
        kd_ref[d] = kd
        ka_ref[d] = kk * a
        kd_sum = kd if kd_sum is None else kd_sum + kd
    r_ref[...] = r
    v_ref[...] = v
    kk_ref[...] = kk
    bonus_ref[...] = _group_sum(r * kd_sum * vec[2:3], ones) * v


def _rwkv_proj(h, seg, mod, norm_g, mu, wr, wk, wv, wl1, wl2, al1, al2, gl1, gl2, vec, w0, a0, tm=256):
    n, dm = h.shape
    tm = min(tm, seg.length)
    per = seg.length // tm
    rid = seg.mod_row(tm)
    hb = tm // SUBLANES
    nblk = n // SUBLANES
    full = lambda a: pl.BlockSpec(a.shape, lambda i, nd=a.ndim: (0,) * nd)
    tile = pl.BlockSpec((tm, dm), lambda i: (i, 0))
    tile2 = pl.BlockSpec((2, tm, dm), lambda i: (0, i, 0))
    ones = _ones_blockdiag(RWKV_HEAD)
    one = jax.ShapeDtypeStruct((n, dm), F32)
    two = jax.ShapeDtypeStruct((2, n, dm), F32)
    return pl.pallas_call(
        functools.partial(_rwkv_proj_kernel, per=per),
        grid=(n // tm,),
        in_specs=[
            tile,
            pl.BlockSpec((SUBLANES, dm), lambda i: (jnp.maximum(i * hb - 1, 0), 0)),
            pl.BlockSpec((SUBLANES, dm), lambda i: (jnp.minimum((i + 1) * hb, nblk - 1), 0)),
            pl.BlockSpec((1, 6, dm), lambda i: (rid(i), 0, 0)),
            full(norm_g), full(mu), full(wr), full(wk), full(wv), full(wl1), full(wl2), full(al1), full(al2),
            full(gl1), full(gl2), full(vec), full(w0), full(a0), full(ones),
        ],
        out_specs=[tile, tile, tile, tile2, tile2, tile2, tile, tile],
        out_shape=[one, one, one, two, two, two, one, one],
        compiler_params=_cparams("parallel"),
        name="rwkv_proj",
    )(h, h, h, mod, norm_g, mu, wr, wk, wv, wl1, wl2, al1, al2, gl1, gl2, vec, w0, a0, ones)


RWKV_TC = 32


def _rwkv_chain_steps(tc, nv, load_pieces, store_pieces, st, mk, mv, ys):
    nk = st.shape[0]
    ys[...] = jnp.zeros_like(ys)
    for piece in load_pieces(0, 0):
        piece()

    def step(i, slot):
        other = 1 - slot
        side = load_pieces(jnp.minimum(i + 1, tc - 1), other) + store_pieces(jnp.maximum(i - 1, 0), other)
        every = (2 * nk) // (len(side) + 1)
        at = {(p + 1) * every: piece for p, piece in enumerate(side)}
        row = lambda q, k: mk[slot, q, k:k + 1, :]
        vv = mv[slot]
        sa = None
        for k in range(nk):
            term = st[k] * row(0, k)
            sa = term if sa is None else sa + term
            if k + 1 in at:
                at[k + 1]()
        y = None
        for k in range(nk):
            s = st[k] * row(2, k) + (vv * row(3, k) - sa * row(1, k))
            st[k] = s
            term = s * row(4, k)
            y = term if y is None else y + term
            if nk + k + 1 in at:
                at[nk + k + 1]()
        ys[slot] = y

    def two_steps(j, carry):
        step(2 * j, 0)
        step(2 * j + 1, 1)
        return carry

    lax.fori_loop(0, tc // 2, two_steps, 0)
    for piece in store_pieces(tc - 1, 1):
        piece()


def _heads_to_rows(s_scr, x, lanes, rows_per_head):
    for h in range(RWKV_HEADS):
        s_scr[h * rows_per_head:(h + 1) * rows_per_head, lanes] = x[:, h * RWKV_HEAD:(h + 1) * RWKV_HEAD]


def _rwkv_scan_grid_kernel(kk_ref, r_ref, v_ref, ka_ref, w_ref, kd_ref, s0_ref, y_ref, sf_ref,
                           st, mk, mv, ys, s_scr):
    d = pl.program_id(0)
    c = pl.program_id(2)
    tc, nb, _ = kk_ref.shape
    nk = RWKV_HEAD
    lo, hi = slice(0, nk), slice(nk, 2 * nk)

    @pl.when(c == 0)
    def _():
        st[...] = s0_ref[0]

    def load_pieces(i, slot):
        t = jnp.where(d == 0, i, tc - 1 - i)
        pairs = ((kk_ref, None, ka_ref, 0), (w_ref, 0, kd_ref, 0), (r_ref, None, v_ref, None))

        def piece(p):
            ra, da, rb, db = pairs[p]
            xa = ra[t] if da is None else ra[da, t]
            xb = rb[t] if db is None else rb[db, t]
            _heads_to_rows(s_scr.at[p], xa, lo, nb)
            _heads_to_rows(s_scr.at[p], xb, hi, nb)
            tt = s_scr[p].T
            mk[slot, 2 * p] = tt[lo]
            if p < 2:
                mk[slot, 2 * p + 1] = tt[hi]
            else:
                mv[slot] = tt[hi]
        return [functools.partial(piece, p) for p in range(3)]

    def store_pieces(i, slot):
        t = jnp.where(d == 0, i, tc - 1 - i)

        def piece():
            yt = ys[slot].T
            for h in range(RWKV_HEADS):
                y_ref[0, t, :, h * nk:(h + 1) * nk] = yt[h * nb:(h + 1) * nb, :]
        return [piece]

    _rwkv_chain_steps(tc, nk, load_pieces, store_pieces, st, mk, mv, ys)

    @pl.when(c == pl.num_programs(2) - 1)
    def _():
        sf_ref[0] = st[...]


def _rwkv_scan_grid(kk, r, v, ka, w, kd, s0):
    n, nseq, dm = kk.shape
    nk = RWKV_HEAD
    nb = SUBLANES
    tc = RWKV_TC
    nch = n // tc
    tch = lambda d, t: jnp.where(d == 0, t, nch - 1 - t)
    shared = pl.BlockSpec((tc, nb, dm), lambda d, l, t: (tch(d, t), l, 0))
    perdir = pl.BlockSpec((1, tc, nb, dm), lambda d, l, t: (d, tch(d, t), l, 0))
    sspec = pl.BlockSpec((1, nk, nk, LANES), lambda d, l, t: (d, 0, 0, l))
    return pl.pallas_call(
        _rwkv_scan_grid_kernel,
        grid=(2, nseq // nb, nch),
        in_specs=[shared, shared, shared, perdir, perdir, perdir, sspec],
        out_specs=[perdir, sspec],
        out_shape=[jax.ShapeDtypeStruct((2, n, nseq, dm), F32), jax.ShapeDtypeStruct(s0.shape, F32)],
        scratch_shapes=[pltpu.VMEM((nk, nk, LANES), F32), pltpu.VMEM((2, 5, nk, LANES), F32),
                        pltpu.VMEM((2, nk, LANES), F32), pltpu.VMEM((2, nk, LANES), F32),
                        pltpu.VMEM((3, LANES, LANES), F32)],
        compiler_params=_cparams("parallel", "parallel", "arbitrary"),
        name="rwkv_scan_grid",
    )(kk, r, v, ka, w, kd, s0)


def _rwkv_scan_lanes_kernel(kkf, rf, vf, kaf, wf, kdf, kkr, rr, vr, kar, wr, kdr, s0_ref, yf_ref, yr_ref, sf_ref,
                            st, mk, mv, ys, s_scr, y_scr):
    c = pl.program_id(0)
    tc, nb, _ = kkf.shape
    nk = RWKV_HEAD
    chains = s_scr.shape[1]
    vs = LANES // chains
    nv = nk // vs
    lo, hi = slice(0, nk), slice(nk, 2 * nk)

    @pl.when(c == 0)
    def _():
        st[...] = s0_ref[...]

    def load_pieces(i, slot):
        j = tc - 1 - i
        both = lambda f, r: jnp.concatenate([f, r], axis=0)
        rep = lambda a: a if vs == 1 else jnp.concatenate([a] * vs, axis=1)

        def piece(p):
            if p == 0:
                xa, xb = both(kkf[i], kkr[j]), both(kaf[0, i], kar[0, j])
            elif p == 1:
                xa, xb = both(wf[0, i], wr[0, j]), both(kdf[0, i], kdr[0, j])
            else:
                xa, xb = both(rf[i], rr[j]), both(vf[i], vr[j])
            _heads_to_rows(s_scr.at[p], xa, lo, 2 * nb)
            _heads_to_rows(s_scr.at[p], xb, hi, 2 * nb)
            tt = s_scr[p].T
            mk[slot, 2 * p] = rep(tt[lo])
            if p < 2:
                mk[slot, 2 * p + 1] = rep(tt[hi])
            else:
                tv = tt[hi]
                mv[slot] = tv if vs == 1 else jnp.concatenate([tv[s * nv:(s + 1) * nv] for s in range(vs)], axis=1)
        return [functools.partial(piece, p) for p in range(3)]

    def store_pieces(i, slot):
        def piece():
            yt = ys[slot].T
            for s in range(vs):
                for h in range(RWKV_HEADS):
                    r0 = s * chains + h * 2 * nb
                    y_scr[0:2 * nb, h * nk + s * nv:h * nk + (s + 1) * nv] = yt[r0:r0 + 2 * nb, :]
            yf_ref[i] = y_scr[0:nb, :]
            yr_ref[tc - 1 - i] = y_scr[nb:2 * nb, :]
        return [piece]

    _rwkv_chain_steps(tc, nv, load_pieces, store_pieces, st, mk, mv, ys)

    @pl.when(c == pl.num_programs(0) - 1)
    def _():
        sf_ref[...] = st[...]


def _rwkv_scan_lanes(kk, r, v, ka, w, kd, s0):
    n, nb, dm = kk.shape
    nk = RWKV_HEAD
    nv = s0.shape[1]
    chains = RWKV_HEADS * 2 * nb
    tc = RWKV_TC
    nch = n // tc
    sf_ = pl.BlockSpec((tc, nb, dm), lambda t: (t, 0, 0))
    sr_ = pl.BlockSpec((tc, nb, dm), lambda t: (nch - 1 - t, 0, 0))
    df_ = pl.BlockSpec((1, tc, nb, dm), lambda t: (0, t, 0, 0))
    dr_ = pl.BlockSpec((1, tc, nb, dm), lambda t: (1, nch - 1 - t, 0, 0))
    sspec = pl.BlockSpec((nk, nv, LANES), lambda t: (0, 0, 0))
    yshape = jax.ShapeDtypeStruct((n, nb, dm), F32)
    return pl.pallas_call(
        _rwkv_scan_lanes_kernel,
        grid=(nch,),
        in_specs=[sf_, sf_, sf_, df_, df_, df_, sr_, sr_, sr_, dr_, dr_, dr_, sspec],
        out_specs=[sf_, sr_, sspec],
        out_shape=[yshape, yshape, jax.ShapeDtypeStruct((nk, nv, LANES), F32)],
        scratch_shapes=[pltpu.VMEM((nk, nv, LANES), F32), pltpu.VMEM((2, 5, nk, LANES), F32),
                        pltpu.VMEM((2, nv, LANES), F32), pltpu.VMEM((2, nv, LANES), F32),
                        pltpu.VMEM((3, chains, LANES), F32), pltpu.VMEM((SUBLANES, dm), F32)],
        compiler_params=_cparams("arbitrary"),
        name="rwkv_scan_lanes",
    )(kk, r, v, ka, w, kd, kk, r, v, ka, w, kd, s0)


def _rwkv_out_kernel(h_ref, mod_ref, yf_ref, yb_ref, bonus_ref, g_ref, lng_ref, lnb_ref, wo_ref, ones_ref, o_ref):
    ones = ones_ref[...]
    y = yf_ref[...] + yb_ref[...]
    inv = 1.0 / RWKV_HEAD
    mean = _group_sum(y, ones) * inv
    yc = y - mean
    var = _group_sum(yc * yc, ones) * inv
    yn = yc * lax.rsqrt(var + RWKV_GN_EPS) * lng_ref[...] + lnb_ref[...] + bonus_ref[...]
    out = _dot((yn * g_ref[...]).astype(BF16), wo_ref[...])
    o_ref[...] = h_ref[...] + mod_ref[0][2:3] * out


def _rwkv_out(h, seg, mod, yf, yb, bonus, g, ln_g, ln_b, wo, tm=256):
    n, dm = h.shape
    tm = min(tm, seg.length)
    rid = seg.mod_row(tm)
    tile = pl.BlockSpec((tm, dm), lambda i: (i, 0))
    row = pl.BlockSpec((1, dm), lambda i: (0, 0))
    ones = _ones_blockdiag(RWKV_HEAD)
    return pl.pallas_call(
        _rwkv_out_kernel,
        grid=(n // tm,),
        in_specs=[tile, pl.BlockSpec((1, 6, dm), lambda i: (rid(i), 0, 0)), tile, tile, tile, tile, row, row,
                  pl.BlockSpec((dm, dm), lambda i: (0, 0)), pl.BlockSpec(ones.shape, lambda i: (0, 0))],
        out_specs=tile,
        out_shape=jax.ShapeDtypeStruct((n, dm), F32),
        compiler_params=_cparams("parallel"),
        name="rwkv_out",
    )(h, mod, yf, yb, bonus, g, ln_g, ln_b, wo, ones)


def _rwkv_layer(h, seg, mod, norm_g, s0, rw):
    nseq, length = seg.nseq, seg.length
    hh, nk = RWKV_HEADS, RWKV_HEAD
    r, v, kk, w, ka, kd, g, bonus = _rwkv_proj(h, seg, mod, norm_g, *rw['proj'])
    per_dir = nseq * hh
    n = nseq * length
    dm = hh * nk

    def tmaj(x):
        lead = x.shape[:-2]
        nl = len(lead)
        return jnp.transpose(x.reshape(*lead, nseq, length, dm), tuple(range(nl)) + (nl + 1, nl, nl + 2))

    def nat(y):
        lead = y.shape[:-3]
        nl = len(lead)
        return jnp.transpose(y, tuple(range(nl)) + (nl + 1, nl, nl + 2)).reshape(*lead, n, dm)

    if per_dir % LANES == 0:
        ng = nseq // SUBLANES
        s0_s = jnp.transpose(s0.reshape(ng, SUBLANES, 2, hh, nk, nk), (2, 5, 4, 0, 3, 1)).reshape(2, nk, nk, per_dir)
        y_t, sf = _rwkv_scan_grid(tmaj(kk), tmaj(r), tmaj(v), tmaj(ka), tmaj(w), tmaj(kd), s0_s)
        y2 = nat(y_t)
        yf, yb = y2[0], y2[1]
        s_fin = jnp.transpose(sf.reshape(2, nk, nk, ng, hh, SUBLANES), (3, 5, 0, 4, 2, 1)).reshape(nseq, 2, hh, nk, nk)
    else:
        vs = LANES // (2 * per_dir)
        assert vs * 2 * per_dir == LANES
        nv = nk // vs
        s0_s = jnp.transpose(s0.reshape(nseq, 2, hh, vs, nv, nk), (5, 4, 3, 2, 1, 0)).reshape(nk, nv, LANES)
        y_f, y_r, sf = _rwkv_scan_lanes(tmaj(kk), tmaj(r), tmaj(v), tmaj(ka), tmaj(w), tmaj(kd), s0_s)
        yf, yb = nat(y_f), nat(y_r)
        s_fin = jnp.transpose(sf.reshape(nk, nv, vs, hh, 2, nseq), (5, 4, 3, 2, 1, 0)).reshape(nseq, 2, hh, nk, nk)
    h_new = _rwkv_out(h, seg, mod, yf, yb, bonus, g, *rw['out'])
    return h_new, s_fin


def _rope_tables(length, half, first_lane, period):
    lane = jnp.arange(LANES) % period - first_lane
    active = (lane >= 0) & (lane < 4 * half)
    quarter = jnp.clip(lane, 0, 4 * half - 1) // half
    freq = ROPE_BASE ** (-(jnp.clip(lane, 0, 4 * half - 1) % half).astype(F32) / half)
    t = jnp.arange(length)
    pos = jnp.where(quarter[None, :] < 2, (t // GRID_W)[:, None], (t % GRID_W)[:, None]).astype(F32)
    ang = pos * freq[None, :]
    cos = jnp.where(active[None], jnp.cos(ang), 1.0)
    sin = jnp.where(active[None], jnp.sin(ang), 0.0)
    first = (quarter % 2 == 0)[None]
    return cos, jnp.where(first, -sin, 0.0), jnp.where(first, 0.0, sin)


def _rope(x, cos, sa, sb, half):
    w = x.shape[-1]
    rep = w // LANES
    tile = (lambda a: a) if rep == 1 else (lambda a: jnp.concatenate([a] * rep, axis=-1))
    return x * tile(cos) + pltpu.roll(x, w - half, 1) * tile(sa) + pltpu.roll(x, half, 1) * tile(sb)


def _attn_kernel(*refs, groups, dq, dv, hpb, scale, band, has_ctx, has_sink):
    it = iter(refs)
    q_ref, k_ref, v_ref = next(it), next(it), next(it)
    kc_ref, vc_ref = (next(it), next(it)) if has_ctx else (None, None)
    sink_ref = next(it) if has_sink else None
    o_ref = next(it)
    tq = q_ref.shape[1]
    lk = k_ref.shape[1]
    rows = groups * tq
    if band:
        lw = min(lk, tq + 2 * WINDOW)
        q0 = pl.program_id(2) * tq
        start = pl.multiple_of(jnp.clip(q0 - WINDOW, 0, lk - lw), LANES)
        ksl = pl.ds(start, lw)
        qpos = q0 + (lax.broadcasted_iota(jnp.int32, (rows, lw), 0) & (tq - 1))
        kpos = start + lax.broadcasted_iota(jnp.int32, (rows, lw), 1)
        valid = jnp.abs(qpos - kpos) <= WINDOW
    else:
        ksl = slice(None)
    for j in range(hpb):
        kh = k_ref[0, ksl, j * dq:(j + 1) * dq].astype(BF16)
        vh = v_ref[0, ksl, j * dv:(j + 1) * dv].astype(BF16)
        heads = range(j * groups, (j + 1) * groups)
        qs = [q_ref[0, :, hq * dq:(hq + 1) * dq].astype(BF16) for hq in heads]
        qh = qs[0] if groups == 1 else jnp.concatenate(qs, axis=0)
        s = _dot_nt(qh, kh) * scale
        if band:
            s = jnp.where(valid, s, -jnp.inf)
        m = jnp.max(s, axis=-1, keepdims=True)
        if has_ctx:
            kch = kc_ref[0, :, j * dq:(j + 1) * dq].astype(BF16)
            vch = vc_ref[0, :, j * dv:(j + 1) * dv].astype(BF16)
            sc = _dot_nt(qh, kch) * scale
            m = jnp.maximum(m, jnp.max(sc, axis=-1, keepdims=True))
        if has_sink:
            sk = [jnp.broadcast_to(sink_ref[0, :, hq:hq + 1], (tq, 1)) for hq in heads]
            snk = sk[0] if groups == 1 else jnp.concatenate(sk, axis=0)
            m = jnp.maximum(m, snk)
        p = jnp.exp(s - m)
        den = jnp.sum(p, axis=-1, keepdims=True)
        acc = _dot(p.astype(BF16), vh)
        if has_ctx:
            pc = jnp.exp(sc - m)
            den = den + jnp.sum(pc, axis=-1, keepdims=True)
            acc = acc + _dot(pc.astype(BF16), vch)
        if has_sink:
            den = den + jnp.exp(snk - m)
        out = acc / den
        for g, hq in enumerate(heads):
            o_ref[0, :, hq * dv:(hq + 1) * dv] = out[g * tq:(g + 1) * tq]


def _attention(q, k, v, ctx, sink, *, heads, groups, dq, dv, scale, band, tq=256):
    b, lq, _ = q.shape
    lk = k.shape[1]
    hpb = 2
    nhb = heads // hpb
    ins = [q, k, v]
    specs = [
        pl.BlockSpec((1, tq, hpb * groups * dq), lambda bi, hb, i: (bi, i, hb)),
        pl.BlockSpec((1, lk, hpb * dq), lambda bi, hb, i: (bi, 0, hb)),
        pl.BlockSpec((1, lk, hpb * dv), lambda bi, hb, i: (bi, 0, hb)),
    ]
    if ctx is not None:
        lc = ctx[0].shape[1]
        ins += list(ctx)
        specs += [pl.BlockSpec((1, lc, hpb * dq), lambda bi, hb, i: (bi, 0, hb)),
                  pl.BlockSpec((1, lc, hpb * dv), lambda bi, hb, i: (bi, 0, hb))]
    if sink is not None:
        sk = jnp.pad(sink.reshape(nhb, 1, hpb * groups), ((0, 0), (0, 0), (0, LANES - hpb * groups)))
        ins.append(sk)
        specs.append(pl.BlockSpec((1, 1, LANES), lambda bi, hb, i: (hb, 0, 0)))
    return pl.pallas_call(
        functools.partial(_attn_kernel, groups=groups, dq=dq, dv=dv, hpb=hpb, scale=scale, band=band,
                          has_ctx=ctx is not None, has_sink=sink is not None),
        grid=(b, nhb, lq // tq),
        in_specs=specs,
        out_specs=pl.BlockSpec((1, tq, hpb * groups * dv), lambda bi, hb, i: (bi, i, hb)),
        out_shape=jax.ShapeDtypeStruct((b, lq, heads * groups * dv), F32),
        compiler_params=_cparams("parallel", "parallel", "arbitrary"),
        name="attention",
    )(*ins)


def _proj_res_kernel(h_ref, mod_ref, a_ref, w_ref, o_ref):
    o_ref[...] = h_ref[...] + mod_ref[0][2:3] * _dot(a_ref[...].astype(BF16), w_ref[...])


def _proj_residual(h, seg, mod, a, w, tm=256):
    n, dm = h.shape
    tm = min(tm, seg.length)
    rid = seg.mod_row(tm)
    return pl.pallas_call(
        _proj_res_kernel,
        grid=(n // tm,),
        in_specs=[pl.BlockSpec((tm, dm), lambda i: (i, 0)), pl.BlockSpec((1, 6, dm), lambda i: (rid(i), 0, 0)),
                  pl.BlockSpec((tm, a.shape[1]), lambda i: (i, 0)), pl.BlockSpec(w.shape, lambda i: (0, 0))],
        out_specs=pl.BlockSpec((tm, dm), lambda i: (i, 0)),
        out_shape=jax.ShapeDtypeStruct((n, dm), F32),
        compiler_params=_cparams("parallel"),
        name="proj_residual",
    )(h, mod, a, w)


GQA_NQ = GQA_HEADS * GQA_HEAD_DIM
GQA_NK = GQA_KV_HEADS * GQA_HEAD_DIM
GQA_ROPE_HALF = GQA_HEAD_DIM // 4


def _gqa_qkv_kernel(*refs, use_rope):
    if use_rope:
        h_ref, mod_ref, g_ref, w_ref, gq_ref, gk_ref, ones_ref, cos_ref, sa_ref, sb_ref, q_ref, k_ref, v_ref = refs
    else:
        h_ref, mod_ref, g_ref, w_ref, gq_ref, gk_ref, ones_ref, q_ref, k_ref, v_ref = refs
    mod = mod_ref[0]
    u = _modulate(h_ref[...], g_ref[...], mod[0:1], mod[1:2]).astype(BF16)
    qkv = _dot(u, w_ref[...])
    ones = ones_ref[...]
    inv = 1.0 / GQA_HEAD_DIM
    q = qkv[:, :GQA_NQ]
    k = qkv[:, GQA_NQ:GQA_NQ + GQA_NK]
    q = q * lax.rsqrt(_group_sum(q * q, ones) * inv + NORM_EPS) * gq_ref[...]
    k = k * lax.rsqrt(_group_sum(k * k, ones) * inv + NORM_EPS) * gk_ref[...]
    if use_rope:
        cos, sa, sb = cos_ref[...], sa_ref[...], sb_ref[...]
        q = _rope(q, cos, sa, sb, GQA_ROPE_HALF)
        k = _rope(k, cos, sa, sb, GQA_ROPE_HALF)
    q_ref[...] = q
    k_ref[...] = k
    v_ref[...] = qkv[:, GQA_NQ + GQA_NK:]


def _gqa_qkv(h, seg, mod, norm_g, w_qkv, gq, gk, rope, tm=256):
    n, dm = h.shape
    tm = min(tm, seg.length)
    per = seg.length // tm
    rid = seg.mod_row(tm)
    ones = _ones_blockdiag(GQA_HEAD_DIM)
    full = lambda a: pl.BlockSpec(a.shape, lambda i, nd=a.ndim: (0,) * nd)
    ins = [h, mod, norm_g, w_qkv, gq, gk, ones]
    specs = [pl.BlockSpec((tm, dm), lambda i: (i, 0)), pl.BlockSpec((1, 6, dm), lambda i: (rid(i), 0, 0)),
             full(norm_g), full(w_qkv), full(gq), full(gk), full(ones)]
    if rope is not None:
        ins += list(rope)
        specs += [pl.BlockSpec((tm, LANES), lambda i: (i % per, 0))] * 3
    return pl.pallas_call(
        functools.partial(_gqa_qkv_kernel, use_rope=rope is not None),
        grid=(n // tm,),
        in_specs=specs,
        out_specs=[pl.BlockSpec((tm, GQA_NQ), lambda i: (i, 0)), pl.BlockSpec((tm, GQA_NK), lambda i: (i, 0)),
                   pl.BlockSpec((tm, GQA_NK), lambda i: (i, 0))],
        out_shape=[jax.ShapeDtypeStruct((n, GQA_NQ), F32), jax.ShapeDtypeStruct((n, GQA_NK), F32),
                   jax.ShapeDtypeStruct((n, GQA_NK), F32)],
        compiler_params=_cparams("parallel"),
        name="gqa_qkv",
    )(*ins)


def _gqa_layer(h, seg, mod, norm_g, gw, ctx):
    w_qkv, gq, gk, sink, w_o = gw
    latent = ctx is not None
    rope = _rope_tables(seg.length, GQA_ROPE_HALF, 0, 4 * GQA_ROPE_HALF) if latent else None
    q, k, v = _gqa_qkv(h, seg, mod, norm_g, w_qkv, gq, gk, rope)
    b3 = lambda a: a.reshape(seg.nseq, seg.length, a.shape[-1])
    att = _attention(b3(q), b3(k), b3(v), ctx, sink, heads=GQA_KV_HEADS, groups=GQA_HEADS // GQA_KV_HEADS,
                     dq=GQA_HEAD_DIM, dv=GQA_HEAD_DIM, scale=GQA_HEAD_DIM ** -0.5, band=latent)
    h_new = _proj_residual(h, seg, mod, att.reshape(seg.n, -1), w_o)
    return h_new, k, v


MLA_QW = MLA_HEADS * MLA_PAD
MLA_VW = MLA_HEADS * MLA_V
MLA_CKR = MLA_KV_LORA + LANES
MLA_ROPE_HALF = MLA_ROPE // 4


def _mla_expand(ckv, kr, wuk_ref, wuv_ref, kg_ref, ones_ref):
    cb = ckv.astype(BF16)
    kn = _dot(cb, wuk_ref[...])
    kr_t = jnp.concatenate([pltpu.roll(kr, MLA_NOPE, 1)] * MLA_HEADS, axis=-1)
    k = kn + kr_t
    k = k * lax.rsqrt(_group_sum(k * k, ones_ref[...]) * (1.0 / MLA_QK) + NORM_EPS) * kg_ref[...]
    return k, _dot(cb, wuv_ref[...])


def _mla_proj_kernel(*refs, use_rope):
    (h_ref, mod_ref, g_ref, wdq_ref, qlg_ref, wuq_ref, qg_ref, wdkv_ref, kvg_ref, wuk_ref, wuv_ref, kg_ref,
     ones_ref) = refs[:13]
    if use_rope:
        cos_ref, sa_ref, sb_ref = refs[13:16]
    q_ref, k_ref, v_ref, ckv_ref, kr_ref = refs[-5:]
    mod = mod_ref[0]
    u = _modulate(h_ref[...], g_ref[...], mod[0:1], mod[1:2]).astype(BF16)
    qd = _dot(u, wdq_ref[...])
    qd = qd * lax.rsqrt(jnp.mean(qd * qd, axis=-1, keepdims=True) + NORM_EPS) * qlg_ref[...]
    q = _dot(qd.astype(BF16), wuq_ref[...])
    q = q * lax.rsqrt(_group_sum(q * q, ones_ref[...]) * (1.0 / MLA_QK) + NORM_EPS) * qg_ref[...]
    ckr = _dot(u, wdkv_ref[...])
    ckv = ckr[:, :MLA_KV_LORA]
    ckv = ckv * lax.rsqrt(jnp.mean(ckv * ckv, axis=-1, keepdims=True) + NORM_EPS) * kvg_ref[...]
    kr = ckr[:, MLA_KV_LORA:]
    k, v = _mla_expand(ckv, kr, wuk_ref, wuv_ref, kg_ref, ones_ref)
    if use_rope:
        cos, sa, sb = cos_ref[...], sa_ref[...], sb_ref[...]
        q = _rope(q, cos, sa, sb, MLA_ROPE_HALF)
        k = _rope(k, cos, sa, sb, MLA_ROPE_HALF)
    q_ref[...] = q.astype(BF16)
    k_ref[...] = k.astype(BF16)
    v_ref[...] = v.astype(BF16)
    ckv_ref[...] = ckv
    kr_ref[...] = kr


def _mla_proj(h, seg, mod, norm_g, pw, rope, tm=256):
    n, dm = h.shape
    tm = min(tm, seg.length)
    per = seg.length // tm
    rid = seg.mod_row(tm)
    ones = _ones_blockdiag(MLA_PAD)
    full = lambda a: pl.BlockSpec(a.shape, lambda i, nd=a.ndim: (0,) * nd)
    ins = [h, mod, norm_g, *pw, ones]
    specs = [pl.BlockSpec((tm, dm), lambda i: (i, 0)), pl.BlockSpec((1, 6, dm), lambda i: (rid(i), 0, 0)),
             full(norm_g)] + [full(a) for a in pw] + [full(ones)]
    if rope is not None:
        ins += list(rope)
        specs += [pl.BlockSpec((tm, LANES), lambda i: (i % per, 0))] * 3
    widths = (MLA_QW, MLA_QW, MLA_VW, MLA_KV_LORA, LANES)
    return pl.pallas_call(
        functools.partial(_mla_proj_kernel, use_rope=rope is not None),
        grid=(n // tm,),
        in_specs=specs,
        out_specs=[pl.BlockSpec((tm, w), lambda i: (i, 0)) for w in widths],
        out_shape=[jax.ShapeDtypeStruct((n, w), BF16 if j < 3 else F32) for j, w in enumerate(widths)],
        compiler_params=_cparams("parallel"),
        name="mla_proj",
    )(*ins)


def _mla_ctx_kernel(ckv_ref, kr_ref, wuk_ref, wuv_ref, kg_ref, ones_ref, k_ref, v_ref):
    k, v = _mla_expand(ckv_ref[...], kr_ref[...], wuk_ref, wuv_ref, kg_ref, ones_ref)
    k_ref[...] = k.astype(BF16)
    v_ref[...] = v.astype(BF16)


def _mla_ctx_expand(ckv, kr, wuk, wuv, kg, tm=256):
    n = ckv.shape[0]
    tm = min(tm, n)
    ones = _ones_blockdiag(MLA_PAD)
    full = lambda a: pl.BlockSpec(a.shape, lambda i, nd=a.ndim: (0,) * nd)
    return pl.pallas_call(
        _mla_ctx_kernel,
        grid=(n // tm,),
        in_specs=[pl.BlockSpec((tm, MLA_KV_LORA), lambda i: (i, 0)), pl.BlockSpec((tm, LANES), lambda i: (i, 0)),
                  full(wuk), full(wuv), full(kg), full(ones)],
        out_specs=[pl.BlockSpec((tm, MLA_QW), lambda i: (i, 0)), pl.BlockSpec((tm, MLA_VW), lambda i: (i, 0))],
        out_shape=[jax.ShapeDtypeStruct((n, MLA_QW), BF16), jax.ShapeDtypeStruct((n, MLA_VW), BF16)],
        compiler_params=_cparams("parallel"),
        name="mla_ctx_expand",
    )(ckv, kr, wuk, wuv, kg, ones)


def _mla_weights(w_dq, q_lora_g, w_uq, w_dkv, kv_g, w_ukv, q_g, k_g, w_o):
    padh = lambda a: jnp.pad(a, [(0, 0)] * (a.ndim - 1) + [(0, MLA_PAD - a.shape[-1])])
    wuq = padh(w_uq.reshape(MLA_Q_LORA, MLA_HEADS, MLA_QK)).reshape(MLA_Q_LORA, MLA_QW).astype(BF16)
    wdkv = jnp.pad(w_dkv, ((0, 0), (0, MLA_CKR - w_dkv.shape[1]))).astype(BF16)
    ukv = w_ukv.reshape(MLA_KV_LORA, MLA_HEADS, MLA_NOPE + MLA_V)
    wuk = padh(ukv[:, :, :MLA_NOPE]).reshape(MLA_KV_LORA, MLA_QW).astype(BF16)
    wuv = ukv[:, :, MLA_NOPE:].reshape(MLA_KV_LORA, MLA_VW).astype(BF16)
    tile_h = lambda g: jnp.tile(padh(g), MLA_HEADS)[None]
    proj = (w_dq.astype(BF16), q_lora_g[None], wuq, tile_h(q_g), wdkv, kv_g[None], wuk, wuv, tile_h(k_g))
    return proj, w_o.astype(BF16)


def _mla_layer(h, seg, mod, norm_g, mw, cache):
    proj, w_o = mw
    latent = cache is not None
    rope = _rope_tables(seg.length, MLA_ROPE_HALF, MLA_NOPE, MLA_PAD) if latent else None
    q, k, v, ckv, kr = _mla_proj(h, seg, mod, norm_g, proj, rope)
    ctx = None
    if latent:
        c_ckv, c_kr = cache
        lc = c_ckv.shape[1]
        kr_p = jnp.pad(c_kr.reshape(-1, MLA_ROPE), ((0, 0), (0, LANES - MLA_ROPE)))
        kc, vc = _mla_ctx_expand(c_ckv.reshape(-1, MLA_KV_LORA), kr_p, proj[6], proj[7], proj[8])
        ctx = (kc.reshape(seg.nseq, lc, MLA_QW), vc.reshape(seg.nseq, lc, MLA_VW))
    b3 = lambda a: a.reshape(seg.nseq, seg.length, a.shape[-1])
    att = _attention(b3(q), b3(k), b3(v), ctx, None, heads=MLA_HEADS, groups=1, dq=MLA_PAD, dv=MLA_V,
                     scale=MLA_QK ** -0.5, band=False)
    h_new = _proj_residual(h, seg, mod, att.reshape(seg.n, -1), w_o)
    return h_new, ckv, kr[:, :MLA_ROPE]


def kernel(x_prompt, x_sample, state_s5_re, state_s5_im, state_rwkv, cache_gqa_k, cache_gqa_v, cache_mla_ckv, cache_mla_krope, c, c_ctx, ada_w, ada_b, norm1_g, norm2_g, s5_lambda_re, s5_lambda_im, s5_log_step, s5_b_re, s5_b_im, s5_c_re, s5_c_im, s5_d, s5_glu_w1, s5_glu_w2, rwkv_mu, rwkv_w_r, rwkv_w_k, rwkv_w_v, rwkv_w_o, rwkv_w0, rwkv_w_l1, rwkv_w_l2, rwkv_a0, rwkv_a_l1, rwkv_a_l2, rwkv_g_l1, rwkv_g_l2, rwkv_k_k, rwkv_k_a, rwkv_r_k, rwkv_ln_g, rwkv_ln_b, gqa_w_qkv, gqa_q_norm, gqa_k_norm, gqa_sink, gqa_w_o, mla_w_dq, mla_q_lora_norm, mla_w_uq, mla_w_dkv, mla_kv_norm, mla_w_ukv, mla_q_norm, mla_k_norm, mla_w_o, moe_w_group, moe_b_group, moe_w_expert, moe_b_expert, moe_w_gate, moe_w_up, moe_w_down):
    bp, lp, dm = x_prompt.shape
    bs, ls, _ = x_sample.shape
    depth = ada_w.shape[0]
    assert dm == D_MODEL and 1 + bs <= MOD_ROWS
    seg_p = _Seg(bp, lp, 0, True)
    seg_s = _Seg(bs, ls, 1, False)
    bf = lambda a: a.astype(BF16)

    cond = jnp.zeros((MOD_ROWS, dm), F32).at[0].set(c_ctx).at[1:1 + bs].set(c)
    mods = _ada_all(cond, ada_w, ada_b).reshape(depth, MOD_ROWS, 6, dm)

    hp = x_prompt.reshape(bp * lp, dm)
    hs = x_sample.reshape(bs * ls, dm)
    outs = {}
    for layer in range(depth):
        kind = layer % 4
        mod = mods[layer]
        g1 = norm1_g[layer][None]
        if kind == 0:
            sw = _s5_weights(s5_lambda_re, s5_lambda_im, s5_log_step, s5_b_re, s5_b_im, s5_c_re, s5_c_im)
            w1, w2, dsk = bf(s5_glu_w1), bf(s5_glu_w2), s5_d[None]
            zeros = jnp.zeros((bp, 2, S5_N), F32)
            hp, f_re, f_im = _s5_layer(hp, seg_p, mod, g1, zeros, zeros, sw, dsk, w1, w2, True)
            hs, _, _ = _s5_layer(hs, seg_s, mod, g1, state_s5_re.reshape(bs, 2, S5_N),
                                 state_s5_im.reshape(bs, 2, S5_N), sw, dsk, w1, w2, False)
            outs['s5_re'] = f_re.reshape(bp, 2, S5_GROUPS, S5_STATE)
            outs['s5_im'] = f_im.reshape(bp, 2, S5_GROUPS, S5_STATE)
        elif kind == 1:
            vec = jnp.zeros((SUBLANES, dm), F32).at[0].set(rwkv_k_k).at[1].set(rwkv_k_a).at[2].set(rwkv_r_k.reshape(-1))
            rw = dict(
                proj=(rwkv_mu, bf(rwkv_w_r), bf(rwkv_w_k), bf(rwkv_w_v), bf(rwkv_w_l1), bf(rwkv_w_l2),
                      bf(rwkv_a_l1), bf(rwkv_a_l2), bf(rwkv_g_l1), bf(rwkv_g_l2), vec,
                      rwkv_w0[:, None, :], rwkv_a0[:, None, :]),
                out=(rwkv_ln_g[None], rwkv_ln_b[None], bf(rwkv_w_o)))
            zeros = jnp.zeros((bp, 2, RWKV_HEADS, RWKV_HEAD, RWKV_HEAD), F32)
            hp, outs['rwkv'] = _rwkv_layer(hp, seg_p, mod, g1, zeros, rw)
            hs, _ = _rwkv_layer(hs, seg_s, mod, g1, state_rwkv, rw)
        elif kind == 2:
            gw = (bf(gqa_w_qkv), jnp.tile(gqa_q_norm, GQA_HEADS)[None], jnp.tile(gqa_k_norm, GQA_KV_HEADS)[None],
                  gqa_sink, bf(gqa_w_o))
            hp, kp, vp = _gqa_layer(hp, seg_p, mod, g1, gw, None)
            outs['gqa_k'] = kp.reshape(bp, lp, GQA_KV_HEADS, GQA_HEAD_DIM)
            outs['gqa_v'] = vp.reshape(bp, lp, GQA_KV_HEADS, GQA_HEAD_DIM)
            lc = cache_gqa_k.shape[1]
            ctx = (cache_gqa_k.reshape(bs, lc, GQA_NK), cache_gqa_v.reshape(bs, lc, GQA_NK))
            hs, _, _ = _gqa_layer(hs, seg_s, mod, g1, gw, ctx)
        else:
            mw = _mla_weights(mla_w_dq, mla_q_lora_norm, mla_w_uq, mla_w_dkv, mla_kv_norm, mla_w_ukv, mla_q_norm,
                              mla_k_norm, mla_w_o)
            hp, ckv_p, kr_p = _mla_layer(hp, seg_p, mod, g1, mw, None)
            outs['mla_ckv'] = ckv_p.reshape(bp, lp, MLA_KV_LORA)
            outs['mla_kr'] = kr_p.reshape(bp, lp, MLA_ROPE)
            hs, _, _ = _mla_layer(hs, seg_s, mod, g1, mw, (cache_mla_ckv, cache_mla_krope))
        wr = jnp.zeros((dm, ROUTE_W), F32).at[:, :MOE_GROUPS].set(moe_w_group[layer])
        wr = wr.at[:, MOE_GROUPS:MOE_GROUPS + MOE_EXPERTS].set(moe_w_expert[layer])
        br = jnp.zeros((1, ROUTE_W), F32).at[0, :MOE_GROUPS].set(moe_b_group[layer])
        br = br.at[0, MOE_GROUPS:MOE_GROUPS + MOE_EXPERTS].set(moe_b_expert[layer])
        wr_hi, wr_lo = _hi_lo(wr)
        mo = (mod, norm2_g[layer][None], wr_hi, wr_lo, br, bf(moe_w_gate[layer]), bf(moe_w_up[layer]),
              bf(moe_w_down[layer]))
        hp = _moe_layer(hp, seg_p, *mo)
        hs = _moe_layer(hs, seg_s, *mo)
    return (hp.reshape(bp, lp, dm), hs.reshape(bs, ls, dm), outs['s5_re'], outs['s5_im'], outs['rwkv'],
            outs['gqa_k'], outs['gqa_v'], outs['mla_ckv'], outs['mla_kr'])
```

```python
import functools
import math

import jax
import jax.numpy as jnp
from jax import lax
from jax.experimental import pallas as pl
from jax.experimental.pallas import tpu as pltpu

F32 = jnp.float32
BF16 = jnp.bfloat16

D_MODEL = 1024
NORM_EPS = 1e-6
ROPE_BASE = 10000.0
GRID_W = 64
S5_GROUP = 16
S5_GROUPS = D_MODEL // S5_GROUP
S5_STATE = 64
S5_N = S5_GROUPS * S5_STATE
RWKV_HEAD = 64
RWKV_HEADS = D_MODEL // RWKV_HEAD
RWKV_DECAY_SCALE = math.exp(-0.5)
RWKV_GN_EPS = 64e-5
GQA_HEADS = 16
GQA_KV_HEADS = 4
GQA_HEAD_DIM = 64
WINDOW = 128
MLA_HEADS = 16
MLA_Q_LORA = 384
MLA_KV_LORA = 256
MLA_NOPE = 64
MLA_ROPE = 32
MLA_QK = MLA_NOPE + MLA_ROPE
MLA_V = 64
MLA_PAD = 128
MOE_GROUPS = 4
MOE_PER_GROUP = 4
MOE_EXPERTS = 16
MOE_HIDDEN = 512

LANES = 128
SUBLANES = 8
MXU_DIM = 256
VMEM_LIMIT = 56 * 1024 * 1024
MOD_ROWS = 8


def _cparams(*sem):
    return pltpu.CompilerParams(dimension_semantics=sem, vmem_limit_bytes=VMEM_LIMIT)


def _dot(a, b):
    return jnp.dot(a, b, preferred_element_type=F32)


def _dot_nt(a, b):
    return lax.dot_general(a, b, (((1,), (1,)), ((), ())), preferred_element_type=F32)


def _split3(x):
    hi = x.astype(BF16)
    r1 = x - hi.astype(F32)
    mid = r1.astype(BF16)
    lo = (r1 - mid.astype(F32)).astype(BF16)
    return hi, mid, lo


def _dot_x3(x, w_hi, w_lo):
    hi, mid, _ = _split3(x)
    return _dot(hi, w_hi) + (_dot(mid, w_hi) + _dot(hi, w_lo))


def _group_sum(x, ones_bd):
    n = x.shape[-1]
    outs = []
    for c in range(n // MXU_DIM):
        hi, mid, lo = _split3(x[:, c * MXU_DIM:(c + 1) * MXU_DIM])
        outs.append(_dot(hi, ones_bd) + (_dot(mid, ones_bd) + _dot(lo, ones_bd)))
    return outs[0] if len(outs) == 1 else jnp.concatenate(outs, axis=-1)


def _modulate(x, g, shift, scale):
    ms = jnp.mean(x * x, axis=-1, keepdims=True)
    return x * lax.rsqrt(ms + NORM_EPS) * g * (1.0 + scale) + shift


def _sigmoid(x):
    return 1.0 / (1.0 + jnp.exp(-x))


def _silu(x):
    return x * _sigmoid(x)


def _ones_blockdiag(group):
    i = jnp.arange(MXU_DIM) // group
    return (i[:, None] == i[None, :]).astype(BF16)


def _hi_lo(w):
    hi = w.astype(BF16)
    return hi, (w - hi.astype(F32)).astype(BF16)


class _Seg:
    def __init__(self, nseq, length, mod0, shared_mod):
        self.nseq, self.length, self.mod0, self.shared_mod = nseq, length, mod0, shared_mod
        self.n = nseq * length

    def mod_row(self, tm):
        if self.shared_mod:
            return lambda i: self.mod0
        per = self.length // tm
        return lambda i: self.mod0 + i // per

    def mod_rows(self, mod, k):
        if self.shared_mod:
            return jnp.broadcast_to(mod[self.mod0, k][None], (self.nseq, mod.shape[-1]))
        return mod[self.mod0:self.mod0 + self.nseq, k]


def _ada_kernel(cond_ref, w_ref, b_ref, o_ref):
    s = _silu(cond_ref[...])
    w = w_ref[0]
    w_hi = w.astype(BF16)
    w_lo = (w - w_hi.astype(F32)).astype(BF16)
    o_ref[0] = _dot_x3(s, w_hi, w_lo) + b_ref[0]


def _ada_all(cond, ada_w, ada_b):
    depth, d, n6 = ada_w.shape
    tn = 1536
    return pl.pallas_call(
        _ada_kernel,
        grid=(depth, n6 // tn),
        in_specs=[
            pl.BlockSpec((MOD_ROWS, d), lambda l, j: (0, 0)),
            pl.BlockSpec((1, d, tn), lambda l, j: (l, 0, j)),
            pl.BlockSpec((1, 1, tn), lambda l, j: (l, 0, j)),
        ],
        out_specs=pl.BlockSpec((1, MOD_ROWS, tn), lambda l, j: (l, 0, j)),
        out_shape=jax.ShapeDtypeStruct((depth, MOD_ROWS, n6), F32),
        compiler_params=_cparams("parallel", "parallel"),
        name="ada_mod",
    )(cond, ada_w, ada_b.reshape(depth, 1, n6))


ROUTE_W = LANES


def _route(logits):
    lane = lax.broadcasted_iota(jnp.int32, logits.shape, 1)
    neg = jnp.float32(-jnp.inf)
    big = jnp.int32(ROUTE_W)
    gmask = lane < MOE_GROUPS
    gl = jnp.where(gmask, logits, neg)
    gmax = jnp.max(gl, axis=-1, keepdims=True)
    g_top = jnp.min(jnp.where(gl == gmax, lane, big), axis=-1, keepdims=True)
    p_g = 1.0 / jnp.sum(jnp.where(gmask, jnp.exp(gl - gmax), 0.0), axis=-1, keepdims=True)
    lo = MOE_GROUPS + MOE_PER_GROUP * g_top
    emask = (lane >= lo) & (lane < lo + MOE_PER_GROUP)
    el = jnp.where(emask, logits, neg)
    m1 = jnp.max(el, axis=-1, keepdims=True)
    i1 = jnp.min(jnp.where(el == m1, lane, big), axis=-1, keepdims=True)
    el2 = jnp.where(lane == i1, neg, el)
    m2 = jnp.max(el2, axis=-1, keepdims=True)
    i2 = jnp.min(jnp.where(el2 == m2, lane, big), axis=-1, keepdims=True)
    e2 = jnp.exp(m2 - m1)
    w1 = p_g / (1.0 + e2)
    w2 = p_g * e2 / (1.0 + e2)
    gates = jnp.where(lane == i1, w1, 0.0) + jnp.where(lane == i2, w2, 0.0)
    return gates, g_top


def _lane_col(x, idx):
    lane = lax.broadcasted_iota(jnp.int32, x.shape, 1)
    return jnp.sum(jnp.where(lane == idx, x, 0.0), axis=-1, keepdims=True)


def _moe_kernel(h_ref, mod_ref, g_ref, wr_hi_ref, wr_lo_ref, br_ref, tri_ref, wg_ref, wu_ref, wd_ref, o_ref,
                v_scr, gate_scr, pos_scr, post_scr, x_scr, gx_scr, y_scr, yall_hi, yall_lo, acc_scr, cnt_scr,
                *, rows):
    e = pl.program_id(1)
    grp = e // MOE_PER_GROUP
    tm = v_scr.shape[0]
    nchunk = x_scr.shape[0] // rows

    @pl.when(e == 0)
    def _():
        mod = mod_ref[0]
        v = _modulate(h_ref[...], g_ref[...], mod[3:4], mod[4:5])
        logits = _dot_x3(v, wr_hi_ref[...], wr_lo_ref[...]) + br_ref[...]
        gates, g_top = _route(logits)
        gate_scr[...] = gates
        v_scr[...] = v.astype(BF16)
        acc_scr[...] = jnp.zeros_like(acc_scr)
        lane = lax.broadcasted_iota(jnp.int32, gates.shape, 1)
        member = lane == g_top
        incl = _dot(tri_ref[...], jnp.where(member, 1.0, 0.0).astype(BF16))
        pos = jnp.where(member, incl - 1.0, -1.0)
        own = jnp.sum(jnp.where(member, pos, 0.0), axis=-1, keepdims=True)
        first = jnp.where(own < rows, g_top.astype(F32) * rows + own, -1.0)
        pos_scr[...] = jnp.where(lane == MOE_GROUPS, first, pos)
        pos_t = pos.T
        for g in range(MOE_GROUPS):
            post_scr[g] = jnp.broadcast_to(pos_t[g:g + 1], post_scr.shape[1:])
            cnt_scr[g] = jnp.sum(jnp.where(member & (lane == g), 1.0, 0.0)).astype(jnp.int32)

    cnt = cnt_scr[grp]

    @pl.when(e % MOE_PER_GROUP == 0)
    def _():
        g_hi, g_mid, g_lo = _split3(gate_scr[...])
        y_scr[0:rows] = jnp.zeros((rows, y_scr.shape[1]), F32)
        for c in range(nchunk):
            @pl.when(cnt > c * rows)
            def _(c=c):
                slot = c * rows + lax.broadcasted_iota(jnp.int32, (rows, tm), 0)
                onehot = jnp.where(post_scr[grp, 0:1, :].astype(jnp.int32) == slot, 1.0, 0.0).astype(BF16)
                sl = slice(c * rows, (c + 1) * rows)
                x_scr[sl] = _dot(onehot, v_scr[...]).astype(BF16)
                gx_scr[sl] = _dot(onehot, g_hi) + (_dot(onehot, g_mid) + _dot(onehot, g_lo))
                if c > 0:
                    y_scr[sl] = jnp.zeros((rows, y_scr.shape[1]), F32)

    for c in range(nchunk):
        @pl.when(cnt > c * rows)
        def _(c=c):
            sl = slice(c * rows, (c + 1) * rows)
            x = x_scr[sl]
            gate = _lane_col(gx_scr[sl], e + MOE_GROUPS)
            hh = (_silu(_dot(x, wg_ref[0])) * _dot(x, wu_ref[0]) * gate).astype(BF16)
            y_scr[sl] += _dot(hh, wd_ref[0])

    @pl.when(e % MOE_PER_GROUP == MOE_PER_GROUP - 1)
    def _():
        y_hi, y_lo = _hi_lo(y_scr[0:rows])
        first_rows = pl.ds(pl.multiple_of(grp * rows, 16), rows)
        yall_hi[first_rows] = y_hi
        yall_lo[first_rows] = y_lo
        pos = _lane_col(pos_scr[...], grp).astype(jnp.int32)
        for c in range(1, nchunk):
            @pl.when(cnt > c * rows)
            def _(c=c):
                slot = c * rows + lax.broadcasted_iota(jnp.int32, (tm, rows), 1)
                onehot_t = jnp.where(pos == slot, 1.0, 0.0).astype(BF16)
                y_hi, y_lo = _hi_lo(y_scr[c * rows:(c + 1) * rows])
                acc_scr[...] += _dot(onehot_t, y_hi) + _dot(onehot_t, y_lo)

    @pl.when(e == pl.num_programs(1) - 1)
    def _():
        first = _lane_col(pos_scr[...], MOE_GROUPS).astype(jnp.int32)
        slot = lax.broadcasted_iota(jnp.int32, (tm, yall_hi.shape[0]), 1)
        onehot_t = jnp.where(first == slot, 1.0, 0.0).astype(BF16)
        moe = acc_scr[...] + (_dot(onehot_t, yall_hi[...]) + _dot(onehot_t, yall_lo[...]))
        o_ref[...] = h_ref[...] + mod_ref[0][5:6] * moe


MOE_TM = 1024
MOE_ROWS = 320


def _moe_layer(h, seg, mod, norm_g, wr_hi, wr_lo, br, wg, wu, wd, tm=MOE_TM, rows=MOE_ROWS):
    n, d = h.shape
    nexp, _, f = wg.shape
    tm = min(tm, n if seg.shared_mod else seg.length)
    rows = min(rows, tm)
    nchunk = -(-tm // rows)
    rid = seg.mod_row(tm)
    tri = jnp.tri(tm, dtype=BF16)
    return pl.pallas_call(
        functools.partial(_moe_kernel, rows=rows),
        grid=(n // tm, nexp),
        in_specs=[
            pl.BlockSpec((tm, d), lambda i, e: (i, 0)),
            pl.BlockSpec((1, 6, d), lambda i, e: (rid(i), 0, 0)),
            pl.BlockSpec((1, d), lambda i, e: (0, 0)),
            pl.BlockSpec((d, ROUTE_W), lambda i, e: (0, 0)),
            pl.BlockSpec((d, ROUTE_W), lambda i, e: (0, 0)),
            pl.BlockSpec((1, ROUTE_W), lambda i, e: (0, 0)),
            pl.BlockSpec((tm, tm), lambda i, e: (0, 0)),
            pl.BlockSpec((1, d, f), lambda i, e: (e, 0, 0)),
            pl.BlockSpec((1, d, f), lambda i, e: (e, 0, 0)),
            pl.BlockSpec((1, f, d), lambda i, e: (e, 0, 0)),
        ],
        out_specs=pl.BlockSpec((tm, d), lambda i, e: (i, 0)),
        out_shape=jax.ShapeDtypeStruct((n, d), F32),
        scratch_shapes=[
            pltpu.VMEM((tm, d), BF16),
            pltpu.VMEM((tm, ROUTE_W), F32),
            pltpu.VMEM((tm, ROUTE_W), F32),
            pltpu.VMEM((MOE_GROUPS, SUBLANES, tm), F32),
            pltpu.VMEM((nchunk * rows, d), BF16),
            pltpu.VMEM((nchunk * rows, ROUTE_W), F32),
            pltpu.VMEM((nchunk * rows, d), F32),
            pltpu.VMEM((MOE_GROUPS * rows, d), BF16),
            pltpu.VMEM((MOE_GROUPS * rows, d), BF16),
            pltpu.VMEM((tm, d), F32),
            pltpu.SMEM((MOE_GROUPS,), jnp.int32),
        ],
        compiler_params=_cparams("parallel", "arbitrary"),
        name="moe",
    )(h, mod, norm_g, wr_hi, wr_lo, br, tri, wg, wu, wd)


S5_TL = 32
S5_CG = MXU_DIM // S5_GROUP
S5_NCH = D_MODEL // MXU_DIM
S5_CN = S5_CG * S5_STATE
S5_SCAN_W = 512


def _s5_scan_kernel(h_ref, sh_ref, sc_ref, g_ref, s0r_ref, s0i_ref, ar_ref, ai_ref, wb_ref, wcr_ref, wci_ref,
                    y_ref, fr_ref, fi_ref, bur, bui, str_, sti):
    d = pl.program_id(1)
    c = pl.program_id(2)
    tl, nb, dm = h_ref.shape

    @pl.when(c == 0)
    def _():
        str_[...] = s0r_ref[0]
        sti[...] = s0i_ref[0]

    u = _modulate(h_ref[...], g_ref[...], sh_ref[...], sc_ref[...])
    u2 = u.reshape(tl * nb, dm).astype(BF16)
    for cc in range(S5_NCH):
        r = _dot(u2[:, cc * MXU_DIM:(cc + 1) * MXU_DIM], wb_ref[0, cc])
        bur[:, cc * S5_CN:(cc + 1) * S5_CN] = r[:, :S5_CN]
        bui[:, cc * S5_CN:(cc + 1) * S5_CN] = r[:, S5_CN:]

    for lc in range(S5_N // S5_SCAN_W):
        cols = slice(lc * S5_SCAN_W, (lc + 1) * S5_SCAN_W)
        a_r = jnp.broadcast_to(ar_ref[0, :, cols], (nb, S5_SCAN_W))
        a_i = jnp.broadcast_to(ai_ref[0, :, cols], (nb, S5_SCAN_W))

        def body(i, carry, cols=cols, a_r=a_r, a_i=a_i):
            s_r, s_i = carry
            tt = jnp.where(d == 0, i, tl - 1 - i)
            rows = pl.ds(pl.multiple_of(tt * nb, nb), nb)
            n_r = a_r * s_r - a_i * s_i + bur[rows, cols]
            n_i = a_r * s_i + a_i * s_r + bui[rows, cols]
            bur[rows, cols] = n_r
            bui[rows, cols] = n_i
            return n_r, n_i

        s_r, s_i = lax.fori_loop(0, tl, body, (str_[:, cols], sti[:, cols]))
        str_[:, cols] = s_r
        sti[:, cols] = s_i

    for cc in range(S5_NCH):
        cols = slice(cc * S5_CN, (cc + 1) * S5_CN)
        y = _dot(bur[:, cols].astype(BF16), wcr_ref[0, cc]) + _dot(bui[:, cols].astype(BF16), wci_ref[0, cc])
        y_ref[0, :, :, cc * MXU_DIM:(cc + 1) * MXU_DIM] = y.reshape(tl, nb, MXU_DIM)

    @pl.when(c == pl.num_programs(2) - 1)
    def _():
        fr_ref[0] = str_[...]
        fi_ref[0] = sti[...]


def _s5_scan(h_t, shift, scale, norm_g, s0r, s0i, a_r, a_i, wb, wcr, wci):
    length, nbt, dm = h_t.shape
    nb = SUBLANES
    tl = S5_TL
    nch = length // tl

    def tch(d, c):
        return jnp.where(d == 0, c, nch - 1 - c)

    st_spec = pl.BlockSpec((1, nb, S5_N), lambda b, d, c: (d, b, 0))
    a_spec = pl.BlockSpec((1, 1, S5_N), lambda b, d, c: (d, 0, 0))
    return pl.pallas_call(
        _s5_scan_kernel,
        grid=(nbt // nb, 2, nch),
        in_specs=[
            pl.BlockSpec((tl, nb, dm), lambda b, d, c: (tch(d, c), b, 0)),
            pl.BlockSpec((nb, dm), lambda b, d, c: (b, 0)),
            pl.BlockSpec((nb, dm), lambda b, d, c: (b, 0)),
            pl.BlockSpec((1, dm), lambda b, d, c: (0, 0)),
            st_spec, st_spec, a_spec, a_spec,
            pl.BlockSpec((1, S5_NCH, MXU_DIM, 2 * S5_CN), lambda b, d, c: (d, 0, 0, 0)),
            pl.BlockSpec((1, S5_NCH, S5_CN, MXU_DIM), lambda b, d, c: (d, 0, 0, 0)),
            pl.BlockSpec((1, S5_NCH, S5_CN, MXU_DIM), lambda b, d, c: (d, 0, 0, 0)),
        ],
        out_specs=[
            pl.BlockSpec((1, tl, nb, dm), lambda b, d, c: (d, tch(d, c), b, 0)),
            st_spec, st_spec,
        ],
        out_shape=[
            jax.ShapeDtypeStruct((2, length, nbt, dm), F32),
            jax.ShapeDtypeStruct((2, nbt, S5_N), F32),
            jax.ShapeDtypeStruct((2, nbt, S5_N), F32),
        ],
        scratch_shapes=[
            pltpu.VMEM((tl * nb, S5_N), F32), pltpu.VMEM((tl * nb, S5_N), F32),
            pltpu.VMEM((nb, S5_N), F32), pltpu.VMEM((nb, S5_N), F32),
        ],
        compiler_params=_cparams("parallel", "arbitrary", "arbitrary"),
        name="s5_scan",
    )(h_t, shift, scale, norm_g, s0r, s0i, a_r, a_i, wb, wcr, wci)


def _s5_corr_kernel(fr_ref, fi_ref, pr_ref, pi_ref, ar_ref, ai_ref, wcr_ref, wci_ref, y_in_ref, y_ref,
                    zr, zi, str_, sti, *, nseq):
    d = pl.program_id(0)
    c = pl.program_id(1)
    tl, nb, _ = y_ref.shape[1:]

    @pl.when(c == 0)
    def _():
        f_r, f_i = fr_ref[0], fi_ref[0]
        p_r, p_i = pr_ref[0], pi_ref[0]
        row = lax.broadcasted_iota(jnp.int32, f_r.shape, 0)
        e_r = jnp.zeros_like(f_r)
        e_i = jnp.zeros_like(f_i)
        for _ in range(nb // nseq - 1):
            t_r = f_r + (p_r * e_r - p_i * e_i)
            t_i = f_i + (p_r * e_i + p_i * e_r)
            fwd_r = jnp.where(row < nseq, 0.0, pltpu.roll(t_r, nseq, 0))
            fwd_i = jnp.where(row < nseq, 0.0, pltpu.roll(t_i, nseq, 0))
            bwd_r = jnp.where(row >= nb - nseq, 0.0, pltpu.roll(t_r, nb - nseq, 0))
            bwd_i = jnp.where(row >= nb - nseq, 0.0, pltpu.roll(t_i, nb - nseq, 0))
            e_r = jnp.where(d == 0, fwd_r, bwd_r)
            e_i = jnp.where(d == 0, fwd_i, bwd_i)
        str_[...] = e_r
        sti[...] = e_i

    for lc in range(S5_N // S5_SCAN_W):
        cols = slice(lc * S5_SCAN_W, (lc + 1) * S5_SCAN_W)
        a_r = jnp.broadcast_to(ar_ref[0, :, cols], (nb, S5_SCAN_W))
        a_i = jnp.broadcast_to(ai_ref[0, :, cols], (nb, S5_SCAN_W))

        def body(i, carry, cols=cols, a_r=a_r, a_i=a_i):
            s_r, s_i = carry
            tt = jnp.where(d == 0, i, tl - 1 - i)
            rows = pl.ds(pl.multiple_of(tt * nb, nb), nb)
            n_r = a_r * s_r - a_i * s_i
            n_i = a_r * s_i + a_i * s_r
            zr[rows, cols] = n_r
            zi[rows, cols] = n_i
            return n_r, n_i

        s_r, s_i = lax.fori_loop(0, tl, body, (str_[:, cols], sti[:, cols]))
        str_[:, cols] = s_r
        sti[:, cols] = s_i

    for cc in range(S5_NCH):
        cols = slice(cc * S5_CN, (cc + 1) * S5_CN)
        y = _dot(zr[:, cols].astype(BF16), wcr_ref[0, cc]) + _dot(zi[:, cols].astype(BF16), wci_ref[0, cc])
        lanes = slice(cc * MXU_DIM, (cc + 1) * MXU_DIM)
        y_ref[0, :, :, lanes] = y_in_ref[0, :, :, lanes] + y.reshape(tl, nb, MXU_DIM)


def _s5_corr(y, f_r, f_i, p_r, p_i, a_r, a_i, wcr, wci, nseq):
    _, length, nb, dm = y.shape
    tl = S5_TL
    nch = length // tl
    tch = lambda d, c: jnp.where(d == 0, c, nch - 1 - c)
    st_spec = pl.BlockSpec((1, nb, S5_N), lambda d, c: (d, 0, 0))
    a_spec = pl.BlockSpec((1, 1, S5_N), lambda d, c: (d, 0, 0))
    w_spec = pl.BlockSpec((1, S5_NCH, S5_CN, MXU_DIM), lambda d, c: (d, 0, 0, 0))
    y_spec = pl.BlockSpec((1, tl, nb, dm), lambda d, c: (d, tch(d, c), 0, 0))
    return pl.pallas_call(
        functools.partial(_s5_corr_kernel, nseq=nseq),
        grid=(2, nch),
        in_specs=[st_spec, st_spec, a_spec, a_spec, a_spec, a_spec, w_spec, w_spec, y_spec],
        out_specs=y_spec,
        out_shape=jax.ShapeDtypeStruct(y.shape, F32),
        scratch_shapes=[pltpu.VMEM((tl * nb, S5_N), F32), pltpu.VMEM((tl * nb, S5_N), F32),
                        pltpu.VMEM((nb, S5_N), F32), pltpu.VMEM((nb, S5_N), F32)],
        input_output_aliases={8: 0},
        compiler_params=_cparams("parallel", "arbitrary"),
        name="s5_corr",
    )(f_r, f_i, p_r, p_i, a_r, a_i, wcr, wci, y)


def _gelu_tanh(x):
    return 0.5 * x * (1.0 + jnp.tanh(0.7978845608028654 * (x + 0.044715 * (x * x * x))))


def _s5_glu_kernel(h_ref, mod_ref, g_ref, dsk_ref, yf_ref, yb_ref, w1_ref, w2_ref, o_ref):
    mod = mod_ref[0]
    h = h_ref[...]
    u = _modulate(h, g_ref[...], mod[0:1], mod[1:2])
    y = u * dsk_ref[...] + (yf_ref[0] + yb_ref[0])
    z = _gelu_tanh(y).astype(BF16)
    out = _dot(z, w1_ref[...]) * _sigmoid(_dot(z, w2_ref[...]))
    o_ref[...] = h + mod[2:3] * out


def _s5_glu(h, seg, mod, norm_g, d_skip, y_t, w1, w2, tm=256):
    n, dm = h.shape
    _, seg_len, nbt, _ = y_t.shape
    y2 = y_t.reshape(2, seg_len, nbt * dm)
    tm = min(tm, seg_len)
    per_seq = seg.length // tm
    per_seg = seg_len // tm
    rid = seg.mod_row(tm)
    yidx = lambda d: (lambda i: (d, (i % per_seq) % per_seg, ((i % per_seq) // per_seg) * seg.nseq + i // per_seq))
    return pl.pallas_call(
        _s5_glu_kernel,
        grid=(n // tm,),
        in_specs=[
            pl.BlockSpec((tm, dm), lambda i: (i, 0)),
            pl.BlockSpec((1, 6, dm), lambda i: (rid(i), 0, 0)),
            pl.BlockSpec((1, dm), lambda i: (0, 0)),
            pl.BlockSpec((1, dm), lambda i: (0, 0)),
            pl.BlockSpec((1, tm, dm), yidx(0)),
            pl.BlockSpec((1, tm, dm), yidx(1)),
            pl.BlockSpec((dm, dm), lambda i: (0, 0)),
            pl.BlockSpec((dm, dm), lambda i: (0, 0)),
        ],
        out_specs=pl.BlockSpec((tm, dm), lambda i: (i, 0)),
        out_shape=jax.ShapeDtypeStruct((n, dm), F32),
        compiler_params=_cparams("parallel"),
        name="s5_glu",
    )(h, mod, norm_g, d_skip, y2, y2, w1, w2)


def _s5_weights(lam_re, lam_im, log_step, b_re, b_im, c_re, c_im):
    dt = jnp.exp(log_step)
    mag = jnp.exp(lam_re * dt)
    ab_re = mag * jnp.cos(lam_im * dt)
    ab_im = mag * jnp.sin(lam_im * dt)
    nr = ab_re - 1.0
    den = lam_re * lam_re + lam_im * lam_im
    f_re = (nr * lam_re + ab_im * lam_im) / den
    f_im = (ab_im * lam_re - nr * lam_im) / den
    fb_re = f_re[..., None] * b_re - f_im[..., None] * b_im
    fb_im = f_re[..., None] * b_im + f_im[..., None] * b_re
    eye = jnp.eye(S5_CG, dtype=F32)

    def bd_in(w):
        w = w.reshape(2, S5_NCH, S5_CG, S5_STATE, S5_GROUP)
        m = jnp.einsum('dnjpc,jk->dnjckp', w, eye)
        return m.reshape(2, S5_NCH, MXU_DIM, S5_CN)

    def bd_out(w):
        w = w.reshape(2, S5_NCH, S5_CG, S5_GROUP, S5_STATE)
        m = jnp.einsum('dnjcp,jk->dnjpkc', w, eye)
        return m.reshape(2, S5_NCH, S5_CN, MXU_DIM)

    wb = jnp.concatenate([bd_in(fb_re), bd_in(fb_im)], axis=-1).astype(BF16)
    return (ab_re.reshape(2, 1, S5_N), ab_im.reshape(2, 1, S5_N), wb,
            bd_out(c_re).astype(BF16), bd_out(-c_im).astype(BF16),
            (lam_re * dt).reshape(2, 1, S5_N), (lam_im * dt).reshape(2, 1, S5_N))


def _to_time_major(x, seg, nbt):
    xt = jnp.transpose(x.reshape(seg.nseq, seg.length, -1), (1, 0, 2))
    if nbt > seg.nseq:
        xt = jnp.pad(xt, ((0, 0), (0, nbt - seg.nseq), (0, 0)))
    return xt


def _pad_rows(x, nbt):
    return x if x.shape[0] == nbt else jnp.pad(x, ((0, nbt - x.shape[0]),) + ((0, 0),) * (x.ndim - 1))


def _s5_layer(h, seg, mod, norm_g, s0_re, s0_im, sw, d_skip, w1, w2, need_final):
    a_r, a_i, wb, wcr, wci, la_r, la_i = sw
    nseg = SUBLANES // seg.nseq if SUBLANES % seg.nseq == 0 else 1
    if not need_final and nseg > 1 and seg.length % (nseg * S5_TL) == 0:
        nseq, seg_len, dm = seg.nseq, seg.length // nseg, h.shape[1]
        h_t = jnp.transpose(h.reshape(nseq, nseg, seg_len, dm), (2, 1, 0, 3)).reshape(seg_len, SUBLANES, dm)
        shift = jnp.tile(seg.mod_rows(mod, 0), (nseg, 1))
        scale = jnp.tile(seg.mod_rows(mod, 1), (nseg, 1))

        def first_rows(s0):
            z = jnp.zeros((nseg - 1, nseq, S5_N), F32)
            fwd = jnp.concatenate([s0[None, :, 0], z], axis=0).reshape(SUBLANES, S5_N)
            bwd = jnp.concatenate([z, s0[None, :, 1]], axis=0).reshape(SUBLANES, S5_N)
            return jnp.stack([fwd, bwd])

        y_t, f_r, f_i = _s5_scan(h_t, shift, scale, norm_g, first_rows(s0_re), first_rows(s0_im), a_r, a_i,
                                 wb, wcr, wci)
        mag = jnp.exp(la_r * seg_len)
        y_t = _s5_corr(y_t, f_r, f_i, mag * jnp.cos(la_i * seg_len), mag * jnp.sin(la_i * seg_len), a_r, a_i,
                       wcr, wci, nseq)
        h_new = _s5_glu(h, seg, mod, norm_g, d_skip, y_t, w1, w2)
        return h_new, None, None
    nbt = -(-seg.nseq // SUBLANES) * SUBLANES
    h_t = _to_time_major(h, seg, nbt)
    shift = _pad_rows(seg.mod_rows(mod, 0), nbt)
    scale = _pad_rows(seg.mod_rows(mod, 1), nbt)
    pad = ((0, 0), (0, nbt - seg.nseq), (0, 0))
    s0r = jnp.pad(jnp.transpose(s0_re, (1, 0, 2)), pad)
    s0i = jnp.pad(jnp.transpose(s0_im, (1, 0, 2)), pad)
    y_t, f_r, f_i = _s5_scan(h_t, shift, scale, norm_g, s0r, s0i, a_r, a_i, wb, wcr, wci)
    h_new = _s5_glu(h, seg, mod, norm_g, d_skip, y_t, w1, w2)
    return h_new, jnp.transpose(f_r[:, :seg.nseq], (1, 0, 2)), jnp.transpose(f_i[:, :seg.nseq], (1, 0, 2))


def _rwkv_proj_kernel(h_ref, hp_ref, hn_ref, mod_ref, g_ref, mu_ref, wr_ref, wk_ref, wv_ref,
                      wl1_ref, wl2_ref, al1_ref, al2_ref, gl1_ref, gl2_ref, vec_ref, w0_ref, a0_ref, ones_ref,
                      r_ref, v_ref, kk_ref, w_ref, ka_ref, kd_ref, g_out_ref, bonus_ref, *, per):
    i = pl.program_id(0)
    mod = mod_ref[0]
    gn = g_ref[...]
    tm = h_ref.shape[0]
    u = _modulate(h_ref[...], gn, mod[0:1], mod[1:2])
    first = (i % per) == 0
    last = (i % per) == per - 1
    up = _modulate(hp_ref[SUBLANES - 1:SUBLANES, :], gn, mod[0:1], mod[1:2]) * jnp.where(first, 0.0, 1.0)
    un = _modulate(hn_ref[0:1, :], gn, mod[0:1], mod[1:2]) * jnp.where(last, 0.0, 1.0)
    row = lax.broadcasted_iota(jnp.int32, u.shape, 0)
    prev = jnp.where(row == 0, up, pltpu.roll(u, 1, 0))
    nxt = jnp.where(row == tm - 1, un, pltpu.roll(u, tm - 1, 0))
    dx = 0.5 * (prev + nxt) - u
    mu = mu_ref[...]

    def mix(j):
        return (u + dx * mu[j:j + 1]).astype(BF16)

    vec = vec_ref[...]
    r = _dot(mix(0), wr_ref[...])
    k = _dot(mix(2), wk_ref[...])
    v = _dot(mix(3), wv_ref[...])
    ones = ones_ref[...]
    kk = k * vec[0:1]
    kk = kk * lax.rsqrt(_group_sum(kk * kk, ones) + 1e-12)
    g_out_ref[...] = _dot(_sigmoid(_dot(mix(5), gl1_ref[...])).astype(BF16), gl2_ref[...])
    xw = mix(1)
    xa = mix(4)
    kd_sum = None
    for d in range(2):
        wl = _dot(jnp.tanh(_dot(xw, wl1_ref[d])).astype(BF16), wl2_ref[d])
        w_ref[d] = jnp.exp(-RWKV_DECAY_SCALE * _sigmoid(w0_ref[d] + wl))
        al = _dot(_dot(xa, al1_ref[d]).astype(BF16), al2_ref[d])
        a = _sigmoid(a0_ref[d] + al)
        kd = k * (1.0 + (a - 1.0) * vec[1:2])
        kd_ref[d] = kd
        ka_ref[d] = kk * a
        kd_sum = kd if kd_sum is None else kd_sum + kd
    r_ref[...] = r
    v_ref[...] = v
    kk_ref[...] = kk
    bonus_ref[...] = _group_sum(r * kd_sum * vec[2:3], ones) * v


def _rwkv_proj(h, seg, mod, norm_g, mu, wr, wk, wv, wl1, wl2, al1, al2, gl1, gl2, vec, w0, a0, tm=256):
    n, dm = h.shape
    tm = min(tm, seg.length)
    per = seg.length // tm
    rid = seg.mod_row(tm)
    hb = tm // SUBLANES
    nblk = n // SUBLANES
    full = lambda a: pl.BlockSpec(a.shape, lambda i, nd=a.ndim: (0,) * nd)
    tile = pl.BlockSpec((tm, dm), lambda i: (i, 0))
    tile2 = pl.BlockSpec((2, tm, dm), lambda i: (0, i, 0))
    ones = _ones_blockdiag(RWKV_HEAD)
    one = jax.ShapeDtypeStruct((n, dm), F32)
    two = jax.ShapeDtypeStruct((2, n, dm), F32)
    return pl.pallas_call(
        functools.partial(_rwkv_proj_kernel, per=per),
        grid=(n // tm,),
        in_specs=[
            tile,
            pl.BlockSpec((SUBLANES, dm), lambda i: (jnp.maximum(i * hb - 1, 0), 0)),
            pl.BlockSpec((SUBLANES, dm), lambda i: (jnp.minimum((i + 1) * hb, nblk - 1), 0)),
            pl.BlockSpec((1, 6, dm), lambda i: (rid(i), 0, 0)),
            full(norm_g), full(mu), full(wr), full(wk), full(wv), full(wl1), full(wl2), full(al1), full(al2),
            full(gl1), full(gl2), full(vec), full(w0), full(a0), full(ones),
        ],
        out_specs=[tile, tile, tile, tile2, tile2, tile2, tile, tile],
        out_shape=[one, one, one, two, two, two, one, one],
        compiler_params=_cparams("parallel"),
        name="rwkv_proj",
    )(h, h, h, mod, norm_g, mu, wr, wk, wv, wl1, wl2, al1, al2, gl1, gl2, vec, w0, a0, ones)


RWKV_TC = 32


def _rwkv_chain_steps(tc, nv, load_pieces, store_pieces, st, mk, mv, ys):
    nk = st.shape[0]
    ys[...] = jnp.zeros_like(ys)
    for piece in load_pieces(0, 0):
        piece()

    def step(i, slot):
        other = 1 - slot
        side = load_pieces(jnp.minimum(i + 1, tc - 1), other) + store_pieces(jnp.maximum(i - 1, 0), other)
        every = (2 * nk) // (len(side) + 1)
        at = {(p + 1) * every: piece for p, piece in enumerate(side)}
        row = lambda q, k: mk[slot, q, k:k + 1, :]
        vv = mv[slot]
        sa = None
        for k in range(nk):
            term = st[k] * row(0, k)
            sa = term if sa is None else sa + term
            if k + 1 in at:
                at[k + 1]()
        y = None
        for k in range(nk):
            s = st[k] * row(2, k) + (vv * row(3, k) - sa * row(1, k))
            st[k] = s
            term = s * row(4, k)
            y = term if y is None else y + term
            if nk + k + 1 in at:
                at[nk + k + 1]()
        ys[slot] = y

    def two_steps(j, carry):
        step(2 * j, 0)
        step(2 * j + 1, 1)
        return carry

    lax.fori_loop(0, tc // 2, two_steps, 0)
    for piece in store_pieces(tc - 1, 1):
        piece()


def _heads_to_rows(s_scr, x, lanes, rows_per_head):
    for h in range(RWKV_HEADS):
        s_scr[h * rows_per_head:(h + 1) * rows_per_head, lanes] = x[:, h * RWKV_HEAD:(h + 1) * RWKV_HEAD]


def _rwkv_scan_grid_kernel(kk_ref, r_ref, v_ref, ka_ref, w_ref, kd_ref, s0_ref, y_ref, sf_ref,
                           st, mk, mv, ys, s_scr):
    d = pl.program_id(0)
    c = pl.program_id(2)
    tc, nb, _ = kk_ref.shape
    nk = RWKV_HEAD
    lo, hi = slice(0, nk), slice(nk, 2 * nk)

    @pl.when(c == 0)
    def _():
        st[...] = s0_ref[0]

    def load_pieces(i, slot):
        t = jnp.where(d == 0, i, tc - 1 - i)
        pairs = ((kk_ref, None, ka_ref, 0), (w_ref, 0, kd_ref, 0), (r_ref, None, v_ref, None))

        def piece(p):
            ra, da, rb, db = pairs[p]
            xa = ra[t] if da is None else ra[da, t]
            xb = rb[t] if db is None else rb[db, t]
            _heads_to_rows(s_scr.at[p], xa, lo, nb)
            _heads_to_rows(s_scr.at[p], xb, hi, nb)
            tt = s_scr[p].T
            mk[slot, 2 * p] = tt[lo]
            if p < 2:
                mk[slot, 2 * p + 1] = tt[hi]
            else:
                mv[slot] = tt[hi]
        return [functools.partial(piece, p) for p in range(3)]

    def store_pieces(i, slot):
        t = jnp.where(d == 0, i, tc - 1 - i)

        def piece():
            yt = ys[slot].T
            for h in range(RWKV_HEADS):
                y_ref[0, t, :, h * nk:(h + 1) * nk] = yt[h * nb:(h + 1) * nb, :]
        return [piece]

    _rwkv_chain_steps(tc, nk, load_pieces, store_pieces, st, mk, mv, ys)

    @pl.when(c == pl.num_programs(2) - 1)
    def _():
        sf_ref[0] = st[...]


def _rwkv_scan_grid(kk, r, v, ka, w, kd, s0):
    n, nseq, dm = kk.shape
    nk = RWKV_HEAD
    nb = SUBLANES
    tc = RWKV_TC
    nch = n // tc
    tch = lambda d, t: jnp.where(d == 0, t, nch - 1 - t)
    shared = pl.BlockSpec((tc, nb, dm), lambda d, l, t: (tch(d, t), l, 0))
    perdir = pl.BlockSpec((1, tc, nb, dm), lambda d, l, t: (d, tch(d, t), l, 0))
    sspec = pl.BlockSpec((1, nk, nk, LANES), lambda d, l, t: (d, 0, 0, l))
    return pl.pallas_call(
        _rwkv_scan_grid_kernel,
        grid=(2, nseq // nb, nch),
        in_specs=[shared, shared, shared, perdir, perdir, perdir, sspec],
        out_specs=[perdir, sspec],
        out_shape=[jax.ShapeDtypeStruct((2, n, nseq, dm), F32), jax.ShapeDtypeStruct(s0.shape, F32)],
        scratch_shapes=[pltpu.VMEM((nk, nk, LANES), F32), pltpu.VMEM((2, 5, nk, LANES), F32),
                        pltpu.VMEM((2, nk, LANES), F32), pltpu.VMEM((2, nk, LANES), F32),
                        pltpu.VMEM((3, LANES, LANES), F32)],
        compiler_params=_cparams("parallel", "parallel", "arbitrary"),
        name="rwkv_scan_grid",
    )(kk, r, v, ka, w, kd, s0)


def _rwkv_scan_lanes_kernel(kkf, rf, vf, kaf, wf, kdf, kkr, rr, vr, kar, wr, kdr, s0_ref, yf_ref, yr_ref, sf_ref,
                            st, mk, mv, ys, s_scr, y_scr):
    c = pl.program_id(0)
    nb, tc, _ = kkf.shape
    nk = RWKV_HEAD
    chains = s_scr.shape[1]
    vs = LANES // chains
    nv = nk // vs
    lo, hi = slice(0, nk), slice(nk, 2 * nk)

    @pl.when(c == 0)
    def _():
        st[...] = s0_ref[...]

    def load_pieces(i, slot):
        j = tc - 1 - i

        def both(f, r, lead=()):
            rows = [f[lead + (b, pl.ds(i, 1))] for b in range(nb)] + [r[lead + (b, pl.ds(j, 1))] for b in range(nb)]
            return jnp.concatenate(rows, axis=0)

        rep = lambda a: a if vs == 1 else jnp.concatenate([a] * vs, axis=1)

        def piece(p):
            if p == 0:
                xa, xb = both(kkf, kkr), both(kaf, kar, (0,))
            elif p == 1:
                xa, xb = both(wf, wr, (0,)), both(kdf, kdr, (0,))
            else:
                xa, xb = both(rf, rr), both(vf, vr)
            _heads_to_rows(s_scr.at[p], xa, lo, 2 * nb)
            _heads_to_rows(s_scr.at[p], xb, hi, 2 * nb)
            tt = s_scr[p].T
            mk[slot, 2 * p] = rep(tt[lo])
            if p < 2:
                mk[slot, 2 * p + 1] = rep(tt[hi])
            else:
                tv = tt[hi]
                mv[slot] = tv if vs == 1 else jnp.concatenate([tv[s * nv:(s + 1) * nv] for s in range(vs)], axis=1)
        return [functools.partial(piece, p) for p in range(3)]

    def store_pieces(i, slot):
        def piece():
            yt = ys[slot].T
            for s in range(vs):
                for h in range(RWKV_HEADS):
                    r0 = s * chains + h * 2 * nb
                    y_scr[0:2 * nb, h * nk + s * nv:h * nk + (s + 1) * nv] = yt[r0:r0 + 2 * nb, :]
            for b in range(nb):
                yf_ref[b, pl.ds(i, 1), :] = y_scr[b:b + 1, :]
                yr_ref[b, pl.ds(tc - 1 - i, 1), :] = y_scr[nb + b:nb + b + 1, :]
        return [piece]

    _rwkv_chain_steps(tc, nv, load_pieces, store_pieces, st, mk, mv, ys)

    @pl.when(c == pl.num_programs(0) - 1)
    def _():
        sf_ref[...] = st[...]


def _rwkv_scan_lanes(kk, r, v, ka, w, kd, s0):
    nb, n, dm = kk.shape
    nk = RWKV_HEAD
    nv = s0.shape[1]
    chains = RWKV_HEADS * 2 * nb
    tc = RWKV_TC
    nch = n // tc
    sf_ = pl.BlockSpec((nb, tc, dm), lambda t: (0, t, 0))
    sr_ = pl.BlockSpec((nb, tc, dm), lambda t: (0, nch - 1 - t, 0))
    df_ = pl.BlockSpec((1, nb, tc, dm), lambda t: (0, 0, t, 0))
    dr_ = pl.BlockSpec((1, nb, tc, dm), lambda t: (1, 0, nch - 1 - t, 0))
    sspec = pl.BlockSpec((nk, nv, LANES), lambda t: (0, 0, 0))
    yshape = jax.ShapeDtypeStruct((nb, n, dm), F32)
    return pl.pallas_call(
        _rwkv_scan_lanes_kernel,
        grid=(nch,),
        in_specs=[sf_, sf_, sf_, df_, df_, df_, sr_, sr_, sr_, dr_, dr_, dr_, sspec],
        out_specs=[sf_, sr_, sspec],
        out_shape=[yshape, yshape, jax.ShapeDtypeStruct((nk, nv, LANES), F32)],
        scratch_shapes=[pltpu.VMEM((nk, nv, LANES), F32), pltpu.VMEM((2, 5, nk, LANES), F32),
                        pltpu.VMEM((2, nv, LANES), F32), pltpu.VMEM((2, nv, LANES), F32),
                        pltpu.VMEM((3, chains, LANES), F32), pltpu.VMEM((SUBLANES, dm), F32)],
        compiler_params=_cparams("arbitrary"),
        name="rwkv_scan_lanes",
    )(kk, r, v, ka, w, kd, kk, r, v, ka, w, kd, s0)


def _rwkv_out_kernel(h_ref, mod_ref, yf_ref, yb_ref, bonus_ref, g_ref, lng_ref, lnb_ref, wo_ref, ones_ref, o_ref):
    ones = ones_ref[...]
    y = yf_ref[...] + yb_ref[...]
    inv = 1.0 / RWKV_HEAD
    mean = _group_sum(y, ones) * inv
    yc = y - mean
    var = _group_sum(yc * yc, ones) * inv
    yn = yc * lax.rsqrt(var + RWKV_GN_EPS) * lng_ref[...] + lnb_ref[...] + bonus_ref[...]
    out = _dot((yn * g_ref[...]).astype(BF16), wo_ref[...])
    o_ref[...] = h_ref[...] + mod_ref[0][2:3] * out


def _rwkv_out(h, seg, mod, yf, yb, bonus, g, ln_g, ln_b, wo, tm=256):
    n, dm = h.shape
    tm = min(tm, seg.length)
    rid = seg.mod_row(tm)
    tile = pl.BlockSpec((tm, dm), lambda i: (i, 0))
    row = pl.BlockSpec((1, dm), lambda i: (0, 0))
    ones = _ones_blockdiag(RWKV_HEAD)
    return pl.pallas_call(
        _rwkv_out_kernel,
        grid=(n // tm,),
        in_specs=[tile, pl.BlockSpec((1, 6, dm), lambda i: (rid(i), 0, 0)), tile, tile, tile, tile, row, row,
                  pl.BlockSpec((dm, dm), lambda i: (0, 0)), pl.BlockSpec(ones.shape, lambda i: (0, 0))],
        out_specs=tile,
        out_shape=jax.ShapeDtypeStruct((n, dm), F32),
        compiler_params=_cparams("parallel"),
        name="rwkv_out",
    )(h, mod, yf, yb, bonus, g, ln_g, ln_b, wo, ones)


def _rwkv_layer(h, seg, mod, norm_g, s0, rw):
    nseq, length = seg.nseq, seg.length
    hh, nk = RWKV_HEADS, RWKV_HEAD
    r, v, kk, w, ka, kd, g, bonus = _rwkv_proj(h, seg, mod, norm_g, *rw['proj'])
    per_dir = nseq * hh
    n = nseq * length
    dm = hh * nk

    seq3 = lambda x: x.reshape(*x.shape[:-2], nseq, length, dm)

    def tmaj(x):
        nl = x.ndim - 2
        return jnp.transpose(seq3(x), tuple(range(nl)) + (nl + 1, nl, nl + 2))

    if per_dir % LANES == 0:
        ng = nseq // SUBLANES
        s0_s = jnp.transpose(s0.reshape(ng, SUBLANES, 2, hh, nk, nk), (2, 5, 4, 0, 3, 1)).reshape(2, nk, nk, per_dir)
        y_t, sf = _rwkv_scan_grid(tmaj(kk), tmaj(r), tmaj(v), tmaj(ka), tmaj(w), tmaj(kd), s0_s)
        y2 = jnp.transpose(y_t, (0, 2, 1, 3)).reshape(2, n, dm)
        yf, yb = y2[0], y2[1]
        s_fin = jnp.transpose(sf.reshape(2, nk, nk, ng, hh, SUBLANES), (3, 5, 0, 4, 2, 1)).reshape(nseq, 2, hh, nk, nk)
    else:
        vs = LANES // (2 * per_dir)
        assert vs * 2 * per_dir == LANES
        nv = nk // vs
        s0_s = jnp.transpose(s0.reshape(nseq, 2, hh, vs, nv, nk), (5, 4, 3, 2, 1, 0)).reshape(nk, nv, LANES)
        y_f, y_r, sf = _rwkv_scan_lanes(seq3(kk), seq3(r), seq3(v), seq3(ka), seq3(w), seq3(kd), s0_s)
        yf, yb = y_f.reshape(n, dm), y_r.reshape(n, dm)
        s_fin = jnp.transpose(sf.reshape(nk, nv, vs, hh, 2, nseq), (5, 4, 3, 2, 1, 0)).reshape(nseq, 2, hh, nk, nk)
    h_new = _rwkv_out(h, seg, mod, yf, yb, bonus, g, *rw['out'])
    return h_new, s_fin


def _rope_tables(length, half, first_lane, period):
    lane = jnp.arange(LANES) % period - first_lane
    active = (lane >= 0) & (lane < 4 * half)
    quarter = jnp.clip(lane, 0, 4 * half - 1) // half
    freq = ROPE_BASE ** (-(jnp.clip(lane, 0, 4 * half - 1) % half).astype(F32) / half)
    t = jnp.arange(length)
    pos = jnp.where(quarter[None, :] < 2, (t // GRID_W)[:, None], (t % GRID_W)[:, None]).astype(F32)
    ang = pos * freq[None, :]
    cos = jnp.where(active[None], jnp.cos(ang), 1.0)
    sin = jnp.where(active[None], jnp.sin(ang), 0.0)
    first = (quarter % 2 == 0)[None]
    return cos, jnp.where(first, -sin, 0.0), jnp.where(first, 0.0, sin)


def _rope(x, cos, sa, sb, half):
    w = x.shape[-1]
    rep = w // LANES
    tile = (lambda a: a) if rep == 1 else (lambda a: jnp.concatenate([a] * rep, axis=-1))
    return x * tile(cos) + pltpu.roll(x, w - half, 1) * tile(sa) + pltpu.roll(x, half, 1) * tile(sb)


def _attn_kernel(*refs, groups, dq, dv, hpb, scale, band, has_ctx, has_sink):
    it = iter(refs)
    q_ref, k_ref, v_ref = next(it), next(it), next(it)
    kc_ref, vc_ref = (next(it), next(it)) if has_ctx else (None, None)
    sink_ref = next(it) if has_sink else None
    o_ref = next(it)
    tq = q_ref.shape[1]
    lk = k_ref.shape[1]
    rows = groups * tq
    if band:
        lw = min(lk, tq + 2 * WINDOW)
        q0 = pl.program_id(2) * tq
        start = pl.multiple_of(jnp.clip(q0 - WINDOW, 0, lk - lw), LANES)
        ksl = pl.ds(start, lw)
        qpos = q0 + (lax.broadcasted_iota(jnp.int32, (rows, lw), 0) & (tq - 1))
        kpos = start + lax.broadcasted_iota(jnp.int32, (rows, lw), 1)
        valid = jnp.abs(qpos - kpos) <= WINDOW
    else:
        ksl = slice(None)
    for j in range(hpb):
        kh = k_ref[0, ksl, j * dq:(j + 1) * dq].astype(BF16)
        vh = v_ref[0, ksl, j * dv:(j + 1) * dv].astype(BF16)
        heads = range(j * groups, (j + 1) * groups)
        qs = [q_ref[0, :, hq * dq:(hq + 1) * dq].astype(BF16) for hq in heads]
        qh = qs[0] if groups == 1 else jnp.concatenate(qs, axis=0)
        s = _dot_nt(qh, kh) * scale
        if band:
            s = jnp.where(valid, s, -jnp.inf)
        m = jnp.max(s, axis=-1, keepdims=True)
        if has_ctx:
            kch = kc_ref[0, :, j * dq:(j + 1) * dq].astype(BF16)
            vch = vc_ref[0, :, j * dv:(j + 1) * dv].astype(BF16)
            sc = _dot_nt(qh, kch) * scale
            m = jnp.maximum(m, jnp.max(sc, axis=-1, keepdims=True))
        if has_sink:
            sk = [jnp.broadcast_to(sink_ref[0, :, hq:hq + 1], (tq, 1)) for hq in heads]
            snk = sk[0] if groups == 1 else jnp.concatenate(sk, axis=0)
            m = jnp.maximum(m, snk)
        p = jnp.exp(s - m)
        den = jnp.sum(p, axis=-1, keepdims=True)
        acc = _dot(p.astype(BF16), vh)
        if has_ctx:
            pc = jnp.exp(sc - m)
            den = den + jnp.sum(pc, axis=-1, keepdims=True)
            acc = acc + _dot(pc.astype(BF16), vch)
        if has_sink:
            den = den + jnp.exp(snk - m)
        out = acc / den
        for g, hq in enumerate(heads):
            o_ref[0, :, hq * dv:(hq + 1) * dv] = out[g * tq:(g + 1) * tq]


def _attention(q, k, v, ctx, sink, *, heads, groups, dq, dv, scale, band, tq=256):
    b, lq, _ = q.shape
    lk = k.shape[1]
    hpb = 2
    nhb = heads // hpb
    ins = [q, k, v]
    specs = [
        pl.BlockSpec((1, tq, hpb * groups * dq), lambda bi, hb, i: (bi, i, hb)),
        pl.BlockSpec((1, lk, hpb * dq), lambda bi, hb, i: (bi, 0, hb)),
        pl.BlockSpec((1, lk, hpb * dv), lambda bi, hb, i: (bi, 0, hb)),
    ]
    if ctx is not None:
        lc = ctx[0].shape[1]
        ins += list(ctx)
        specs += [pl.BlockSpec((1, lc, hpb * dq), lambda bi, hb, i: (bi, 0, hb)),
                  pl.BlockSpec((1, lc, hpb * dv), lambda bi, hb, i: (bi, 0, hb))]
    if sink is not None:
        sk = jnp.pad(sink.reshape(nhb, 1, hpb * groups), ((0, 0), (0, 0), (0, LANES - hpb * groups)))
        ins.append(sk)
        specs.append(pl.BlockSpec((1, 1, LANES), lambda bi, hb, i: (hb, 0, 0)))
    return pl.pallas_call(
        functools.partial(_attn_kernel, groups=groups, dq=dq, dv=dv, hpb=hpb, scale=scale, band=band,
                          has_ctx=ctx is not None, has_sink=sink is not None),
        grid=(b, nhb, lq // tq),
        in_specs=specs,
        out_specs=pl.BlockSpec((1, tq, hpb * groups * dv), lambda bi, hb, i: (bi, i, hb)),
        out_shape=jax.ShapeDtypeStruct((b, lq, heads * groups * dv), F32),
        compiler_params=_cparams("parallel", "parallel", "arbitrary"),
        name="attention",
    )(*ins)


def _proj_res_kernel(h_ref, mod_ref, a_ref, w_ref, o_ref):
    o_ref[...] = h_ref[...] + mod_ref[0][2:3] * _dot(a_ref[...].astype(BF16), w_ref[...])


def _proj_residual(h, seg, mod, a, w, tm=256):
    n, dm = h.shape
    tm = min(tm, seg.length)
    rid = seg.mod_row(tm)
    return pl.pallas_call(
        _proj_res_kernel,
        grid=(n // tm,),
        in_specs=[pl.BlockSpec((tm, dm), lambda i: (i, 0)), pl.BlockSpec((1, 6, dm), lambda i: (rid(i), 0, 0)),
                  pl.BlockSpec((tm, a.shape[1]), lambda i: (i, 0)), pl.BlockSpec(w.shape, lambda i: (0, 0))],
        out_specs=pl.BlockSpec((tm, dm), lambda i: (i, 0)),
        out_shape=jax.ShapeDtypeStruct((n, dm), F32),
        compiler_params=_cparams("parallel"),
        name="proj_residual",
    )(h, mod, a, w)


GQA_NQ = GQA_HEADS * GQA_HEAD_DIM
GQA_NK = GQA_KV_HEADS * GQA_HEAD_DIM
GQA_ROPE_HALF = GQA_HEAD_DIM // 4


def _gqa_qkv_kernel(*refs, use_rope):
    if use_rope:
        h_ref, mod_ref, g_ref, w_ref, gq_ref, gk_ref, ones_ref, cos_ref, sa_ref, sb_ref, q_ref, k_ref, v_ref = refs
    else:
        h_ref, mod_ref, g_ref, w_ref, gq_ref, gk_ref, ones_ref, q_ref, k_ref, v_ref = refs
    mod = mod_ref[0]
    u = _modulate(h_ref[...], g_ref[...], mod[0:1], mod[1:2]).astype(BF16)
    qkv = _dot(u, w_ref[...])
    ones = ones_ref[...]
    inv = 1.0 / GQA_HEAD_DIM
    q = qkv[:, :GQA_NQ]
    k = qkv[:, GQA_NQ:GQA_NQ + GQA_NK]
    q = q * lax.rsqrt(_group_sum(q * q, ones) * inv + NORM_EPS) * gq_ref[...]
    k = k * lax.rsqrt(_group_sum(k * k, ones) * inv + NORM_EPS) * gk_ref[...]
    if use_rope:
        cos, sa, sb = cos_ref[...], sa_ref[...], sb_ref[...]
        q = _rope(q, cos, sa, sb, GQA_ROPE_HALF)
        k = _rope(k, cos, sa, sb, GQA_ROPE_HALF)
    q_ref[...] = q
    k_ref[...] = k
    v_ref[...] = qkv[:, GQA_NQ + GQA_NK:]


def _gqa_qkv(h, seg, mod, norm_g, w_qkv, gq, gk, rope, tm=256):
    n, dm = h.shape
    tm = min(tm, seg.length)
    per = seg.length // tm
    rid = seg.mod_row(tm)
    ones = _ones_blockdiag(GQA_HEAD_DIM)
    full = lambda a: pl.BlockSpec(a.shape, lambda i, nd=a.ndim: (0,) * nd)
    ins = [h, mod, norm_g, w_qkv, gq, gk, ones]
    specs = [pl.BlockSpec((tm, dm), lambda i: (i, 0)), pl.BlockSpec((1, 6, dm), lambda i: (rid(i), 0, 0)),
             full(norm_g), full(w_qkv), full(gq), full(gk), full(ones)]
    if rope is not None:
        ins += list(rope)
        specs += [pl.BlockSpec((tm, LANES), lambda i: (i % per, 0))] * 3
    return pl.pallas_call(
        functools.partial(_gqa_qkv_kernel, use_rope=rope is not None),
        grid=(n // tm,),
        in_specs=specs,
        out_specs=[pl.BlockSpec((tm, GQA_NQ), lambda i: (i, 0)), pl.BlockSpec((tm, GQA_NK), lambda i: (i, 0)),
                   pl.BlockSpec((tm, GQA_NK), lambda i: (i, 0))],
        out_shape=[jax.ShapeDtypeStruct((n, GQA_NQ), F32), jax.ShapeDtypeStruct((n, GQA_NK), F32),
                   jax.ShapeDtypeStruct((n, GQA_NK), F32)],
        compiler_params=_cparams("parallel"),
        name="gqa_qkv",
    )(*ins)


def _gqa_layer(h, seg, mod, norm_g, gw, ctx):
    w_qkv, gq, gk, sink, w_o = gw
    latent = ctx is not None
    rope = _rope_tables(seg.length, GQA_ROPE_HALF, 0, 4 * GQA_ROPE_HALF) if latent else None
    q, k, v = _gqa_qkv(h, seg, mod, norm_g, w_qkv, gq, gk, rope)
    b3 = lambda a: a.reshape(seg.nseq, seg.length, a.shape[-1])
    att = _attention(b3(q), b3(k), b3(v), ctx, sink, heads=GQA_KV_HEADS, groups=GQA_HEADS // GQA_KV_HEADS,
                     dq=GQA_HEAD_DIM, dv=GQA_HEAD_DIM, scale=GQA_HEAD_DIM ** -0.5, band=latent)
    h_new = _proj_residual(h, seg, mod, att.reshape(seg.n, -1), w_o)
    return h_new, k, v


MLA_QW = MLA_HEADS * MLA_PAD
MLA_VW = MLA_HEADS * MLA_V
MLA_CKR = MLA_KV_LORA + LANES
MLA_ROPE_HALF = MLA_ROPE // 4


def _mla_expand(ckv, kr, wuk_ref, wuv_ref, kg_ref, ones_ref):
    cb = ckv.astype(BF16)
    kn = _dot(cb, wuk_ref[...])
    kr_t = jnp.concatenate([pltpu.roll(kr, MLA_NOPE, 1)] * MLA_HEADS, axis=-1)
    k = kn + kr_t
    k = k * lax.rsqrt(_group_sum(k * k, ones_ref[...]) * (1.0 / MLA_QK) + NORM_EPS) * kg_ref[...]
    return k, _dot(cb, wuv_ref[...])


def _mla_proj_kernel(*refs, use_rope):
    (h_ref, mod_ref, g_ref, wdq_ref, qlg_ref, wuq_ref, qg_ref, wdkv_ref, kvg_ref, wuk_ref, wuv_ref, kg_ref,
     ones_ref) = refs[:13]
    if use_rope:
        cos_ref, sa_ref, sb_ref = refs[13:16]
    q_ref, k_ref, v_ref, ckv_ref, kr_ref = refs[-5:]
    mod = mod_ref[0]
    u = _modulate(h_ref[...], g_ref[...], mod[0:1], mod[1:2]).astype(BF16)
    qd = _dot(u, wdq_ref[...])
    qd = qd * lax.rsqrt(jnp.mean(qd * qd, axis=-1, keepdims=True) + NORM_EPS) * qlg_ref[...]
    q = _dot(qd.astype(BF16), wuq_ref[...])
    q = q * lax.rsqrt(_group_sum(q * q, ones_ref[...]) * (1.0 / MLA_QK) + NORM_EPS) * qg_ref[...]
    ckr = _dot(u, wdkv_ref[...])
    ckv = ckr[:, :MLA_KV_LORA]
    ckv = ckv * lax.rsqrt(jnp.mean(ckv * ckv, axis=-1, keepdims=True) + NORM_EPS) * kvg_ref[...]
    kr = ckr[:, MLA_KV_LORA:]
    k, v = _mla_expand(ckv, kr, wuk_ref, wuv_ref, kg_ref, ones_ref)
    if use_rope:
        cos, sa, sb = cos_ref[...], sa_ref[...], sb_ref[...]
        q = _rope(q, cos, sa, sb, MLA_ROPE_HALF)
        k = _rope(k, cos, sa, sb, MLA_ROPE_HALF)
    q_ref[...] = q.astype(BF16)
    k_ref[...] = k.astype(BF16)
    v_ref[...] = v.astype(BF16)
    ckv_ref[...] = ckv
    kr_ref[...] = kr


def _mla_proj(h, seg, mod, norm_g, pw, rope, tm=256):
    n, dm = h.shape
    tm = min(tm, seg.length)
    per = seg.length // tm
    rid = seg.mod_row(tm)
    ones = _ones_blockdiag(MLA_PAD)
    full = lambda a: pl.BlockSpec(a.shape, lambda i, nd=a.ndim: (0,) * nd)
    ins = [h, mod, norm_g, *pw, ones]
    specs = [pl.BlockSpec((tm, dm), lambda i: (i, 0)), pl.BlockSpec((1, 6, dm), lambda i: (rid(i), 0, 0)),
             full(norm_g)] + [full(a) for a in pw] + [full(ones)]
    if rope is not None:
        ins += list(rope)
        specs += [pl.BlockSpec((tm, LANES), lambda i: (i % per, 0))] * 3
    widths = (MLA_QW, MLA_QW, MLA_VW, MLA_KV_LORA, LANES)
    return pl.pallas_call(
        functools.partial(_mla_proj_kernel, use_rope=rope is not None),
        grid=(n // tm,),
        in_specs=specs,
        out_specs=[pl.BlockSpec((tm, w), lambda i: (i, 0)) for w in widths],
        out_shape=[jax.ShapeDtypeStruct((n, w), BF16 if j < 3 else F32) for j, w in enumerate(widths)],
        compiler_params=_cparams("parallel"),
        name="mla_proj",
    )(*ins)


def _mla_ctx_kernel(ckv_ref, kr_ref, wuk_ref, wuv_ref, kg_ref, ones_ref, k_ref, v_ref):
    k, v = _mla_expand(ckv_ref[...], kr_ref[...], wuk_ref, wuv_ref, kg_ref, ones_ref)
    k_ref[...] = k.astype(BF16)
    v_ref[...] = v.astype(BF16)


def _mla_ctx_expand(ckv, kr, wuk, wuv, kg, tm=256):
    n = ckv.shape[0]
    tm = min(tm, n)
    ones = _ones_blockdiag(MLA_PAD)
    full = lambda a: pl.BlockSpec(a.shape, lambda i, nd=a.ndim: (0,) * nd)
    return pl.pallas_call(
        _mla_ctx_kernel,
        grid=(n // tm,),
        in_specs=[pl.BlockSpec((tm, MLA_KV_LORA), lambda i: (i, 0)), pl.BlockSpec((tm, LANES), lambda i: (i, 0)),
                  full(wuk), full(wuv), full(kg), full(ones)],
        out_specs=[pl.BlockSpec((tm, MLA_QW), lambda i: (i, 0)), pl.BlockSpec((tm, MLA_VW), lambda i: (i, 0))],
        out_shape=[jax.ShapeDtypeStruct((n, MLA_QW), BF16), jax.ShapeDtypeStruct((n, MLA_VW), BF16)],
        compiler_params=_cparams("parallel"),
        name="mla_ctx_expand",
    )(ckv, kr, wuk, wuv, kg, ones)


def _mla_weights(w_dq, q_lora_g, w_uq, w_dkv, kv_g, w_ukv, q_g, k_g, w_o):
    padh = lambda a: jnp.pad(a, [(0, 0)] * (a.ndim - 1) + [(0, MLA_PAD - a.shape[-1])])
    wuq = padh(w_uq.reshape(MLA_Q_LORA, MLA_HEADS, MLA_QK)).reshape(MLA_Q_LORA, MLA_QW).astype(BF16)
    wdkv = jnp.pad(w_dkv, ((0, 0), (0, MLA_CKR - w_dkv.shape[1]))).astype(BF16)
    ukv = w_ukv.reshape(MLA_KV_LORA, MLA_HEADS, MLA_NOPE + MLA_V)
    wuk = padh(ukv[:, :, :MLA_NOPE]).reshape(MLA_KV_LORA, MLA_QW).astype(BF16)
    wuv = ukv[:, :, MLA_NOPE:].reshape(MLA_KV_LORA, MLA_VW).astype(BF16)
    tile_h = lambda g: jnp.tile(padh(g), MLA_HEADS)[None]
    proj = (w_dq.astype(BF16), q_lora_g[None], wuq, tile_h(q_g), wdkv, kv_g[None], wuk, wuv, tile_h(k_g))
    return proj, w_o.astype(BF16)


def _mla_layer(h, seg, mod, norm_g, mw, cache):
    proj, w_o = mw
    latent = cache is not None
    rope = _rope_tables(seg.length, MLA_ROPE_HALF, MLA_NOPE, MLA_PAD) if latent else None
    q, k, v, ckv, kr = _mla_proj(h, seg, mod, norm_g, proj, rope)
    ctx = None
    if latent:
        c_ckv, c_kr = cache
        lc = c_ckv.shape[1]
        kr_p = jnp.pad(c_kr.reshape(-1, MLA_ROPE), ((0, 0), (0, LANES - MLA_ROPE)))
        kc, vc = _mla_ctx_expand(c_ckv.reshape(-1, MLA_KV_LORA), kr_p, proj[6], proj[7], proj[8])
        ctx = (kc.reshape(seg.nseq, lc, MLA_QW), vc.reshape(seg.nseq, lc, MLA_VW))
    b3 = lambda a: a.reshape(seg.nseq, seg.length, a.shape[-1])
    att = _attention(b3(q), b3(k), b3(v), ctx, None, heads=MLA_HEADS, groups=1, dq=MLA_PAD, dv=MLA_V,
                     scale=MLA_QK ** -0.5, band=False)
    h_new = _proj_residual(h, seg, mod, att.reshape(seg.n, -1), w_o)
    return h_new, ckv, kr[:, :MLA_ROPE]


def kernel(x_prompt, x_sample, state_s5_re, state_s5_im, state_rwkv, cache_gqa_k, cache_gqa_v, cache_mla_ckv, cache_mla_krope, c, c_ctx, ada_w, ada_b, norm1_g, norm2_g, s5_lambda_re, s5_lambda_im, s5_log_step, s5_b_re, s5_b_im, s5_c_re, s5_c_im, s5_d, s5_glu_w1, s5_glu_w2, rwkv_mu, rwkv_w_r, rwkv_w_k, rwkv_w_v, rwkv_w_o, rwkv_w0, rwkv_w_l1, rwkv_w_l2, rwkv_a0, rwkv_a_l1, rwkv_a_l2, rwkv_g_l1, rwkv_g_l2, rwkv_k_k, rwkv_k_a, rwkv_r_k, rwkv_ln_g, rwkv_ln_b, gqa_w_qkv, gqa_q_norm, gqa_k_norm, gqa_sink, gqa_w_o, mla_w_dq, mla_q_lora_norm, mla_w_uq, mla_w_dkv, mla_kv_norm, mla_w_ukv, mla_q_norm, mla_k_norm, mla_w_o, moe_w_group, moe_b_group, moe_w_expert, moe_b_expert, moe_w_gate, moe_w_up, moe_w_down):
    bp, lp, dm = x_prompt.shape
    bs, ls, _ = x_sample.shape
    depth = ada_w.shape[0]
    assert dm == D_MODEL and 1 + bs <= MOD_ROWS
    seg_p = _Seg(bp, lp, 0, True)
    seg_s = _Seg(bs, ls, 1, False)
    bf = lambda a: a.astype(BF16)

    cond = jnp.zeros((MOD_ROWS, dm), F32).at[0].set(c_ctx).at[1:1 + bs].set(c)
    mods = _ada_all(cond, ada_w, ada_b).reshape(depth, MOD_ROWS, 6, dm)

    hp = x_prompt.reshape(bp * lp, dm)
    hs = x_sample.reshape(bs * ls, dm)
    outs = {}
    for layer in range(depth):
        kind = layer % 4
        mod = mods[layer]
        g1 = norm1_g[layer][None]
        if kind == 0:
            sw = _s5_weights(s5_lambda_re, s5_lambda_im, s5_log_step, s5_b_re, s5_b_im, s5_c_re, s5_c_im)
            w1, w2, dsk = bf(s5_glu_w1), bf(s5_glu_w2), s5_d[None]
            zeros = jnp.zeros((bp, 2, S5_N), F32)
            hp, f_re, f_im = _s5_layer(hp, seg_p, mod, g1, zeros, zeros, sw, dsk, w1, w2, True)
            hs, _, _ = _s5_layer(hs, seg_s, mod, g1, state_s5_re.reshape(bs, 2, S5_N),
                                 state_s5_im.reshape(bs, 2, S5_N), sw, dsk, w1, w2, False)
            outs['s5_re'] = f_re.reshape(bp, 2, S5_GROUPS, S5_STATE)
            outs['s5_im'] = f_im.reshape(bp, 2, S5_GROUPS, S5_STATE)
        elif kind == 1:
            vec = jnp.zeros((SUBLANES, dm), F32).at[0].set(rwkv_k_k).at[1].set(rwkv_k_a).at[2].set(rwkv_r_k.reshape(-1))
            rw = dict(
                proj=(rwkv_mu, bf(rwkv_w_r), bf(rwkv_w_k), bf(rwkv_w_v), bf(rwkv_w_l1), bf(rwkv_w_l2),
                      bf(rwkv_a_l1), bf(rwkv_a_l2), bf(rwkv_g_l1), bf(rwkv_g_l2), vec,
                      rwkv_w0[:, None, :], rwkv_a0[:, None, :]),
                out=(rwkv_ln_g[None], rwkv_ln_b[None], bf(rwkv_w_o)))
            zeros = jnp.zeros((bp, 2, RWKV_HEADS, RWKV_HEAD, RWKV_HEAD), F32)
            hp, outs['rwkv'] = _rwkv_layer(hp, seg_p, mod, g1, zeros, rw)
            hs, _ = _rwkv_layer(hs, seg_s, mod, g1, state_rwkv, rw)
        elif kind == 2:
            gw = (bf(gqa_w_qkv), jnp.tile(gqa_q_norm, GQA_HEADS)[None], jnp.tile(gqa_k_norm, GQA_KV_HEADS)[None],
                  gqa_sink, bf(gqa_w_o))
            hp, kp, vp = _gqa_layer(hp, seg_p, mod, g1, gw, None)
            outs['gqa_k'] = kp.reshape(bp, lp, GQA_KV_HEADS, GQA_HEAD_DIM)
            outs['gqa_v'] = vp.reshape(bp, lp, GQA_KV_HEADS, GQA_HEAD_DIM)
            lc = cache_gqa_k.shape[1]
            ctx = (cache_gqa_k.reshape(bs, lc, GQA_NK), cache_gqa_v.reshape(bs, lc, GQA_NK))
            hs, _, _ = _gqa_layer(hs, seg_s, mod, g1, gw, ctx)
        else:
            mw = _mla_weights(mla_w_dq, mla_q_lora_norm, mla_w_uq, mla_w_dkv, mla_kv_norm, mla_w_ukv, mla_q_norm,
                              mla_k_norm, mla_w_o)
            hp, ckv_p, kr_p = _mla_layer(hp, seg_p, mod, g1, mw, None)
            outs['mla_ckv'] = ckv_p.reshape(bp, lp, MLA_KV_LORA)
            outs['mla_kr'] = kr_p.reshape(bp, lp, MLA_ROPE)
            hs, _, _ = _mla_layer(hs, seg_s, mod, g1, mw, (cache_mla_ckv, cache_mla_krope))
        wr = jnp.zeros((dm, ROUTE_W), F32).at[:, :MOE_GROUPS].set(moe_w_group[layer])
        wr = wr.at[:, MOE_GROUPS:MOE_GROUPS + MOE_EXPERTS].set(moe_w_expert[layer])
        br = jnp.zeros((1, ROUTE_W), F32).at[0, :MOE_GROUPS].set(moe_b_group[layer])
        br = br.at[0, MOE_GROUPS:MOE_GROUPS + MOE_EXPERTS].set(moe_b_expert[layer])
        wr_hi, wr_lo = _hi_lo(wr)
        mo = (mod, norm2_g[layer][None], wr_hi, wr_lo, br, bf(moe_w_gate[layer]), bf(moe_w_up[layer]),
              bf(moe_w_down[layer]))
        hp = _moe_layer(hp, seg_p, *mo)
        hs = _moe_layer(hs, seg_s, *mo)
    return (hp.reshape(bp, lp, dm), hs.reshape(bs, ls, dm), outs['s5_re'], outs['s5_im'], outs['rwkv'],
            outs['gqa_k'], outs['gqa_v'], outs['mla_ckv'], outs['mla_kr'])
```

```python
import functools
import math

import jax
import jax.numpy as jnp
from jax import lax
from jax.experimental import pallas as pl
from jax.experimental.pallas import tpu as pltpu

F32 = jnp.float32
BF16 = jnp.bfloat16

D_MODEL = 1024
NORM_EPS = 1e-6
ROPE_BASE = 10000.0
GRID_W = 64
S5_GROUP = 16
S5_GROUPS = D_MODEL // S5_GROUP
S5_STATE = 64
S5_N = S5_GROUPS * S5_STATE
RWKV_HEAD = 64
RWKV_HEADS = D_MODEL // RWKV_HEAD
RWKV_DECAY_SCALE = math.exp(-0.5)
RWKV_GN_EPS = 64e-5
GQA_HEADS = 16
GQA_KV_HEADS = 4
GQA_HEAD_DIM = 64
WINDOW = 128
MLA_HEADS = 16
MLA_Q_LORA = 384
MLA_KV_LORA = 256
MLA_NOPE = 64
MLA_ROPE = 32
MLA_QK = MLA_NOPE + MLA_ROPE
MLA_V = 64
MLA_PAD = 128
MOE_GROUPS = 4
MOE_PER_GROUP = 4
MOE_EXPERTS = 16
MOE_HIDDEN = 512

LANES = 128
SUBLANES = 8
MXU_DIM = 256
VMEM_LIMIT = 56 * 1024 * 1024
MOD_ROWS = 8


def _cparams(*sem):
    return pltpu.CompilerParams(dimension_semantics=sem, vmem_limit_bytes=VMEM_LIMIT)


def _dot(a, b):
    return jnp.dot(a, b, preferred_element_type=F32)


def _dot_nt(a, b):
    return lax.dot_general(a, b, (((1,), (1,)), ((), ())), preferred_element_type=F32)


def _split3(x):
    hi = x.astype(BF16)
    r1 = x - hi.astype(F32)
    mid = r1.astype(BF16)
    lo = (r1 - mid.astype(F32)).astype(BF16)
    return hi, mid, lo


def _dot_x3(x, w_hi, w_lo):
    hi, mid, _ = _split3(x)
    return _dot(hi, w_hi) + (_dot(mid, w_hi) + _dot(hi, w_lo))


def _group_sum(x, ones_bd):
    n = x.shape[-1]
    outs = []
    for c in range(n // MXU_DIM):
        hi, mid, lo = _split3(x[:, c * MXU_DIM:(c + 1) * MXU_DIM])
        outs.append(_dot(hi, ones_bd) + (_dot(mid, ones_bd) + _dot(lo, ones_bd)))
    return outs[0] if len(outs) == 1 else jnp.concatenate(outs, axis=-1)


def _modulate(x, g, shift, scale):
    ms = jnp.mean(x * x, axis=-1, keepdims=True)
    return x * lax.rsqrt(ms + NORM_EPS) * g * (1.0 + scale) + shift


def _sigmoid(x):
    return 1.0 / (1.0 + jnp.exp(-x))


def _silu(x):
    return x * _sigmoid(x)


def _ones_blockdiag(group):
    i = jnp.arange(MXU_DIM) // group
    return (i[:, None] == i[None, :]).astype(BF16)


def _hi_lo(w):
    hi = w.astype(BF16)
    return hi, (w - hi.astype(F32)).astype(BF16)


class _Seg:
    def __init__(self, nseq, length, mod0, shared_mod):
        self.nseq, self.length, self.mod0, self.shared_mod = nseq, length, mod0, shared_mod
        self.n = nseq * length

    def mod_row(self, tm):
        if self.shared_mod:
            return lambda i: self.mod0
        per = self.length // tm
        return lambda i: self.mod0 + i // per

    def mod_rows(self, mod, k):
        if self.shared_mod:
            return jnp.broadcast_to(mod[self.mod0, k][None], (self.nseq, mod.shape[-1]))
        return mod[self.mod0:self.mod0 + self.nseq, k]


def _ada_kernel(cond_ref, w_ref, b_ref, o_ref):
    s = _silu(cond_ref[...])
    w = w_ref[0]
    w_hi = w.astype(BF16)
    w_lo = (w - w_hi.astype(F32)).astype(BF16)
    o_ref[0] = _dot_x3(s, w_hi, w_lo) + b_ref[0]


def _ada_all(cond, ada_w, ada_b):
    depth, d, n6 = ada_w.shape
    tn = 1536
    return pl.pallas_call(
        _ada_kernel,
        grid=(depth, n6 // tn),
        in_specs=[
            pl.BlockSpec((MOD_ROWS, d), lambda l, j: (0, 0)),
            pl.BlockSpec((1, d, tn), lambda l, j: (l, 0, j)),
            pl.BlockSpec((1, 1, tn), lambda l, j: (l, 0, j)),
        ],
        out_specs=pl.BlockSpec((1, MOD_ROWS, tn), lambda l, j: (l, 0, j)),
        out_shape=jax.ShapeDtypeStruct((depth, MOD_ROWS, n6), F32),
        compiler_params=_cparams("parallel", "parallel"),
        name="ada_mod",
    )(cond, ada_w, ada_b.reshape(depth, 1, n6))


ROUTE_W = LANES


def _route(logits):
    lane = lax.broadcasted_iota(jnp.int32, logits.shape, 1)
    neg = jnp.float32(-jnp.inf)
    big = jnp.int32(ROUTE_W)
    gmask = lane < MOE_GROUPS
    gl = jnp.where(gmask, logits, neg)
    gmax = jnp.max(gl, axis=-1, keepdims=True)
    g_top = jnp.min(jnp.where(gl == gmax, lane, big), axis=-1, keepdims=True)
    p_g = 1.0 / jnp.sum(jnp.where(gmask, jnp.exp(gl - gmax), 0.0), axis=-1, keepdims=True)
    lo = MOE_GROUPS + MOE_PER_GROUP * g_top
    emask = (lane >= lo) & (lane < lo + MOE_PER_GROUP)
    el = jnp.where(emask, logits, neg)
    m1 = jnp.max(el, axis=-1, keepdims=True)
    i1 = jnp.min(jnp.where(el == m1, lane, big), axis=-1, keepdims=True)
    el2 = jnp.where(lane == i1, neg, el)
    m2 = jnp.max(el2, axis=-1, keepdims=True)
    i2 = jnp.min(jnp.where(el2 == m2, lane, big), axis=-1, keepdims=True)
    e2 = jnp.exp(m2 - m1)
    w1 = p_g / (1.0 + e2)
    w2 = p_g * e2 / (1.0 + e2)
    gates = jnp.where(lane == i1, w1, 0.0) + jnp.where(lane == i2, w2, 0.0)
    return gates, g_top


def _lane_col(x, idx):
    lane = lax.broadcasted_iota(jnp.int32, x.shape, 1)
    return jnp.sum(jnp.where(lane == idx, x, 0.0), axis=-1, keepdims=True)


def _moe_kernel(h_ref, mod_ref, g_ref, wr_hi_ref, wr_lo_ref, br_ref, tri_ref, wg_ref, wu_ref, wd_ref, o_ref,
                v_scr, gate_scr, pos_scr, post_scr, x_scr, gx_scr, y_scr, yall_hi, yall_lo, acc_scr, cnt_scr,
                *, rows):
    e = pl.program_id(1)
    grp = e // MOE_PER_GROUP
    tm = v_scr.shape[0]
    nchunk = x_scr.shape[0] // rows

    @pl.when(e == 0)
    def _():
        mod = mod_ref[0]
        v = _modulate(h_ref[...], g_ref[...], mod[3:4], mod[4:5])
        logits = _dot_x3(v, wr_hi_ref[...], wr_lo_ref[...]) + br_ref[...]
        gates, g_top = _route(logits)
        gate_scr[...] = gates
        v_scr[...] = v.astype(BF16)
        lane = lax.broadcasted_iota(jnp.int32, gates.shape, 1)
        member = lane == g_top
        incl = _dot(tri_ref[...], jnp.where(member, 1.0, 0.0).astype(BF16))
        pos = jnp.where(member, incl - 1.0, -1.0)
        own = jnp.sum(jnp.where(member, pos, 0.0), axis=-1, keepdims=True)
        first = jnp.where(own < rows, g_top.astype(F32) * rows + own, -1.0)
        pos_scr[...] = jnp.where(lane == MOE_GROUPS, first, pos)
        pos_t = pos.T
        for g in range(MOE_GROUPS):
            post_scr[g] = jnp.broadcast_to(pos_t[g:g + 1], post_scr.shape[1:])
            cnt_scr[g] = jnp.sum(jnp.where(member & (lane == g), 1.0, 0.0)).astype(jnp.int32)

    cnt = cnt_scr[grp]
    spills = functools.reduce(jnp.logical_or, [cnt_scr[g] > rows for g in range(MOE_GROUPS)])

    @pl.when(jnp.logical_and(e == 0, spills))
    def _():
        acc_scr[...] = jnp.zeros_like(acc_scr)

    @pl.when(e % MOE_PER_GROUP == 0)
    def _():
        g_hi, g_mid, g_lo = _split3(gate_scr[...])
        y_scr[0:rows] = jnp.zeros((rows, y_scr.shape[1]), F32)
        for c in range(nchunk):
            @pl.when(cnt > c * rows)
            def _(c=c):
                slot = c * rows + lax.broadcasted_iota(jnp.int32, (rows, tm), 0)
                onehot = jnp.where(post_scr[grp, 0:1, :].astype(jnp.int32) == slot, 1.0, 0.0).astype(BF16)
                sl = slice(c * rows, (c + 1) * rows)
                x_scr[sl] = _dot(onehot, v_scr[...]).astype(BF16)
                gx_scr[sl] = _dot(onehot, g_hi) + (_dot(onehot, g_mid) + _dot(onehot, g_lo))
                if c > 0:
                    y_scr[sl] = jnp.zeros((rows, y_scr.shape[1]), F32)

    for c in range(nchunk):
        @pl.when(cnt > c * rows)
        def _(c=c):
            sl = slice(c * rows, (c + 1) * rows)
            x = x_scr[sl]
            gate = _lane_col(gx_scr[sl], e + MOE_GROUPS)
            hh = (_silu(_dot(x, wg_ref[0])) * _dot(x, wu_ref[0]) * gate).astype(BF16)
            y_scr[sl] += _dot(hh, wd_ref[0])

    @pl.when(e % MOE_PER_GROUP == MOE_PER_GROUP - 1)
    def _():
        y_hi, y_lo = _hi_lo(y_scr[0:rows])
        first_rows = pl.ds(pl.multiple_of(grp * rows, 16), rows)
        yall_hi[first_rows] = y_hi
        yall_lo[first_rows] = y_lo
        pos = _lane_col(pos_scr[...], grp).astype(jnp.int32)
        for c in range(1, nchunk):
            @pl.when(cnt > c * rows)
            def _(c=c):
                slot = c * rows + lax.broadcasted_iota(jnp.int32, (tm, rows), 1)
                onehot_t = jnp.where(pos == slot, 1.0, 0.0).astype(BF16)
                y_hi, y_lo = _hi_lo(y_scr[c * rows:(c + 1) * rows])
                acc_scr[...] += _dot(onehot_t, y_hi) + _dot(onehot_t, y_lo)

    @pl.when(e == pl.num_programs(1) - 1)
    def _():
        first = _lane_col(pos_scr[...], MOE_GROUPS).astype(jnp.int32)
        slot = lax.broadcasted_iota(jnp.int32, (tm, yall_hi.shape[0]), 1)
        onehot_t = jnp.where(first == slot, 1.0, 0.0).astype(BF16)
        moe = _dot(onehot_t, yall_hi[...]) + _dot(onehot_t, yall_lo[...])
        o_ref[...] = h_ref[...] + mod_ref[0][5:6] * moe

    @pl.when(jnp.logical_and(e == pl.num_programs(1) - 1, spills))
    def _():
        o_ref[...] += mod_ref[0][5:6] * acc_scr[...]


MOE_TM = 1024
MOE_ROWS = 320


def _moe_layer(h, seg, mod, norm_g, wr_hi, wr_lo, br, wg, wu, wd, tm=MOE_TM, rows=MOE_ROWS):
    n, d = h.shape
    nexp, _, f = wg.shape
    tm = min(tm, n if seg.shared_mod else seg.length)
    rows = min(rows, tm)
    nchunk = -(-tm // rows)
    rid = seg.mod_row(tm)
    tri = jnp.tri(tm, dtype=BF16)
    return pl.pallas_call(
        functools.partial(_moe_kernel, rows=rows),
        grid=(n // tm, nexp),
        in_specs=[
            pl.BlockSpec((tm, d), lambda i, e: (i, 0)),
            pl.BlockSpec((1, 6, d), lambda i, e: (rid(i), 0, 0)),
            pl.BlockSpec((1, d), lambda i, e: (0, 0)),
            pl.BlockSpec((d, ROUTE_W), lambda i, e: (0, 0)),
            pl.BlockSpec((d, ROUTE_W), lambda i, e: (0, 0)),
            pl.BlockSpec((1, ROUTE_W), lambda i, e: (0, 0)),
            pl.BlockSpec((tm, tm), lambda i, e: (0, 0)),
            pl.BlockSpec((1, d, f), lambda i, e: (e, 0, 0)),
            pl.BlockSpec((1, d, f), lambda i, e: (e, 0, 0)),
            pl.BlockSpec((1, f, d), lambda i, e: (e, 0, 0)),
        ],
        out_specs=pl.BlockSpec((tm, d), lambda i, e: (i, 0)),
        out_shape=jax.ShapeDtypeStruct((n, d), F32),
        scratch_shapes=[
            pltpu.VMEM((tm, d), BF16),
            pltpu.VMEM((tm, ROUTE_W), F32),
            pltpu.VMEM((tm, ROUTE_W), F32),
            pltpu.VMEM((MOE_GROUPS, SUBLANES, tm), F32),
            pltpu.VMEM((nchunk * rows, d), BF16),
            pltpu.VMEM((nchunk * rows, ROUTE_W), F32),
            pltpu.VMEM((nchunk * rows, d), F32),
            pltpu.VMEM((MOE_GROUPS * rows, d), BF16),
            pltpu.VMEM((MOE_GROUPS * rows, d), BF16),
            pltpu.VMEM((tm, d), F32),
            pltpu.SMEM((MOE_GROUPS,), jnp.int32),
        ],
        compiler_params=_cparams("parallel", "arbitrary"),
        name="moe",
    )(h, mod, norm_g, wr_hi, wr_lo, br, tri, wg, wu, wd)


S5_TL = 32
S5_CG = MXU_DIM // S5_GROUP
S5_NCH = D_MODEL // MXU_DIM
S5_CN = S5_CG * S5_STATE
S5_SCAN_W = 512


def _s5_scan_kernel(h_ref, sh_ref, sc_ref, g_ref, s0r_ref, s0i_ref, ar_ref, ai_ref, wb_ref, wcr_ref, wci_ref,
                    y_ref, fr_ref, fi_ref, bur, bui, str_, sti):
    d = pl.program_id(1)
    c = pl.program_id(2)
    tl, nb, dm = h_ref.shape

    @pl.when(c == 0)
    def _():
        str_[...] = s0r_ref[0]
        sti[...] = s0i_ref[0]

    u = _modulate(h_ref[...], g_ref[...], sh_ref[...], sc_ref[...])
    u2 = u.reshape(tl * nb, dm).astype(BF16)
    for cc in range(S5_NCH):
        r = _dot(u2[:, cc * MXU_DIM:(cc + 1) * MXU_DIM], wb_ref[0, cc])
        bur[:, cc * S5_CN:(cc + 1) * S5_CN] = r[:, :S5_CN]
        bui[:, cc * S5_CN:(cc + 1) * S5_CN] = r[:, S5_CN:]

    for lc in range(S5_N // S5_SCAN_W):
        cols = slice(lc * S5_SCAN_W, (lc + 1) * S5_SCAN_W)
        a_r = jnp.broadcast_to(ar_ref[0, :, cols], (nb, S5_SCAN_W))
        a_i = jnp.broadcast_to(ai_ref[0, :, cols], (nb, S5_SCAN_W))

        def body(i, carry, cols=cols, a_r=a_r, a_i=a_i):
            s_r, s_i = carry
            tt = jnp.where(d == 0, i, tl - 1 - i)
            rows = pl.ds(pl.multiple_of(tt * nb, nb), nb)
            n_r = a_r * s_r - a_i * s_i + bur[rows, cols]
            n_i = a_r * s_i + a_i * s_r + bui[rows, cols]
            bur[rows, cols] = n_r
            bui[rows, cols] = n_i
            return n_r, n_i

        s_r, s_i = lax.fori_loop(0, tl, body, (str_[:, cols], sti[:, cols]))
        str_[:, cols] = s_r
        sti[:, cols] = s_i

    for cc in range(S5_NCH):
        cols = slice(cc * S5_CN, (cc + 1) * S5_CN)
        y = _dot(bur[:, cols].astype(BF16), wcr_ref[0, cc]) + _dot(bui[:, cols].astype(BF16), wci_ref[0, cc])
        y_ref[0, :, :, cc * MXU_DIM:(cc + 1) * MXU_DIM] = y.reshape(tl, nb, MXU_DIM)

    @pl.when(c == pl.num_programs(2) - 1)
    def _():
        fr_ref[0] = str_[...]
        fi_ref[0] = sti[...]


def _s5_scan(h_t, shift, scale, norm_g, s0r, s0i, a_r, a_i, wb, wcr, wci):
    length, nbt, dm = h_t.shape
    nb = SUBLANES
    tl = S5_TL
    nch = length // tl

    def tch(d, c):
        return jnp.where(d == 0, c, nch - 1 - c)

    st_spec = pl.BlockSpec((1, nb, S5_N), lambda b, d, c: (d, b, 0))
    a_spec = pl.BlockSpec((1, 1, S5_N), lambda b, d, c: (d, 0, 0))
    return pl.pallas_call(
        _s5_scan_kernel,
        grid=(nbt // nb, 2, nch),
        in_specs=[
            pl.BlockSpec((tl, nb, dm), lambda b, d, c: (tch(d, c), b, 0)),
            pl.BlockSpec((nb, dm), lambda b, d, c: (b, 0)),
            pl.BlockSpec((nb, dm), lambda b, d, c: (b, 0)),
            pl.BlockSpec((1, dm), lambda b, d, c: (0, 0)),
            st_spec, st_spec, a_spec, a_spec,
            pl.BlockSpec((1, S5_NCH, MXU_DIM, 2 * S5_CN), lambda b, d, c: (d, 0, 0, 0)),
            pl.BlockSpec((1, S5_NCH, S5_CN, MXU_DIM), lambda b, d, c: (d, 0, 0, 0)),
            pl.BlockSpec((1, S5_NCH, S5_CN, MXU_DIM), lambda b, d, c: (d, 0, 0, 0)),
        ],
        out_specs=[
            pl.BlockSpec((1, tl, nb, dm), lambda b, d, c: (d, tch(d, c), b, 0)),
            st_spec, st_spec,
        ],
        out_shape=[
            jax.ShapeDtypeStruct((2, length, nbt, dm), F32),
            jax.ShapeDtypeStruct((2, nbt, S5_N), F32),
            jax.ShapeDtypeStruct((2, nbt, S5_N), F32),
        ],
        scratch_shapes=[
            pltpu.VMEM((tl * nb, S5_N), F32), pltpu.VMEM((tl * nb, S5_N), F32),
            pltpu.VMEM((nb, S5_N), F32), pltpu.VMEM((nb, S5_N), F32),
        ],
        compiler_params=_cparams("parallel", "arbitrary", "arbitrary"),
        name="s5_scan",
    )(h_t, shift, scale, norm_g, s0r, s0i, a_r, a_i, wb, wcr, wci)


def _s5_corr_kernel(fr_ref, fi_ref, pr_ref, pi_ref, ar_ref, ai_ref, wcr_ref, wci_ref, y_in_ref, y_ref,
                    zr, zi, str_, sti, *, nseq):
    d = pl.program_id(0)
    c = pl.program_id(1)
    tl, nb, _ = y_ref.shape[1:]

    @pl.when(c == 0)
    def _():
        f_r, f_i = fr_ref[0], fi_ref[0]
        p_r, p_i = pr_ref[0], pi_ref[0]
        row = lax.broadcasted_iota(jnp.int32, f_r.shape, 0)
        e_r = jnp.zeros_like(f_r)
        e_i = jnp.zeros_like(f_i)
        for _ in range(nb // nseq - 1):
            t_r = f_r + (p_r * e_r - p_i * e_i)
            t_i = f_i + (p_r * e_i + p_i * e_r)
            fwd_r = jnp.where(row < nseq, 0.0, pltpu.roll(t_r, nseq, 0))
            fwd_i = jnp.where(row < nseq, 0.0, pltpu.roll(t_i, nseq, 0))
            bwd_r = jnp.where(row >= nb - nseq, 0.0, pltpu.roll(t_r, nb - nseq, 0))
            bwd_i = jnp.where(row >= nb - nseq, 0.0, pltpu.roll(t_i, nb - nseq, 0))
            e_r = jnp.where(d == 0, fwd_r, bwd_r)
            e_i = jnp.where(d == 0, fwd_i, bwd_i)
        str_[...] = e_r
        sti[...] = e_i

    for lc in range(S5_N // S5_SCAN_W):
        cols = slice(lc * S5_SCAN_W, (lc + 1) * S5_SCAN_W)
        a_r = jnp.broadcast_to(ar_ref[0, :, cols], (nb, S5_SCAN_W))
        a_i = jnp.broadcast_to(ai_ref[0, :, cols], (nb, S5_SCAN_W))

        def body(i, carry, cols=cols, a_r=a_r, a_i=a_i):
            s_r, s_i = carry
            tt = jnp.where(d == 0, i, tl - 1 - i)
            rows = pl.ds(pl.multiple_of(tt * nb, nb), nb)
            n_r = a_r * s_r - a_i * s_i
            n_i = a_r * s_i + a_i * s_r
            zr[rows, cols] = n_r
            zi[rows, cols] = n_i
            return n_r, n_i

        s_r, s_i = lax.fori_loop(0, tl, body, (str_[:, cols], sti[:, cols]))
        str_[:, cols] = s_r
        sti[:, cols] = s_i

    for cc in range(S5_NCH):
        cols = slice(cc * S5_CN, (cc + 1) * S5_CN)
        y = _dot(zr[:, cols].astype(BF16), wcr_ref[0, cc]) + _dot(zi[:, cols].astype(BF16), wci_ref[0, cc])
        lanes = slice(cc * MXU_DIM, (cc + 1) * MXU_DIM)
        y_ref[0, :, :, lanes] = y_in_ref[0, :, :, lanes] + y.reshape(tl, nb, MXU_DIM)


def _s5_corr(y, f_r, f_i, p_r, p_i, a_r, a_i, wcr, wci, nseq):
    _, length, nb, dm = y.shape
    tl = S5_TL
    nch = length // tl
    tch = lambda d, c: jnp.where(d == 0, c, nch - 1 - c)
    st_spec = pl.BlockSpec((1, nb, S5_N), lambda d, c: (d, 0, 0))
    a_spec = pl.BlockSpec((1, 1, S5_N), lambda d, c: (d, 0, 0))
    w_spec = pl.BlockSpec((1, S5_NCH, S5_CN, MXU_DIM), lambda d, c: (d, 0, 0, 0))
    y_spec = pl.BlockSpec((1, tl, nb, dm), lambda d, c: (d, tch(d, c), 0, 0))
    return pl.pallas_call(
        functools.partial(_s5_corr_kernel, nseq=nseq),
        grid=(2, nch),
        in_specs=[st_spec, st_spec, a_spec, a_spec, a_spec, a_spec, w_spec, w_spec, y_spec],
        out_specs=y_spec,
        out_shape=jax.ShapeDtypeStruct(y.shape, F32),
        scratch_shapes=[pltpu.VMEM((tl * nb, S5_N), F32), pltpu.VMEM((tl * nb, S5_N), F32),
                        pltpu.VMEM((nb, S5_N), F32), pltpu.VMEM((nb, S5_N), F32)],
        input_output_aliases={8: 0},
        compiler_params=_cparams("parallel", "arbitrary"),
        name="s5_corr",
    )(f_r, f_i, p_r, p_i, a_r, a_i, wcr, wci, y)


def _gelu_tanh(x):
    return 0.5 * x * (1.0 + jnp.tanh(0.7978845608028654 * (x + 0.044715 * (x * x * x))))


def _s5_glu_kernel(h_ref, mod_ref, g_ref, dsk_ref, yf_ref, yb_ref, w1_ref, w2_ref, o_ref):
    mod = mod_ref[0]
    h = h_ref[...]
    u = _modulate(h, g_ref[...], mod[0:1], mod[1:2])
    y = u * dsk_ref[...] + (yf_ref[0] + yb_ref[0])
    z = _gelu_tanh(y).astype(BF16)
    out = _dot(z, w1_ref[...]) * _sigmoid(_dot(z, w2_ref[...]))
    o_ref[...] = h + mod[2:3] * out


def _s5_glu(h, seg, mod, norm_g, d_skip, y_t, w1, w2, tm=256):
    n, dm = h.shape
    _, seg_len, nbt, _ = y_t.shape
    y2 = y_t.reshape(2, seg_len, nbt * dm)
    tm = min(tm, seg_len)
    per_seq = seg.length // tm
    per_seg = seg_len // tm
    rid = seg.mod_row(tm)
    yidx = lambda d: (lambda i: (d, (i % per_seq) % per_seg, ((i % per_seq) // per_seg) * seg.nseq + i // per_seq))
    return pl.pallas_call(
        _s5_glu_kernel,
        grid=(n // tm,),
        in_specs=[
            pl.BlockSpec((tm, dm), lambda i: (i, 0)),
            pl.BlockSpec((1, 6, dm), lambda i: (rid(i), 0, 0)),
            pl.BlockSpec((1, dm), lambda i: (0, 0)),
            pl.BlockSpec((1, dm), lambda i: (0, 0)),
            pl.BlockSpec((1, tm, dm), yidx(0)),
            pl.BlockSpec((1, tm, dm), yidx(1)),
            pl.BlockSpec((dm, dm), lambda i: (0, 0)),
            pl.BlockSpec((dm, dm), lambda i: (0, 0)),
        ],
        out_specs=pl.BlockSpec((tm, dm), lambda i: (i, 0)),
        out_shape=jax.ShapeDtypeStruct((n, dm), F32),
        compiler_params=_cparams("parallel"),
        name="s5_glu",
    )(h, mod, norm_g, d_skip, y2, y2, w1, w2)


def _s5_weights(lam_re, lam_im, log_step, b_re, b_im, c_re, c_im):
    dt = jnp.exp(log_step)
    mag = jnp.exp(lam_re * dt)
    ab_re = mag * jnp.cos(lam_im * dt)
    ab_im = mag * jnp.sin(lam_im * dt)
    nr = ab_re - 1.0
    den = lam_re * lam_re + lam_im * lam_im
    f_re = (nr * lam_re + ab_im * lam_im) / den
    f_im = (ab_im * lam_re - nr * lam_im) / den
    fb_re = f_re[..., None] * b_re - f_im[..., None] * b_im
    fb_im = f_re[..., None] * b_im + f_im[..., None] * b_re
    eye = jnp.eye(S5_CG, dtype=F32)

    def bd_in(w):
        w = w.reshape(2, S5_NCH, S5_CG, S5_STATE, S5_GROUP)
        m = jnp.einsum('dnjpc,jk->dnjckp', w, eye)
        return m.reshape(2, S5_NCH, MXU_DIM, S5_CN)

    def bd_out(w):
        w = w.reshape(2, S5_NCH, S5_CG, S5_GROUP, S5_STATE)
        m = jnp.einsum('dnjcp,jk->dnjpkc', w, eye)
        return m.reshape(2, S5_NCH, S5_CN, MXU_DIM)

    wb = jnp.concatenate([bd_in(fb_re), bd_in(fb_im)], axis=-1).astype(BF16)
    return (ab_re.reshape(2, 1, S5_N), ab_im.reshape(2, 1, S5_N), wb,
            bd_out(c_re).astype(BF16), bd_out(-c_im).astype(BF16),
            (lam_re * dt).reshape(2, 1, S5_N), (lam_im * dt).reshape(2, 1, S5_N))


def _to_time_major(x, seg, nbt):
    xt = jnp.transpose(x.reshape(seg.nseq, seg.length, -1), (1, 0, 2))
    if nbt > seg.nseq:
        xt = jnp.pad(xt, ((0, 0), (0, nbt - seg.nseq), (0, 0)))
    return xt


def _pad_rows(x, nbt):
    return x if x.shape[0] == nbt else jnp.pad(x, ((0, nbt - x.shape[0]),) + ((0, 0),) * (x.ndim - 1))


def _s5_layer(h, seg, mod, norm_g, s0_re, s0_im, sw, d_skip, w1, w2, need_final):
    a_r, a_i, wb, wcr, wci, la_r, la_i = sw
    nseg = SUBLANES // seg.nseq if SUBLANES % seg.nseq == 0 else 1
    if not need_final and nseg > 1 and seg.length % (nseg * S5_TL) == 0:
        nseq, seg_len, dm = seg.nseq, seg.length // nseg, h.shape[1]
        h_t = jnp.transpose(h.reshape(nseq, nseg, seg_len, dm), (2, 1, 0, 3)).reshape(seg_len, SUBLANES, dm)
        shift = jnp.tile(seg.mod_rows(mod, 0), (nseg, 1))
        scale = jnp.tile(seg.mod_rows(mod, 1), (nseg, 1))

        def first_rows(s0):
            z = jnp.zeros((nseg - 1, nseq, S5_N), F32)
            fwd = jnp.concatenate([s0[None, :, 0], z], axis=0).reshape(SUBLANES, S5_N)
            bwd = jnp.concatenate([z, s0[None, :, 1]], axis=0).reshape(SUBLANES, S5_N)
            return jnp.stack([fwd, bwd])

        y_t, f_r, f_i = _s5_scan(h_t, shift, scale, norm_g, first_rows(s0_re), first_rows(s0_im), a_r, a_i,
                                 wb, wcr, wci)
        mag = jnp.exp(la_r * seg_len)
        y_t = _s5_corr(y_t, f_r, f_i, mag * jnp.cos(la_i * seg_len), mag * jnp.sin(la_i * seg_len), a_r, a_i,
                       wcr, wci, nseq)
        h_new = _s5_glu(h, seg, mod, norm_g, d_skip, y_t, w1, w2)
        return h_new, None, None
    nbt = -(-seg.nseq // SUBLANES) * SUBLANES
    h_t = _to_time_major(h, seg, nbt)
    shift = _pad_rows(seg.mod_rows(mod, 0), nbt)
    scale = _pad_rows(seg.mod_rows(mod, 1), nbt)
    pad = ((0, 0), (0, nbt - seg.nseq), (0, 0))
    s0r = jnp.pad(jnp.transpose(s0_re, (1, 0, 2)), pad)
    s0i = jnp.pad(jnp.transpose(s0_im, (1, 0, 2)), pad)
    y_t, f_r, f_i = _s5_scan(h_t, shift, scale, norm_g, s0r, s0i, a_r, a_i, wb, wcr, wci)
    h_new = _s5_glu(h, seg, mod, norm_g, d_skip, y_t, w1, w2)
    return h_new, jnp.transpose(f_r[:, :seg.nseq], (1, 0, 2)), jnp.transpose(f_i[:, :seg.nseq], (1, 0, 2))


def _rwkv_proj_kernel(h_ref, hp_ref, hn_ref, mod_ref, g_ref, mu_ref, wr_ref, wk_ref, wv_ref,
                      wl1_ref, wl2_ref, al1_ref, al2_ref, gl1_ref, gl2_ref, vec_ref, w0_ref, a0_ref, ones_ref,
                      r_ref, v_ref, kk_ref, w_ref, ka_ref, kd_ref, g_out_ref, bonus_ref, *, per):
    i = pl.program_id(0)
    mod = mod_ref[0]
    gn = g_ref[...]
    tm = h_ref.shape[0]
    u = _modulate(h_ref[...], gn, mod[0:1], mod[1:2])
    first = (i % per) == 0
    last = (i % per) == per - 1
    up = _modulate(hp_ref[SUBLANES - 1:SUBLANES, :], gn, mod[0:1], mod[1:2]) * jnp.where(first, 0.0, 1.0)
    un = _modulate(hn_ref[0:1, :], gn, mod[0:1], mod[1:2]) * jnp.where(last, 0.0, 1.0)
    row = lax.broadcasted_iota(jnp.int32, u.shape, 0)
    prev = jnp.where(row == 0, up, pltpu.roll(u, 1, 0))
    nxt = jnp.where(row == tm - 1, un, pltpu.roll(u, tm - 1, 0))
    dx = 0.5 * (prev + nxt) - u
    mu = mu_ref[...]

    def mix(j):
        return (u + dx * mu[j:j + 1]).astype(BF16)

    vec = vec_ref[...]
    r = _dot(mix(0), wr_ref[...])
    k = _dot(mix(2), wk_ref[...])
    v = _dot(mix(3), wv_ref[...])
    ones = ones_ref[...]
    kk = k * vec[0:1]
    kk = kk * lax.rsqrt(_group_sum(kk * kk, ones) + 1e-12)
    g_out_ref[...] = _dot(_sigmoid(_dot(mix(5), gl1_ref[...])).astype(BF16), gl2_ref[...])
    xw = mix(1)
    xa = mix(4)
    kd_sum = None
    for d in range(2):
        wl = _dot(jnp.tanh(_dot(xw, wl1_ref[d])).astype(BF16), wl2_ref[d])
        w_ref[d] = jnp.exp(-RWKV_DECAY_SCALE * _sigmoid(w0_ref[d] + wl))
        al = _dot(_dot(xa, al1_ref[d]).astype(BF16), al2_ref[d])
        a = _sigmoid(a0_ref[d] + al)
        kd = k * (1.0 + (a - 1.0) * vec[1:2])
        kd_ref[d] = kd
        ka_ref[d] = kk * a
        kd_sum = kd if kd_sum is None else kd_sum + kd
    r_ref[...] = r
    v_ref[...] = v
    kk_ref[...] = kk
    bonus_ref[...] = _group_sum(r * kd_sum * vec[2:3], ones) * v


def _rwkv_proj(h, seg, mod, norm_g, mu, wr, wk, wv, wl1, wl2, al1, al2, gl1, gl2, vec, w0, a0, tm=256):
    n, dm = h.shape
    tm = min(tm, seg.length)
    per = seg.length // tm
    rid = seg.mod_row(tm)
    hb = tm // SUBLANES
    nblk = n // SUBLANES
    full = lambda a: pl.BlockSpec(a.shape, lambda i, nd=a.ndim: (0,) * nd)
    tile = pl.BlockSpec((tm, dm), lambda i: (i, 0))
    tile2 = pl.BlockSpec((2, tm, dm), lambda i: (0, i, 0))
    ones = _ones_blockdiag(RWKV_HEAD)
    one = jax.ShapeDtypeStruct((n, dm), F32)
    two = jax.ShapeDtypeStruct((2, n, dm), F32)
    return pl.pallas_call(
        functools.partial(_rwkv_proj_kernel, per=per),
        grid=(n // tm,),
        in_specs=[
            tile,
            pl.BlockSpec((SUBLANES, dm), lambda i: (jnp.maximum(i * hb - 1, 0), 0)),
            pl.BlockSpec((SUBLANES, dm), lambda i: (jnp.minimum((i + 1) * hb, nblk - 1), 0)),
            pl.BlockSpec((1, 6, dm), lambda i: (rid(i), 0, 0)),
            full(norm_g), full(mu), full(wr), full(wk), full(wv), full(wl1), full(wl2), full(al1), full(al2),
            full(gl1), full(gl2), full(vec), full(w0), full(a0), full(ones),
        ],
        out_specs=[tile, tile, tile, tile2, tile2, tile2, tile, tile],
        out_shape=[one, one, one, two, two, two, one, one],
        compiler_params=_cparams("parallel"),
        name="rwkv_proj",
    )(h, h, h, mod, norm_g, mu, wr, wk, wv, wl1, wl2, al1, al2, gl1, gl2, vec, w0, a0, ones)


RWKV_TC = 32
RWKV_SLOTS = 2
RWKV_AHEAD = 1


def _rwkv_chain_steps(tc, nv, load_pieces, store_pieces, st, mk, mv, ys):
    nk = st.shape[0]
    ns, ahead = RWKV_SLOTS, RWKV_AHEAD
    ys[...] = jnp.zeros_like(ys)
    for a in range(ahead):
        for piece in load_pieces(a, a):
            piece()

    def step(i, slot):
        side = (load_pieces(jnp.minimum(i + ahead, tc - 1), (slot + ahead) % ns)
                + store_pieces(jnp.maximum(i - 1, 0), (slot - 1) % ns))
        every = (2 * nk) // (len(side) + 1)
        at = {(p + 1) * every: piece for p, piece in enumerate(side)}
        row = lambda q, k: mk[slot, q, k:k + 1, :]
        vv = mv[slot]
        sa = None
        for k in range(nk):
            term = st[k] * row(0, k)
            sa = term if sa is None else sa + term
            if k + 1 in at:
                at[k + 1]()
        y = None
        for k in range(nk):
            s = st[k] * row(2, k) + (vv * row(3, k) - sa * row(1, k))
            st[k] = s
            term = s * row(4, k)
            y = term if y is None else y + term
            if nk + k + 1 in at:
                at[nk + k + 1]()
        ys[slot] = y

    def some_steps(j, carry):
        for u in range(ns):
            step(ns * j + u, u)
        return carry

    lax.fori_loop(0, tc // ns, some_steps, 0)
    for piece in store_pieces(tc - 1, (tc - 1) % ns):
        piece()


def _heads_to_rows(s_scr, x, lanes, rows_per_head):
    for h in range(RWKV_HEADS):
        s_scr[h * rows_per_head:(h + 1) * rows_per_head, lanes] = x[:, h * RWKV_HEAD:(h + 1) * RWKV_HEAD]


def _rwkv_scan_grid_kernel(kk_ref, r_ref, v_ref, ka_ref, w_ref, kd_ref, s0_ref, y_ref, sf_ref,
                           st, mk, mv, ys, s_scr):
    d = pl.program_id(0)
    c = pl.program_id(2)
    tc, nb, _ = kk_ref.shape
    nk = RWKV_HEAD
    lo, hi = slice(0, nk), slice(nk, 2 * nk)

    @pl.when(c == 0)
    def _():
        st[...] = s0_ref[0]

    def load_pieces(i, slot):
        t = jnp.where(d == 0, i, tc - 1 - i)
        pairs = ((kk_ref, None, ka_ref, 0), (w_ref, 0, kd_ref, 0), (r_ref, None, v_ref, None))

        def piece(p):
            ra, da, rb, db = pairs[p]
            xa = ra[t] if da is None else ra[da, t]
            xb = rb[t] if db is None else rb[db, t]
            scr = s_scr.at[3 * slot + p]
            _heads_to_rows(scr, xa, lo, nb)
            _heads_to_rows(scr, xb, hi, nb)
            tt = scr[...].T
            mk[slot, 2 * p] = tt[lo]
            if p < 2:
                mk[slot, 2 * p + 1] = tt[hi]
            else:
                mv[slot] = tt[hi]
        return [functools.partial(piece, p) for p in range(3)]

    def store_pieces(i, slot):
        t = jnp.where(d == 0, i, tc - 1 - i)

        def piece():
            yt = ys[slot].T
            for h in range(RWKV_HEADS):
                y_ref[0, t, :, h * nk:(h + 1) * nk] = yt[h * nb:(h + 1) * nb, :]
        return [piece]

    _rwkv_chain_steps(tc, nk, load_pieces, store_pieces, st, mk, mv, ys)

    @pl.when(c == pl.num_programs(2) - 1)
    def _():
        sf_ref[0] = st[...]


def _rwkv_scan_grid(kk, r, v, ka, w, kd, s0):
    n, nseq, dm = kk.shape
    nk = RWKV_HEAD
    nb = SUBLANES
    tc = RWKV_TC
    nch = n // tc
    tch = lambda d, t: jnp.where(d == 0, t, nch - 1 - t)
    shared = pl.BlockSpec((tc, nb, dm), lambda d, l, t: (tch(d, t), l, 0))
    perdir = pl.BlockSpec((1, tc, nb, dm), lambda d, l, t: (d, tch(d, t), l, 0))
    sspec = pl.BlockSpec((1, nk, nk, LANES), lambda d, l, t: (d, 0, 0, l))
    return pl.pallas_call(
        _rwkv_scan_grid_kernel,
        grid=(2, nseq // nb, nch),
        in_specs=[shared, shared, shared, perdir, perdir, perdir, sspec],
        out_specs=[perdir, sspec],
        out_shape=[jax.ShapeDtypeStruct((2, n, nseq, dm), F32), jax.ShapeDtypeStruct(s0.shape, F32)],
        scratch_shapes=[pltpu.VMEM((nk, nk, LANES), F32), pltpu.VMEM((RWKV_SLOTS, 5, nk, LANES), F32),
                        pltpu.VMEM((RWKV_SLOTS, nk, LANES), F32), pltpu.VMEM((RWKV_SLOTS, nk, LANES), F32),
                        pltpu.VMEM((3 * RWKV_SLOTS, LANES, LANES), F32)],
        compiler_params=_cparams("parallel", "parallel", "arbitrary"),
        name="rwkv_scan_grid",
    )(kk, r, v, ka, w, kd, s0)


def _rwkv_scan_lanes_kernel(kkf, rf, vf, kaf, wf, kdf, kkr, rr, vr, kar, wr, kdr, s0_ref, yf_ref, yr_ref, sf_ref,
                            st, mk, mv, ys, s_scr, y_scr):
    c = pl.program_id(0)
    nb, tc, _ = kkf.shape
    nk = RWKV_HEAD
    chains = s_scr.shape[1]
    vs = LANES // chains
    nv = nk // vs
    lo, hi = slice(0, nk), slice(nk, 2 * nk)

    @pl.when(c == 0)
    def _():
        st[...] = s0_ref[...]

    def load_pieces(i, slot):
        j = tc - 1 - i

        def both(f, r, lead=()):
            rows = [f[lead + (b, pl.ds(i, 1))] for b in range(nb)] + [r[lead + (b, pl.ds(j, 1))] for b in range(nb)]
            return jnp.concatenate(rows, axis=0)

        rep = lambda a: a if vs == 1 else jnp.concatenate([a] * vs, axis=1)

        def piece(p):
            if p == 0:
                xa, xb = both(kkf, kkr), both(kaf, kar, (0,))
            elif p == 1:
                xa, xb = both(wf, wr, (0,)), both(kdf, kdr, (0,))
            else:
                xa, xb = both(rf, rr), both(vf, vr)
            scr = s_scr.at[3 * slot + p]
            _heads_to_rows(scr, xa, lo, 2 * nb)
            _heads_to_rows(scr, xb, hi, 2 * nb)
            tt = scr[...].T
            mk[slot, 2 * p] = rep(tt[lo])
            if p < 2:
                mk[slot, 2 * p + 1] = rep(tt[hi])
            else:
                tv = tt[hi]
                mv[slot] = tv if vs == 1 else jnp.concatenate([tv[s * nv:(s + 1) * nv] for s in range(vs)], axis=1)
        return [functools.partial(piece, p) for p in range(3)]

    def store_pieces(i, slot):
        def piece():
            yt = ys[slot].T
            for s in range(vs):
                for h in range(RWKV_HEADS):
                    r0 = s * chains + h * 2 * nb
                    y_scr[slot, 0:2 * nb, h * nk + s * nv:h * nk + (s + 1) * nv] = yt[r0:r0 + 2 * nb, :]
            for b in range(nb):
                yf_ref[b, pl.ds(i, 1), :] = y_scr[slot, b:b + 1, :]
                yr_ref[b, pl.ds(tc - 1 - i, 1), :] = y_scr[slot, nb + b:nb + b + 1, :]
        return [piece]

    _rwkv_chain_steps(tc, nv, load_pieces, store_pieces, st, mk, mv, ys)

    @pl.when(c == pl.num_programs(0) - 1)
    def _():
        sf_ref[...] = st[...]


def _rwkv_scan_lanes(kk, r, v, ka, w, kd, s0):
    nb, n, dm = kk.shape
    nk = RWKV_HEAD
    nv = s0.shape[1]
    chains = RWKV_HEADS * 2 * nb
    tc = RWKV_TC
    nch = n // tc
    sf_ = pl.BlockSpec((nb, tc, dm), lambda t: (0, t, 0))
    sr_ = pl.BlockSpec((nb, tc, dm), lambda t: (0, nch - 1 - t, 0))
    df_ = pl.BlockSpec((1, nb, tc, dm), lambda t: (0, 0, t, 0))
    dr_ = pl.BlockSpec((1, nb, tc, dm), lambda t: (1, 0, nch - 1 - t, 0))
    sspec = pl.BlockSpec((nk, nv, LANES), lambda t: (0, 0, 0))
    yshape = jax.ShapeDtypeStruct((nb, n, dm), F32)
    return pl.pallas_call(
        _rwkv_scan_lanes_kernel,
        grid=(nch,),
        in_specs=[sf_, sf_, sf_, df_, df_, df_, sr_, sr_, sr_, dr_, dr_, dr_, sspec],
        out_specs=[sf_, sr_, sspec],
        out_shape=[yshape, yshape, jax.ShapeDtypeStruct((nk, nv, LANES), F32)],
        scratch_shapes=[pltpu.VMEM((nk, nv, LANES), F32), pltpu.VMEM((RWKV_SLOTS, 5, nk, LANES), F32),
                        pltpu.VMEM((RWKV_SLOTS, nv, LANES), F32), pltpu.VMEM((RWKV_SLOTS, nv, LANES), F32),
                        pltpu.VMEM((3 * RWKV_SLOTS, chains, LANES), F32),
                        pltpu.VMEM((RWKV_SLOTS, SUBLANES, dm), F32)],
        compiler_params=_cparams("arbitrary"),
        name="rwkv_scan_lanes",
    )(kk, r, v, ka, w, kd, kk, r, v, ka, w, kd, s0)


def _rwkv_out_kernel(h_ref, mod_ref, yf_ref, yb_ref, bonus_ref, g_ref, lng_ref, lnb_ref, wo_ref, ones_ref, o_ref):
    ones = ones_ref[...]
    y = (yf_ref[...] + yb_ref[...]).reshape(h_ref.shape)
    inv = 1.0 / RWKV_HEAD
    mean = _group_sum(y, ones) * inv
    yc = y - mean
    var = _group_sum(yc * yc, ones) * inv
    yn = yc * lax.rsqrt(var + RWKV_GN_EPS) * lng_ref[...] + lnb_ref[...] + bonus_ref[...]
    out = _dot((yn * g_ref[...]).astype(BF16), wo_ref[...])
    o_ref[...] = h_ref[...] + mod_ref[0][2:3] * out


def _rwkv_out(h, seg, mod, yf, yb, bonus, g, ln_g, ln_b, wo, tm=256):
    n, dm = h.shape
    tm = min(tm, seg.length)
    rid = seg.mod_row(tm)
    tile = pl.BlockSpec((tm, dm), lambda i: (i, 0))
    row = pl.BlockSpec((1, dm), lambda i: (0, 0))
    ones = _ones_blockdiag(RWKV_HEAD)
    stacked = yf.ndim == 3
    yf_spec = pl.BlockSpec((1, tm, dm), lambda i: (0, i, 0)) if stacked else tile
    yb_spec = pl.BlockSpec((1, tm, dm), lambda i: (1, i, 0)) if stacked else tile
    return pl.pallas_call(
        _rwkv_out_kernel,
        grid=(n // tm,),
        in_specs=[tile, pl.BlockSpec((1, 6, dm), lambda i: (rid(i), 0, 0)), yf_spec, yb_spec, tile, tile, row, row,
                  pl.BlockSpec((dm, dm), lambda i: (0, 0)), pl.BlockSpec(ones.shape, lambda i: (0, 0))],
        out_specs=tile,
        out_shape=jax.ShapeDtypeStruct((n, dm), F32),
        compiler_params=_cparams("parallel"),
        name="rwkv_out",
    )(h, mod, yf, yb, bonus, g, ln_g, ln_b, wo, ones)


def _rwkv_layer(h, seg, mod, norm_g, s0, rw):
    nseq, length = seg.nseq, seg.length
    hh, nk = RWKV_HEADS, RWKV_HEAD
    r, v, kk, w, ka, kd, g, bonus = _rwkv_proj(h, seg, mod, norm_g, *rw['proj'])
    per_dir = nseq * hh
    n = nseq * length
    dm = hh * nk

    seq3 = lambda x: x.reshape(*x.shape[:-2], nseq, length, dm)

    def tmaj(x):
        nl = x.ndim - 2
        return jnp.transpose(seq3(x), tuple(range(nl)) + (nl + 1, nl, nl + 2))

    if per_dir % LANES == 0:
        ng = nseq // SUBLANES
        s0_s = jnp.transpose(s0.reshape(ng, SUBLANES, 2, hh, nk, nk), (2, 5, 4, 0, 3, 1)).reshape(2, nk, nk, per_dir)
        y_t, sf = _rwkv_scan_grid(tmaj(kk), tmaj(r), tmaj(v), tmaj(ka), tmaj(w), tmaj(kd), s0_s)
        yf = yb = jnp.transpose(y_t, (0, 2, 1, 3)).reshape(2, n, dm)
        s_fin = jnp.transpose(sf.reshape(2, nk, nk, ng, hh, SUBLANES), (3, 5, 0, 4, 2, 1)).reshape(nseq, 2, hh, nk, nk)
    else:
        vs = LANES // (2 * per_dir)
        assert vs * 2 * per_dir == LANES
        nv = nk // vs
        s0_s = jnp.transpose(s0.reshape(nseq, 2, hh, vs, nv, nk), (5, 4, 3, 2, 1, 0)).reshape(nk, nv, LANES)
        y_f, y_r, sf = _rwkv_scan_lanes(seq3(kk), seq3(r), seq3(v), seq3(ka), seq3(w), seq3(kd), s0_s)
        yf, yb = y_f.reshape(n, dm), y_r.reshape(n, dm)
        s_fin = jnp.transpose(sf.reshape(nk, nv, vs, hh, 2, nseq), (5, 4, 3, 2, 1, 0)).reshape(nseq, 2, hh, nk, nk)
    h_new = _rwkv_out(h, seg, mod, yf, yb, bonus, g, *rw['out'])
    return h_new, s_fin


def _rope_tables(length, half, first_lane, period):
    lane = jnp.arange(LANES) % period - first_lane
    active = (lane >= 0) & (lane < 4 * half)
    quarter = jnp.clip(lane, 0, 4 * half - 1) // half
    freq = ROPE_BASE ** (-(jnp.clip(lane, 0, 4 * half - 1) % half).astype(F32) / half)
    t = jnp.arange(length)
    pos = jnp.where(quarter[None, :] < 2, (t // GRID_W)[:, None], (t % GRID_W)[:, None]).astype(F32)
    ang = pos * freq[None, :]
    cos = jnp.where(active[None], jnp.cos(ang), 1.0)
    sin = jnp.where(active[None], jnp.sin(ang), 0.0)
    first = (quarter % 2 == 0)[None]
    return cos, jnp.where(first, -sin, 0.0), jnp.where(first, 0.0, sin)


def _rope(x, cos, sa, sb, half):
    w = x.shape[-1]
    rep = w // LANES
    tile = (lambda a: a) if rep == 1 else (lambda a: jnp.concatenate([a] * rep, axis=-1))
    return x * tile(cos) + pltpu.roll(x, w - half, 1) * tile(sa) + pltpu.roll(x, half, 1) * tile(sb)


def _attn_kernel(*refs, groups, dq, dv, hpb, scale, band, has_ctx, has_sink):
    it = iter(refs)
    q_ref, k_ref, v_ref = next(it), next(it), next(it)
    kc_ref, vc_ref = (next(it), next(it)) if has_ctx else (None, None)
    sink_ref = next(it) if has_sink else None
    o_ref = next(it)
    tq = q_ref.shape[1]
    lk = k_ref.shape[1]
    rows = groups * tq
    if band:
        lw = min(lk, tq + 2 * WINDOW)
        q0 = pl.program_id(2) * tq
        start = pl.multiple_of(jnp.clip(q0 - WINDOW, 0, lk - lw), LANES)
        ksl = pl.ds(start, lw)
        qpos = q0 + (lax.broadcasted_iota(jnp.int32, (rows, lw), 0) & (tq - 1))
        kpos = start + lax.broadcasted_iota(jnp.int32, (rows, lw), 1)
        valid = jnp.abs(qpos - kpos) <= WINDOW
    else:
        ksl = slice(None)
    for j in range(hpb):
        kh = k_ref[0, ksl, j * dq:(j + 1) * dq].astype(BF16)
        vh = v_ref[0, ksl, j * dv:(j + 1) * dv].astype(BF16)
        heads = range(j * groups, (j + 1) * groups)
        qs = [q_ref[0, :, hq * dq:(hq + 1) * dq].astype(BF16) for hq in heads]
        qh = qs[0] if groups == 1 else jnp.concatenate(qs, axis=0)
        s = _dot_nt(qh, kh) * scale
        if band:
            s = jnp.where(valid, s, -jnp.inf)
        m = jnp.max(s, axis=-1, keepdims=True)
        if has_ctx:
            kch = kc_ref[0, :, j * dq:(j + 1) * dq].astype(BF16)
            vch = vc_ref[0, :, j * dv:(j + 1) * dv].astype(BF16)
            sc = _dot_nt(qh, kch) * scale
            m = jnp.maximum(m, jnp.max(sc, axis=-1, keepdims=True))
        if has_sink:
            sk = [jnp.broadcast_to(sink_ref[0, :, hq:hq + 1], (tq, 1)) for hq in heads]
            snk = sk[0] if groups == 1 else jnp.concatenate(sk, axis=0)
            m = jnp.maximum(m, snk)
        p = jnp.exp(s - m)
        den = jnp.sum(p, axis=-1, keepdims=True)
        acc = _dot(p.astype(BF16), vh)
        if has_ctx:
            pc = jnp.exp(sc - m)
            den = den + jnp.sum(pc, axis=-1, keepdims=True)
            acc = acc + _dot(pc.astype(BF16), vch)
        if has_sink:
            den = den + jnp.exp(snk - m)
        out = acc / den
        for g, hq in enumerate(heads):
            o_ref[0, :, hq * dv:(hq + 1) * dv] = out[g * tq:(g + 1) * tq]


def _attention(q, k, v, ctx, sink, *, heads, groups, dq, dv, scale, band, tq=256):
    b, lq, _ = q.shape
    lk = k.shape[1]
    hpb = 2
    nhb = heads // hpb
    ins = [q, k, v]
    specs = [
        pl.BlockSpec((1, tq, hpb * groups * dq), lambda bi, hb, i: (bi, i, hb)),
        pl.BlockSpec((1, lk, hpb * dq), lambda bi, hb, i: (bi, 0, hb)),
        pl.BlockSpec((1, lk, hpb * dv), lambda bi, hb, i: (bi, 0, hb)),
    ]
    if ctx is not None:
        lc = ctx[0].shape[1]
        ins += list(ctx)
        specs += [pl.BlockSpec((1, lc, hpb * dq), lambda bi, hb, i: (bi, 0, hb)),
                  pl.BlockSpec((1, lc, hpb * dv), lambda bi, hb, i: (bi, 0, hb))]
    if sink is not None:
        sk = jnp.pad(sink.reshape(nhb, 1, hpb * groups), ((0, 0), (0, 0), (0, LANES - hpb * groups)))
        ins.append(sk)
        specs.append(pl.BlockSpec((1, 1, LANES), lambda bi, hb, i: (hb, 0, 0)))
    return pl.pallas_call(
        functools.partial(_attn_kernel, groups=groups, dq=dq, dv=dv, hpb=hpb, scale=scale, band=band,
                          has_ctx=ctx is not None, has_sink=sink is not None),
        grid=(b, nhb, lq // tq),
        in_specs=specs,
        out_specs=pl.BlockSpec((1, tq, hpb * groups * dv), lambda bi, hb, i: (bi, i, hb)),
        out_shape=jax.ShapeDtypeStruct((b, lq, heads * groups * dv), F32),
        compiler_params=_cparams("parallel", "parallel", "arbitrary"),
        name="attention",
    )(*ins)


def _proj_res_kernel(h_ref, mod_ref, a_ref, w_ref, o_ref):
    o_ref[...] = h_ref[...] + mod_ref[0][2:3] * _dot(a_ref[...].astype(BF16), w_ref[...])


def _proj_residual(h, seg, mod, a, w, tm=256):
    n, dm = h.shape
    tm = min(tm, seg.length)
    rid = seg.mod_row(tm)
    return pl.pallas_call(
        _proj_res_kernel,
        grid=(n // tm,),
        in_specs=[pl.BlockSpec((tm, dm), lambda i: (i, 0)), pl.BlockSpec((1, 6, dm), lambda i: (rid(i), 0, 0)),
                  pl.BlockSpec((tm, a.shape[1]), lambda i: (i, 0)), pl.BlockSpec(w.shape, lambda i: (0, 0))],
        out_specs=pl.BlockSpec((tm, dm), lambda i: (i, 0)),
        out_shape=jax.ShapeDtypeStruct((n, dm), F32),
        compiler_params=_cparams("parallel"),
        name="proj_residual",
    )(h, mod, a, w)


GQA_NQ = GQA_HEADS * GQA_HEAD_DIM
GQA_NK = GQA_KV_HEADS * GQA_HEAD_DIM
GQA_ROPE_HALF = GQA_HEAD_DIM // 4


def _gqa_qkv_kernel(*refs, use_rope):
    if use_rope:
        h_ref, mod_ref, g_ref, w_ref, gq_ref, gk_ref, ones_ref, cos_ref, sa_ref, sb_ref, q_ref, k_ref, v_ref = refs
    else:
        h_ref, mod_ref, g_ref, w_ref, gq_ref, gk_ref, ones_ref, q_ref, k_ref, v_ref = refs
    mod = mod_ref[0]
    u = _modulate(h_ref[...], g_ref[...], mod[0:1], mod[1:2]).astype(BF16)
    qkv = _dot(u, w_ref[...])
    ones = ones_ref[...]
    inv = 1.0 / GQA_HEAD_DIM
    q = qkv[:, :GQA_NQ]
    k = qkv[:, GQA_NQ:GQA_NQ + GQA_NK]
    q = q * lax.rsqrt(_group_sum(q * q, ones) * inv + NORM_EPS) * gq_ref[...]
    k = k * lax.rsqrt(_group_sum(k * k, ones) * inv + NORM_EPS) * gk_ref[...]
    if use_rope:
        cos, sa, sb = cos_ref[...], sa_ref[...], sb_ref[...]
        q = _rope(q, cos, sa, sb, GQA_ROPE_HALF)
        k = _rope(k, cos, sa, sb, GQA_ROPE_HALF)
    q_ref[...] = q
    k_ref[...] = k
    v_ref[...] = qkv[:, GQA_NQ + GQA_NK:]


def _gqa_qkv(h, seg, mod, norm_g, w_qkv, gq, gk, rope, tm=256):
    n, dm = h.shape
    tm = min(tm, seg.length)
    per = seg.length // tm
    rid = seg.mod_row(tm)
    ones = _ones_blockdiag(GQA_HEAD_DIM)
    full = lambda a: pl.BlockSpec(a.shape, lambda i, nd=a.ndim: (0,) * nd)
    ins = [h, mod, norm_g, w_qkv, gq, gk, ones]
    specs = [pl.BlockSpec((tm, dm), lambda i: (i, 0)), pl.BlockSpec((1, 6, dm), lambda i: (rid(i), 0, 0)),
             full(norm_g), full(w_qkv), full(gq), full(gk), full(ones)]
    if rope is not None:
        ins += list(rope)
        specs += [pl.BlockSpec((tm, LANES), lambda i: (i % per, 0))] * 3
    return pl.pallas_call(
        functools.partial(_gqa_qkv_kernel, use_rope=rope is not None),
        grid=(n // tm,),
        in_specs=specs,
        out_specs=[pl.BlockSpec((tm, GQA_NQ), lambda i: (i, 0)), pl.BlockSpec((tm, GQA_NK), lambda i: (i, 0)),
                   pl.BlockSpec((tm, GQA_NK), lambda i: (i, 0))],
        out_shape=[jax.ShapeDtypeStruct((n, GQA_NQ), F32), jax.ShapeDtypeStruct((n, GQA_NK), F32),
                   jax.ShapeDtypeStruct((n, GQA_NK), F32)],
        compiler_params=_cparams("parallel"),
        name="gqa_qkv",
    )(*ins)


def _gqa_layer(h, seg, mod, norm_g, gw, ctx):
    w_qkv, gq, gk, sink, w_o = gw
    latent = ctx is not None
    rope = _rope_tables(seg.length, GQA_ROPE_HALF, 0, 4 * GQA_ROPE_HALF) if latent else None
    q, k, v = _gqa_qkv(h, seg, mod, norm_g, w_qkv, gq, gk, rope)
    b3 = lambda a: a.reshape(seg.nseq, seg.length, a.shape[-1])
    att = _attention(b3(q), b3(k), b3(v), ctx, sink, heads=GQA_KV_HEADS, groups=GQA_HEADS // GQA_KV_HEADS,
                     dq=GQA_HEAD_DIM, dv=GQA_HEAD_DIM, scale=GQA_HEAD_DIM ** -0.5, band=latent)
    h_new = _proj_residual(h, seg, mod, att.reshape(seg.n, -1), w_o)
    return h_new, k, v


MLA_QW = MLA_HEADS * MLA_PAD
MLA_VW = MLA_HEADS * MLA_V
MLA_CKR = MLA_KV_LORA + LANES
MLA_ROPE_HALF = MLA_ROPE // 4


def _mla_expand(ckv, kr, wuk_ref, wuv_ref, kg_ref, ones_ref):
    cb = ckv.astype(BF16)
    kn = _dot(cb, wuk_ref[...])
    kr_t = jnp.concatenate([pltpu.roll(kr, MLA_NOPE, 1)] * MLA_HEADS, axis=-1)
    k = kn + kr_t
    k = k * lax.rsqrt(_group_sum(k * k, ones_ref[...]) * (1.0 / MLA_QK) + NORM_EPS) * kg_ref[...]
    return k, _dot(cb, wuv_ref[...])


def _mla_proj_kernel(*refs, use_rope):
    (h_ref, mod_ref, g_ref, wdq_ref, qlg_ref, wuq_ref, qg_ref, wdkv_ref, kvg_ref, wuk_ref, wuv_ref, kg_ref,
     ones_ref) = refs[:13]
    if use_rope:
        cos_ref, sa_ref, sb_ref = refs[13:16]
    q_ref, k_ref, v_ref, ckv_ref, kr_ref = refs[-5:]
    mod = mod_ref[0]
    u = _modulate(h_ref[...], g_ref[...], mod[0:1], mod[1:2]).astype(BF16)
    qd = _dot(u, wdq_ref[...])
    qd = qd * lax.rsqrt(jnp.mean(qd * qd, axis=-1, keepdims=True) + NORM_EPS) * qlg_ref[...]
    q = _dot(qd.astype(BF16), wuq_ref[...])
    q = q * lax.rsqrt(_group_sum(q * q, ones_ref[...]) * (1.0 / MLA_QK) + NORM_EPS) * qg_ref[...]
    ckr = _dot(u, wdkv_ref[...])
    ckv = ckr[:, :MLA_KV_LORA]
    ckv = ckv * lax.rsqrt(jnp.mean(ckv * ckv, axis=-1, keepdims=True) + NORM_EPS) * kvg_ref[...]
    kr = ckr[:, MLA_KV_LORA:]
    k, v = _mla_expand(ckv, kr, wuk_ref, wuv_ref, kg_ref, ones_ref)
    if use_rope:
        cos, sa, sb = cos_ref[...], sa_ref[...], sb_ref[...]
        q = _rope(q, cos, sa, sb, MLA_ROPE_HALF)
        k = _rope(k, cos, sa, sb, MLA_ROPE_HALF)
    q_ref[...] = q.astype(BF16)
    k_ref[...] = k.astype(BF16)
    v_ref[...] = v.astype(BF16)
    ckv_ref[...] = ckv
    kr_ref[...] = kr


def _mla_proj(h, seg, mod, norm_g, pw, rope, tm=256):
    n, dm = h.shape
    tm = min(tm, seg.length)
    per = seg.length // tm
    rid = seg.mod_row(tm)
    ones = _ones_blockdiag(MLA_PAD)
    full = lambda a: pl.BlockSpec(a.shape, lambda i, nd=a.ndim: (0,) * nd)
    ins = [h, mod, norm_g, *pw, ones]
    specs = [pl.BlockSpec((tm, dm), lambda i: (i, 0)), pl.BlockSpec((1, 6, dm), lambda i: (rid(i), 0, 0)),
             full(norm_g)] + [full(a) for a in pw] + [full(ones)]
    if rope is not None:
        ins += list(rope)
        specs += [pl.BlockSpec((tm, LANES), lambda i: (i % per, 0))] * 3
    widths = (MLA_QW, MLA_QW, MLA_VW, MLA_KV_LORA, LANES)
    return pl.pallas_call(
        functools.partial(_mla_proj_kernel, use_rope=rope is not None),
        grid=(n // tm,),
        in_specs=specs,
        out_specs=[pl.BlockSpec((tm, w), lambda i: (i, 0)) for w in widths],
        out_shape=[jax.ShapeDtypeStruct((n, w), BF16 if j < 3 else F32) for j, w in enumerate(widths)],
        compiler_params=_cparams("parallel"),
        name="mla_proj",
    )(*ins)


def _mla_ctx_kernel(ckv_ref, kr_ref, wuk_ref, wuv_ref, kg_ref, ones_ref, k_ref, v_ref):
    k, v = _mla_expand(ckv_ref[...], kr_ref[...], wuk_ref, wuv_ref, kg_ref, ones_ref)
    k_ref[...] = k.astype(BF16)
    v_ref[...] = v.astype(BF16)


def _mla_ctx_expand(ckv, kr, wuk, wuv, kg, tm=256):
    n = ckv.shape[0]
    tm = min(tm, n)
    ones = _ones_blockdiag(MLA_PAD)
    full = lambda a: pl.BlockSpec(a.shape, lambda i, nd=a.ndim: (0,) * nd)
    return pl.pallas_call(
        _mla_ctx_kernel,
        grid=(n // tm,),
        in_specs=[pl.BlockSpec((tm, MLA_KV_LORA), lambda i: (i, 0)), pl.BlockSpec((tm, LANES), lambda i: (i, 0)),
                  full(wuk), full(wuv), full(kg), full(ones)],
        out_specs=[pl.BlockSpec((tm, MLA_QW), lambda i: (i, 0)), pl.BlockSpec((tm, MLA_VW), lambda i: (i, 0))],
        out_shape=[jax.ShapeDtypeStruct((n, MLA_QW), BF16), jax.ShapeDtypeStruct((n, MLA_VW), BF16)],
        compiler_params=_cparams("parallel"),
        name="mla_ctx_expand",
    )(ckv, kr, wuk, wuv, kg, ones)


def _mla_weights(w_dq, q_lora_g, w_uq, w_dkv, kv_g, w_ukv, q_g, k_g, w_o):
    padh = lambda a: jnp.pad(a, [(0, 0)] * (a.ndim - 1) + [(0, MLA_PAD - a.shape[-1])])
    wuq = padh(w_uq.reshape(MLA_Q_LORA, MLA_HEADS, MLA_QK)).reshape(MLA_Q_LORA, MLA_QW).astype(BF16)
    wdkv = jnp.pad(w_dkv, ((0, 0), (0, MLA_CKR - w_dkv.shape[1]))).astype(BF16)
    ukv = w_ukv.reshape(MLA_KV_LORA, MLA_HEADS, MLA_NOPE + MLA_V)
    wuk = padh(ukv[:, :, :MLA_NOPE]).reshape(MLA_KV_LORA, MLA_QW).astype(BF16)
    wuv = ukv[:, :, MLA_NOPE:].reshape(MLA_KV_LORA, MLA_VW).astype(BF16)
    tile_h = lambda g: jnp.tile(padh(g), MLA_HEADS)[None]
    proj = (w_dq.astype(BF16), q_lora_g[None], wuq, tile_h(q_g), wdkv, kv_g[None], wuk, wuv, tile_h(k_g))
    return proj, w_o.astype(BF16)


def _mla_layer(h, seg, mod, norm_g, mw, cache):
    proj, w_o = mw
    latent = cache is not None
    rope = _rope_tables(seg.length, MLA_ROPE_HALF, MLA_NOPE, MLA_PAD) if latent else None
    q, k, v, ckv, kr = _mla_proj(h, seg, mod, norm_g, proj, rope)
    ctx = None
    if latent:
        c_ckv, c_kr = cache
        lc = c_ckv.shape[1]
        kr_p = jnp.pad(c_kr.reshape(-1, MLA_ROPE), ((0, 0), (0, LANES - MLA_ROPE)))
        kc, vc = _mla_ctx_expand(c_ckv.reshape(-1, MLA_KV_LORA), kr_p, proj[6], proj[7], proj[8])
        ctx = (kc.reshape(seg.nseq, lc, MLA_QW), vc.reshape(seg.nseq, lc, MLA_VW))
    b3 = lambda a: a.reshape(seg.nseq, seg.length, a.shape[-1])
    att = _attention(b3(q), b3(k), b3(v), ctx, None, heads=MLA_HEADS, groups=1, dq=MLA_PAD, dv=MLA_V,
                     scale=MLA_QK ** -0.5, band=False)
    h_new = _proj_residual(h, seg, mod, att.reshape(seg.n, -1), w_o)
    return h_new, ckv, kr[:, :MLA_ROPE]


def kernel(x_prompt, x_sample, state_s5_re, state_s5_im, state_rwkv, cache_gqa_k, cache_gqa_v, cache_mla_ckv, cache_mla_krope, c, c_ctx, ada_w, ada_b, norm1_g, norm2_g, s5_lambda_re, s5_lambda_im, s5_log_step, s5_b_re, s5_b_im, s5_c_re, s5_c_im, s5_d, s5_glu_w1, s5_glu_w2, rwkv_mu, rwkv_w_r, rwkv_w_k, rwkv_w_v, rwkv_w_o, rwkv_w0, rwkv_w_l1, rwkv_w_l2, rwkv_a0, rwkv_a_l1, rwkv_a_l2, rwkv_g_l1, rwkv_g_l2, rwkv_k_k, rwkv_k_a, rwkv_r_k, rwkv_ln_g, rwkv_ln_b, gqa_w_qkv, gqa_q_norm, gqa_k_norm, gqa_sink, gqa_w_o, mla_w_dq, mla_q_lora_norm, mla_w_uq, mla_w_dkv, mla_kv_norm, mla_w_ukv, mla_q_norm, mla_k_norm, mla_w_o, moe_w_group, moe_b_group, moe_w_expert, moe_b_expert, moe_w_gate, moe_w_up, moe_w_down):
    bp, lp, dm = x_prompt.shape
    bs, ls, _ = x_sample.shape
    depth = ada_w.shape[0]
    assert dm == D_MODEL and 1 + bs <= MOD_ROWS
    seg_p = _Seg(bp, lp, 0, True)
    seg_s = _Seg(bs, ls, 1, False)
    bf = lambda a: a.astype(BF16)

    cond = jnp.zeros((MOD_ROWS, dm), F32).at[0].set(c_ctx).at[1:1 + bs].set(c)
    mods = _ada_all(cond, ada_w, ada_b).reshape(depth, MOD_ROWS, 6, dm)

    hp = x_prompt.reshape(bp * lp, dm)
    hs = x_sample.reshape(bs * ls, dm)
    outs = {}
    for layer in range(depth):
        kind = layer % 4
        mod = mods[layer]
        g1 = norm1_g[layer][None]
        if kind == 0:
            sw = _s5_weights(s5_lambda_re, s5_lambda_im, s5_log_step, s5_b_re, s5_b_im, s5_c_re, s5_c_im)
            w1, w2, dsk = bf(s5_glu_w1), bf(s5_glu_w2), s5_d[None]
            zeros = jnp.zeros((bp, 2, S5_N), F32)
            hp, f_re, f_im = _s5_layer(hp, seg_p, mod, g1, zeros, zeros, sw, dsk, w1, w2, True)
            hs, _, _ = _s5_layer(hs, seg_s, mod, g1, state_s5_re.reshape(bs, 2, S5_N),
                                 state_s5_im.reshape(bs, 2, S5_N), sw, dsk, w1, w2, False)
            outs['s5_re'] = f_re.reshape(bp, 2, S5_GROUPS, S5_STATE)
            outs['s5_im'] = f_im.reshape(bp, 2, S5_GROUPS, S5_STATE)
        elif kind == 1:
            vec = jnp.zeros((SUBLANES, dm), F32).at[0].set(rwkv_k_k).at[1].set(rwkv_k_a).at[2].set(rwkv_r_k.reshape(-1))
            rw = dict(
                proj=(rwkv_mu, bf(rwkv_w_r), bf(rwkv_w_k), bf(rwkv_w_v), bf(rwkv_w_l1), bf(rwkv_w_l2),
                      bf(rwkv_a_l1), bf(rwkv_a_l2), bf(rwkv_g_l1), bf(rwkv_g_l2), vec,
                      rwkv_w0[:, None, :], rwkv_a0[:, None, :]),
                out=(rwkv_ln_g[None], rwkv_ln_b[None], bf(rwkv_w_o)))
            zeros = jnp.zeros((bp, 2, RWKV_HEADS, RWKV_HEAD, RWKV_HEAD), F32)
            hp, outs['rwkv'] = _rwkv_layer(hp, seg_p, mod, g1, zeros, rw)
            hs, _ = _rwkv_layer(hs, seg_s, mod, g1, state_rwkv, rw)
        elif kind == 2:
            gw = (bf(gqa_w_qkv), jnp.tile(gqa_q_norm, GQA_HEADS)[None], jnp.tile(gqa_k_norm, GQA_KV_HEADS)[None],
                  gqa_sink, bf(gqa_w_o))
            hp, kp, vp = _gqa_layer(hp, seg_p, mod, g1, gw, None)
            outs['gqa_k'] = kp.reshape(bp, lp, GQA_KV_HEADS, GQA_HEAD_DIM)
            outs['gqa_v'] = vp.reshape(bp, lp, GQA_KV_HEADS, GQA_HEAD_DIM)
            lc = cache_gqa_k.shape[1]
            ctx = (cache_gqa_k.reshape(bs, lc, GQA_NK), cache_gqa_v.reshape(bs, lc, GQA_NK))
            hs, _, _ = _gqa_layer(hs, seg_s, mod, g1, gw, ctx)
        else:
            mw = _mla_weights(mla_w_dq, mla_q_lora_norm, mla_w_uq, mla_w_dkv, mla_kv_norm, mla_w_ukv, mla_q_norm,
                              mla_k_norm, mla_w_o)
            hp, ckv_p, kr_p = _mla_layer(hp, seg_p, mod, g1, mw, None)
            outs['mla_ckv'] = ckv_p.reshape(bp, lp, MLA_KV_LORA)
            outs['mla_kr'] = kr_p.reshape(bp, lp, MLA_ROPE)
            hs, _, _ = _mla_layer(hs, seg_s, mod, g1, mw, (cache_mla_ckv, cache_mla_krope))
        wr = jnp.zeros((dm, ROUTE_W), F32).at[:, :MOE_GROUPS].set(moe_w_group[layer])
        wr = wr.at[:, MOE_GROUPS:MOE_GROUPS + MOE_EXPERTS].set(moe_w_expert[layer])
        br = jnp.zeros((1, ROUTE_W), F32).at[0, :MOE_GROUPS].set(moe_b_group[layer])
        br = br.at[0, MOE_GROUPS:MOE_GROUPS + MOE_EXPERTS].set(moe_b_expert[layer])
        wr_hi, wr_lo = _hi_lo(wr)
        mo = (mod, norm2_g[layer][None], wr_hi, wr_lo, br, bf(moe_w_gate[layer]), bf(moe_w_up[layer]),
              bf(moe_w_down[layer]))
        hp = _moe_layer(hp, seg_p, *mo)
        hs = _moe_layer(hs, seg_s, *mo)
    return (hp.reshape(bp, lp, dm), hs.reshape(bs, ls, dm), outs['s5_re'], outs['s5_im'], outs['rwkv'],
            outs['gqa_k'], outs['gqa_v'], outs['mla_ckv'], outs['mla_kr'])
```

```python
import functools
import math

import jax
import jax.numpy as jnp
from jax import lax
from jax.experimental import pallas as pl
from jax.experimental.pallas import tpu as pltpu

F32 = jnp.float32
BF16 = jnp.bfloat16

D_MODEL = 1024
NORM_EPS = 1e-6
ROPE_BASE = 10000.0
GRID_W = 64
S5_GROUP = 16
S5_GROUPS = D_MODEL // S5_GROUP
S5_STATE = 64
S5_N = S5_GROUPS * S5_STATE
RWKV_HEAD = 64
RWKV_HEADS = D_MODEL // RWKV_HEAD
RWKV_DECAY_SCALE = math.exp(-0.5)
RWKV_GN_EPS = 64e-5
GQA_HEADS = 16
GQA_KV_HEADS = 4
GQA_HEAD_DIM = 64
WINDOW = 128
MLA_HEADS = 16
MLA_Q_LORA = 384
MLA_KV_LORA = 256
MLA_NOPE = 64
MLA_ROPE = 32
MLA_QK = MLA_NOPE + MLA_ROPE
MLA_V = 64
MLA_PAD = 128
MOE_GROUPS = 4
MOE_PER_GROUP = 4
MOE_EXPERTS = 16
MOE_HIDDEN = 512

LANES = 128
SUBLANES = 8
MXU_DIM = 256
VMEM_LIMIT = 56 * 1024 * 1024
MOD_ROWS = 8


def _cparams(*sem):
    return pltpu.CompilerParams(dimension_semantics=sem, vmem_limit_bytes=VMEM_LIMIT)


def _dot(a, b):
    return jnp.dot(a, b, preferred_element_type=F32)


def _dot_nt(a, b):
    return lax.dot_general(a, b, (((1,), (1,)), ((), ())), preferred_element_type=F32)


def _split3(x):
    hi = x.astype(BF16)
    r1 = x - hi.astype(F32)
    mid = r1.astype(BF16)
    lo = (r1 - mid.astype(F32)).astype(BF16)
    return hi, mid, lo


def _dot_x3(x, w_hi, w_lo):
    hi, mid, _ = _split3(x)
    return _dot(hi, w_hi) + (_dot(mid, w_hi) + _dot(hi, w_lo))


def _group_sum(x, ones_bd):
    n = x.shape[-1]
    outs = []
    for c in range(n // MXU_DIM):
        hi, mid, lo = _split3(x[:, c * MXU_DIM:(c + 1) * MXU_DIM])
        outs.append(_dot(hi, ones_bd) + (_dot(mid, ones_bd) + _dot(lo, ones_bd)))
    return outs[0] if len(outs) == 1 else jnp.concatenate(outs, axis=-1)


def _modulate(x, g, shift, scale):
    ms = jnp.mean(x * x, axis=-1, keepdims=True)
    return x * lax.rsqrt(ms + NORM_EPS) * g * (1.0 + scale) + shift


def _sigmoid(x):
    return 1.0 / (1.0 + jnp.exp(-x))


def _silu(x):
    return x * _sigmoid(x)


def _ones_blockdiag(group):
    i = jnp.arange(MXU_DIM) // group
    return (i[:, None] == i[None, :]).astype(BF16)


def _hi_lo(w):
    hi = w.astype(BF16)
    return hi, (w - hi.astype(F32)).astype(BF16)


class _Seg:
    def __init__(self, nseq, length, mod0, shared_mod):
        self.nseq, self.length, self.mod0, self.shared_mod = nseq, length, mod0, shared_mod
        self.n = nseq * length

    def mod_row(self, tm):
        if self.shared_mod:
            return lambda i: self.mod0
        per = self.length // tm
        return lambda i: self.mod0 + i // per

    def mod_rows(self, mod, k):
        if self.shared_mod:
            return jnp.broadcast_to(mod[self.mod0, k][None], (self.nseq, mod.shape[-1]))
        return mod[self.mod0:self.mod0 + self.nseq, k]


def _ada_kernel(cond_ref, w_ref, b_ref, o_ref):
    s = _silu(cond_ref[...])
    w = w_ref[0]
    w_hi = w.astype(BF16)
    w_lo = (w - w_hi.astype(F32)).astype(BF16)
    o_ref[0] = _dot_x3(s, w_hi, w_lo) + b_ref[0]


def _ada_all(cond, ada_w, ada_b):
    depth, d, n6 = ada_w.shape
    tn = 1536
    return pl.pallas_call(
        _ada_kernel,
        grid=(depth, n6 // tn),
        in_specs=[
            pl.BlockSpec((MOD_ROWS, d), lambda l, j: (0, 0)),
            pl.BlockSpec((1, d, tn), lambda l, j: (l, 0, j)),
            pl.BlockSpec((1, 1, tn), lambda l, j: (l, 0, j)),
        ],
        out_specs=pl.BlockSpec((1, MOD_ROWS, tn), lambda l, j: (l, 0, j)),
        out_shape=jax.ShapeDtypeStruct((depth, MOD_ROWS, n6), F32),
        compiler_params=_cparams("parallel", "parallel"),
        name="ada_mod",
    )(cond, ada_w, ada_b.reshape(depth, 1, n6))


ROUTE_W = LANES


def _route(logits):
    lane = lax.broadcasted_iota(jnp.int32, logits.shape, 1)
    neg = jnp.float32(-jnp.inf)
    big = jnp.int32(ROUTE_W)
    gmask = lane < MOE_GROUPS
    gl = jnp.where(gmask, logits, neg)
    gmax = jnp.max(gl, axis=-1, keepdims=True)
    g_top = jnp.min(jnp.where(gl == gmax, lane, big), axis=-1, keepdims=True)
    p_g = 1.0 / jnp.sum(jnp.where(gmask, jnp.exp(gl - gmax), 0.0), axis=-1, keepdims=True)
    lo = MOE_GROUPS + MOE_PER_GROUP * g_top
    emask = (lane >= lo) & (lane < lo + MOE_PER_GROUP)
    el = jnp.where(emask, logits, neg)
    m1 = jnp.max(el, axis=-1, keepdims=True)
    i1 = jnp.min(jnp.where(el == m1, lane, big), axis=-1, keepdims=True)
    el2 = jnp.where(lane == i1, neg, el)
    m2 = jnp.max(el2, axis=-1, keepdims=True)
    i2 = jnp.min(jnp.where(el2 == m2, lane, big), axis=-1, keepdims=True)
    e2 = jnp.exp(m2 - m1)
    w1 = p_g / (1.0 + e2)
    w2 = p_g * e2 / (1.0 + e2)
    gates = jnp.where(lane == i1, w1, 0.0) + jnp.where(lane == i2, w2, 0.0)
    return gates, g_top


def _lane_col(x, idx):
    lane = lax.broadcasted_iota(jnp.int32, x.shape, 1)
    return jnp.sum(jnp.where(lane == idx, x, 0.0), axis=-1, keepdims=True)


def _moe_kernel(h_ref, mod_ref, g_ref, wr_hi_ref, wr_lo_ref, br_ref, tri_ref, wg_ref, wu_ref, wd_ref, o_ref,
                v_scr, gate_scr, pos_scr, post_scr, x_scr, gx_scr, y_scr, yall_hi, yall_lo, acc_scr, cnt_scr,
                *, rows):
    e = pl.program_id(1)
    grp = e // MOE_PER_GROUP
    tm = v_scr.shape[0]
    nchunk = x_scr.shape[0] // rows

    @pl.when(e == 0)
    def _():
        mod = mod_ref[0]
        v = _modulate(h_ref[...], g_ref[...], mod[3:4], mod[4:5])
        logits = _dot_x3(v, wr_hi_ref[...], wr_lo_ref[...]) + br_ref[...]
        gates, g_top = _route(logits)
        gate_scr[...] = gates
        v_scr[...] = v.astype(BF16)
        lane = lax.broadcasted_iota(jnp.int32, gates.shape, 1)
        member = lane == g_top
        incl = _dot(tri_ref[...], jnp.where(member, 1.0, 0.0).astype(BF16))
        pos = jnp.where(member, incl - 1.0, -1.0)
        own = jnp.sum(jnp.where(member, pos, 0.0), axis=-1, keepdims=True)
        first = jnp.where(own < rows, g_top.astype(F32) * rows + own, -1.0)
        pos_scr[...] = jnp.where(lane == MOE_GROUPS, first, pos)
        pos_t = pos.T
        for g in range(MOE_GROUPS):
            post_scr[g] = jnp.broadcast_to(pos_t[g:g + 1], post_scr.shape[1:])
            cnt_scr[g] = jnp.sum(jnp.where(member & (lane == g), 1.0, 0.0)).astype(jnp.int32)

    cnt = cnt_scr[grp]
    spills = functools.reduce(jnp.logical_or, [cnt_scr[g] > rows for g in range(MOE_GROUPS)])

    @pl.when(jnp.logical_and(e == 0, spills))
    def _():
        acc_scr[...] = jnp.zeros_like(acc_scr)

    @pl.when(e % MOE_PER_GROUP == 0)
    def _():
        g_hi, g_mid, g_lo = _split3(gate_scr[...])
        y_scr[0:rows] = jnp.zeros((rows, y_scr.shape[1]), F32)
        for c in range(nchunk):
            @pl.when(cnt > c * rows)
            def _(c=c):
                slot = c * rows + lax.broadcasted_iota(jnp.int32, (rows, tm), 0)
                onehot = jnp.where(post_scr[grp, 0:1, :].astype(jnp.int32) == slot, 1.0, 0.0).astype(BF16)
                sl = slice(c * rows, (c + 1) * rows)
                x_scr[sl] = _dot(onehot, v_scr[...]).astype(BF16)
                gx_scr[sl] = _dot(onehot, g_hi) + (_dot(onehot, g_mid) + _dot(onehot, g_lo))
                if c > 0:
                    y_scr[sl] = jnp.zeros((rows, y_scr.shape[1]), F32)

    for c in range(nchunk):
        @pl.when(cnt > c * rows)
        def _(c=c):
            sl = slice(c * rows, (c + 1) * rows)
            x = x_scr[sl]
            gate = _lane_col(gx_scr[sl], e + MOE_GROUPS)
            hh = (_silu(_dot(x, wg_ref[0])) * _dot(x, wu_ref[0]) * gate).astype(BF16)
            y_scr[sl] += _dot(hh, wd_ref[0])

    @pl.when(e % MOE_PER_GROUP == MOE_PER_GROUP - 1)
    def _():
        y_hi, y_lo = _hi_lo(y_scr[0:rows])
        first_rows = pl.ds(pl.multiple_of(grp * rows, 16), rows)
        yall_hi[first_rows] = y_hi
        yall_lo[first_rows] = y_lo
        pos = _lane_col(pos_scr[...], grp).astype(jnp.int32)
        for c in range(1, nchunk):
            @pl.when(cnt > c * rows)
            def _(c=c):
                slot = c * rows + lax.broadcasted_iota(jnp.int32, (tm, rows), 1)
                onehot_t = jnp.where(pos == slot, 1.0, 0.0).astype(BF16)
                y_hi, y_lo = _hi_lo(y_scr[c * rows:(c + 1) * rows])
                acc_scr[...] += _dot(onehot_t, y_hi) + _dot(onehot_t, y_lo)

    @pl.when(e == pl.num_programs(1) - 1)
    def _():
        first = _lane_col(pos_scr[...], MOE_GROUPS).astype(jnp.int32)
        slot = lax.broadcasted_iota(jnp.int32, (tm, yall_hi.shape[0]), 1)
        onehot_t = jnp.where(first == slot, 1.0, 0.0).astype(BF16)
        moe = _dot(onehot_t, yall_hi[...]) + _dot(onehot_t, yall_lo[...])
        o_ref[...] = h_ref[...] + mod_ref[0][5:6] * moe

    @pl.when(jnp.logical_and(e == pl.num_programs(1) - 1, spills))
    def _():
        o_ref[...] += mod_ref[0][5:6] * acc_scr[...]


MOE_TM = 1024
MOE_ROWS = 288


def _moe_layer(h, seg, mod, norm_g, wr_hi, wr_lo, br, wg, wu, wd, tm=MOE_TM, rows=MOE_ROWS):
    n, d = h.shape
    nexp, _, f = wg.shape
    tm = min(tm, n if seg.shared_mod else seg.length)
    rows = min(rows, tm)
    nchunk = -(-tm // rows)
    rid = seg.mod_row(tm)
    tri = jnp.tri(tm, dtype=BF16)
    return pl.pallas_call(
        functools.partial(_moe_kernel, rows=rows),
        grid=(n // tm, nexp),
        in_specs=[
            pl.BlockSpec((tm, d), lambda i, e: (i, 0)),
            pl.BlockSpec((1, 6, d), lambda i, e: (rid(i), 0, 0)),
            pl.BlockSpec((1, d), lambda i, e: (0, 0)),
            pl.BlockSpec((d, ROUTE_W), lambda i, e: (0, 0)),
            pl.BlockSpec((d, ROUTE_W), lambda i, e: (0, 0)),
            pl.BlockSpec((1, ROUTE_W), lambda i, e: (0, 0)),
            pl.BlockSpec((tm, tm), lambda i, e: (0, 0)),
            pl.BlockSpec((1, d, f), lambda i, e: (e, 0, 0)),
            pl.BlockSpec((1, d, f), lambda i, e: (e, 0, 0)),
            pl.BlockSpec((1, f, d), lambda i, e: (e, 0, 0)),
        ],
        out_specs=pl.BlockSpec((tm, d), lambda i, e: (i, 0)),
        out_shape=jax.ShapeDtypeStruct((n, d), F32),
        scratch_shapes=[
            pltpu.VMEM((tm, d), BF16),
            pltpu.VMEM((tm, ROUTE_W), F32),
            pltpu.VMEM((tm, ROUTE_W), F32),
            pltpu.VMEM((MOE_GROUPS, SUBLANES, tm), F32),
            pltpu.VMEM((nchunk * rows, d), BF16),
            pltpu.VMEM((nchunk * rows, ROUTE_W), F32),
            pltpu.VMEM((nchunk * rows, d), F32),
            pltpu.VMEM((MOE_GROUPS * rows, d), BF16),
            pltpu.VMEM((MOE_GROUPS * rows, d), BF16),
            pltpu.VMEM((tm, d), F32),
            pltpu.SMEM((MOE_GROUPS,), jnp.int32),
        ],
        compiler_params=_cparams("parallel", "arbitrary"),
        name="moe",
    )(h, mod, norm_g, wr_hi, wr_lo, br, tri, wg, wu, wd)


S5_TL = 32
S5_CG = MXU_DIM // S5_GROUP
S5_NCH = D_MODEL // MXU_DIM
S5_CN = S5_CG * S5_STATE
S5_SCAN_W = 512


def _s5_scan_kernel(h_ref, sh_ref, sc_ref, g_ref, s0r_ref, s0i_ref, ar_ref, ai_ref, wb_ref, wcr_ref, wci_ref,
                    y_ref, fr_ref, fi_ref, bur, bui, str_, sti):
    d = pl.program_id(1)
    c = pl.program_id(2)
    tl, nb, dm = h_ref.shape

    @pl.when(c == 0)
    def _():
        str_[...] = s0r_ref[0]
        sti[...] = s0i_ref[0]

    u = _modulate(h_ref[...], g_ref[...], sh_ref[...], sc_ref[...])
    u2 = u.reshape(tl * nb, dm).astype(BF16)
    for cc in range(S5_NCH):
        r = _dot(u2[:, cc * MXU_DIM:(cc + 1) * MXU_DIM], wb_ref[0, cc])
        bur[:, cc * S5_CN:(cc + 1) * S5_CN] = r[:, :S5_CN]
        bui[:, cc * S5_CN:(cc + 1) * S5_CN] = r[:, S5_CN:]

    for lc in range(S5_N // S5_SCAN_W):
        cols = slice(lc * S5_SCAN_W, (lc + 1) * S5_SCAN_W)
        a_r = jnp.broadcast_to(ar_ref[0, :, cols], (nb, S5_SCAN_W))
        a_i = jnp.broadcast_to(ai_ref[0, :, cols], (nb, S5_SCAN_W))

        def body(i, carry, cols=cols, a_r=a_r, a_i=a_i):
            s_r, s_i = carry
            tt = jnp.where(d == 0, i, tl - 1 - i)
            rows = pl.ds(pl.multiple_of(tt * nb, nb), nb)
            n_r = a_r * s_r - a_i * s_i + bur[rows, cols]
            n_i = a_r * s_i + a_i * s_r + bui[rows, cols]
            bur[rows, cols] = n_r
            bui[rows, cols] = n_i
            return n_r, n_i

        s_r, s_i = lax.fori_loop(0, tl, body, (str_[:, cols], sti[:, cols]))
        str_[:, cols] = s_r
        sti[:, cols] = s_i

    for cc in range(S5_NCH):
        cols = slice(cc * S5_CN, (cc + 1) * S5_CN)
        y = _dot(bur[:, cols].astype(BF16), wcr_ref[0, cc]) + _dot(bui[:, cols].astype(BF16), wci_ref[0, cc])
        y_ref[0, :, :, cc * MXU_DIM:(cc + 1) * MXU_DIM] = y.reshape(tl, nb, MXU_DIM)

    @pl.when(c == pl.num_programs(2) - 1)
    def _():
        fr_ref[0] = str_[...]
        fi_ref[0] = sti[...]


def _s5_scan(h_t, shift, scale, norm_g, s0r, s0i, a_r, a_i, wb, wcr, wci):
    length, nbt, dm = h_t.shape
    nb = SUBLANES
    tl = S5_TL
    nch = length // tl

    def tch(d, c):
        return jnp.where(d == 0, c, nch - 1 - c)

    st_spec = pl.BlockSpec((1, nb, S5_N), lambda b, d, c: (d, b, 0))
    a_spec = pl.BlockSpec((1, 1, S5_N), lambda b, d, c: (d, 0, 0))
    return pl.pallas_call(
        _s5_scan_kernel,
        grid=(nbt // nb, 2, nch),
        in_specs=[
            pl.BlockSpec((tl, nb, dm), lambda b, d, c: (tch(d, c), b, 0)),
            pl.BlockSpec((nb, dm), lambda b, d, c: (b, 0)),
            pl.BlockSpec((nb, dm), lambda b, d, c: (b, 0)),
            pl.BlockSpec((1, dm), lambda b, d, c: (0, 0)),
            st_spec, st_spec, a_spec, a_spec,
            pl.BlockSpec((1, S5_NCH, MXU_DIM, 2 * S5_CN), lambda b, d, c: (d, 0, 0, 0)),
            pl.BlockSpec((1, S5_NCH, S5_CN, MXU_DIM), lambda b, d, c: (d, 0, 0, 0)),
            pl.BlockSpec((1, S5_NCH, S5_CN, MXU_DIM), lambda b, d, c: (d, 0, 0, 0)),
        ],
        out_specs=[
            pl.BlockSpec((1, tl, nb, dm), lambda b, d, c: (d, tch(d, c), b, 0)),
            st_spec, st_spec,
        ],
        out_shape=[
            jax.ShapeDtypeStruct((2, length, nbt, dm), F32),
            jax.ShapeDtypeStruct((2, nbt, S5_N), F32),
            jax.ShapeDtypeStruct((2, nbt, S5_N), F32),
        ],
        scratch_shapes=[
            pltpu.VMEM((tl * nb, S5_N), F32), pltpu.VMEM((tl * nb, S5_N), F32),
            pltpu.VMEM((nb, S5_N), F32), pltpu.VMEM((nb, S5_N), F32),
        ],
        compiler_params=_cparams("parallel", "arbitrary", "arbitrary"),
        name="s5_scan",
    )(h_t, shift, scale, norm_g, s0r, s0i, a_r, a_i, wb, wcr, wci)


def _s5_corr_kernel(fr_ref, fi_ref, pr_ref, pi_ref, ar_ref, ai_ref, wcr_ref, wci_ref, y_in_ref, y_ref,
                    zr, zi, str_, sti, *, nseq):
    d = pl.program_id(0)
    c = pl.program_id(1)
    tl, nb, _ = y_ref.shape[1:]

    @pl.when(c == 0)
    def _():
        f_r, f_i = fr_ref[0], fi_ref[0]
        p_r, p_i = pr_ref[0], pi_ref[0]
        row = lax.broadcasted_iota(jnp.int32, f_r.shape, 0)
        e_r = jnp.zeros_like(f_r)
        e_i = jnp.zeros_like(f_i)
        for _ in range(nb // nseq - 1):
            t_r = f_r + (p_r * e_r - p_i * e_i)
            t_i = f_i + (p_r * e_i + p_i * e_r)
            fwd_r = jnp.where(row < nseq, 0.0, pltpu.roll(t_r, nseq, 0))
            fwd_i = jnp.where(row < nseq, 0.0, pltpu.roll(t_i, nseq, 0))
            bwd_r = jnp.where(row >= nb - nseq, 0.0, pltpu.roll(t_r, nb - nseq, 0))
            bwd_i = jnp.where(row >= nb - nseq, 0.0, pltpu.roll(t_i, nb - nseq, 0))
            e_r = jnp.where(d == 0, fwd_r, bwd_r)
            e_i = jnp.where(d == 0, fwd_i, bwd_i)
        str_[...] = e_r
        sti[...] = e_i

    for lc in range(S5_N // S5_SCAN_W):
        cols = slice(lc * S5_SCAN_W, (lc + 1) * S5_SCAN_W)
        a_r = jnp.broadcast_to(ar_ref[0, :, cols], (nb, S5_SCAN_W))
        a_i = jnp.broadcast_to(ai_ref[0, :, cols], (nb, S5_SCAN_W))

        def body(i, carry, cols=cols, a_r=a_r, a_i=a_i):
            s_r, s_i = carry
            tt = jnp.where(d == 0, i, tl - 1 - i)
            rows = pl.ds(pl.multiple_of(tt * nb, nb), nb)
            n_r = a_r * s_r - a_i * s_i
            n_i = a_r * s_i + a_i * s_r
            zr[rows, cols] = n_r
            zi[rows, cols] = n_i
            return n_r, n_i

        s_r, s_i = lax.fori_loop(0, tl, body, (str_[:, cols], sti[:, cols]))
        str_[:, cols] = s_r
        sti[:, cols] = s_i

    for cc in range(S5_NCH):
        cols = slice(cc * S5_CN, (cc + 1) * S5_CN)
        y = _dot(zr[:, cols].astype(BF16), wcr_ref[0, cc]) + _dot(zi[:, cols].astype(BF16), wci_ref[0, cc])
        lanes = slice(cc * MXU_DIM, (cc + 1) * MXU_DIM)
        y_ref[0, :, :, lanes] = y_in_ref[0, :, :, lanes] + y.reshape(tl, nb, MXU_DIM)


def _s5_corr(y, f_r, f_i, p_r, p_i, a_r, a_i, wcr, wci, nseq):
    _, length, nb, dm = y.shape
    tl = S5_TL
    nch = length // tl
    tch = lambda d, c: jnp.where(d == 0, c, nch - 1 - c)
    st_spec = pl.BlockSpec((1, nb, S5_N), lambda d, c: (d, 0, 0))
    a_spec = pl.BlockSpec((1, 1, S5_N), lambda d, c: (d, 0, 0))
    w_spec = pl.BlockSpec((1, S5_NCH, S5_CN, MXU_DIM), lambda d, c: (d, 0, 0, 0))
    y_spec = pl.BlockSpec((1, tl, nb, dm), lambda d, c: (d, tch(d, c), 0, 0))
    return pl.pallas_call(
        functools.partial(_s5_corr_kernel, nseq=nseq),
        grid=(2, nch),
        in_specs=[st_spec, st_spec, a_spec, a_spec, a_spec, a_spec, w_spec, w_spec, y_spec],
        out_specs=y_spec,
        out_shape=jax.ShapeDtypeStruct(y.shape, F32),
        scratch_shapes=[pltpu.VMEM((tl * nb, S5_N), F32), pltpu.VMEM((tl * nb, S5_N), F32),
                        pltpu.VMEM((nb, S5_N), F32), pltpu.VMEM((nb, S5_N), F32)],
        input_output_aliases={8: 0},
        compiler_params=_cparams("parallel", "arbitrary"),
        name="s5_corr",
    )(f_r, f_i, p_r, p_i, a_r, a_i, wcr, wci, y)


def _gelu_tanh(x):
    return 0.5 * x * (1.0 + jnp.tanh(0.7978845608028654 * (x + 0.044715 * (x * x * x))))


def _s5_glu_kernel(h_ref, mod_ref, g_ref, dsk_ref, yf_ref, yb_ref, w1_ref, w2_ref, o_ref):
    mod = mod_ref[0]
    h = h_ref[...]
    u = _modulate(h, g_ref[...], mod[0:1], mod[1:2])
    y = u * dsk_ref[...] + (yf_ref[0] + yb_ref[0])
    z = _gelu_tanh(y).astype(BF16)
    out = _dot(z, w1_ref[...]) * _sigmoid(_dot(z, w2_ref[...]))
    o_ref[...] = h + mod[2:3] * out


def _s5_glu(h, seg, mod, norm_g, d_skip, y_t, w1, w2, tm=256):
    n, dm = h.shape
    _, seg_len, nbt, _ = y_t.shape
    y2 = y_t.reshape(2, seg_len, nbt * dm)
    tm = min(tm, seg_len)
    per_seq = seg.length // tm
    per_seg = seg_len // tm
    rid = seg.mod_row(tm)
    yidx = lambda d: (lambda i: (d, (i % per_seq) % per_seg, ((i % per_seq) // per_seg) * seg.nseq + i // per_seq))
    return pl.pallas_call(
        _s5_glu_kernel,
        grid=(n // tm,),
        in_specs=[
            pl.BlockSpec((tm, dm), lambda i: (i, 0)),
            pl.BlockSpec((1, 6, dm), lambda i: (rid(i), 0, 0)),
            pl.BlockSpec((1, dm), lambda i: (0, 0)),
            pl.BlockSpec((1, dm), lambda i: (0, 0)),
            pl.BlockSpec((1, tm, dm), yidx(0)),
            pl.BlockSpec((1, tm, dm), yidx(1)),
            pl.BlockSpec((dm, dm), lambda i: (0, 0)),
            pl.BlockSpec((dm, dm), lambda i: (0, 0)),
        ],
        out_specs=pl.BlockSpec((tm, dm), lambda i: (i, 0)),
        out_shape=jax.ShapeDtypeStruct((n, dm), F32),
        compiler_params=_cparams("parallel"),
        name="s5_glu",
    )(h, mod, norm_g, d_skip, y2, y2, w1, w2)


def _s5_weights(lam_re, lam_im, log_step, b_re, b_im, c_re, c_im):
    dt = jnp.exp(log_step)
    mag = jnp.exp(lam_re * dt)
    ab_re = mag * jnp.cos(lam_im * dt)
    ab_im = mag * jnp.sin(lam_im * dt)
    nr = ab_re - 1.0
    den = lam_re * lam_re + lam_im * lam_im
    f_re = (nr * lam_re + ab_im * lam_im) / den
    f_im = (ab_im * lam_re - nr * lam_im) / den
    fb_re = f_re[..., None] * b_re - f_im[..., None] * b_im
    fb_im = f_re[..., None] * b_im + f_im[..., None] * b_re
    eye = jnp.eye(S5_CG, dtype=F32)

    def bd_in(w):
        w = w.reshape(2, S5_NCH, S5_CG, S5_STATE, S5_GROUP)
        m = jnp.einsum('dnjpc,jk->dnjckp', w, eye)
        return m.reshape(2, S5_NCH, MXU_DIM, S5_CN)

    def bd_out(w):
        w = w.reshape(2, S5_NCH, S5_CG, S5_GROUP, S5_STATE)
        m = jnp.einsum('dnjcp,jk->dnjpkc', w, eye)
        return m.reshape(2, S5_NCH, S5_CN, MXU_DIM)

    wb = jnp.concatenate([bd_in(fb_re), bd_in(fb_im)], axis=-1).astype(BF16)
    return (ab_re.reshape(2, 1, S5_N), ab_im.reshape(2, 1, S5_N), wb,
            bd_out(c_re).astype(BF16), bd_out(-c_im).astype(BF16),
            (lam_re * dt).reshape(2, 1, S5_N), (lam_im * dt).reshape(2, 1, S5_N))


def _to_time_major(x, seg, nbt):
    xt = jnp.transpose(x.reshape(seg.nseq, seg.length, -1), (1, 0, 2))
    if nbt > seg.nseq:
        xt = jnp.pad(xt, ((0, 0), (0, nbt - seg.nseq), (0, 0)))
    return xt


def _pad_rows(x, nbt):
    return x if x.shape[0] == nbt else jnp.pad(x, ((0, nbt - x.shape[0]),) + ((0, 0),) * (x.ndim - 1))


def _s5_layer(h, seg, mod, norm_g, s0_re, s0_im, sw, d_skip, w1, w2, need_final):
    a_r, a_i, wb, wcr, wci, la_r, la_i = sw
    nseg = SUBLANES // seg.nseq if SUBLANES % seg.nseq == 0 else 1
    if not need_final and nseg > 1 and seg.length % (nseg * S5_TL) == 0:
        nseq, seg_len, dm = seg.nseq, seg.length // nseg, h.shape[1]
        h_t = jnp.transpose(h.reshape(nseq, nseg, seg_len, dm), (2, 1, 0, 3)).reshape(seg_len, SUBLANES, dm)
        shift = jnp.tile(seg.mod_rows(mod, 0), (nseg, 1))
        scale = jnp.tile(seg.mod_rows(mod, 1), (nseg, 1))

        def first_rows(s0):
            z = jnp.zeros((nseg - 1, nseq, S5_N), F32)
            fwd = jnp.concatenate([s0[None, :, 0], z], axis=0).reshape(SUBLANES, S5_N)
            bwd = jnp.concatenate([z, s0[None, :, 1]], axis=0).reshape(SUBLANES, S5_N)
            return jnp.stack([fwd, bwd])

        y_t, f_r, f_i = _s5_scan(h_t, shift, scale, norm_g, first_rows(s0_re), first_rows(s0_im), a_r, a_i,
                                 wb, wcr, wci)
        mag = jnp.exp(la_r * seg_len)
        y_t = _s5_corr(y_t, f_r, f_i, mag * jnp.cos(la_i * seg_len), mag * jnp.sin(la_i * seg_len), a_r, a_i,
                       wcr, wci, nseq)
        h_new = _s5_glu(h, seg, mod, norm_g, d_skip, y_t, w1, w2)
        return h_new, None, None
    nbt = -(-seg.nseq // SUBLANES) * SUBLANES
    h_t = _to_time_major(h, seg, nbt)
    shift = _pad_rows(seg.mod_rows(mod, 0), nbt)
    scale = _pad_rows(seg.mod_rows(mod, 1), nbt)
    pad = ((0, 0), (0, nbt - seg.nseq), (0, 0))
    s0r = jnp.pad(jnp.transpose(s0_re, (1, 0, 2)), pad)
    s0i = jnp.pad(jnp.transpose(s0_im, (1, 0, 2)), pad)
    y_t, f_r, f_i = _s5_scan(h_t, shift, scale, norm_g, s0r, s0i, a_r, a_i, wb, wcr, wci)
    h_new = _s5_glu(h, seg, mod, norm_g, d_skip, y_t, w1, w2)
    return h_new, jnp.transpose(f_r[:, :seg.nseq], (1, 0, 2)), jnp.transpose(f_i[:, :seg.nseq], (1, 0, 2))


def _rwkv_proj_kernel(h_ref, hp_ref, hn_ref, mod_ref, g_ref, mu_ref, wr_ref, wk_ref, wv_ref,
                      wl1_ref, wl2_ref, al1_ref, al2_ref, gl1_ref, gl2_ref, vec_ref, w0_ref, a0_ref, ones_ref,
                      r_ref, v_ref, kk_ref, w_ref, ka_ref, kd_ref, g_out_ref, bonus_ref, *, per):
    i = pl.program_id(0)
    mod = mod_ref[0]
    gn = g_ref[...]
    tm = h_ref.shape[0]
    u = _modulate(h_ref[...], gn, mod[0:1], mod[1:2])
    first = (i % per) == 0
    last = (i % per) == per - 1
    up = _modulate(hp_ref[SUBLANES - 1:SUBLANES, :], gn, mod[0:1], mod[1:2]) * jnp.where(first, 0.0, 1.0)
    un = _modulate(hn_ref[0:1, :], gn, mod[0:1], mod[1:2]) * jnp.where(last, 0.0, 1.0)
    row = lax.broadcasted_iota(jnp.int32, u.shape, 0)
    prev = jnp.where(row == 0, up, pltpu.roll(u, 1, 0))
    nxt = jnp.where(row == tm - 1, un, pltpu.roll(u, tm - 1, 0))
    dx = 0.5 * (prev + nxt) - u
    mu = mu_ref[...]

    def mix(j):
        return (u + dx * mu[j:j + 1]).astype(BF16)

    vec = vec_ref[...]
    r = _dot(mix(0), wr_ref[...])
    k = _dot(mix(2), wk_ref[...])
    v = _dot(mix(3), wv_ref[...])
    ones = ones_ref[...]
    kk = k * vec[0:1]
    kk = kk * lax.rsqrt(_group_sum(kk * kk, ones) + 1e-12)
    g_out_ref[...] = _dot(_sigmoid(_dot(mix(5), gl1_ref[...])).astype(BF16), gl2_ref[...])
    xw = mix(1)
    xa = mix(4)
    kd_sum = None
    for d in range(2):
        wl = _dot(jnp.tanh(_dot(xw, wl1_ref[d])).astype(BF16), wl2_ref[d])
        w_ref[d] = jnp.exp(-RWKV_DECAY_SCALE * _sigmoid(w0_ref[d] + wl))
        al = _dot(_dot(xa, al1_ref[d]).astype(BF16), al2_ref[d])
        a = _sigmoid(a0_ref[d] + al)
        kd = k * (1.0 + (a - 1.0) * vec[1:2])
        kd_ref[d] = kd
        ka_ref[d] = kk * a
        kd_sum = kd if kd_sum is None else kd_sum + kd
    r_ref[...] = r
    v_ref[...] = v
    kk_ref[...] = kk
    bonus_ref[...] = _group_sum(r * kd_sum * vec[2:3], ones) * v


def _rwkv_proj(h, seg, mod, norm_g, mu, wr, wk, wv, wl1, wl2, al1, al2, gl1, gl2, vec, w0, a0, tm=256):
    n, dm = h.shape
    tm = min(tm, seg.length)
    per = seg.length // tm
    rid = seg.mod_row(tm)
    hb = tm // SUBLANES
    nblk = n // SUBLANES
    full = lambda a: pl.BlockSpec(a.shape, lambda i, nd=a.ndim: (0,) * nd)
    tile = pl.BlockSpec((tm, dm), lambda i: (i, 0))
    tile2 = pl.BlockSpec((2, tm, dm), lambda i: (0, i, 0))
    ones = _ones_blockdiag(RWKV_HEAD)
    one = jax.ShapeDtypeStruct((n, dm), F32)
    two = jax.ShapeDtypeStruct((2, n, dm), F32)
    return pl.pallas_call(
        functools.partial(_rwkv_proj_kernel, per=per),
        grid=(n // tm,),
        in_specs=[
            tile,
            pl.BlockSpec((SUBLANES, dm), lambda i: (jnp.maximum(i * hb - 1, 0), 0)),
            pl.BlockSpec((SUBLANES, dm), lambda i: (jnp.minimum((i + 1) * hb, nblk - 1), 0)),
            pl.BlockSpec((1, 6, dm), lambda i: (rid(i), 0, 0)),
            full(norm_g), full(mu), full(wr), full(wk), full(wv), full(wl1), full(wl2), full(al1), full(al2),
            full(gl1), full(gl2), full(vec), full(w0), full(a0), full(ones),
        ],
        out_specs=[tile, tile, tile, tile2, tile2, tile2, tile, tile],
        out_shape=[one, one, one, two, two, two, one, one],
        compiler_params=_cparams("parallel"),
        name="rwkv_proj",
    )(h, h, h, mod, norm_g, mu, wr, wk, wv, wl1, wl2, al1, al2, gl1, gl2, vec, w0, a0, ones)


RWKV_TC = 32
RWKV_SLOTS = 2
RWKV_AHEAD = 1


def _rwkv_chain_steps(tc, nv, load_pieces, store_pieces, st, mk, mv, ys):
    nk = st.shape[0]
    ns, ahead = RWKV_SLOTS, RWKV_AHEAD
    ys[...] = jnp.zeros_like(ys)
    for a in range(ahead):
        for piece in load_pieces(a, a):
            piece()

    def step(i, slot):
        side = (load_pieces(jnp.minimum(i + ahead, tc - 1), (slot + ahead) % ns)
                + store_pieces(jnp.maximum(i - 1, 0), (slot - 1) % ns))
        every = (2 * nk) // (len(side) + 1)
        at = {(p + 1) * every: piece for p, piece in enumerate(side)}
        row = lambda q, k: mk[slot, q, k:k + 1, :]
        vv = mv[slot]
        sa = None
        for k in range(nk):
            term = st[k] * row(0, k)
            sa = term if sa is None else sa + term
            if k + 1 in at:
                at[k + 1]()
        y = None
        for k in range(nk):
            s = st[k] * row(2, k) + (vv * row(3, k) - sa * row(1, k))
            st[k] = s
            term = s * row(4, k)
            y = term if y is None else y + term
            if nk + k + 1 in at:
                at[nk + k + 1]()
        ys[slot] = y

    def some_steps(j, carry):
        for u in range(ns):
            step(ns * j + u, u)
        return carry

    lax.fori_loop(0, tc // ns, some_steps, 0)
    for piece in store_pieces(tc - 1, (tc - 1) % ns):
        piece()


def _heads_to_rows(s_scr, x, lanes, rows_per_head):
    for h in range(RWKV_HEADS):
        s_scr[h * rows_per_head:(h + 1) * rows_per_head, lanes] = x[:, h * RWKV_HEAD:(h + 1) * RWKV_HEAD]


def _rwkv_scan_grid_kernel(kk_ref, r_ref, v_ref, ka_ref, w_ref, kd_ref, s0_ref, y_ref, sf_ref,
                           st, mk, mv, ys, s_scr):
    d = pl.program_id(0)
    c = pl.program_id(2)
    tc, nb, _ = kk_ref.shape
    nk = RWKV_HEAD
    lo, hi = slice(0, nk), slice(nk, 2 * nk)

    @pl.when(c == 0)
    def _():
        st[...] = s0_ref[0]

    def load_pieces(i, slot):
        t = jnp.where(d == 0, i, tc - 1 - i)
        pairs = ((kk_ref, None, ka_ref, 0), (w_ref, 0, kd_ref, 0), (r_ref, None, v_ref, None))

        def piece(p):
            ra, da, rb, db = pairs[p]
            xa = ra[t] if da is None else ra[da, t]
            xb = rb[t] if db is None else rb[db, t]
            scr = s_scr.at[3 * slot + p]
            _heads_to_rows(scr, xa, lo, nb)
            _heads_to_rows(scr, xb, hi, nb)
            tt = scr[...].T
            mk[slot, 2 * p] = tt[lo]
            if p < 2:
                mk[slot, 2 * p + 1] = tt[hi]
            else:
                mv[slot] = tt[hi]
        return [functools.partial(piece, p) for p in range(3)]

    def store_pieces(i, slot):
        t = jnp.where(d == 0, i, tc - 1 - i)

        def piece():
            yt = ys[slot].T
            for h in range(RWKV_HEADS):
                y_ref[0, t, :, h * nk:(h + 1) * nk] = yt[h * nb:(h + 1) * nb, :]
        return [piece]

    _rwkv_chain_steps(tc, nk, load_pieces, store_pieces, st, mk, mv, ys)

    @pl.when(c == pl.num_programs(2) - 1)
    def _():
        sf_ref[0] = st[...]


def _rwkv_scan_grid(kk, r, v, ka, w, kd, s0):
    n, nseq, dm = kk.shape
    nk = RWKV_HEAD
    nb = SUBLANES
    tc = RWKV_TC
    nch = n // tc
    tch = lambda d, t: jnp.where(d == 0, t, nch - 1 - t)
    shared = pl.BlockSpec((tc, nb, dm), lambda d, l, t: (tch(d, t), l, 0))
    perdir = pl.BlockSpec((1, tc, nb, dm), lambda d, l, t: (d, tch(d, t), l, 0))
    sspec = pl.BlockSpec((1, nk, nk, LANES), lambda d, l, t: (d, 0, 0, l))
    return pl.pallas_call(
        _rwkv_scan_grid_kernel,
        grid=(2, nseq // nb, nch),
        in_specs=[shared, shared, shared, perdir, perdir, perdir, sspec],
        out_specs=[perdir, sspec],
        out_shape=[jax.ShapeDtypeStruct((2, n, nseq, dm), F32), jax.ShapeDtypeStruct(s0.shape, F32)],
        scratch_shapes=[pltpu.VMEM((nk, nk, LANES), F32), pltpu.VMEM((RWKV_SLOTS, 5, nk, LANES), F32),
                        pltpu.VMEM((RWKV_SLOTS, nk, LANES), F32), pltpu.VMEM((RWKV_SLOTS, nk, LANES), F32),
                        pltpu.VMEM((3 * RWKV_SLOTS, LANES, LANES), F32)],
        compiler_params=_cparams("parallel", "parallel", "arbitrary"),
        name="rwkv_scan_grid",
    )(kk, r, v, ka, w, kd, s0)


def _rwkv_scan_lanes_kernel(kkf, rf, vf, kaf, wf, kdf, kkr, rr, vr, kar, wr, kdr, s0_ref, yf_ref, yr_ref, sf_ref,
                            st, mk, mv, ys, s_scr, y_scr):
    c = pl.program_id(0)
    nb, tc, _ = kkf.shape
    nk = RWKV_HEAD
    chains = s_scr.shape[1]
    vs = LANES // chains
    nv = nk // vs
    lo, hi = slice(0, nk), slice(nk, 2 * nk)

    @pl.when(c == 0)
    def _():
        st[...] = s0_ref[...]

    def load_pieces(i, slot):
        j = tc - 1 - i

        def both(f, r, lead=()):
            rows = [f[lead + (b, pl.ds(i, 1))] for b in range(nb)] + [r[lead + (b, pl.ds(j, 1))] for b in range(nb)]
            return jnp.concatenate(rows, axis=0)

        rep = lambda a: a if vs == 1 else jnp.concatenate([a] * vs, axis=1)

        def piece(p):
            if p == 0:
                xa, xb = both(kkf, kkr), both(kaf, kar, (0,))
            elif p == 1:
                xa, xb = both(wf, wr, (0,)), both(kdf, kdr, (0,))
            else:
                xa, xb = both(rf, rr), both(vf, vr)
            scr = s_scr.at[3 * slot + p]
            _heads_to_rows(scr, xa, lo, 2 * nb)
            _heads_to_rows(scr, xb, hi, 2 * nb)
            tt = scr[...].T
            mk[slot, 2 * p] = rep(tt[lo])
            if p < 2:
                mk[slot, 2 * p + 1] = rep(tt[hi])
            else:
                tv = tt[hi]
                mv[slot] = tv if vs == 1 else jnp.concatenate([tv[s * nv:(s + 1) * nv] for s in range(vs)], axis=1)
        return [functools.partial(piece, p) for p in range(3)]

    def store_pieces(i, slot):
        def piece():
            yt = ys[slot].T
            for s in range(vs):
                for h in range(RWKV_HEADS):
                    r0 = s * chains + h * 2 * nb
                    y_scr[slot, 0:2 * nb, h * nk + s * nv:h * nk + (s + 1) * nv] = yt[r0:r0 + 2 * nb, :]
            for b in range(nb):
                yf_ref[b, pl.ds(i, 1), :] = y_scr[slot, b:b + 1, :]
                yr_ref[b, pl.ds(tc - 1 - i, 1), :] = y_scr[slot, nb + b:nb + b + 1, :]
        return [piece]

    _rwkv_chain_steps(tc, nv, load_pieces, store_pieces, st, mk, mv, ys)

    @pl.when(c == pl.num_programs(0) - 1)
    def _():
        sf_ref[...] = st[...]


def _rwkv_scan_lanes(kk, r, v, ka, w, kd, s0):
    nb, n, dm = kk.shape
    nk = RWKV_HEAD
    nv = s0.shape[1]
    chains = RWKV_HEADS * 2 * nb
    tc = RWKV_TC
    nch = n // tc
    sf_ = pl.BlockSpec((nb, tc, dm), lambda t: (0, t, 0))
    sr_ = pl.BlockSpec((nb, tc, dm), lambda t: (0, nch - 1 - t, 0))
    df_ = pl.BlockSpec((1, nb, tc, dm), lambda t: (0, 0, t, 0))
    dr_ = pl.BlockSpec((1, nb, tc, dm), lambda t: (1, 0, nch - 1 - t, 0))
    sspec = pl.BlockSpec((nk, nv, LANES), lambda t: (0, 0, 0))
    yshape = jax.ShapeDtypeStruct((nb, n, dm), F32)
    return pl.pallas_call(
        _rwkv_scan_lanes_kernel,
        grid=(nch,),
        in_specs=[sf_, sf_, sf_, df_, df_, df_, sr_, sr_, sr_, dr_, dr_, dr_, sspec],
        out_specs=[sf_, sr_, sspec],
        out_shape=[yshape, yshape, jax.ShapeDtypeStruct((nk, nv, LANES), F32)],
        scratch_shapes=[pltpu.VMEM((nk, nv, LANES), F32), pltpu.VMEM((RWKV_SLOTS, 5, nk, LANES), F32),
                        pltpu.VMEM((RWKV_SLOTS, nv, LANES), F32), pltpu.VMEM((RWKV_SLOTS, nv, LANES), F32),
                        pltpu.VMEM((3 * RWKV_SLOTS, chains, LANES), F32),
                        pltpu.VMEM((RWKV_SLOTS, SUBLANES, dm), F32)],
        compiler_params=_cparams("arbitrary"),
        name="rwkv_scan_lanes",
    )(kk, r, v, ka, w, kd, kk, r, v, ka, w, kd, s0)


def _rwkv_out_kernel(h_ref, mod_ref, yf_ref, yb_ref, bonus_ref, g_ref, lng_ref, lnb_ref, wo_ref, ones_ref, o_ref):
    ones = ones_ref[...]
    y = (yf_ref[...] + yb_ref[...]).reshape(h_ref.shape)
    inv = 1.0 / RWKV_HEAD
    mean = _group_sum(y, ones) * inv
    yc = y - mean
    var = _group_sum(yc * yc, ones) * inv
    yn = yc * lax.rsqrt(var + RWKV_GN_EPS) * lng_ref[...] + lnb_ref[...] + bonus_ref[...]
    out = _dot((yn * g_ref[...]).astype(BF16), wo_ref[...])
    o_ref[...] = h_ref[...] + mod_ref[0][2:3] * out


def _rwkv_out(h, seg, mod, yf, yb, bonus, g, ln_g, ln_b, wo, tm=256):
    n, dm = h.shape
    tm = min(tm, seg.length)
    rid = seg.mod_row(tm)
    tile = pl.BlockSpec((tm, dm), lambda i: (i, 0))
    row = pl.BlockSpec((1, dm), lambda i: (0, 0))
    ones = _ones_blockdiag(RWKV_HEAD)
    stacked = yf.ndim == 3
    yf_spec = pl.BlockSpec((1, tm, dm), lambda i: (0, i, 0)) if stacked else tile
    yb_spec = pl.BlockSpec((1, tm, dm), lambda i: (1, i, 0)) if stacked else tile
    return pl.pallas_call(
        _rwkv_out_kernel,
        grid=(n // tm,),
        in_specs=[tile, pl.BlockSpec((1, 6, dm), lambda i: (rid(i), 0, 0)), yf_spec, yb_spec, tile, tile, row, row,
                  pl.BlockSpec((dm, dm), lambda i: (0, 0)), pl.BlockSpec(ones.shape, lambda i: (0, 0))],
        out_specs=tile,
        out_shape=jax.ShapeDtypeStruct((n, dm), F32),
        compiler_params=_cparams("parallel"),
        name="rwkv_out",
    )(h, mod, yf, yb, bonus, g, ln_g, ln_b, wo, ones)


def _rwkv_layer(h, seg, mod, norm_g, s0, rw):
    nseq, length = seg.nseq, seg.length
    hh, nk = RWKV_HEADS, RWKV_HEAD
    r, v, kk, w, ka, kd, g, bonus = _rwkv_proj(h, seg, mod, norm_g, *rw['proj'])
    per_dir = nseq * hh
    n = nseq * length
    dm = hh * nk

    seq3 = lambda x: x.reshape(*x.shape[:-2], nseq, length, dm)

    def tmaj(x):
        nl = x.ndim - 2
        return jnp.transpose(seq3(x), tuple(range(nl)) + (nl + 1, nl, nl + 2))

    if per_dir % LANES == 0:
        ng = nseq // SUBLANES
        s0_s = jnp.transpose(s0.reshape(ng, SUBLANES, 2, hh, nk, nk), (2, 5, 4, 0, 3, 1)).reshape(2, nk, nk, per_dir)
        y_t, sf = _rwkv_scan_grid(tmaj(kk), tmaj(r), tmaj(v), tmaj(ka), tmaj(w), tmaj(kd), s0_s)
        yf = yb = jnp.transpose(y_t, (0, 2, 1, 3)).reshape(2, n, dm)
        s_fin = jnp.transpose(sf.reshape(2, nk, nk, ng, hh, SUBLANES), (3, 5, 0, 4, 2, 1)).reshape(nseq, 2, hh, nk, nk)
    else:
        vs = LANES // (2 * per_dir)
        assert vs * 2 * per_dir == LANES
        nv = nk // vs
        s0_s = jnp.transpose(s0.reshape(nseq, 2, hh, vs, nv, nk), (5, 4, 3, 2, 1, 0)).reshape(nk, nv, LANES)
        y_f, y_r, sf = _rwkv_scan_lanes(seq3(kk), seq3(r), seq3(v), seq3(ka), seq3(w), seq3(kd), s0_s)
        yf, yb = y_f.reshape(n, dm), y_r.reshape(n, dm)
        s_fin = jnp.transpose(sf.reshape(nk, nv, vs, hh, 2, nseq), (5, 4, 3, 2, 1, 0)).reshape(nseq, 2, hh, nk, nk)
    h_new = _rwkv_out(h, seg, mod, yf, yb, bonus, g, *rw['out'])
    return h_new, s_fin


def _rope_tables(length, half, first_lane, period):
    lane = jnp.arange(LANES) % period - first_lane
    active = (lane >= 0) & (lane < 4 * half)
    quarter = jnp.clip(lane, 0, 4 * half - 1) // half
    freq = ROPE_BASE ** (-(jnp.clip(lane, 0, 4 * half - 1) % half).astype(F32) / half)
    t = jnp.arange(length)
    pos = jnp.where(quarter[None, :] < 2, (t // GRID_W)[:, None], (t % GRID_W)[:, None]).astype(F32)
    ang = pos * freq[None, :]
    cos = jnp.where(active[None], jnp.cos(ang), 1.0)
    sin = jnp.where(active[None], jnp.sin(ang), 0.0)
    first = (quarter % 2 == 0)[None]
    return cos, jnp.where(first, -sin, 0.0), jnp.where(first, 0.0, sin)


def _rope(x, cos, sa, sb, half):
    w = x.shape[-1]
    rep = w // LANES
    tile = (lambda a: a) if rep == 1 else (lambda a: jnp.concatenate([a] * rep, axis=-1))
    return x * tile(cos) + pltpu.roll(x, w - half, 1) * tile(sa) + pltpu.roll(x, half, 1) * tile(sb)


def _attn_kernel(*refs, groups, dq, dv, hpb, scale, band, has_ctx, has_sink):
    it = iter(refs)
    q_ref, k_ref, v_ref = next(it), next(it), next(it)
    kc_ref, vc_ref = (next(it), next(it)) if has_ctx else (None, None)
    sink_ref = next(it) if has_sink else None
    o_ref = next(it)
    tq = q_ref.shape[1]
    lk = k_ref.shape[1]
    rows = groups * tq
    if band:
        lw = min(lk, tq + 2 * WINDOW)
        q0 = pl.program_id(2) * tq
        start = pl.multiple_of(jnp.clip(q0 - WINDOW, 0, lk - lw), LANES)
        ksl = pl.ds(start, lw)
        qpos = q0 + (lax.broadcasted_iota(jnp.int32, (rows, lw), 0) & (tq - 1))
        kpos = start + lax.broadcasted_iota(jnp.int32, (rows, lw), 1)
        valid = jnp.abs(qpos - kpos) <= WINDOW
    else:
        ksl = slice(None)
    for j in range(hpb):
        kh = k_ref[0, ksl, j * dq:(j + 1) * dq].astype(BF16)
        vh = v_ref[0, ksl, j * dv:(j + 1) * dv].astype(BF16)
        heads = range(j * groups, (j + 1) * groups)
        qs = [q_ref[0, :, hq * dq:(hq + 1) * dq].astype(BF16) for hq in heads]
        qh = qs[0] if groups == 1 else jnp.concatenate(qs, axis=0)
        s = _dot_nt(qh, kh) * scale
        if band:
            s = jnp.where(valid, s, -jnp.inf)
        m = jnp.max(s, axis=-1, keepdims=True)
        if has_ctx:
            kch = kc_ref[0, :, j * dq:(j + 1) * dq].astype(BF16)
            vch = vc_ref[0, :, j * dv:(j + 1) * dv].astype(BF16)
            sc = _dot_nt(qh, kch) * scale
            m = jnp.maximum(m, jnp.max(sc, axis=-1, keepdims=True))
        if has_sink:
            sk = [jnp.broadcast_to(sink_ref[0, :, hq:hq + 1], (tq, 1)) for hq in heads]
            snk = sk[0] if groups == 1 else jnp.concatenate(sk, axis=0)
            m = jnp.maximum(m, snk)
        p = jnp.exp(s - m)
        den = jnp.sum(p, axis=-1, keepdims=True)
        acc = _dot(p.astype(BF16), vh)
        if has_ctx:
            pc = jnp.exp(sc - m)
            den = den + jnp.sum(pc, axis=-1, keepdims=True)
            acc = acc + _dot(pc.astype(BF16), vch)
        if has_sink:
            den = den + jnp.exp(snk - m)
        out = acc / den
        for g, hq in enumerate(heads):
            o_ref[0, :, hq * dv:(hq + 1) * dv] = out[g * tq:(g + 1) * tq]


def _attention(q, k, v, ctx, sink, *, heads, groups, dq, dv, scale, band, tq=256):
    b, lq, _ = q.shape
    lk = k.shape[1]
    hpb = 2
    nhb = heads // hpb
    ins = [q, k, v]
    specs = [
        pl.BlockSpec((1, tq, hpb * groups * dq), lambda bi, hb, i: (bi, i, hb)),
        pl.BlockSpec((1, lk, hpb * dq), lambda bi, hb, i: (bi, 0, hb)),
        pl.BlockSpec((1, lk, hpb * dv), lambda bi, hb, i: (bi, 0, hb)),
    ]
    if ctx is not None:
        lc = ctx[0].shape[1]
        ins += list(ctx)
        specs += [pl.BlockSpec((1, lc, hpb * dq), lambda bi, hb, i: (bi, 0, hb)),
                  pl.BlockSpec((1, lc, hpb * dv), lambda bi, hb, i: (bi, 0, hb))]
    if sink is not None:
        sk = jnp.pad(sink.reshape(nhb, 1, hpb * groups), ((0, 0), (0, 0), (0, LANES - hpb * groups)))
        ins.append(sk)
        specs.append(pl.BlockSpec((1, 1, LANES), lambda bi, hb, i: (hb, 0, 0)))
    return pl.pallas_call(
        functools.partial(_attn_kernel, groups=groups, dq=dq, dv=dv, hpb=hpb, scale=scale, band=band,
                          has_ctx=ctx is not None, has_sink=sink is not None),
        grid=(b, nhb, lq // tq),
        in_specs=specs,
        out_specs=pl.BlockSpec((1, tq, hpb * groups * dv), lambda bi, hb, i: (bi, i, hb)),
        out_shape=jax.ShapeDtypeStruct((b, lq, heads * groups * dv), F32),
        compiler_params=_cparams("parallel", "parallel", "arbitrary"),
        name="attention",
    )(*ins)


def _proj_res_kernel(h_ref, mod_ref, a_ref, w_ref, o_ref):
    o_ref[...] = h_ref[...] + mod_ref[0][2:3] * _dot(a_ref[...].astype(BF16), w_ref[...])


def _proj_residual(h, seg, mod, a, w, tm=256):
    n, dm = h.shape
    tm = min(tm, seg.length)
    rid = seg.mod_row(tm)
    return pl.pallas_call(
        _proj_res_kernel,
        grid=(n // tm,),
        in_specs=[pl.BlockSpec((tm, dm), lambda i: (i, 0)), pl.BlockSpec((1, 6, dm), lambda i: (rid(i), 0, 0)),
                  pl.BlockSpec((tm, a.shape[1]), lambda i: (i, 0)), pl.BlockSpec(w.shape, lambda i: (0, 0))],
        out_specs=pl.BlockSpec((tm, dm), lambda i: (i, 0)),
        out_shape=jax.ShapeDtypeStruct((n, dm), F32),
        compiler_params=_cparams("parallel"),
        name="proj_residual",
    )(h, mod, a, w)


GQA_NQ = GQA_HEADS * GQA_HEAD_DIM
GQA_NK = GQA_KV_HEADS * GQA_HEAD_DIM
GQA_ROPE_HALF = GQA_HEAD_DIM // 4


def _gqa_qkv_kernel(*refs, use_rope):
    if use_rope:
        h_ref, mod_ref, g_ref, w_ref, gq_ref, gk_ref, ones_ref, cos_ref, sa_ref, sb_ref, q_ref, k_ref, v_ref = refs
    else:
        h_ref, mod_ref, g_ref, w_ref, gq_ref, gk_ref, ones_ref, q_ref, k_ref, v_ref = refs
    mod = mod_ref[0]
    u = _modulate(h_ref[...], g_ref[...], mod[0:1], mod[1:2]).astype(BF16)
    qkv = _dot(u, w_ref[...])
    ones = ones_ref[...]
    inv = 1.0 / GQA_HEAD_DIM
    q = qkv[:, :GQA_NQ]
    k = qkv[:, GQA_NQ:GQA_NQ + GQA_NK]
    q = q * lax.rsqrt(_group_sum(q * q, ones) * inv + NORM_EPS) * gq_ref[...]
    k = k * lax.rsqrt(_group_sum(k * k, ones) * inv + NORM_EPS) * gk_ref[...]
    if use_rope:
        cos, sa, sb = cos_ref[...], sa_ref[...], sb_ref[...]
        q = _rope(q, cos, sa, sb, GQA_ROPE_HALF)
        k = _rope(k, cos, sa, sb, GQA_ROPE_HALF)
    q_ref[...] = q
    k_ref[...] = k
    v_ref[...] = qkv[:, GQA_NQ + GQA_NK:]


def _gqa_qkv(h, seg, mod, norm_g, w_qkv, gq, gk, rope, tm=256):
    n, dm = h.shape
    tm = min(tm, seg.length)
    per = seg.length // tm
    rid = seg.mod_row(tm)
    ones = _ones_blockdiag(GQA_HEAD_DIM)
    full = lambda a: pl.BlockSpec(a.shape, lambda i, nd=a.ndim: (0,) * nd)
    ins = [h, mod, norm_g, w_qkv, gq, gk, ones]
    specs = [pl.BlockSpec((tm, dm), lambda i: (i, 0)), pl.BlockSpec((1, 6, dm), lambda i: (rid(i), 0, 0)),
             full(norm_g), full(w_qkv), full(gq), full(gk), full(ones)]
    if rope is not None:
        ins += list(rope)
        specs += [pl.BlockSpec((tm, LANES), lambda i: (i % per, 0))] * 3
    return pl.pallas_call(
        functools.partial(_gqa_qkv_kernel, use_rope=rope is not None),
        grid=(n // tm,),
        in_specs=specs,
        out_specs=[pl.BlockSpec((tm, GQA_NQ), lambda i: (i, 0)), pl.BlockSpec((tm, GQA_NK), lambda i: (i, 0)),
                   pl.BlockSpec((tm, GQA_NK), lambda i: (i, 0))],
        out_shape=[jax.ShapeDtypeStruct((n, GQA_NQ), F32), jax.ShapeDtypeStruct((n, GQA_NK), F32),
                   jax.ShapeDtypeStruct((n, GQA_NK), F32)],
        compiler_params=_cparams("parallel"),
        name="gqa_qkv",
    )(*ins)


def _gqa_layer(h, seg, mod, norm_g, gw, ctx):
    w_qkv, gq, gk, sink, w_o = gw
    latent = ctx is not None
    rope = _rope_tables(seg.length, GQA_ROPE_HALF, 0, 4 * GQA_ROPE_HALF) if latent else None
    q, k, v = _gqa_qkv(h, seg, mod, norm_g, w_qkv, gq, gk, rope)
    b3 = lambda a: a.reshape(seg.nseq, seg.length, a.shape[-1])
    att = _attention(b3(q), b3(k), b3(v), ctx, sink, heads=GQA_KV_HEADS, groups=GQA_HEADS // GQA_KV_HEADS,
                     dq=GQA_HEAD_DIM, dv=GQA_HEAD_DIM, scale=GQA_HEAD_DIM ** -0.5, band=latent)
    h_new = _proj_residual(h, seg, mod, att.reshape(seg.n, -1), w_o)
    return h_new, k, v


MLA_QW = MLA_HEADS * MLA_PAD
MLA_VW = MLA_HEADS * MLA_V
MLA_CKR = MLA_KV_LORA + LANES
MLA_ROPE_HALF = MLA_ROPE // 4


def _mla_expand(ckv, kr, wuk_ref, wuv_ref, kg_ref, ones_ref):
    cb = ckv.astype(BF16)
    kn = _dot(cb, wuk_ref[...])
    kr_t = jnp.concatenate([pltpu.roll(kr, MLA_NOPE, 1)] * MLA_HEADS, axis=-1)
    k = kn + kr_t
    k = k * lax.rsqrt(_group_sum(k * k, ones_ref[...]) * (1.0 / MLA_QK) + NORM_EPS) * kg_ref[...]
    return k, _dot(cb, wuv_ref[...])


def _mla_proj_kernel(*refs, use_rope):
    (h_ref, mod_ref, g_ref, wdq_ref, qlg_ref, wuq_ref, qg_ref, wdkv_ref, kvg_ref, wuk_ref, wuv_ref, kg_ref,
     ones_ref) = refs[:13]
    if use_rope:
        cos_ref, sa_ref, sb_ref = refs[13:16]
    q_ref, k_ref, v_ref, ckv_ref, kr_ref = refs[-5:]
    mod = mod_ref[0]
    u = _modulate(h_ref[...], g_ref[...], mod[0:1], mod[1:2]).astype(BF16)
    qd = _dot(u, wdq_ref[...])
    qd = qd * lax.rsqrt(jnp.mean(qd * qd, axis=-1, keepdims=True) + NORM_EPS) * qlg_ref[...]
    q = _dot(qd.astype(BF16), wuq_ref[...])
    q = q * lax.rsqrt(_group_sum(q * q, ones_ref[...]) * (1.0 / MLA_QK) + NORM_EPS) * qg_ref[...]
    ckr = _dot(u, wdkv_ref[...])
    ckv = ckr[:, :MLA_KV_LORA]
    ckv = ckv * lax.rsqrt(jnp.mean(ckv * ckv, axis=-1, keepdims=True) + NORM_EPS) * kvg_ref[...]
    kr = ckr[:, MLA_KV_LORA:]
    k, v = _mla_expand(ckv, kr, wuk_ref, wuv_ref, kg_ref, ones_ref)
    if use_rope:
        cos, sa, sb = cos_ref[...], sa_ref[...], sb_ref[...]
        q = _rope(q, cos, sa, sb, MLA_ROPE_HALF)
        k = _rope(k, cos, sa, sb, MLA_ROPE_HALF)
    q_ref[...] = q.astype(BF16)
    k_ref[...] = k.astype(BF16)
    v_ref[...] = v.astype(BF16)
    ckv_ref[...] = ckv
    kr_ref[...] = kr


def _mla_proj(h, seg, mod, norm_g, pw, rope, tm=256):
    n, dm = h.shape
    tm = min(tm, seg.length)
    per = seg.length // tm
    rid = seg.mod_row(tm)
    ones = _ones_blockdiag(MLA_PAD)
    full = lambda a: pl.BlockSpec(a.shape, lambda i, nd=a.ndim: (0,) * nd)
    ins = [h, mod, norm_g, *pw, ones]
    specs = [pl.BlockSpec((tm, dm), lambda i: (i, 0)), pl.BlockSpec((1, 6, dm), lambda i: (rid(i), 0, 0)),
             full(norm_g)] + [full(a) for a in pw] + [full(ones)]
    if rope is not None:
        ins += list(rope)
        specs += [pl.BlockSpec((tm, LANES), lambda i: (i % per, 0))] * 3
    widths = (MLA_QW, MLA_QW, MLA_VW, MLA_KV_LORA, LANES)
    return pl.pallas_call(
        functools.partial(_mla_proj_kernel, use_rope=rope is not None),
        grid=(n // tm,),
        in_specs=specs,
        out_specs=[pl.BlockSpec((tm, w), lambda i: (i, 0)) for w in widths],
        out_shape=[jax.ShapeDtypeStruct((n, w), BF16 if j < 3 else F32) for j, w in enumerate(widths)],
        compiler_params=_cparams("parallel"),
        name="mla_proj",
    )(*ins)


def _mla_ctx_kernel(ckv_ref, kr_ref, wuk_ref, wuv_ref, kg_ref, ones_ref, k_ref, v_ref):
    k, v = _mla_expand(ckv_ref[...], kr_ref[...], wuk_ref, wuv_ref, kg_ref, ones_ref)
    k_ref[...] = k.astype(BF16)
    v_ref[...] = v.astype(BF16)


def _mla_ctx_expand(ckv, kr, wuk, wuv, kg, tm=256):
    n = ckv.shape[0]
    tm = min(tm, n)
    ones = _ones_blockdiag(MLA_PAD)
    full = lambda a: pl.BlockSpec(a.shape, lambda i, nd=a.ndim: (0,) * nd)
    return pl.pallas_call(
        _mla_ctx_kernel,
        grid=(n // tm,),
        in_specs=[pl.BlockSpec((tm, MLA_KV_LORA), lambda i: (i, 0)), pl.BlockSpec((tm, LANES), lambda i: (i, 0)),
                  full(wuk), full(wuv), full(kg), full(ones)],
        out_specs=[pl.BlockSpec((tm, MLA_QW), lambda i: (i, 0)), pl.BlockSpec((tm, MLA_VW), lambda i: (i, 0))],
        out_shape=[jax.ShapeDtypeStruct((n, MLA_QW), BF16), jax.ShapeDtypeStruct((n, MLA_VW), BF16)],
        compiler_params=_cparams("parallel"),
        name="mla_ctx_expand",
    )(ckv, kr, wuk, wuv, kg, ones)


def _mla_weights(w_dq, q_lora_g, w_uq, w_dkv, kv_g, w_ukv, q_g, k_g, w_o):
    padh = lambda a: jnp.pad(a, [(0, 0)] * (a.ndim - 1) + [(0, MLA_PAD - a.shape[-1])])
    wuq = padh(w_uq.reshape(MLA_Q_LORA, MLA_HEADS, MLA_QK)).reshape(MLA_Q_LORA, MLA_QW).astype(BF16)
    wdkv = jnp.pad(w_dkv, ((0, 0), (0, MLA_CKR - w_dkv.shape[1]))).astype(BF16)
    ukv = w_ukv.reshape(MLA_KV_LORA, MLA_HEADS, MLA_NOPE + MLA_V)
    wuk = padh(ukv[:, :, :MLA_NOPE]).reshape(MLA_KV_LORA, MLA_QW).astype(BF16)
    wuv = ukv[:, :, MLA_NOPE:].reshape(MLA_KV_LORA, MLA_VW).astype(BF16)
    tile_h = lambda g: jnp.tile(padh(g), MLA_HEADS)[None]
    proj = (w_dq.astype(BF16), q_lora_g[None], wuq, tile_h(q_g), wdkv, kv_g[None], wuk, wuv, tile_h(k_g))
    return proj, w_o.astype(BF16)


def _mla_layer(h, seg, mod, norm_g, mw, cache):
    proj, w_o = mw
    latent = cache is not None
    rope = _rope_tables(seg.length, MLA_ROPE_HALF, MLA_NOPE, MLA_PAD) if latent else None
    q, k, v, ckv, kr = _mla_proj(h, seg, mod, norm_g, proj, rope)
    ctx = None
    if latent:
        c_ckv, c_kr = cache
        lc = c_ckv.shape[1]
        kr_p = jnp.pad(c_kr.reshape(-1, MLA_ROPE), ((0, 0), (0, LANES - MLA_ROPE)))
        kc, vc = _mla_ctx_expand(c_ckv.reshape(-1, MLA_KV_LORA), kr_p, proj[6], proj[7], proj[8])
        ctx = (kc.reshape(seg.nseq, lc, MLA_QW), vc.reshape(seg.nseq, lc, MLA_VW))
    b3 = lambda a: a.reshape(seg.nseq, seg.length, a.shape[-1])
    att = _attention(b3(q), b3(k), b3(v), ctx, None, heads=MLA_HEADS, groups=1, dq=MLA_PAD, dv=MLA_V,
                     scale=MLA_QK ** -0.5, band=False)
    h_new = _proj_residual(h, seg, mod, att.reshape(seg.n, -1), w_o)
    return h_new, ckv, kr[:, :MLA_ROPE]


def kernel(x_prompt, x_sample, state_s5_re, state_s5_im, state_rwkv, cache_gqa_k, cache_gqa_v, cache_mla_ckv, cache_mla_krope, c, c_ctx, ada_w, ada_b, norm1_g, norm2_g, s5_lambda_re, s5_lambda_im, s5_log_step, s5_b_re, s5_b_im, s5_c_re, s5_c_im, s5_d, s5_glu_w1, s5_glu_w2, rwkv_mu, rwkv_w_r, rwkv_w_k, rwkv_w_v, rwkv_w_o, rwkv_w0, rwkv_w_l1, rwkv_w_l2, rwkv_a0, rwkv_a_l1, rwkv_a_l2, rwkv_g_l1, rwkv_g_l2, rwkv_k_k, rwkv_k_a, rwkv_r_k, rwkv_ln_g, rwkv_ln_b, gqa_w_qkv, gqa_q_norm, gqa_k_norm, gqa_sink, gqa_w_o, mla_w_dq, mla_q_lora_norm, mla_w_uq, mla_w_dkv, mla_kv_norm, mla_w_ukv, mla_q_norm, mla_k_norm, mla_w_o, moe_w_group, moe_b_group, moe_w_expert, moe_b_expert, moe_w_gate, moe_w_up, moe_w_down):
    bp, lp, dm = x_prompt.shape
    bs, ls, _ = x_sample.shape
    depth = ada_w.shape[0]
    assert dm == D_MODEL and 1 + bs <= MOD_ROWS
    seg_p = _Seg(bp, lp, 0, True)
    seg_s = _Seg(bs, ls, 1, False)
    bf = lambda a: a.astype(BF16)

    cond = jnp.zeros((MOD_ROWS, dm), F32).at[0].set(c_ctx).at[1:1 + bs].set(c)
    mods = _ada_all(cond, ada_w, ada_b).reshape(depth, MOD_ROWS, 6, dm)

    hp = x_prompt.reshape(bp * lp, dm)
    hs = x_sample.reshape(bs * ls, dm)
    outs = {}
    for layer in range(depth):
        kind = layer % 4
        mod = mods[layer]
        g1 = norm1_g[layer][None]
        if kind == 0:
            sw = _s5_weights(s5_lambda_re, s5_lambda_im, s5_log_step, s5_b_re, s5_b_im, s5_c_re, s5_c_im)
            w1, w2, dsk = bf(s5_glu_w1), bf(s5_glu_w2), s5_d[None]
            zeros = jnp.zeros((bp, 2, S5_N), F32)
            hp, f_re, f_im = _s5_layer(hp, seg_p, mod, g1, zeros, zeros, sw, dsk, w1, w2, True)
            hs, _, _ = _s5_layer(hs, seg_s, mod, g1, state_s5_re.reshape(bs, 2, S5_N),
                                 state_s5_im.reshape(bs, 2, S5_N), sw, dsk, w1, w2, False)
            outs['s5_re'] = f_re.reshape(bp, 2, S5_GROUPS, S5_STATE)
            outs['s5_im'] = f_im.reshape(bp, 2, S5_GROUPS, S5_STATE)
        elif kind == 1:
            vec = jnp.zeros((SUBLANES, dm), F32).at[0].set(rwkv_k_k).at[1].set(rwkv_k_a).at[2].set(rwkv_r_k.reshape(-1))
            rw = dict(
                proj=(rwkv_mu, bf(rwkv_w_r), bf(rwkv_w_k), bf(rwkv_w_v), bf(rwkv_w_l1), bf(rwkv_w_l2),
                      bf(rwkv_a_l1), bf(rwkv_a_l2), bf(rwkv_g_l1), bf(rwkv_g_l2), vec,
                      rwkv_w0[:, None, :], rwkv_a0[:, None, :]),
                out=(rwkv_ln_g[None], rwkv_ln_b[None], bf(rwkv_w_o)))
            zeros = jnp.zeros((bp, 2, RWKV_HEADS, RWKV_HEAD, RWKV_HEAD), F32)
            hp, outs['rwkv'] = _rwkv_layer(hp, seg_p, mod, g1, zeros, rw)
            hs, _ = _rwkv_layer(hs, seg_s, mod, g1, state_rwkv, rw)
        elif kind == 2:
            gw = (bf(gqa_w_qkv), jnp.tile(gqa_q_norm, GQA_HEADS)[None], jnp.tile(gqa_k_norm, GQA_KV_HEADS)[None],
                  gqa_sink, bf(gqa_w_o))
            hp, kp, vp = _gqa_layer(hp, seg_p, mod, g1, gw, None)
            outs['gqa_k'] = kp.reshape(bp, lp, GQA_KV_HEADS, GQA_HEAD_DIM)
            outs['gqa_v'] = vp.reshape(bp, lp, GQA_KV_HEADS, GQA_HEAD_DIM)
            lc = cache_gqa_k.shape[1]
            ctx = (cache_gqa_k.reshape(bs, lc, GQA_NK), cache_gqa_v.reshape(bs, lc, GQA_NK))
            hs, _, _ = _gqa_layer(hs, seg_s, mod, g1, gw, ctx)
        else:
            mw = _mla_weights(mla_w_dq, mla_q_lora_norm, mla_w_uq, mla_w_dkv, mla_kv_norm, mla_w_ukv, mla_q_norm,
                              mla_k_norm, mla_w_o)
            hp, ckv_p, kr_p = _mla_layer(hp, seg_p, mod, g1, mw, None)
            outs['mla_ckv'] = ckv_p.reshape(bp, lp, MLA_KV_LORA)
            outs['mla_kr'] = kr_p.reshape(bp, lp, MLA_ROPE)
            hs, _, _ = _mla_layer(hs, seg_s, mod, g1, mw, (cache_mla_ckv, cache_mla_krope))
        wr = jnp.zeros((dm, ROUTE_W), F32).at[:, :MOE_GROUPS].set(moe_w_group[layer])
        wr = wr.at[:, MOE_GROUPS:MOE_GROUPS + MOE_EXPERTS].set(moe_w_expert[layer])
        br = jnp.zeros((1, ROUTE_W), F32).at[0, :MOE_GROUPS].set(moe_b_group[layer])
        br = br.at[0, MOE_GROUPS:MOE_GROUPS + MOE_EXPERTS].set(moe_b_expert[layer])
        wr_hi, wr_lo = _hi_lo(wr)
        mo = (mod, norm2_g[layer][None], wr_hi, wr_lo, br, bf(moe_w_gate[layer]), bf(moe_w_up[layer]),
              bf(moe_w_down[layer]))
        hp = _moe_layer(hp, seg_p, *mo)
        hs = _moe_layer(hs, seg_s, *mo)
    return (hp.reshape(bp, lp, dm), hs.reshape(bs, ls, dm), outs['s5_re'], outs['s5_im'], outs['rwkv'],
            outs['gqa_k'], outs['gqa_v'], outs['mla_ckv'], outs['mla_kr'])
```

```python
import functools
import math

import jax
import jax.numpy as jnp
from jax import lax
from jax.experimental import pallas as pl
from jax.experimental.pallas import tpu as pltpu

F32 = jnp.float32
BF16 = jnp.bfloat16

D_MODEL = 1024
NORM_EPS = 1e-6
ROPE_BASE = 10000.0
GRID_W = 64
S5_GROUP = 16
S5_GROUPS = D_MODEL // S5_GROUP
S5_STATE = 64
S5_N = S5_GROUPS * S5_STATE
RWKV_HEAD = 64
RWKV_HEADS = D_MODEL // RWKV_HEAD
RWKV_DECAY_SCALE = math.exp(-0.5)
RWKV_GN_EPS = 64e-5
GQA_HEADS = 16
GQA_KV_HEADS = 4
GQA_HEAD_DIM = 64
WINDOW = 128
MLA_HEADS = 16
MLA_Q_LORA = 384
MLA_KV_LORA = 256
MLA_NOPE = 64
MLA_ROPE = 32
MLA_QK = MLA_NOPE + MLA_ROPE
MLA_V = 64
MLA_PAD = 128
MOE_GROUPS = 4
MOE_PER_GROUP = 4
MOE_EXPERTS = 16
MOE_HIDDEN = 512

LANES = 128
SUBLANES = 8
MXU_DIM = 256
VMEM_LIMIT = 56 * 1024 * 1024
MOD_ROWS = 8


def _cparams(*sem):
    return pltpu.CompilerParams(dimension_semantics=sem, vmem_limit_bytes=VMEM_LIMIT)


def _dot(a, b):
    return jnp.dot(a, b, preferred_element_type=F32)


def _dot_nt(a, b):
    return lax.dot_general(a, b, (((1,), (1,)), ((), ())), preferred_element_type=F32)


def _split3(x):
    hi = x.astype(BF16)
    r1 = x - hi.astype(F32)
    mid = r1.astype(BF16)
    lo = (r1 - mid.astype(F32)).astype(BF16)
    return hi, mid, lo


def _dot_x3(x, w_hi, w_lo):
    hi, mid, _ = _split3(x)
    return _dot(hi, w_hi) + (_dot(mid, w_hi) + _dot(hi, w_lo))


def _group_sum(x, ones_bd):
    n = x.shape[-1]
    outs = []
    for c in range(n // MXU_DIM):
        hi, mid, lo = _split3(x[:, c * MXU_DIM:(c + 1) * MXU_DIM])
        outs.append(_dot(hi, ones_bd) + (_dot(mid, ones_bd) + _dot(lo, ones_bd)))
    return outs[0] if len(outs) == 1 else jnp.concatenate(outs, axis=-1)


def _modulate(x, g, shift, scale):
    ms = jnp.mean(x * x, axis=-1, keepdims=True)
    return x * lax.rsqrt(ms + NORM_EPS) * g * (1.0 + scale) + shift


def _sigmoid(x):
    return 1.0 / (1.0 + jnp.exp(-x))


def _silu(x):
    return x * _sigmoid(x)


def _ones_blockdiag(group):
    i = jnp.arange(MXU_DIM) // group
    return (i[:, None] == i[None, :]).astype(BF16)


def _hi_lo(w):
    hi = w.astype(BF16)
    return hi, (w - hi.astype(F32)).astype(BF16)


class _Seg:
    def __init__(self, nseq, length, mod0, shared_mod):
        self.nseq, self.length, self.mod0, self.shared_mod = nseq, length, mod0, shared_mod
        self.n = nseq * length

    def mod_row(self, tm):
        if self.shared_mod:
            return lambda i: self.mod0
        per = self.length // tm
        return lambda i: self.mod0 + i // per

    def mod_rows(self, mod, k):
        if self.shared_mod:
            return jnp.broadcast_to(mod[self.mod0, k][None], (self.nseq, mod.shape[-1]))
        return mod[self.mod0:self.mod0 + self.nseq, k]


def _ada_kernel(cond_ref, w_ref, b_ref, o_ref):
    s = _silu(cond_ref[...])
    w = w_ref[0]
    w_hi = w.astype(BF16)
    w_lo = (w - w_hi.astype(F32)).astype(BF16)
    o_ref[0] = _dot_x3(s, w_hi, w_lo) + b_ref[0]


def _ada_all(cond, ada_w, ada_b):
    depth, d, n6 = ada_w.shape
    tn = 1536
    return pl.pallas_call(
        _ada_kernel,
        grid=(depth, n6 // tn),
        in_specs=[
            pl.BlockSpec((MOD_ROWS, d), lambda l, j: (0, 0)),
            pl.BlockSpec((1, d, tn), lambda l, j: (l, 0, j)),
            pl.BlockSpec((1, 1, tn), lambda l, j: (l, 0, j)),
        ],
        out_specs=pl.BlockSpec((1, MOD_ROWS, tn), lambda l, j: (l, 0, j)),
        out_shape=jax.ShapeDtypeStruct((depth, MOD_ROWS, n6), F32),
        compiler_params=_cparams("parallel", "parallel"),
        name="ada_mod",
    )(cond, ada_w, ada_b.reshape(depth, 1, n6))


ROUTE_W = LANES


def _route(logits):
    lane = lax.broadcasted_iota(jnp.int32, logits.shape, 1)
    neg = jnp.float32(-jnp.inf)
    big = jnp.int32(ROUTE_W)
    gmask = lane < MOE_GROUPS
    gl = jnp.where(gmask, logits, neg)
    gmax = jnp.max(gl, axis=-1, keepdims=True)
    g_top = jnp.min(jnp.where(gl == gmax, lane, big), axis=-1, keepdims=True)
    p_g = 1.0 / jnp.sum(jnp.where(gmask, jnp.exp(gl - gmax), 0.0), axis=-1, keepdims=True)
    lo = MOE_GROUPS + MOE_PER_GROUP * g_top
    emask = (lane >= lo) & (lane < lo + MOE_PER_GROUP)
    el = jnp.where(emask, logits, neg)
    m1 = jnp.max(el, axis=-1, keepdims=True)
    i1 = jnp.min(jnp.where(el == m1, lane, big), axis=-1, keepdims=True)
    el2 = jnp.where(lane == i1, neg, el)
    m2 = jnp.max(el2, axis=-1, keepdims=True)
    i2 = jnp.min(jnp.where(el2 == m2, lane, big), axis=-1, keepdims=True)
    e2 = jnp.exp(m2 - m1)
    w1 = p_g / (1.0 + e2)
    w2 = p_g * e2 / (1.0 + e2)
    gates = jnp.where(lane == i1, w1, 0.0) + jnp.where(lane == i2, w2, 0.0)
    return gates, g_top


def _lane_col(x, idx):
    lane = lax.broadcasted_iota(jnp.int32, x.shape, 1)
    return jnp.sum(jnp.where(lane == idx, x, 0.0), axis=-1, keepdims=True)


def _moe_kernel(h_ref, mod_ref, g_ref, wr_hi_ref, wr_lo_ref, br_ref, tri_ref, wg_ref, wu_ref, wd_ref, o_ref,
                v_scr, gate_scr, pos_scr, post_scr, x_scr, gx_scr, y_scr, yall_hi, yall_lo, acc_scr, cnt_scr,
                *, rows):
    e = pl.program_id(1)
    grp = e // MOE_PER_GROUP
    tm = v_scr.shape[0]
    nchunk = x_scr.shape[0] // rows

    @pl.when(e == 0)
    def _():
        mod = mod_ref[0]
        v = _modulate(h_ref[...], g_ref[...], mod[3:4], mod[4:5])
        logits = _dot_x3(v, wr_hi_ref[...], wr_lo_ref[...]) + br_ref[...]
        gates, g_top = _route(logits)
        gate_scr[...] = gates
        v_scr[...] = v.astype(BF16)
        lane = lax.broadcasted_iota(jnp.int32, gates.shape, 1)
        member = lane == g_top
        incl = _dot(tri_ref[...], jnp.where(member, 1.0, 0.0).astype(BF16))
        pos = jnp.where(member, incl - 1.0, -1.0)
        own = jnp.sum(jnp.where(member, pos, 0.0), axis=-1, keepdims=True)
        first = jnp.where(own < rows, g_top.astype(F32) * rows + own, -1.0)
        pos_scr[...] = jnp.where(lane == MOE_GROUPS, first, pos)
        pos_t = pos.T
        for g in range(MOE_GROUPS):
            post_scr[g] = jnp.broadcast_to(pos_t[g:g + 1], post_scr.shape[1:])
            cnt_scr[g] = jnp.sum(jnp.where(member & (lane == g), 1.0, 0.0)).astype(jnp.int32)

    cnt = cnt_scr[grp]
    spills = functools.reduce(jnp.logical_or, [cnt_scr[g] > rows for g in range(MOE_GROUPS)])

    @pl.when(jnp.logical_and(e == 0, spills))
    def _():
        acc_scr[...] = jnp.zeros_like(acc_scr)

    @pl.when(e % MOE_PER_GROUP == 0)
    def _():
        g_hi, g_mid, g_lo = _split3(gate_scr[...])
        y_scr[0:rows] = jnp.zeros((rows, y_scr.shape[1]), F32)
        for c in range(nchunk):
            @pl.when(cnt > c * rows)
            def _(c=c):
                slot = c * rows + lax.broadcasted_iota(jnp.int32, (rows, tm), 0)
                onehot = jnp.where(post_scr[grp, 0:1, :].astype(jnp.int32) == slot, 1.0, 0.0).astype(BF16)
                sl = slice(c * rows, (c + 1) * rows)
                x_scr[sl] = _dot(onehot, v_scr[...]).astype(BF16)
                gx_scr[sl] = _dot(onehot, g_hi) + (_dot(onehot, g_mid) + _dot(onehot, g_lo))
                if c > 0:
                    y_scr[sl] = jnp.zeros((rows, y_scr.shape[1]), F32)

    for c in range(nchunk):
        @pl.when(cnt > c * rows)
        def _(c=c):
            sl = slice(c * rows, (c + 1) * rows)
            x = x_scr[sl]
            gate = _lane_col(gx_scr[sl], e + MOE_GROUPS)
            hh = (_silu(_dot(x, wg_ref[0])) * _dot(x, wu_ref[0]) * gate).astype(BF16)
            y_scr[sl] += _dot(hh, wd_ref[0])

    @pl.when(e % MOE_PER_GROUP == MOE_PER_GROUP - 1)
    def _():
        y_hi, y_lo = _hi_lo(y_scr[0:rows])
        first_rows = pl.ds(pl.multiple_of(grp * rows, 16), rows)
        yall_hi[first_rows] = y_hi
        yall_lo[first_rows] = y_lo
        pos = _lane_col(pos_scr[...], grp).astype(jnp.int32)
        for c in range(1, nchunk):
            @pl.when(cnt > c * rows)
            def _(c=c):
                slot = c * rows + lax.broadcasted_iota(jnp.int32, (tm, rows), 1)
                onehot_t = jnp.where(pos == slot, 1.0, 0.0).astype(BF16)
                y_hi, y_lo = _hi_lo(y_scr[c * rows:(c + 1) * rows])
                acc_scr[...] += _dot(onehot_t, y_hi) + _dot(onehot_t, y_lo)

    @pl.when(e == pl.num_programs(1) - 1)
    def _():
        first = _lane_col(pos_scr[...], MOE_GROUPS).astype(jnp.int32)
        slot = lax.broadcasted_iota(jnp.int32, (tm, yall_hi.shape[0]), 1)
        onehot_t = jnp.where(first == slot, 1.0, 0.0).astype(BF16)
        moe = _dot(onehot_t, yall_hi[...]) + _dot(onehot_t, yall_lo[...])
        o_ref[...] = h_ref[...] + mod_ref[0][5:6] * moe

    @pl.when(jnp.logical_and(e == pl.num_programs(1) - 1, spills))
    def _():
        o_ref[...] += mod_ref[0][5:6] * acc_scr[...]


MOE_TM = 1024
MOE_ROWS = 320


def _moe_layer(h, seg, mod, norm_g, wr_hi, wr_lo, br, wg, wu, wd, tm=MOE_TM, rows=MOE_ROWS):
    n, d = h.shape
    nexp, _, f = wg.shape
    tm = min(tm, n if seg.shared_mod else seg.length)
    rows = min(rows, tm)
    nchunk = -(-tm // rows)
    rid = seg.mod_row(tm)
    tri = jnp.tri(tm, dtype=BF16)
    return pl.pallas_call(
        functools.partial(_moe_kernel, rows=rows),
        grid=(n // tm, nexp),
        in_specs=[
            pl.BlockSpec((tm, d), lambda i, e: (i, 0)),
            pl.BlockSpec((1, 6, d), lambda i, e: (rid(i), 0, 0)),
            pl.BlockSpec((1, d), lambda i, e: (0, 0)),
            pl.BlockSpec((d, ROUTE_W), lambda i, e: (0, 0)),
            pl.BlockSpec((d, ROUTE_W), lambda i, e: (0, 0)),
            pl.BlockSpec((1, ROUTE_W), lambda i, e: (0, 0)),
            pl.BlockSpec((tm, tm), lambda i, e: (0, 0)),
            pl.BlockSpec((1, d, f), lambda i, e: (e, 0, 0)),
            pl.BlockSpec((1, d, f), lambda i, e: (e, 0, 0)),
            pl.BlockSpec((1, f, d), lambda i, e: (e, 0, 0)),
        ],
        out_specs=pl.BlockSpec((tm, d), lambda i, e: (i, 0)),
        out_shape=jax.ShapeDtypeStruct((n, d), F32),
        scratch_shapes=[
            pltpu.VMEM((tm, d), BF16),
            pltpu.VMEM((tm, ROUTE_W), F32),
            pltpu.VMEM((tm, ROUTE_W), F32),
            pltpu.VMEM((MOE_GROUPS, SUBLANES, tm), F32),
            pltpu.VMEM((nchunk * rows, d), BF16),
            pltpu.VMEM((nchunk * rows, ROUTE_W), F32),
            pltpu.VMEM((nchunk * rows, d), F32),
            pltpu.VMEM((MOE_GROUPS * rows, d), BF16),
            pltpu.VMEM((MOE_GROUPS * rows, d), BF16),
            pltpu.VMEM((tm, d), F32),
            pltpu.SMEM((MOE_GROUPS,), jnp.int32),
        ],
        compiler_params=_cparams("parallel", "arbitrary"),
        name="moe",
    )(h, mod, norm_g, wr_hi, wr_lo, br, tri, wg, wu, wd)


S5_TL = 32
S5_CG = MXU_DIM // S5_GROUP
S5_NCH = D_MODEL // MXU_DIM
S5_CN = S5_CG * S5_STATE
S5_SCAN_W = 512


def _s5_scan_kernel(h_ref, sh_ref, sc_ref, g_ref, s0r_ref, s0i_ref, ar_ref, ai_ref, wb_ref, wcr_ref, wci_ref,
                    y_ref, fr_ref, fi_ref, bur, bui, str_, sti):
    d = pl.program_id(1)
    c = pl.program_id(2)
    tl, nb, dm = h_ref.shape

    @pl.when(c == 0)
    def _():
        str_[...] = s0r_ref[0]
        sti[...] = s0i_ref[0]

    u = _modulate(h_ref[...], g_ref[...], sh_ref[...], sc_ref[...])
    u2 = u.reshape(tl * nb, dm).astype(BF16)
    for cc in range(S5_NCH):
        r = _dot(u2[:, cc * MXU_DIM:(cc + 1) * MXU_DIM], wb_ref[0, cc])
        bur[:, cc * S5_CN:(cc + 1) * S5_CN] = r[:, :S5_CN]
        bui[:, cc * S5_CN:(cc + 1) * S5_CN] = r[:, S5_CN:]

    for lc in range(S5_N // S5_SCAN_W):
        cols = slice(lc * S5_SCAN_W, (lc + 1) * S5_SCAN_W)
        a_r = jnp.broadcast_to(ar_ref[0, :, cols], (nb, S5_SCAN_W))
        a_i = jnp.broadcast_to(ai_ref[0, :, cols], (nb, S5_SCAN_W))

        def body(i, carry, cols=cols, a_r=a_r, a_i=a_i):
            s_r, s_i = carry
            tt = jnp.where(d == 0, i, tl - 1 - i)
            rows = pl.ds(pl.multiple_of(tt * nb, nb), nb)
            n_r = a_r * s_r - a_i * s_i + bur[rows, cols]
            n_i = a_r * s_i + a_i * s_r + bui[rows, cols]
            bur[rows, cols] = n_r
            bui[rows, cols] = n_i
            return n_r, n_i

        s_r, s_i = lax.fori_loop(0, tl, body, (str_[:, cols], sti[:, cols]))
        str_[:, cols] = s_r
        sti[:, cols] = s_i

    for cc in range(S5_NCH):
        cols = slice(cc * S5_CN, (cc + 1) * S5_CN)
        y = _dot(bur[:, cols].astype(BF16), wcr_ref[0, cc]) + _dot(bui[:, cols].astype(BF16), wci_ref[0, cc])
        y_ref[0, :, :, cc * MXU_DIM:(cc + 1) * MXU_DIM] = y.reshape(tl, nb, MXU_DIM)

    @pl.when(c == pl.num_programs(2) - 1)
    def _():
        fr_ref[0] = str_[...]
        fi_ref[0] = sti[...]


def _s5_scan(h_t, shift, scale, norm_g, s0r, s0i, a_r, a_i, wb, wcr, wci):
    length, nbt, dm = h_t.shape
    nb = SUBLANES
    tl = S5_TL
    nch = length // tl

    def tch(d, c):
        return jnp.where(d == 0, c, nch - 1 - c)

    st_spec = pl.BlockSpec((1, nb, S5_N), lambda b, d, c: (d, b, 0))
    a_spec = pl.BlockSpec((1, 1, S5_N), lambda b, d, c: (d, 0, 0))
    return pl.pallas_call(
        _s5_scan_kernel,
        grid=(nbt // nb, 2, nch),
        in_specs=[
            pl.BlockSpec((tl, nb, dm), lambda b, d, c: (tch(d, c), b, 0)),
            pl.BlockSpec((nb, dm), lambda b, d, c: (b, 0)),
            pl.BlockSpec((nb, dm), lambda b, d, c: (b, 0)),
            pl.BlockSpec((1, dm), lambda b, d, c: (0, 0)),
            st_spec, st_spec, a_spec, a_spec,
            pl.BlockSpec((1, S5_NCH, MXU_DIM, 2 * S5_CN), lambda b, d, c: (d, 0, 0, 0)),
            pl.BlockSpec((1, S5_NCH, S5_CN, MXU_DIM), lambda b, d, c: (d, 0, 0, 0)),
            pl.BlockSpec((1, S5_NCH, S5_CN, MXU_DIM), lambda b, d, c: (d, 0, 0, 0)),
        ],
        out_specs=[
            pl.BlockSpec((1, tl, nb, dm), lambda b, d, c: (d, tch(d, c), b, 0)),
            st_spec, st_spec,
        ],
        out_shape=[
            jax.ShapeDtypeStruct((2, length, nbt, dm), F32),
            jax.ShapeDtypeStruct((2, nbt, S5_N), F32),
            jax.ShapeDtypeStruct((2, nbt, S5_N), F32),
        ],
        scratch_shapes=[
            pltpu.VMEM((tl * nb, S5_N), F32), pltpu.VMEM((tl * nb, S5_N), F32),
            pltpu.VMEM((nb, S5_N), F32), pltpu.VMEM((nb, S5_N), F32),
        ],
        compiler_params=_cparams("parallel", "arbitrary", "arbitrary"),
        name="s5_scan",
    )(h_t, shift, scale, norm_g, s0r, s0i, a_r, a_i, wb, wcr, wci)


def _s5_corr_kernel(fr_ref, fi_ref, pr_ref, pi_ref, ar_ref, ai_ref, wcr_ref, wci_ref, y_in_ref, y_ref,
                    zr, zi, str_, sti, *, nseq):
    d = pl.program_id(0)
    c = pl.program_id(1)
    tl, nb, _ = y_ref.shape[1:]

    @pl.when(c == 0)
    def _():
        f_r, f_i = fr_ref[0], fi_ref[0]
        p_r, p_i = pr_ref[0], pi_ref[0]
        row = lax.broadcasted_iota(jnp.int32, f_r.shape, 0)
        e_r = jnp.zeros_like(f_r)
        e_i = jnp.zeros_like(f_i)
        for _ in range(nb // nseq - 1):
            t_r = f_r + (p_r * e_r - p_i * e_i)
            t_i = f_i + (p_r * e_i + p_i * e_r)
            fwd_r = jnp.where(row < nseq, 0.0, pltpu.roll(t_r, nseq, 0))
            fwd_i = jnp.where(row < nseq, 0.0, pltpu.roll(t_i, nseq, 0))
            bwd_r = jnp.where(row >= nb - nseq, 0.0, pltpu.roll(t_r, nb - nseq, 0))
            bwd_i = jnp.where(row >= nb - nseq, 0.0, pltpu.roll(t_i, nb - nseq, 0))
            e_r = jnp.where(d == 0, fwd_r, bwd_r)
            e_i = jnp.where(d == 0, fwd_i, bwd_i)
        str_[...] = e_r
        sti[...] = e_i

    for lc in range(S5_N // S5_SCAN_W):
        cols = slice(lc * S5_SCAN_W, (lc + 1) * S5_SCAN_W)
        a_r = jnp.broadcast_to(ar_ref[0, :, cols], (nb, S5_SCAN_W))
        a_i = jnp.broadcast_to(ai_ref[0, :, cols], (nb, S5_SCAN_W))

        def body(i, carry, cols=cols, a_r=a_r, a_i=a_i):
            s_r, s_i = carry
            tt = jnp.where(d == 0, i, tl - 1 - i)
            rows = pl.ds(pl.multiple_of(tt * nb, nb), nb)
            n_r = a_r * s_r - a_i * s_i
            n_i = a_r * s_i + a_i * s_r
            zr[rows, cols] = n_r
            zi[rows, cols] = n_i
            return n_r, n_i

        s_r, s_i = lax.fori_loop(0, tl, body, (str_[:, cols], sti[:, cols]))
        str_[:, cols] = s_r
        sti[:, cols] = s_i

    for cc in range(S5_NCH):
        cols = slice(cc * S5_CN, (cc + 1) * S5_CN)
        y = _dot(zr[:, cols].astype(BF16), wcr_ref[0, cc]) + _dot(zi[:, cols].astype(BF16), wci_ref[0, cc])
        lanes = slice(cc * MXU_DIM, (cc + 1) * MXU_DIM)
        y_ref[0, :, :, lanes] = y_in_ref[0, :, :, lanes] + y.reshape(tl, nb, MXU_DIM)


def _s5_corr(y, f_r, f_i, p_r, p_i, a_r, a_i, wcr, wci, nseq):
    _, length, nb, dm = y.shape
    tl = S5_TL
    nch = length // tl
    tch = lambda d, c: jnp.where(d == 0, c, nch - 1 - c)
    st_spec = pl.BlockSpec((1, nb, S5_N), lambda d, c: (d, 0, 0))
    a_spec = pl.BlockSpec((1, 1, S5_N), lambda d, c: (d, 0, 0))
    w_spec = pl.BlockSpec((1, S5_NCH, S5_CN, MXU_DIM), lambda d, c: (d, 0, 0, 0))
    y_spec = pl.BlockSpec((1, tl, nb, dm), lambda d, c: (d, tch(d, c), 0, 0))
    return pl.pallas_call(
        functools.partial(_s5_corr_kernel, nseq=nseq),
        grid=(2, nch),
        in_specs=[st_spec, st_spec, a_spec, a_spec, a_spec, a_spec, w_spec, w_spec, y_spec],
        out_specs=y_spec,
        out_shape=jax.ShapeDtypeStruct(y.shape, F32),
        scratch_shapes=[pltpu.VMEM((tl * nb, S5_N), F32), pltpu.VMEM((tl * nb, S5_N), F32),
                        pltpu.VMEM((nb, S5_N), F32), pltpu.VMEM((nb, S5_N), F32)],
        input_output_aliases={8: 0},
        compiler_params=_cparams("parallel", "arbitrary"),
        name="s5_corr",
    )(f_r, f_i, p_r, p_i, a_r, a_i, wcr, wci, y)


def _gelu_tanh(x):
    return 0.5 * x * (1.0 + jnp.tanh(0.7978845608028654 * (x + 0.044715 * (x * x * x))))


def _s5_glu_kernel(h_ref, mod_ref, g_ref, dsk_ref, yf_ref, yb_ref, w1_ref, w2_ref, o_ref):
    mod = mod_ref[0]
    h = h_ref[...]
    u = _modulate(h, g_ref[...], mod[0:1], mod[1:2])
    y = u * dsk_ref[...] + (yf_ref[0] + yb_ref[0])
    z = _gelu_tanh(y).astype(BF16)
    out = _dot(z, w1_ref[...]) * _sigmoid(_dot(z, w2_ref[...]))
    o_ref[...] = h + mod[2:3] * out


def _s5_glu(h, seg, mod, norm_g, d_skip, y_t, w1, w2, tm=256):
    n, dm = h.shape
    _, seg_len, nbt, _ = y_t.shape
    y2 = y_t.reshape(2, seg_len, nbt * dm)
    tm = min(tm, seg_len)
    per_seq = seg.length // tm
    per_seg = seg_len // tm
    rid = seg.mod_row(tm)
    yidx = lambda d: (lambda i: (d, (i % per_seq) % per_seg, ((i % per_seq) // per_seg) * seg.nseq + i // per_seq))
    return pl.pallas_call(
        _s5_glu_kernel,
        grid=(n // tm,),
        in_specs=[
            pl.BlockSpec((tm, dm), lambda i: (i, 0)),
            pl.BlockSpec((1, 6, dm), lambda i: (rid(i), 0, 0)),
            pl.BlockSpec((1, dm), lambda i: (0, 0)),
            pl.BlockSpec((1, dm), lambda i: (0, 0)),
            pl.BlockSpec((1, tm, dm), yidx(0)),
            pl.BlockSpec((1, tm, dm), yidx(1)),
            pl.BlockSpec((dm, dm), lambda i: (0, 0)),
            pl.BlockSpec((dm, dm), lambda i: (0, 0)),
        ],
        out_specs=pl.BlockSpec((tm, dm), lambda i: (i, 0)),
        out_shape=jax.ShapeDtypeStruct((n, dm), F32),
        compiler_params=_cparams("parallel"),
        name="s5_glu",
    )(h, mod, norm_g, d_skip, y2, y2, w1, w2)


def _s5_weights(lam_re, lam_im, log_step, b_re, b_im, c_re, c_im):
    dt = jnp.exp(log_step)
    mag = jnp.exp(lam_re * dt)
    ab_re = mag * jnp.cos(lam_im * dt)
    ab_im = mag * jnp.sin(lam_im * dt)
    nr = ab_re - 1.0
    den = lam_re * lam_re + lam_im * lam_im
    f_re = (nr * lam_re + ab_im * lam_im) / den
    f_im = (ab_im * lam_re - nr * lam_im) / den
    fb_re = f_re[..., None] * b_re - f_im[..., None] * b_im
    fb_im = f_re[..., None] * b_im + f_im[..., None] * b_re
    eye = jnp.eye(S5_CG, dtype=F32)

    def bd_in(w):
        w = w.reshape(2, S5_NCH, S5_CG, S5_STATE, S5_GROUP)
        m = jnp.einsum('dnjpc,jk->dnjckp', w, eye)
        return m.reshape(2, S5_NCH, MXU_DIM, S5_CN)

    def bd_out(w):
        w = w.reshape(2, S5_NCH, S5_CG, S5_GROUP, S5_STATE)
        m = jnp.einsum('dnjcp,jk->dnjpkc', w, eye)
        return m.reshape(2, S5_NCH, S5_CN, MXU_DIM)

    wb = jnp.concatenate([bd_in(fb_re), bd_in(fb_im)], axis=-1).astype(BF16)
    return (ab_re.reshape(2, 1, S5_N), ab_im.reshape(2, 1, S5_N), wb,
            bd_out(c_re).astype(BF16), bd_out(-c_im).astype(BF16),
            (lam_re * dt).reshape(2, 1, S5_N), (lam_im * dt).reshape(2, 1, S5_N))


def _to_time_major(x, seg, nbt):
    xt = jnp.transpose(x.reshape(seg.nseq, seg.length, -1), (1, 0, 2))
    if nbt > seg.nseq:
        xt = jnp.pad(xt, ((0, 0), (0, nbt - seg.nseq), (0, 0)))
    return xt


def _pad_rows(x, nbt):
    return x if x.shape[0] == nbt else jnp.pad(x, ((0, nbt - x.shape[0]),) + ((0, 0),) * (x.ndim - 1))


def _s5_layer(h, seg, mod, norm_g, s0_re, s0_im, sw, d_skip, w1, w2, need_final):
    a_r, a_i, wb, wcr, wci, la_r, la_i = sw
    nseg = SUBLANES // seg.nseq if SUBLANES % seg.nseq == 0 else 1
    if not need_final and nseg > 1 and seg.length % (nseg * S5_TL) == 0:
        nseq, seg_len, dm = seg.nseq, seg.length // nseg, h.shape[1]
        h_t = jnp.transpose(h.reshape(nseq, nseg, seg_len, dm), (2, 1, 0, 3)).reshape(seg_len, SUBLANES, dm)
        shift = jnp.tile(seg.mod_rows(mod, 0), (nseg, 1))
        scale = jnp.tile(seg.mod_rows(mod, 1), (nseg, 1))

        def first_rows(s0):
            z = jnp.zeros((nseg - 1, nseq, S5_N), F32)
            fwd = jnp.concatenate([s0[None, :, 0], z], axis=0).reshape(SUBLANES, S5_N)
            bwd = jnp.concatenate([z, s0[None, :, 1]], axis=0).reshape(SUBLANES, S5_N)
            return jnp.stack([fwd, bwd])

        y_t, f_r, f_i = _s5_scan(h_t, shift, scale, norm_g, first_rows(s0_re), first_rows(s0_im), a_r, a_i,
                                 wb, wcr, wci)
        mag = jnp.exp(la_r * seg_len)
        y_t = _s5_corr(y_t, f_r, f_i, mag * jnp.cos(la_i * seg_len), mag * jnp.sin(la_i * seg_len), a_r, a_i,
                       wcr, wci, nseq)
        h_new = _s5_glu(h, seg, mod, norm_g, d_skip, y_t, w1, w2)
        return h_new, None, None
    nbt = -(-seg.nseq // SUBLANES) * SUBLANES
    h_t = _to_time_major(h, seg, nbt)
    shift = _pad_rows(seg.mod_rows(mod, 0), nbt)
    scale = _pad_rows(seg.mod_rows(mod, 1), nbt)
    pad = ((0, 0), (0, nbt - seg.nseq), (0, 0))
    s0r = jnp.pad(jnp.transpose(s0_re, (1, 0, 2)), pad)
    s0i = jnp.pad(jnp.transpose(s0_im, (1, 0, 2)), pad)
    y_t, f_r, f_i = _s5_scan(h_t, shift, scale, norm_g, s0r, s0i, a_r, a_i, wb, wcr, wci)
    h_new = _s5_glu(h, seg, mod, norm_g, d_skip, y_t, w1, w2)
    return h_new, jnp.transpose(f_r[:, :seg.nseq], (1, 0, 2)), jnp.transpose(f_i[:, :seg.nseq], (1, 0, 2))


def _rwkv_proj_kernel(h_ref, hp_ref, hn_ref, mod_ref, g_ref, mu_ref, wr_ref, wk_ref, wv_ref,
                      wl1_ref, wl2_ref, al1_ref, al2_ref, gl1_ref, gl2_ref, vec_ref, w0_ref, a0_ref, ones_ref,
                      r_ref, v_ref, kk_ref, w_ref, ka_ref, kd_ref, g_out_ref, bonus_ref, *, per):
    i = pl.program_id(0)
    mod = mod_ref[0]
    gn = g_ref[...]
    tm = h_ref.shape[0]
    u = _modulate(h_ref[...], gn, mod[0:1], mod[1:2])
    first = (i % per) == 0
    last = (i % per) == per - 1
    up = _modulate(hp_ref[SUBLANES - 1:SUBLANES, :], gn, mod[0:1], mod[1:2]) * jnp.where(first, 0.0, 1.0)
    un = _modulate(hn_ref[0:1, :], gn, mod[0:1], mod[1:2]) * jnp.where(last, 0.0, 1.0)
    row = lax.broadcasted_iota(jnp.int32, u.shape, 0)
    prev = jnp.where(row == 0, up, pltpu.roll(u, 1, 0))
    nxt = jnp.where(row == tm - 1, un, pltpu.roll(u, tm - 1, 0))
    dx = 0.5 * (prev + nxt) - u
    mu = mu_ref[...]

    def mix(j):
        return (u + dx * mu[j:j + 1]).astype(BF16)

    vec = vec_ref[...]
    r = _dot(mix(0), wr_ref[...])
    k = _dot(mix(2), wk_ref[...])
    v = _dot(mix(3), wv_ref[...])
    ones = ones_ref[...]
    kk = k * vec[0:1]
    kk = kk * lax.rsqrt(_group_sum(kk * kk, ones) + 1e-12)
    g_out_ref[...] = _dot(_sigmoid(_dot(mix(5), gl1_ref[...])).astype(BF16), gl2_ref[...])
    xw = mix(1)
    xa = mix(4)
    kd_sum = None
    for d in range(2):
        wl = _dot(jnp.tanh(_dot(xw, wl1_ref[d])).astype(BF16), wl2_ref[d])
        w_ref[d] = jnp.exp(-RWKV_DECAY_SCALE * _sigmoid(w0_ref[d] + wl))
        al = _dot(_dot(xa, al1_ref[d]).astype(BF16), al2_ref[d])
        a = _sigmoid(a0_ref[d] + al)
        kd = k * (1.0 + (a - 1.0) * vec[1:2])
        kd_ref[d] = kd
        ka_ref[d] = kk * a
        kd_sum = kd if kd_sum is None else kd_sum + kd
    r_ref[...] = r
    v_ref[...] = v
    kk_ref[...] = kk
    bonus_ref[...] = _group_sum(r * kd_sum * vec[2:3], ones) * v


def _rwkv_proj(h, seg, mod, norm_g, mu, wr, wk, wv, wl1, wl2, al1, al2, gl1, gl2, vec, w0, a0, tm=256):
    n, dm = h.shape
    tm = min(tm, seg.length)
    per = seg.length // tm
    rid = seg.mod_row(tm)
    hb = tm // SUBLANES
    nblk = n // SUBLANES
    full = lambda a: pl.BlockSpec(a.shape, lambda i, nd=a.ndim: (0,) * nd)
    tile = pl.BlockSpec((tm, dm), lambda i: (i, 0))
    tile2 = pl.BlockSpec((2, tm, dm), lambda i: (0, i, 0))
    ones = _ones_blockdiag(RWKV_HEAD)
    one = jax.ShapeDtypeStruct((n, dm), F32)
    two = jax.ShapeDtypeStruct((2, n, dm), F32)
    return pl.pallas_call(
        functools.partial(_rwkv_proj_kernel, per=per),
        grid=(n // tm,),
        in_specs=[
            tile,
            pl.BlockSpec((SUBLANES, dm), lambda i: (jnp.maximum(i * hb - 1, 0), 0)),
            pl.BlockSpec((SUBLANES, dm), lambda i: (jnp.minimum((i + 1) * hb, nblk - 1), 0)),
            pl.BlockSpec((1, 6, dm), lambda i: (rid(i), 0, 0)),
            full(norm_g), full(mu), full(wr), full(wk), full(wv), full(wl1), full(wl2), full(al1), full(al2),
            full(gl1), full(gl2), full(vec), full(w0), full(a0), full(ones),
        ],
        out_specs=[tile, tile, tile, tile2, tile2, tile2, tile, tile],
        out_shape=[one, one, one, two, two, two, one, one],
        compiler_params=_cparams("parallel"),
        name="rwkv_proj",
    )(h, h, h, mod, norm_g, mu, wr, wk, wv, wl1, wl2, al1, al2, gl1, gl2, vec, w0, a0, ones)


RWKV_TC = 64
RWKV_SLOTS = 2
RWKV_AHEAD = 1


def _rwkv_chain_steps(tc, nv, load_pieces, store_pieces, st, mk, mv, ys):
    nk = st.shape[0]
    ns, ahead = RWKV_SLOTS, RWKV_AHEAD
    ys[...] = jnp.zeros_like(ys)
    for a in range(ahead):
        for piece in load_pieces(a, a):
            piece()

    def step(i, slot):
        side = (load_pieces(jnp.minimum(i + ahead, tc - 1), (slot + ahead) % ns)
                + store_pieces(jnp.maximum(i - 1, 0), (slot - 1) % ns))
        every = (2 * nk) // (len(side) + 1)
        at = {(p + 1) * every: piece for p, piece in enumerate(side)}
        row = lambda q, k: mk[slot, q, k:k + 1, :]
        vv = mv[slot]
        sa = None
        for k in range(nk):
            term = st[k] * row(0, k)
            sa = term if sa is None else sa + term
            if k + 1 in at:
                at[k + 1]()
        y = None
        for k in range(nk):
            s = st[k] * row(2, k) + (vv * row(3, k) - sa * row(1, k))
            st[k] = s
            term = s * row(4, k)
            y = term if y is None else y + term
            if nk + k + 1 in at:
                at[nk + k + 1]()
        ys[slot] = y

    def some_steps(j, carry):
        for u in range(ns):
            step(ns * j + u, u)
        return carry

    lax.fori_loop(0, tc // ns, some_steps, 0)
    for piece in store_pieces(tc - 1, (tc - 1) % ns):
        piece()


def _heads_to_rows(s_scr, x, lanes, rows_per_head):
    for h in range(RWKV_HEADS):
        s_scr[h * rows_per_head:(h + 1) * rows_per_head, lanes] = x[:, h * RWKV_HEAD:(h + 1) * RWKV_HEAD]


def _rwkv_scan_grid_kernel(kk_ref, r_ref, v_ref, ka_ref, w_ref, kd_ref, s0_ref, y_ref, sf_ref,
                           st, mk, mv, ys, s_scr):
    d = pl.program_id(0)
    c = pl.program_id(2)
    tc, nb, _ = kk_ref.shape
    nk = RWKV_HEAD
    lo, hi = slice(0, nk), slice(nk, 2 * nk)

    @pl.when(c == 0)
    def _():
        st[...] = s0_ref[0]

    def load_pieces(i, slot):
        t = jnp.where(d == 0, i, tc - 1 - i)
        pairs = ((kk_ref, None, ka_ref, 0), (w_ref, 0, kd_ref, 0), (r_ref, None, v_ref, None))

        def piece(p):
            ra, da, rb, db = pairs[p]
            xa = ra[t] if da is None else ra[da, t]
            xb = rb[t] if db is None else rb[db, t]
            scr = s_scr.at[3 * slot + p]
            _heads_to_rows(scr, xa, lo, nb)
            _heads_to_rows(scr, xb, hi, nb)
            tt = scr[...].T
            mk[slot, 2 * p] = tt[lo]
            if p < 2:
                mk[slot, 2 * p + 1] = tt[hi]
            else:
                mv[slot] = tt[hi]
        return [functools.partial(piece, p) for p in range(3)]

    def store_pieces(i, slot):
        t = jnp.where(d == 0, i, tc - 1 - i)

        def piece():
            yt = ys[slot].T
            for h in range(RWKV_HEADS):
                y_ref[0, t, :, h * nk:(h + 1) * nk] = yt[h * nb:(h + 1) * nb, :]
        return [piece]

    _rwkv_chain_steps(tc, nk, load_pieces, store_pieces, st, mk, mv, ys)

    @pl.when(c == pl.num_programs(2) - 1)
    def _():
        sf_ref[0] = st[...]


def _rwkv_scan_grid(kk, r, v, ka, w, kd, s0):
    n, nseq, dm = kk.shape
    nk = RWKV_HEAD
    nb = SUBLANES
    tc = RWKV_TC
    nch = n // tc
    tch = lambda d, t: jnp.where(d == 0, t, nch - 1 - t)
    shared = pl.BlockSpec((tc, nb, dm), lambda d, l, t: (tch(d, t), l, 0))
    perdir = pl.BlockSpec((1, tc, nb, dm), lambda d, l, t: (d, tch(d, t), l, 0))
    sspec = pl.BlockSpec((1, nk, nk, LANES), lambda d, l, t: (d, 0, 0, l))
    return pl.pallas_call(
        _rwkv_scan_grid_kernel,
        grid=(2, nseq // nb, nch),
        in_specs=[shared, shared, shared, perdir, perdir, perdir, sspec],
        out_specs=[perdir, sspec],
        out_shape=[jax.ShapeDtypeStruct((2, n, nseq, dm), F32), jax.ShapeDtypeStruct(s0.shape, F32)],
        scratch_shapes=[pltpu.VMEM((nk, nk, LANES), F32), pltpu.VMEM((RWKV_SLOTS, 5, nk, LANES), F32),
                        pltpu.VMEM((RWKV_SLOTS, nk, LANES), F32), pltpu.VMEM((RWKV_SLOTS, nk, LANES), F32),
                        pltpu.VMEM((3 * RWKV_SLOTS, LANES, LANES), F32)],
        compiler_params=_cparams("parallel", "parallel", "arbitrary"),
        name="rwkv_scan_grid",
    )(kk, r, v, ka, w, kd, s0)


def _rwkv_scan_lanes_kernel(kkf, rf, vf, kaf, wf, kdf, kkr, rr, vr, kar, wr, kdr, s0_ref, yf_ref, yr_ref, sf_ref,
                            st, mk, mv, ys, s_scr, y_scr):
    c = pl.program_id(0)
    nb, tc, _ = kkf.shape
    nk = RWKV_HEAD
    chains = s_scr.shape[1]
    vs = LANES // chains
    nv = nk // vs
    lo, hi = slice(0, nk), slice(nk, 2 * nk)

    @pl.when(c == 0)
    def _():
        st[...] = s0_ref[...]

    def load_pieces(i, slot):
        j = tc - 1 - i

        def both(f, r, lead=()):
            rows = [f[lead + (b, pl.ds(i, 1))] for b in range(nb)] + [r[lead + (b, pl.ds(j, 1))] for b in range(nb)]
            return jnp.concatenate(rows, axis=0)

        rep = lambda a: a if vs == 1 else jnp.concatenate([a] * vs, axis=1)

        def piece(p):
            if p == 0:
                xa, xb = both(kkf, kkr), both(kaf, kar, (0,))
            elif p == 1:
                xa, xb = both(wf, wr, (0,)), both(kdf, kdr, (0,))
            else:
                xa, xb = both(rf, rr), both(vf, vr)
            scr = s_scr.at[3 * slot + p]
            _heads_to_rows(scr, xa, lo, 2 * nb)
            _heads_to_rows(scr, xb, hi, 2 * nb)
            tt = scr[...].T
            mk[slot, 2 * p] = rep(tt[lo])
            if p < 2:
                mk[slot, 2 * p + 1] = rep(tt[hi])
            else:
                tv = tt[hi]
                mv[slot] = tv if vs == 1 else jnp.concatenate([tv[s * nv:(s + 1) * nv] for s in range(vs)], axis=1)
        return [functools.partial(piece, p) for p in range(3)]

    def store_pieces(i, slot):
        def piece():
            yt = ys[slot].T
            for s in range(vs):
                for h in range(RWKV_HEADS):
                    r0 = s * chains + h * 2 * nb
                    y_scr[slot, 0:2 * nb, h * nk + s * nv:h * nk + (s + 1) * nv] = yt[r0:r0 + 2 * nb, :]
            for b in range(nb):
                yf_ref[b, pl.ds(i, 1), :] = y_scr[slot, b:b + 1, :]
                yr_ref[b, pl.ds(tc - 1 - i, 1), :] = y_scr[slot, nb + b:nb + b + 1, :]
        return [piece]

    _rwkv_chain_steps(tc, nv, load_pieces, store_pieces, st, mk, mv, ys)

    @pl.when(c == pl.num_programs(0) - 1)
    def _():
        sf_ref[...] = st[...]


def _rwkv_scan_lanes(kk, r, v, ka, w, kd, s0):
    nb, n, dm = kk.shape
    nk = RWKV_HEAD
    nv = s0.shape[1]
    chains = RWKV_HEADS * 2 * nb
    tc = RWKV_TC
    nch = n // tc
    sf_ = pl.BlockSpec((nb, tc, dm), lambda t: (0, t, 0))
    sr_ = pl.BlockSpec((nb, tc, dm), lambda t: (0, nch - 1 - t, 0))
    df_ = pl.BlockSpec((1, nb, tc, dm), lambda t: (0, 0, t, 0))
    dr_ = pl.BlockSpec((1, nb, tc, dm), lambda t: (1, 0, nch - 1 - t, 0))
    sspec = pl.BlockSpec((nk, nv, LANES), lambda t: (0, 0, 0))
    yshape = jax.ShapeDtypeStruct((nb, n, dm), F32)
    return pl.pallas_call(
        _rwkv_scan_lanes_kernel,
        grid=(nch,),
        in_specs=[sf_, sf_, sf_, df_, df_, df_, sr_, sr_, sr_, dr_, dr_, dr_, sspec],
        out_specs=[sf_, sr_, sspec],
        out_shape=[yshape, yshape, jax.ShapeDtypeStruct((nk, nv, LANES), F32)],
        scratch_shapes=[pltpu.VMEM((nk, nv, LANES), F32), pltpu.VMEM((RWKV_SLOTS, 5, nk, LANES), F32),
                        pltpu.VMEM((RWKV_SLOTS, nv, LANES), F32), pltpu.VMEM((RWKV_SLOTS, nv, LANES), F32),
                        pltpu.VMEM((3 * RWKV_SLOTS, chains, LANES), F32),
                        pltpu.VMEM((RWKV_SLOTS, SUBLANES, dm), F32)],
        compiler_params=_cparams("arbitrary"),
        name="rwkv_scan_lanes",
    )(kk, r, v, ka, w, kd, kk, r, v, ka, w, kd, s0)


def _rwkv_out_kernel(h_ref, mod_ref, yf_ref, yb_ref, bonus_ref, g_ref, lng_ref, lnb_ref, wo_ref, ones_ref, o_ref):
    ones = ones_ref[...]
    y = (yf_ref[...] + yb_ref[...]).reshape(h_ref.shape)
    inv = 1.0 / RWKV_HEAD
    mean = _group_sum(y, ones) * inv
    yc = y - mean
    var = _group_sum(yc * yc, ones) * inv
    yn = yc * lax.rsqrt(var + RWKV_GN_EPS) * lng_ref[...] + lnb_ref[...] + bonus_ref[...]
    out = _dot((yn * g_ref[...]).astype(BF16), wo_ref[...])
    o_ref[...] = h_ref[...] + mod_ref[0][2:3] * out


def _rwkv_out(h, seg, mod, yf, yb, bonus, g, ln_g, ln_b, wo, tm=256):
    n, dm = h.shape
    tm = min(tm, seg.length)
    rid = seg.mod_row(tm)
    tile = pl.BlockSpec((tm, dm), lambda i: (i, 0))
    row = pl.BlockSpec((1, dm), lambda i: (0, 0))
    ones = _ones_blockdiag(RWKV_HEAD)
    stacked = yf.ndim == 3
    yf_spec = pl.BlockSpec((1, tm, dm), lambda i: (0, i, 0)) if stacked else tile
    yb_spec = pl.BlockSpec((1, tm, dm), lambda i: (1, i, 0)) if stacked else tile
    return pl.pallas_call(
        _rwkv_out_kernel,
        grid=(n // tm,),
        in_specs=[tile, pl.BlockSpec((1, 6, dm), lambda i: (rid(i), 0, 0)), yf_spec, yb_spec, tile, tile, row, row,
                  pl.BlockSpec((dm, dm), lambda i: (0, 0)), pl.BlockSpec(ones.shape, lambda i: (0, 0))],
        out_specs=tile,
        out_shape=jax.ShapeDtypeStruct((n, dm), F32),
        compiler_params=_cparams("parallel"),
        name="rwkv_out",
    )(h, mod, yf, yb, bonus, g, ln_g, ln_b, wo, ones)


def _rwkv_layer(h, seg, mod, norm_g, s0, rw):
    nseq, length = seg.nseq, seg.length
    hh, nk = RWKV_HEADS, RWKV_HEAD
    r, v, kk, w, ka, kd, g, bonus = _rwkv_proj(h, seg, mod, norm_g, *rw['proj'])
    per_dir = nseq * hh
    n = nseq * length
    dm = hh * nk

    seq3 = lambda x: x.reshape(*x.shape[:-2], nseq, length, dm)

    def tmaj(x):
        nl = x.ndim - 2
        return jnp.transpose(seq3(x), tuple(range(nl)) + (nl + 1, nl, nl + 2))

    if per_dir % LANES == 0:
        ng = nseq // SUBLANES
        s0_s = jnp.transpose(s0.reshape(ng, SUBLANES, 2, hh, nk, nk), (2, 5, 4, 0, 3, 1)).reshape(2, nk, nk, per_dir)
        y_t, sf = _rwkv_scan_grid(tmaj(kk), tmaj(r), tmaj(v), tmaj(ka), tmaj(w), tmaj(kd), s0_s)
        yf = yb = jnp.transpose(y_t, (0, 2, 1, 3)).reshape(2, n, dm)
        s_fin = jnp.transpose(sf.reshape(2, nk, nk, ng, hh, SUBLANES), (3, 5, 0, 4, 2, 1)).reshape(nseq, 2, hh, nk, nk)
    else:
        vs = LANES // (2 * per_dir)
        assert vs * 2 * per_dir == LANES
        nv = nk // vs
        s0_s = jnp.transpose(s0.reshape(nseq, 2, hh, vs, nv, nk), (5, 4, 3, 2, 1, 0)).reshape(nk, nv, LANES)
        y_f, y_r, sf = _rwkv_scan_lanes(seq3(kk), seq3(r), seq3(v), seq3(ka), seq3(w), seq3(kd), s0_s)
        yf, yb = y_f.reshape(n, dm), y_r.reshape(n, dm)
        s_fin = jnp.transpose(sf.reshape(nk, nv, vs, hh, 2, nseq), (5, 4, 3, 2, 1, 0)).reshape(nseq, 2, hh, nk, nk)
    h_new = _rwkv_out(h, seg, mod, yf, yb, bonus, g, *rw['out'])
    return h_new, s_fin


def _rope_tables(length, half, first_lane, period):
    lane = jnp.arange(LANES) % period - first_lane
    active = (lane >= 0) & (lane < 4 * half)
    quarter = jnp.clip(lane, 0, 4 * half - 1) // half
    freq = ROPE_BASE ** (-(jnp.clip(lane, 0, 4 * half - 1) % half).astype(F32) / half)
    t = jnp.arange(length)
    pos = jnp.where(quarter[None, :] < 2, (t // GRID_W)[:, None], (t % GRID_W)[:, None]).astype(F32)
    ang = pos * freq[None, :]
    cos = jnp.where(active[None], jnp.cos(ang), 1.0)
    sin = jnp.where(active[None], jnp.sin(ang), 0.0)
    first = (quarter % 2 == 0)[None]
    return cos, jnp.where(first, -sin, 0.0), jnp.where(first, 0.0, sin)


def _rope(x, cos, sa, sb, half):
    w = x.shape[-1]
    rep = w // LANES
    tile = (lambda a: a) if rep == 1 else (lambda a: jnp.concatenate([a] * rep, axis=-1))
    return x * tile(cos) + pltpu.roll(x, w - half, 1) * tile(sa) + pltpu.roll(x, half, 1) * tile(sb)


def _attn_kernel(*refs, groups, dq, dv, hpb, scale, band, has_ctx, has_sink):
    it = iter(refs)
    q_ref, k_ref, v_ref = next(it), next(it), next(it)
    kc_ref, vc_ref = (next(it), next(it)) if has_ctx else (None, None)
    sink_ref = next(it) if has_sink else None
    o_ref = next(it)
    tq = q_ref.shape[1]
    lk = k_ref.shape[1]
    rows = groups * tq
    if band:
        lw = min(lk, tq + 2 * WINDOW)
        q0 = pl.program_id(2) * tq
        start = pl.multiple_of(jnp.clip(q0 - WINDOW, 0, lk - lw), LANES)
        ksl = pl.ds(start, lw)
        qpos = q0 + (lax.broadcasted_iota(jnp.int32, (rows, lw), 0) & (tq - 1))
        kpos = start + lax.broadcasted_iota(jnp.int32, (rows, lw), 1)
        valid = jnp.abs(qpos - kpos) <= WINDOW
    else:
        ksl = slice(None)
    for j in range(hpb):
        kh = k_ref[0, ksl, j * dq:(j + 1) * dq].astype(BF16)
        vh = v_ref[0, ksl, j * dv:(j + 1) * dv].astype(BF16)
        heads = range(j * groups, (j + 1) * groups)
        qs = [q_ref[0, :, hq * dq:(hq + 1) * dq].astype(BF16) for hq in heads]
        qh = qs[0] if groups == 1 else jnp.concatenate(qs, axis=0)
        s = _dot_nt(qh, kh) * scale
        if band:
            s = jnp.where(valid, s, -jnp.inf)
        m = jnp.max(s, axis=-1, keepdims=True)
        if has_ctx:
            kch = kc_ref[0, :, j * dq:(j + 1) * dq].astype(BF16)
            vch = vc_ref[0, :, j * dv:(j + 1) * dv].astype(BF16)
            sc = _dot_nt(qh, kch) * scale
            m = jnp.maximum(m, jnp.max(sc, axis=-1, keepdims=True))
        if has_sink:
            sk = [jnp.broadcast_to(sink_ref[0, :, hq:hq + 1], (tq, 1)) for hq in heads]
            snk = sk[0] if groups == 1 else jnp.concatenate(sk, axis=0)
            m = jnp.maximum(m, snk)
        p = jnp.exp(s - m)
        den = jnp.sum(p, axis=-1, keepdims=True)
        acc = _dot(p.astype(BF16), vh)
        if has_ctx:
            pc = jnp.exp(sc - m)
            den = den + jnp.sum(pc, axis=-1, keepdims=True)
            acc = acc + _dot(pc.astype(BF16), vch)
        if has_sink:
            den = den + jnp.exp(snk - m)
        out = acc / den
        for g, hq in enumerate(heads):
            o_ref[0, :, hq * dv:(hq + 1) * dv] = out[g * tq:(g + 1) * tq]


def _attention(q, k, v, ctx, sink, *, heads, groups, dq, dv, scale, band, tq=256):
    b, lq, _ = q.shape
    lk = k.shape[1]
    hpb = 2
    nhb = heads // hpb
    ins = [q, k, v]
    specs = [
        pl.BlockSpec((1, tq, hpb * groups * dq), lambda bi, hb, i: (bi, i, hb)),
        pl.BlockSpec((1, lk, hpb * dq), lambda bi, hb, i: (bi, 0, hb)),
        pl.BlockSpec((1, lk, hpb * dv), lambda bi, hb, i: (bi, 0, hb)),
    ]
    if ctx is not None:
        lc = ctx[0].shape[1]
        ins += list(ctx)
        specs += [pl.BlockSpec((1, lc, hpb * dq), lambda bi, hb, i: (bi, 0, hb)),
                  pl.BlockSpec((1, lc, hpb * dv), lambda bi, hb, i: (bi, 0, hb))]
    if sink is not None:
        sk = jnp.pad(sink.reshape(nhb, 1, hpb * groups), ((0, 0), (0, 0), (0, LANES - hpb * groups)))
        ins.append(sk)
        specs.append(pl.BlockSpec((1, 1, LANES), lambda bi, hb, i: (hb, 0, 0)))
    return pl.pallas_call(
        functools.partial(_attn_kernel, groups=groups, dq=dq, dv=dv, hpb=hpb, scale=scale, band=band,
                          has_ctx=ctx is not None, has_sink=sink is not None),
        grid=(b, nhb, lq // tq),
        in_specs=specs,
        out_specs=pl.BlockSpec((1, tq, hpb * groups * dv), lambda bi, hb, i: (bi, i, hb)),
        out_shape=jax.ShapeDtypeStruct((b, lq, heads * groups * dv), F32),
        compiler_params=_cparams("parallel", "parallel", "arbitrary"),
        name="attention",
    )(*ins)


def _proj_res_kernel(h_ref, mod_ref, a_ref, w_ref, o_ref):
    o_ref[...] = h_ref[...] + mod_ref[0][2:3] * _dot(a_ref[...].astype(BF16), w_ref[...])


def _proj_residual(h, seg, mod, a, w, tm=256):
    n, dm = h.shape
    tm = min(tm, seg.length)
    rid = seg.mod_row(tm)
    return pl.pallas_call(
        _proj_res_kernel,
        grid=(n // tm,),
        in_specs=[pl.BlockSpec((tm, dm), lambda i: (i, 0)), pl.BlockSpec((1, 6, dm), lambda i: (rid(i), 0, 0)),
                  pl.BlockSpec((tm, a.shape[1]), lambda i: (i, 0)), pl.BlockSpec(w.shape, lambda i: (0, 0))],
        out_specs=pl.BlockSpec((tm, dm), lambda i: (i, 0)),
        out_shape=jax.ShapeDtypeStruct((n, dm), F32),
        compiler_params=_cparams("parallel"),
        name="proj_residual",
    )(h, mod, a, w)


GQA_NQ = GQA_HEADS * GQA_HEAD_DIM
GQA_NK = GQA_KV_HEADS * GQA_HEAD_DIM
GQA_ROPE_HALF = GQA_HEAD_DIM // 4


def _gqa_qkv_kernel(*refs, use_rope):
    if use_rope:
        h_ref, mod_ref, g_ref, w_ref, gq_ref, gk_ref, ones_ref, cos_ref, sa_ref, sb_ref, q_ref, k_ref, v_ref = refs
    else:
        h_ref, mod_ref, g_ref, w_ref, gq_ref, gk_ref, ones_ref, q_ref, k_ref, v_ref = refs
    mod = mod_ref[0]
    u = _modulate(h_ref[...], g_ref[...], mod[0:1], mod[1:2]).astype(BF16)
    qkv = _dot(u, w_ref[...])
    ones = ones_ref[...]
    inv = 1.0 / GQA_HEAD_DIM
    q = qkv[:, :GQA_NQ]
    k = qkv[:, GQA_NQ:GQA_NQ + GQA_NK]
    q = q * lax.rsqrt(_group_sum(q * q, ones) * inv + NORM_EPS) * gq_ref[...]
    k = k * lax.rsqrt(_group_sum(k * k, ones) * inv + NORM_EPS) * gk_ref[...]
    if use_rope:
        cos, sa, sb = cos_ref[...], sa_ref[...], sb_ref[...]
        q = _rope(q, cos, sa, sb, GQA_ROPE_HALF)
        k = _rope(k, cos, sa, sb, GQA_ROPE_HALF)
    q_ref[...] = q
    k_ref[...] = k
    v_ref[...] = qkv[:, GQA_NQ + GQA_NK:]


def _gqa_qkv(h, seg, mod, norm_g, w_qkv, gq, gk, rope, tm=256):
    n, dm = h.shape
    tm = min(tm, seg.length)
    per = seg.length // tm
    rid = seg.mod_row(tm)
    ones = _ones_blockdiag(GQA_HEAD_DIM)
    full = lambda a: pl.BlockSpec(a.shape, lambda i, nd=a.ndim: (0,) * nd)
    ins = [h, mod, norm_g, w_qkv, gq, gk, ones]
    specs = [pl.BlockSpec((tm, dm), lambda i: (i, 0)), pl.BlockSpec((1, 6, dm), lambda i: (rid(i), 0, 0)),
             full(norm_g), full(w_qkv), full(gq), full(gk), full(ones)]
    if rope is not None:
        ins += list(rope)
        specs += [pl.BlockSpec((tm, LANES), lambda i: (i % per, 0))] * 3
    return pl.pallas_call(
        functools.partial(_gqa_qkv_kernel, use_rope=rope is not None),
        grid=(n // tm,),
        in_specs=specs,
        out_specs=[pl.BlockSpec((tm, GQA_NQ), lambda i: (i, 0)), pl.BlockSpec((tm, GQA_NK), lambda i: (i, 0)),
                   pl.BlockSpec((tm, GQA_NK), lambda i: (i, 0))],
        out_shape=[jax.ShapeDtypeStruct((n, GQA_NQ), F32), jax.ShapeDtypeStruct((n, GQA_NK), F32),
                   jax.ShapeDtypeStruct((n, GQA_NK), F32)],
        compiler_params=_cparams("parallel"),
        name="gqa_qkv",
    )(*ins)


def _gqa_layer(h, seg, mod, norm_g, gw, ctx):
    w_qkv, gq, gk, sink, w_o = gw
    latent = ctx is not None
    rope = _rope_tables(seg.length, GQA_ROPE_HALF, 0, 4 * GQA_ROPE_HALF) if latent else None
    q, k, v = _gqa_qkv(h, seg, mod, norm_g, w_qkv, gq, gk, rope)
    b3 = lambda a: a.reshape(seg.nseq, seg.length, a.shape[-1])
    att = _attention(b3(q), b3(k), b3(v), ctx, sink, heads=GQA_KV_HEADS, groups=GQA_HEADS // GQA_KV_HEADS,
                     dq=GQA_HEAD_DIM, dv=GQA_HEAD_DIM, scale=GQA_HEAD_DIM ** -0.5, band=latent)
    h_new = _proj_residual(h, seg, mod, att.reshape(seg.n, -1), w_o)
    return h_new, k, v


MLA_QW = MLA_HEADS * MLA_PAD
MLA_VW = MLA_HEADS * MLA_V
MLA_CKR = MLA_KV_LORA + LANES
MLA_ROPE_HALF = MLA_ROPE // 4


def _mla_expand(ckv, kr, wuk_ref, wuv_ref, kg_ref, ones_ref):
    cb = ckv.astype(BF16)
    kn = _dot(cb, wuk_ref[...])
    kr_t = jnp.concatenate([pltpu.roll(kr, MLA_NOPE, 1)] * MLA_HEADS, axis=-1)
    k = kn + kr_t
    k = k * lax.rsqrt(_group_sum(k * k, ones_ref[...]) * (1.0 / MLA_QK) + NORM_EPS) * kg_ref[...]
    return k, _dot(cb, wuv_ref[...])


def _mla_proj_kernel(*refs, use_rope):
    (h_ref, mod_ref, g_ref, wdq_ref, qlg_ref, wuq_ref, qg_ref, wdkv_ref, kvg_ref, wuk_ref, wuv_ref, kg_ref,
     ones_ref) = refs[:13]
    if use_rope:
        cos_ref, sa_ref, sb_ref = refs[13:16]
    q_ref, k_ref, v_ref, ckv_ref, kr_ref = refs[-5:]
    mod = mod_ref[0]
    u = _modulate(h_ref[...], g_ref[...], mod[0:1], mod[1:2]).astype(BF16)
    qd = _dot(u, wdq_ref[...])
    qd = qd * lax.rsqrt(jnp.mean(qd * qd, axis=-1, keepdims=True) + NORM_EPS) * qlg_ref[...]
    q = _dot(qd.astype(BF16), wuq_ref[...])
    q = q * lax.rsqrt(_group_sum(q * q, ones_ref[...]) * (1.0 / MLA_QK) + NORM_EPS) * qg_ref[...]
    ckr = _dot(u, wdkv_ref[...])
    ckv = ckr[:, :MLA_KV_LORA]
    ckv = ckv * lax.rsqrt(jnp.mean(ckv * ckv, axis=-1, keepdims=True) + NORM_EPS) * kvg_ref[...]
    kr = ckr[:, MLA_KV_LORA:]
    k, v = _mla_expand(ckv, kr, wuk_ref, wuv_ref, kg_ref, ones_ref)
    if use_rope:
        cos, sa, sb = cos_ref[...], sa_ref[...], sb_ref[...]
        q = _rope(q, cos, sa, sb, MLA_ROPE_HALF)
        k = _rope(k, cos, sa, sb, MLA_ROPE_HALF)
    q_ref[...] = q.astype(BF16)
    k_ref[...] = k.astype(BF16)
    v_ref[...] = v.astype(BF16)
    ckv_ref[...] = ckv
    kr_ref[...] = kr


def _mla_proj(h, seg, mod, norm_g, pw, rope, tm=256):
    n, dm = h.shape
    tm = min(tm, seg.length)
    per = seg.length // tm
    rid = seg.mod_row(tm)
    ones = _ones_blockdiag(MLA_PAD)
    full = lambda a: pl.BlockSpec(a.shape, lambda i, nd=a.ndim: (0,) * nd)
    ins = [h, mod, norm_g, *pw, ones]
    specs = [pl.BlockSpec((tm, dm), lambda i: (i, 0)), pl.BlockSpec((1, 6, dm), lambda i: (rid(i), 0, 0)),
             full(norm_g)] + [full(a) for a in pw] + [full(ones)]
    if rope is not None:
        ins += list(rope)
        specs += [pl.BlockSpec((tm, LANES), lambda i: (i % per, 0))] * 3
    widths = (MLA_QW, MLA_QW, MLA_VW, MLA_KV_LORA, LANES)
    return pl.pallas_call(
        functools.partial(_mla_proj_kernel, use_rope=rope is not None),
        grid=(n // tm,),
        in_specs=specs,
        out_specs=[pl.BlockSpec((tm, w), lambda i: (i, 0)) for w in widths],
        out_shape=[jax.ShapeDtypeStruct((n, w), BF16 if j < 3 else F32) for j, w in enumerate(widths)],
        compiler_params=_cparams("parallel"),
        name="mla_proj",
    )(*ins)


def _mla_ctx_kernel(ckv_ref, kr_ref, wuk_ref, wuv_ref, kg_ref, ones_ref, k_ref, v_ref):
    k, v = _mla_expand(ckv_ref[...], kr_ref[...], wuk_ref, wuv_ref, kg_ref, ones_ref)
    k_ref[...] = k.astype(BF16)
    v_ref[...] = v.astype(BF16)


def _mla_ctx_expand(ckv, kr, wuk, wuv, kg, tm=256):
    n = ckv.shape[0]
    tm = min(tm, n)
    ones = _ones_blockdiag(MLA_PAD)
    full = lambda a: pl.BlockSpec(a.shape, lambda i, nd=a.ndim: (0,) * nd)
    return pl.pallas_call(
        _mla_ctx_kernel,
        grid=(n // tm,),
        in_specs=[pl.BlockSpec((tm, MLA_KV_LORA), lambda i: (i, 0)), pl.BlockSpec((tm, LANES), lambda i: (i, 0)),
                  full(wuk), full(wuv), full(kg), full(ones)],
        out_specs=[pl.BlockSpec((tm, MLA_QW), lambda i: (i, 0)), pl.BlockSpec((tm, MLA_VW), lambda i: (i, 0))],
        out_shape=[jax.ShapeDtypeStruct((n, MLA_QW), BF16), jax.ShapeDtypeStruct((n, MLA_VW), BF16)],
        compiler_params=_cparams("parallel"),
        name="mla_ctx_expand",
    )(ckv, kr, wuk, wuv, kg, ones)


def _mla_weights(w_dq, q_lora_g, w_uq, w_dkv, kv_g, w_ukv, q_g, k_g, w_o):
    padh = lambda a: jnp.pad(a, [(0, 0)] * (a.ndim - 1) + [(0, MLA_PAD - a.shape[-1])])
    wuq = padh(w_uq.reshape(MLA_Q_LORA, MLA_HEADS, MLA_QK)).reshape(MLA_Q_LORA, MLA_QW).astype(BF16)
    wdkv = jnp.pad(w_dkv, ((0, 0), (0, MLA_CKR - w_dkv.shape[1]))).astype(BF16)
    ukv = w_ukv.reshape(MLA_KV_LORA, MLA_HEADS, MLA_NOPE + MLA_V)
    wuk = padh(ukv[:, :, :MLA_NOPE]).reshape(MLA_KV_LORA, MLA_QW).astype(BF16)
    wuv = ukv[:, :, MLA_NOPE:].reshape(MLA_KV_LORA, MLA_VW).astype(BF16)
    tile_h = lambda g: jnp.tile(padh(g), MLA_HEADS)[None]
    proj = (w_dq.astype(BF16), q_lora_g[None], wuq, tile_h(q_g), wdkv, kv_g[None], wuk, wuv, tile_h(k_g))
    return proj, w_o.astype(BF16)


def _mla_layer(h, seg, mod, norm_g, mw, cache):
    proj, w_o = mw
    latent = cache is not None
    rope = _rope_tables(seg.length, MLA_ROPE_HALF, MLA_NOPE, MLA_PAD) if latent else None
    q, k, v, ckv, kr = _mla_proj(h, seg, mod, norm_g, proj, rope)
    ctx = None
    if latent:
        c_ckv, c_kr = cache
        lc = c_ckv.shape[1]
        kr_p = jnp.pad(c_kr.reshape(-1, MLA_ROPE), ((0, 0), (0, LANES - MLA_ROPE)))
        kc, vc = _mla_ctx_expand(c_ckv.reshape(-1, MLA_KV_LORA), kr_p, proj[6], proj[7], proj[8])
        ctx = (kc.reshape(seg.nseq, lc, MLA_QW), vc.reshape(seg.nseq, lc, MLA_VW))
    b3 = lambda a: a.reshape(seg.nseq, seg.length, a.shape[-1])
    att = _attention(b3(q), b3(k), b3(v), ctx, None, heads=MLA_HEADS, groups=1, dq=MLA_PAD, dv=MLA_V,
                     scale=MLA_QK ** -0.5, band=False)
    h_new = _proj_residual(h, seg, mod, att.reshape(seg.n, -1), w_o)
    return h_new, ckv, kr[:, :MLA_ROPE]


def kernel(x_prompt, x_sample, state_s5_re, state_s5_im, state_rwkv, cache_gqa_k, cache_gqa_v, cache_mla_ckv, cache_mla_krope, c, c_ctx, ada_w, ada_b, norm1_g, norm2_g, s5_lambda_re, s5_lambda_im, s5_log_step, s5_b_re, s5_b_im, s5_c_re, s5_c_im, s5_d, s5_glu_w1, s5_glu_w2, rwkv_mu, rwkv_w_r, rwkv_w_k, rwkv_w_v, rwkv_w_o, rwkv_w0, rwkv_w_l1, rwkv_w_l2, rwkv_a0, rwkv_a_l1, rwkv_a_l2, rwkv_g_l1, rwkv_g_l2, rwkv_k_k, rwkv_k_a, rwkv_r_k, rwkv_ln_g, rwkv_ln_b, gqa_w_qkv, gqa_q_norm, gqa_k_norm, gqa_sink, gqa_w_o, mla_w_dq, mla_q_lora_norm, mla_w_uq, mla_w_dkv, mla_kv_norm, mla_w_ukv, mla_q_norm, mla_k_norm, mla_w_o, moe_w_group, moe_b_group, moe_w_expert, moe_b_expert, moe_w_gate, moe_w_up, moe_w_down):
    bp, lp, dm = x_prompt.shape
    bs, ls, _ = x_sample.shape
    depth = ada_w.shape[0]
    assert dm == D_MODEL and 1 + bs <= MOD_ROWS
    seg_p = _Seg(bp, lp, 0, True)
    seg_s = _Seg(bs, ls, 1, False)
    bf = lambda a: a.astype(BF16)

    cond = jnp.zeros((MOD_ROWS, dm), F32).at[0].set(c_ctx).at[1:1 + bs].set(c)
    mods = _ada_all(cond, ada_w, ada_b).reshape(depth, MOD_ROWS, 6, dm)

    hp = x_prompt.reshape(bp * lp, dm)
    hs = x_sample.reshape(bs * ls, dm)
    outs = {}
    for layer in range(depth):
        kind = layer % 4
        mod = mods[layer]
        g1 = norm1_g[layer][None]
        if kind == 0:
            sw = _s5_weights(s5_lambda_re, s5_lambda_im, s5_log_step, s5_b_re, s5_b_im, s5_c_re, s5_c_im)
            w1, w2, dsk = bf(s5_glu_w1), bf(s5_glu_w2), s5_d[None]
            zeros = jnp.zeros((bp, 2, S5_N), F32)
            hp, f_re, f_im = _s5_layer(hp, seg_p, mod, g1, zeros, zeros, sw, dsk, w1, w2, True)
            hs, _, _ = _s5_layer(hs, seg_s, mod, g1, state_s5_re.reshape(bs, 2, S5_N),
                                 state_s5_im.reshape(bs, 2, S5_N), sw, dsk, w1, w2, False)
            outs['s5_re'] = f_re.reshape(bp, 2, S5_GROUPS, S5_STATE)
            outs['s5_im'] = f_im.reshape(bp, 2, S5_GROUPS, S5_STATE)
        elif kind == 1:
            vec = jnp.zeros((SUBLANES, dm), F32).at[0].set(rwkv_k_k).at[1].set(rwkv_k_a).at[2].set(rwkv_r_k.reshape(-1))
            rw = dict(
                proj=(rwkv_mu, bf(rwkv_w_r), bf(rwkv_w_k), bf(rwkv_w_v), bf(rwkv_w_l1), bf(rwkv_w_l2),
                      bf(rwkv_a_l1), bf(rwkv_a_l2), bf(rwkv_g_l1), bf(rwkv_g_l2), vec,
                      rwkv_w0[:, None, :], rwkv_a0[:, None, :]),
                out=(rwkv_ln_g[None], rwkv_ln_b[None], bf(rwkv_w_o)))
            zeros = jnp.zeros((bp, 2, RWKV_HEADS, RWKV_HEAD, RWKV_HEAD), F32)
            hp, outs['rwkv'] = _rwkv_layer(hp, seg_p, mod, g1, zeros, rw)
            hs, _ = _rwkv_layer(hs, seg_s, mod, g1, state_rwkv, rw)
        elif kind == 2:
            gw = (bf(gqa_w_qkv), jnp.tile(gqa_q_norm, GQA_HEADS)[None], jnp.tile(gqa_k_norm, GQA_KV_HEADS)[None],
                  gqa_sink, bf(gqa_w_o))
            hp, kp, vp = _gqa_layer(hp, seg_p, mod, g1, gw, None)
            outs['gqa_k'] = kp.reshape(bp, lp, GQA_KV_HEADS, GQA_HEAD_DIM)
            outs['gqa_v'] = vp.reshape(bp, lp, GQA_KV_HEADS, GQA_HEAD_DIM)
            lc = cache_gqa_k.shape[1]
            ctx = (cache_gqa_k.reshape(bs, lc, GQA_NK), cache_gqa_v.reshape(bs, lc, GQA_NK))
            hs, _, _ = _gqa_layer(hs, seg_s, mod, g1, gw, ctx)
        else:
            mw = _mla_weights(mla_w_dq, mla_q_lora_norm, mla_w_uq, mla_w_dkv, mla_kv_norm, mla_w_ukv, mla_q_norm,
                              mla_k_norm, mla_w_o)
            hp, ckv_p, kr_p = _mla_layer(hp, seg_p, mod, g1, mw, None)
            outs['mla_ckv'] = ckv_p.reshape(bp, lp, MLA_KV_LORA)
            outs['mla_kr'] = kr_p.reshape(bp, lp, MLA_ROPE)
            hs, _, _ = _mla_layer(hs, seg_s, mod, g1, mw, (cache_mla_ckv, cache_mla_krope))
        wr = jnp.zeros((dm, ROUTE_W), F32).at[:, :MOE_GROUPS].set(moe_w_group[layer])
        wr = wr.at[:, MOE_GROUPS:MOE_GROUPS + MOE_EXPERTS].set(moe_w_expert[layer])
        br = jnp.zeros((1, ROUTE_W), F32).at[0, :MOE_GROUPS].set(moe_b_group[layer])
        br = br.at[0, MOE_GROUPS:MOE_GROUPS + MOE_EXPERTS].set(moe_b_expert[layer])
        wr_hi, wr_lo = _hi_lo(wr)
        mo = (mod, norm2_g[layer][None], wr_hi, wr_lo, br, bf(moe_w_gate[layer]), bf(moe_w_up[layer]),
              bf(moe_w_down[layer]))
        hp = _moe_layer(hp, seg_p, *mo)
        hs = _moe_layer(hs, seg_s, *mo)
    return (hp.reshape(bp, lp, dm), hs.reshape(bs, ls, dm), outs['s5_re'], outs['s5_im'], outs['rwkv'],
            outs['gqa_k'], outs['gqa_v'], outs['mla_ckv'], outs['mla_kr'])
```
